```python
import math
import jax
import jax.numpy as jnp
from jax import lax
import numpy as np

D_MODEL = 1024
BATCH = 4
SEQ = 8192
DEPTH = 2

N_A = DEPTH // 2
N_B = DEPTH - N_A
N_DENSE = (DEPTH + 1) // 2
N_MOE = DEPTH // 2
N_MOD = 6

MLA_HEADS = 16
MLA_Q_RANK = 384
MLA_KV_RANK = 256
MLA_NOPE = 64
MLA_ROPE = 32
MLA_V = 64
MLA_QK = MLA_NOPE + MLA_ROPE
MLA_IN = MLA_Q_RANK + MLA_KV_RANK + MLA_ROPE
ROPE_THETA = 10000.0
Q_BLOCK = 128

DIL_PATTERNS = ((128, 1), (512, 4), (2048, 16))
DIL_GROUPS = len(DIL_PATTERNS)
DIL_HEADS = 8
DIL_HEAD_DIM = 64
DIL_BLOCK = 128
DIL_Q = DIL_GROUPS * DIL_HEADS * DIL_HEAD_DIM
DIL_KV = 2 * DIL_Q
DIL_OUT = DIL_HEADS * DIL_HEAD_DIM

RPB_BUCKETS = 32
RPB_MAX_DIST = 2048

D_FF = 2816
N_EXPERTS = 8
TOP_K = 2
EPS = 1e-6

kernel_name = "yoco_mla_dilated_moe_hybrid"


def rmsnorm(x, g):
    xf = x.astype(jnp.float32)
    y = xf * lax.rsqrt(jnp.mean(xf * xf, axis=-1, keepdims=True) + EPS)
    return (y * g.astype(jnp.float32)).astype(x.dtype)


def modulate(h, shift, scale):
    return h * (1 + scale[:, None, :]) + shift[:, None, :]


def apply_rope(x, cos, sin):
    xf = x.astype(jnp.float32)
    x1, x2 = jnp.split(xf, 2, axis=-1)
    return jnp.concatenate([x1 * cos - x2 * sin, x1 * sin + x2 * cos], axis=-1).astype(x.dtype)


def swiglu(h, w_gu, w_down):
    gate, up = jnp.split(h @ w_gu, 2, axis=-1)
    return (jax.nn.silu(gate) * up) @ w_down


def causal_attention(q, k, v, scale):
    B, S, H, Dk = q.shape
    nb = S // Q_BLOCK
    qb = q.reshape(B, nb, Q_BLOCK, H, Dk).transpose(1, 0, 2, 3, 4)
    k_pos = jnp.arange(S)

    def one_block(args):
        i, qi = args
        s = jnp.einsum('bqhd,bkhd->bhqk', qi, k, preferred_element_type=jnp.float32) * scale
        q_pos = i * Q_BLOCK + jnp.arange(Q_BLOCK)
        s = jnp.where(k_pos[None, :] <= q_pos[:, None], s, -jnp.inf)
        p = jax.nn.softmax(s, axis=-1)
        return jnp.einsum('bhqk,bkhd->bqhd', p.astype(v.dtype), v)

    o = lax.map(one_block, (jnp.arange(nb), qb))
    return o.transpose(1, 0, 2, 3, 4).reshape(B, S, H, v.shape[-1])


def mla(hn, positions, w_in, g_q, w_q_up, g_kv, w_kv_up, w_out):
    B, S, _ = hn.shape
    q_lat, kv_lat, k_rope = jnp.split(hn @ w_in, [MLA_Q_RANK, MLA_Q_RANK + MLA_KV_RANK], axis=-1)
    q = (rmsnorm(q_lat, g_q) @ w_q_up).reshape(B, S, MLA_HEADS, MLA_QK)
    kv = (rmsnorm(kv_lat, g_kv) @ w_kv_up).reshape(B, S, MLA_HEADS, MLA_NOPE + MLA_V)
    q_nope, q_rope = jnp.split(q, [MLA_NOPE], axis=-1)
    k_nope, v = jnp.split(kv, [MLA_NOPE], axis=-1)
    inv_freq = ROPE_THETA ** (-jnp.arange(0, MLA_ROPE, 2, dtype=jnp.float32) / MLA_ROPE)
    ang = positions.astype(jnp.float32)[..., None] * inv_freq
    cos, sin = jnp.cos(ang), jnp.sin(ang)
    q_rope = apply_rope(q_rope, cos[:, :, None], sin[:, :, None])
    k_rope = apply_rope(k_rope, cos, sin)
    q = jnp.concatenate([q_nope, q_rope], axis=-1)
    k = jnp.concatenate([k_nope, jnp.broadcast_to(k_rope[:, :, None], (B, S, MLA_HEADS, MLA_ROPE))], axis=-1)
    o = causal_attention(q, k, v, MLA_QK ** -0.5)
    return o.reshape(B, S, MLA_HEADS * MLA_V) @ w_out


def rpb_bucket(dist):
    exact = RPB_BUCKETS // 2
    d = jnp.maximum(dist, 0)
    d_f = jnp.maximum(d, 1).astype(jnp.float32)
    large = exact + (jnp.log(d_f / exact) / math.log(RPB_MAX_DIST / exact)
                     * (RPB_BUCKETS - exact)).astype(jnp.int32)
    return jnp.where(d < exact, d, jnp.minimum(large, RPB_BUCKETS - 1))


def dilated_group(q, k, v, table, span, dil):
    B, S, H, Dh = q.shape
    n = S // dil
    nb = -(-n // DIL_BLOCK)
    n_pad = nb * DIL_BLOCK

    def to_sub(t):
        t = t.reshape(B, n, dil, H, Dh).transpose(0, 2, 1, 3, 4)
        return jnp.pad(t, ((0, 0), (0, 0), (0, n_pad - n), (0, 0), (0, 0)))

    def band(t):
        t = jnp.pad(t, ((0, 0), (0, 0), (DIL_BLOCK, 0), (0, 0), (0, 0)))
        prev = t[:, :, :n_pad].reshape(B, dil, nb, DIL_BLOCK, H, Dh)
        cur = t[:, :, DIL_BLOCK:].reshape(B, dil, nb, DIL_BLOCK, H, Dh)
        return jnp.concatenate([prev, cur], axis=3)

    qs = to_sub(q).reshape(B, dil, nb, DIL_BLOCK, H, Dh)
    ks = band(to_sub(k))
    vs = band(to_sub(v))

    dist = jnp.arange(DIL_BLOCK)[:, None] + DIL_BLOCK - jnp.arange(2 * DIL_BLOCK)[None, :]
    key_idx = (jnp.arange(nb)[:, None] - 1) * DIL_BLOCK + jnp.arange(2 * DIL_BLOCK)[None, :]
    mask = ((dist >= 0) & (dist <= span))[None] & (key_idx >= 0)[:, None, :]
    bias = jnp.transpose(table[rpb_bucket(dist * dil)], (2, 0, 1)).astype(jnp.float32)

    s = jnp.einsum('brnqhd,brnkhd->brnhqk', qs, ks, preferred_element_type=jnp.float32)
    s = s * (DIL_HEAD_DIM ** -0.5) + bias
    s = jnp.where(mask[None, None, :, None], s, -jnp.inf)
    m = jnp.max(s, axis=-1, keepdims=True)
    p = jnp.exp(s - m)
    l = jnp.sum(p, axis=-1, keepdims=True)
    o = jnp.einsum('brnhqk,brnkhd->brnqhd', p.astype(v.dtype), vs, preferred_element_type=jnp.float32)
    o = o / jnp.transpose(l, (0, 1, 2, 4, 3, 5))
    lse = jnp.transpose((m + jnp.log(l))[..., 0], (0, 1, 2, 4, 3))

    def from_sub(t):
        t = t.reshape((B, dil, n_pad) + t.shape[4:])[:, :, :n]
        t = jnp.swapaxes(t, 1, 2)
        return t.reshape((B, S) + t.shape[3:])

    return from_sub(o), from_sub(lse)


def shared_kv(h, c_act, g, w_ada_kv, b_ada_kv, w_kv):
    B, S, _ = h.shape
    shift, scale = jnp.split(c_act @ w_ada_kv + b_ada_kv, 2, axis=-1)
    hn = modulate(rmsnorm(h, g), shift, scale)
    return (hn @ w_kv).reshape(B, S, DIL_GROUPS, 2, DIL_HEADS, DIL_HEAD_DIM)


def dilated_attention(hn, kv_b, w_q, w_o, rpb_table):
    B, S, _ = hn.shape
    q = (hn @ w_q).reshape(B, S, DIL_GROUPS, DIL_HEADS, DIL_HEAD_DIM)
    outs, lses = [], []
    for g, (window, dil) in enumerate(DIL_PATTERNS):
        table = rpb_table[:, g * DIL_HEADS:(g + 1) * DIL_HEADS]
        o, lse = dilated_group(q[:, :, g], kv_b[:, :, g, 0], kv_b[:, :, g, 1], table, window // dil, dil)
        outs.append(o)
        lses.append(lse)
    wts = jax.nn.softmax(jnp.stack(lses, axis=0), axis=0)
    o = jnp.sum(jnp.stack(outs, axis=0) * wts[..., None], axis=0)
    return o.reshape(B, S, DIL_OUT).astype(hn.dtype) @ w_o


def moe(h, w_router, w_gu, w_down):
    logits = (h @ w_router).astype(jnp.float32)
    top_val, top_idx = lax.top_k(logits, TOP_K)
    gates = jax.nn.softmax(top_val, axis=-1)
    dense_gates = jnp.sum(jax.nn.one_hot(top_idx, N_EXPERTS, dtype=jnp.float32) * gates[..., None], axis=-2)
    out = jnp.zeros_like(h)
    for e in range(N_EXPERTS):
        out = out + dense_gates[..., e:e + 1].astype(h.dtype) * swiglu(h, w_gu[e], w_down[e])
    return out


def setup_inputs(seed: int = 0) -> dict:
    key = jax.random.key(seed)
    kit = iter(jax.random.split(key, 40))

    def w(shape, fan_in, scale=1.0):
        return (scale * fan_in ** -0.5) * jax.random.normal(next(kit), shape, jnp.float32)

    def gain(shape):
        return 1.0 + 0.05 * jax.random.normal(next(kit), shape, jnp.float32)

    def small(shape, s=0.02):
        return s * jax.random.normal(next(kit), shape, jnp.float32)

    D = D_MODEL
    x = jax.random.normal(next(kit), (BATCH, SEQ, D), jnp.float32)
    c = jax.random.normal(next(kit), (BATCH, D), jnp.float32)
    positions = (jax.random.randint(next(kit), (BATCH, 1), 0, 4096, dtype=jnp.int32)
                 + jnp.arange(SEQ, dtype=jnp.int32)[None, :])
    return {
        "x": x,
        "c": c,
        "positions": positions,
        "g_mix": gain((DEPTH, D)),
        "g_ffn": gain((DEPTH, D)),
        "w_ada": w((DEPTH, D, N_MOD * D), D, 0.5),
        "b_ada": small((DEPTH, N_MOD * D)),
        "w_mla_in": w((N_A, D, MLA_IN), D),
        "g_mla_q": gain((N_A, MLA_Q_RANK)),
        "w_mla_q_up": w((N_A, MLA_Q_RANK, MLA_HEADS * MLA_QK), MLA_Q_RANK),
        "g_mla_kv": gain((N_A, MLA_KV_RANK)),
        "w_mla_kv_up": w((N_A, MLA_KV_RANK, MLA_HEADS * (MLA_NOPE + MLA_V)), MLA_KV_RANK),
        "w_mla_out": w((N_A, MLA_HEADS * MLA_V, D), MLA_HEADS * MLA_V),
        "g_kv_b": gain((D,)),
        "w_ada_kv": w((D, 2 * D), D, 0.5),
        "b_ada_kv": small((2 * D,)),
        "w_kv_b": w((D, DIL_KV), D),
        "rpb_table": small((RPB_BUCKETS, DIL_GROUPS * DIL_HEADS), 0.2),
        "w_q_b": w((N_B, D, DIL_Q), D),
        "w_o_b": w((N_B, DIL_OUT, D), DIL_OUT),
        "w_ffn_gu": w((N_DENSE, D, 2 * D_FF), D),
        "w_ffn_down": w((N_DENSE, D_FF, D), D_FF),
        "w_router": w((N_MOE, D, N_EXPERTS), D),
        "w_exp_gu": w((N_MOE, N_EXPERTS, D, 2 * D_FF), D),
        "w_exp_down": w((N_MOE, N_EXPERTS, D_FF, D), D_FF),
        "g_final": gain((D,)),
    }


def reference(x, c, positions, g_mix, g_ffn, w_ada, b_ada,
              w_mla_in, g_mla_q, w_mla_q_up, g_mla_kv, w_mla_kv_up, w_mla_out,
              g_kv_b, w_ada_kv, b_ada_kv, w_kv_b, rpb_table, w_q_b, w_o_b,
              w_ffn_gu, w_ffn_down, w_router, w_exp_gu, w_exp_down, g_final):
    c_act = jax.nn.silu(c)
    h = x
    kv_b = None
    for layer in range(DEPTH):
        if layer == N_A:
            kv_b = shared_kv(h, c_act, g_kv_b, w_ada_kv, b_ada_kv, w_kv_b)
        mod = c_act @ w_ada[layer] + b_ada[layer]
        sh_m, sc_m, gt_m, sh_f, sc_f, gt_f = jnp.split(mod, N_MOD, axis=-1)
        hn = modulate(rmsnorm(h, g_mix[layer]), sh_m, sc_m)
        if layer < N_A:
            y = mla(hn, positions, w_mla_in[layer], g_mla_q[layer], w_mla_q_up[layer],
                    g_mla_kv[layer], w_mla_kv_up[layer], w_mla_out[layer])
        else:
            j = layer - N_A
            y = dilated_attention(hn, kv_b, w_q_b[j], w_o_b[j], rpb_table)
        h = h + gt_m[:, None, :] * y
        hn = modulate(rmsnorm(h, g_ffn[layer]), sh_f, sc_f)
        if layer % 2 == 0:
            y = swiglu(hn, w_ffn_gu[layer // 2], w_ffn_down[layer // 2])
        else:
            y = moe(hn, w_router[layer // 2], w_exp_gu[layer // 2], w_exp_down[layer // 2])
        h = h + gt_f[:, None, :] * y
    return rmsnorm(h, g_final)
```

```python
import functools
import math

import numpy as np
import jax
import jax.numpy as jnp
from jax import lax
from jax.experimental import pallas as pl
from jax.experimental.pallas import tpu as pltpu

D_MODEL = 1024
N_MOD = 6
EPS = 1e-6

MLA_HEADS = 16
MLA_Q_RANK = 384
MLA_KV_RANK = 256
MLA_NOPE = 64
MLA_ROPE = 32
MLA_V = 64
MLA_QK = MLA_NOPE + MLA_ROPE
ROPE_THETA = 10000.0
HEAD_PAD = 128

DIL_PATTERNS = ((128, 1), (512, 4), (2048, 16))
DIL_GROUPS = len(DIL_PATTERNS)
DIL_HEADS = 8
DIL_HEAD_DIM = 64
DIL_BLOCK = 128
DIL_W = DIL_HEADS * DIL_HEAD_DIM
RPB_BUCKETS = 32
RPB_MAX_DIST = 2048

D_FF = 2816
N_EXPERTS = 8

MASK_VALUE = -1e30
LOG2E = math.log2(math.e)

F32 = jnp.float32
BF16 = jnp.bfloat16

VMEM_LIMIT = 56 * 1024 * 1024


def _cparams(sem):
    return pltpu.CompilerParams(dimension_semantics=sem, vmem_limit_bytes=VMEM_LIMIT)


def _dot(a, b):
    return jnp.dot(a, b, preferred_element_type=F32)


def _dot_nt(a, b):
    return lax.dot_general(a, b, (((1,), (1,)), ((), ())), preferred_element_type=F32)


def _rms_scale(x):
    return lax.rsqrt(jnp.mean(x * x, axis=-1, keepdims=True) + EPS)


def _normmod(x, g, sc, sh):
    return (x * _rms_scale(x)) * (g * (1.0 + sc)) + sh


def _silu(x):
    return x * (1.0 / (1.0 + jnp.exp(-x)))


def _ada_kernel(c_ref, w_ref, b_ref, o_ref):
    c = c_ref[...]
    o_ref[...] = jnp.dot(_silu(c), w_ref[...], preferred_element_type=F32,
                         precision=lax.Precision.HIGHEST) + b_ref[...]


def _ada(c8, w, b, tn=512):
    L, D, N = w.shape
    return pl.pallas_call(
        _ada_kernel,
        grid=(L, N // tn),
        in_specs=[pl.BlockSpec((8, D), lambda l, j: (0, 0)),
                  pl.BlockSpec((None, D, tn), lambda l, j: (l, 0, j)),
                  pl.BlockSpec((None, 1, tn), lambda l, j: (l, 0, j))],
        out_specs=pl.BlockSpec((None, 8, tn), lambda l, j: (l, 0, j)),
        out_shape=jax.ShapeDtypeStruct((L, 8, N), F32),
        compiler_params=_cparams(("parallel", "parallel")),
        name="ada_mod",
    )(c8, w, b)


def _mla_proj_kernel(h_ref, g_ref, sc_ref, sh_ref, cos_ref, sin_ref, w1_ref, gq_ref, gkv_ref,
                     wqa_ref, wqb_ref, wka_ref, wv_ref, q_ref, k_ref, v_ref, *, qscale):
    hn = _normmod(h_ref[...], g_ref[...], sc_ref[...], sh_ref[...]).astype(BF16)
    z = _dot(hn, w1_ref[...])
    ql = z[:, :MLA_Q_RANK]
    kvl = z[:, MLA_Q_RANK:MLA_Q_RANK + MLA_KV_RANK]
    kr = z[:, MLA_Q_RANK + MLA_KV_RANK:MLA_Q_RANK + MLA_KV_RANK + HEAD_PAD]
    krs = z[:, MLA_Q_RANK + MLA_KV_RANK + HEAD_PAD:]
    qn = (ql * _rms_scale(ql) * gq_ref[...]).astype(BF16)
    kvn = (kvl * _rms_scale(kvl) * gkv_ref[...]).astype(BF16)
    cos = cos_ref[...]
    sin = sin_ref[...]
    k_rope = kr * cos + krs * sin
    cos2 = jnp.concatenate([cos, cos], axis=1)
    sin2 = jnp.concatenate([sin, sin], axis=1)
    k_rope2 = jnp.concatenate([k_rope, k_rope], axis=1)
    for hp in range(MLA_HEADS // 2):
        sl = slice(2 * HEAD_PAD * hp, 2 * HEAD_PAD * (hp + 1))
        a = _dot(qn, wqa_ref[:, sl])
        b = _dot(qn, wqb_ref[:, sl])
        q_ref[:, sl] = ((a * cos2 + b * sin2) * qscale).astype(BF16)
        k_ref[:, sl] = (_dot(kvn, wka_ref[:, sl]) + k_rope2).astype(BF16)
    v_ref[...] = _dot(kvn, wv_ref[...]).astype(BF16)


def _mla_proj(h, g, sc, sh, cos_t, sin_t, w1, gq, gkv, wqa, wqb, wka, wv, *, tm, tiles_per_batch):
    T, D = h.shape
    HP = MLA_HEADS * HEAD_PAD
    row = lambda n: pl.BlockSpec((tm, n), lambda i: (i, 0))
    bvec = pl.BlockSpec((None, 1, D), lambda i: (i // tiles_per_batch, 0, 0))
    full = lambda a: pl.BlockSpec(a.shape, lambda i: (0,) * a.ndim)
    return pl.pallas_call(
        functools.partial(_mla_proj_kernel, qscale=(MLA_QK ** -0.5) * LOG2E),
        grid=(T // tm,),
        in_specs=[row(D), full(g), bvec, bvec, row(HEAD_PAD), row(HEAD_PAD), full(w1), full(gq), full(gkv),
                  full(wqa), full(wqb), full(wka), full(wv)],
        out_specs=[row(HP), row(HP), row(MLA_HEADS * MLA_V)],
        out_shape=[jax.ShapeDtypeStruct((T, HP), BF16), jax.ShapeDtypeStruct((T, HP), BF16),
                   jax.ShapeDtypeStruct((T, MLA_HEADS * MLA_V), BF16)],
        compiler_params=_cparams(("parallel",)),
        name="mla_proj",
    )(h, g, sc, sh, cos_t, sin_t, w1, gq, gkv, wqa, wqb, wka, wv)


def _mla_attn_kernel(q_ref, k_ref, v_ref, o_ref, m_sc, l_sc, acc_sc, *, tq):
    qi = pl.program_id(2)
    m_sc[...] = jnp.full(m_sc.shape, -jnp.inf, F32)
    l_sc[...] = jnp.zeros(l_sc.shape, F32)
    acc_sc[...] = jnp.zeros(acc_sc.shape, F32)

    def step(j, diagonal):
        off = pl.multiple_of(j * tq, tq)
        v = v_ref[pl.ds(off, tq), :]
        for hh in range(2):
            q = q_ref[:, HEAD_PAD * hh:HEAD_PAD * (hh + 1)]
            k = k_ref[pl.ds(off, tq), HEAD_PAD * hh:HEAD_PAD * (hh + 1)]
            s = _dot_nt(q, k)
            if diagonal:
                r = lax.broadcasted_iota(jnp.int32, (tq, tq), 0)
                c = lax.broadcasted_iota(jnp.int32, (tq, tq), 1)
                s = jnp.where(r >= c, s, -jnp.inf)
            m_prev = m_sc[hh]
            m_new = jnp.maximum(m_prev, jnp.max(s, axis=1, keepdims=True))
            alpha = jnp.exp2(m_prev - m_new)
            p = jnp.exp2(s - m_new)
            l_sc[hh] = alpha * l_sc[hh] + jnp.sum(p, axis=1, keepdims=True)
            acc_sc[hh] = alpha * acc_sc[hh] + _dot(p.astype(BF16), v)
            m_sc[hh] = m_new

    def body(j, carry):
        step(j, False)
        return carry

    lax.fori_loop(0, qi, body, 0)
    step(qi, True)
    lane = lax.broadcasted_iota(jnp.int32, (1, HEAD_PAD), 1)
    o0 = acc_sc[0] / l_sc[0]
    o1 = acc_sc[1] / l_sc[1]
    o_ref[...] = jnp.where(lane < MLA_V, o0, o1).astype(BF16)


def _mla_attn(q, k, v, *, batch, seq, tq):
    return pl.pallas_call(
        functools.partial(_mla_attn_kernel, tq=tq),
        grid=(batch, MLA_HEADS // 2, seq // tq),
        in_specs=[pl.BlockSpec((None, tq, 2 * HEAD_PAD), lambda b, hp, i: (b, i, hp)),
                  pl.BlockSpec((None, seq, 2 * HEAD_PAD), lambda b, hp, i: (b, 0, hp)),
                  pl.BlockSpec((None, seq, 2 * MLA_V), lambda b, hp, i: (b, 0, hp))],
        out_specs=pl.BlockSpec((None, tq, 2 * MLA_V), lambda b, hp, i: (b, i, hp)),
        out_shape=jax.ShapeDtypeStruct((batch, seq, MLA_HEADS * MLA_V), BF16),
        scratch_shapes=[pltpu.VMEM((2, tq, 1), F32), pltpu.VMEM((2, tq, 1), F32),
                        pltpu.VMEM((2, tq, 2 * MLA_V), F32)],
        compiler_params=_cparams(("parallel", "parallel", "arbitrary")),
        name="mla_attn",
    )(q, k, v)


def _proj_res_kernel(a_ref, w_ref, h_ref, gt_ref, o_ref):
    o_ref[...] = h_ref[...] + gt_ref[...] * _dot(a_ref[...], w_ref[...])


def _proj_res(a, w, h, gt, *, tm, tiles_per_batch):
    T, D = h.shape
    K = a.shape[1]
    return pl.pallas_call(
        _proj_res_kernel,
        grid=(T // tm,),
        in_specs=[pl.BlockSpec((tm, K), lambda i: (i, 0)),
                  pl.BlockSpec((K, D), lambda i: (0, 0)),
                  pl.BlockSpec((tm, D), lambda i: (i, 0)),
                  pl.BlockSpec((None, 1, D), lambda i: (i // tiles_per_batch, 0, 0))],
        out_specs=pl.BlockSpec((tm, D), lambda i: (i, 0)),
        out_shape=jax.ShapeDtypeStruct((T, D), F32),
        compiler_params=_cparams(("parallel",)),
        name="proj_res",
    )(a, w, h, gt)


def _ffn_kernel(h_ref, g_ref, sc_ref, sh_ref, gt_ref, wg_ref, wu_ref, wd_ref, o_ref, hn_sc, acc_sc):
    f = pl.program_id(1)

    @pl.when(f == 0)
    def _():
        hn_sc[...] = _normmod(h_ref[...], g_ref[...], sc_ref[...], sh_ref[...]).astype(BF16)
        acc_sc[...] = jnp.zeros(acc_sc.shape, F32)

    hn = hn_sc[...]
    a = _silu(_dot(hn, wg_ref[...])) * _dot(hn, wu_ref[...])
    acc_sc[...] += _dot(a.astype(BF16), wd_ref[...])

    @pl.when(f == pl.num_programs(1) - 1)
    def _():
        o_ref[...] = h_ref[...] + gt_ref[...] * acc_sc[...]


def _ffn(h, g, sc, sh, gt, w_gu, w_d, *, tm, tf, tiles_per_batch):
    T, D = h.shape
    nf = D_FF // tf
    bvec = pl.BlockSpec((None, 1, D), lambda i, f: (i // tiles_per_batch, 0, 0))
    return pl.pallas_call(
        _ffn_kernel,
        grid=(T // tm, nf),
        in_specs=[pl.BlockSpec((tm, D), lambda i, f: (i, 0)),
                  pl.BlockSpec((1, D), lambda i, f: (0, 0)), bvec, bvec, bvec,
                  pl.BlockSpec((D, tf), lambda i, f: (0, f)),
                  pl.BlockSpec((D, tf), lambda i, f: (0, nf + f)),
                  pl.BlockSpec((tf, D), lambda i, f: (f, 0))],
        out_specs=pl.BlockSpec((tm, D), lambda i, f: (i, 0)),
        out_shape=jax.ShapeDtypeStruct((T, D), F32),
        scratch_shapes=[pltpu.VMEM((tm, D), BF16), pltpu.VMEM((tm, D), F32)],
        compiler_params=_cparams(("parallel", "arbitrary")),
        name="ffn_dense",
    )(h, g, sc, sh, gt, w_gu, w_gu, w_d)


def _l1_proj_kernel(h_ref, gkv_ref, sckv_ref, shkv_ref, gq_ref, scq_ref, shq_ref, wkv_ref, wq_ref,
                    kv_ref, q_ref):
    x = h_ref[...]
    xh = x * _rms_scale(x)
    hn_kv = (xh * (gkv_ref[...] * (1.0 + sckv_ref[...])) + shkv_ref[...]).astype(BF16)
    hn_q = (xh * (gq_ref[...] * (1.0 + scq_ref[...])) + shq_ref[...]).astype(BF16)
    kv_ref[...] = _dot(hn_kv, wkv_ref[...]).astype(BF16)
    q_ref[...] = (_dot(hn_q, wq_ref[...]) * (DIL_HEAD_DIM ** -0.5)).astype(BF16)


def _l1_proj(h, gkv, sckv, shkv, gq, scq, shq, wkv, wq, *, tm, tiles_per_batch):
    T, D = h.shape
    row = lambda n: pl.BlockSpec((tm, n), lambda i: (i, 0))
    bvec = pl.BlockSpec((None, 1, D), lambda i: (i // tiles_per_batch, 0, 0))
    full = lambda a: pl.BlockSpec(a.shape, lambda i: (0,) * a.ndim)
    return pl.pallas_call(
        _l1_proj_kernel,
        grid=(T // tm,),
        in_specs=[row(D), full(gkv), bvec, bvec, full(gq), bvec, bvec, full(wkv), full(wq)],
        out_specs=[row(wkv.shape[1]), row(wq.shape[1])],
        out_shape=[jax.ShapeDtypeStruct((T, wkv.shape[1]), BF16), jax.ShapeDtypeStruct((T, wq.shape[1]), BF16)],
        compiler_params=_cparams(("parallel",)),
        name="l1_proj",
    )(h, gkv, sckv, shkv, gq, scq, shq, wkv, wq)


def _dil_attn_kernel(q_ref, kc_ref, kp_ref, vc_ref, vp_ref, bias_ref, o_ref, lse_ref):
    i = pl.program_id(2)
    lane = lax.broadcasted_iota(jnp.int32, (1, 2 * DIL_HEAD_DIM), 1)
    lo = lane < DIL_HEAD_DIM
    col = lax.broadcasted_iota(jnp.int32, (1, 2 * DIL_BLOCK), 1)
    edge = jnp.where((col < DIL_BLOCK) & (i == 0), MASK_VALUE, 0.0).astype(F32)
    for hp in range(DIL_HEADS // 2):
        sl = slice(2 * DIL_HEAD_DIM * hp, 2 * DIL_HEAD_DIM * (hp + 1))
        q2 = q_ref[:, sl]
        k2 = jnp.concatenate([kp_ref[:, sl], kc_ref[:, sl]], axis=0)
        v2 = jnp.concatenate([vp_ref[:, sl], vc_ref[:, sl]], axis=0)
        outs, lses = [], []
        for hh in range(2):
            keep = lo if hh == 0 else jnp.logical_not(lo)
            qm = jnp.where(keep, q2, jnp.zeros_like(q2))
            s = _dot_nt(qm, k2) + bias_ref[2 * hp + hh] + edge
            m = jnp.max(s, axis=1, keepdims=True)
            p = jnp.exp(s - m)
            l = jnp.sum(p, axis=1, keepdims=True)
            outs.append(_dot(p.astype(BF16), v2) / l)
            lses.append(m + jnp.log(l))
        o_ref[:, sl] = jnp.where(lo, outs[0], outs[1])
        lse_ref[:, sl] = jnp.where(lo, lses[0], lses[1])


def _dil_attn(q, kv, bias, *, group, dil, batch, seq):
    n = seq // dil
    nb = n // DIL_BLOCK
    qv = q.reshape(batch, n, dil * DIL_GROUPS * DIL_W)
    kvv = kv.reshape(batch, n, dil * 2 * DIL_GROUPS * DIL_W)
    blk = (None, DIL_BLOCK, DIL_W)
    kcol = 2 * group
    out_sd = jax.ShapeDtypeStruct((batch, n, dil * DIL_W), F32)
    o, lse = pl.pallas_call(
        _dil_attn_kernel,
        grid=(batch, dil, nb),
        in_specs=[pl.BlockSpec(blk, lambda b, r, i: (b, i, r * DIL_GROUPS + group)),
                  pl.BlockSpec(blk, lambda b, r, i: (b, i, r * 2 * DIL_GROUPS + kcol)),
                  pl.BlockSpec(blk, lambda b, r, i: (b, jnp.maximum(i - 1, 0), r * 2 * DIL_GROUPS + kcol)),
                  pl.BlockSpec(blk, lambda b, r, i: (b, i, r * 2 * DIL_GROUPS + kcol + 1)),
                  pl.BlockSpec(blk, lambda b, r, i: (b, jnp.maximum(i - 1, 0), r * 2 * DIL_GROUPS + kcol + 1)),
                  pl.BlockSpec(bias.shape, lambda b, r, i: (0, 0, 0))],
        out_specs=[pl.BlockSpec(blk, lambda b, r, i: (b, i, r)),
                   pl.BlockSpec(blk, lambda b, r, i: (b, i, r))],
        out_shape=[out_sd, out_sd],
        compiler_params=_cparams(("parallel", "parallel", "arbitrary")),
        name=f"dil_attn_g{group}",
    )(qv, kvv, kvv, kvv, kvv, bias)
    return o.reshape(batch * seq, DIL_W), lse.reshape(batch * seq, DIL_W)


def _rpb_bucket(dist):
    exact = RPB_BUCKETS // 2
    d = jnp.maximum(dist, 0)
    d_f = jnp.maximum(d, 1).astype(F32)
    large = exact + (jnp.log(d_f / exact) / math.log(RPB_MAX_DIST / exact)
                     * (RPB_BUCKETS - exact)).astype(jnp.int32)
    return jnp.where(d < exact, d, jnp.minimum(large, RPB_BUCKETS - 1))


def _dil_bias(table, span, dil):
    dist = jnp.arange(DIL_BLOCK)[:, None] + DIL_BLOCK - jnp.arange(2 * DIL_BLOCK)[None, :]
    band = (dist >= 0) & (dist <= span)
    bias = jnp.transpose(table[_rpb_bucket(dist * dil)], (2, 0, 1)).astype(F32)
    return jnp.where(band[None], bias, MASK_VALUE)


def _dil_out_kernel(o0_ref, o1_ref, o2_ref, l0_ref, l1_ref, l2_ref, w_ref, h_ref, gt_ref, out_ref):
    l0, l1, l2 = l0_ref[...], l1_ref[...], l2_ref[...]
    m = jnp.maximum(jnp.maximum(l0, l1), l2)
    w0, w1, w2 = jnp.exp(l0 - m), jnp.exp(l1 - m), jnp.exp(l2 - m)
    o = (o0_ref[...] * w0 + o1_ref[...] * w1 + o2_ref[...] * w2) / (w0 + w1 + w2)
    out_ref[...] = h_ref[...] + gt_ref[...] * _dot(o.astype(BF16), w_ref[...])


def _dil_out(os_, ls_, w, h, gt, *, tm, tiles_per_batch):
    T, D = h.shape
    row = lambda n: pl.BlockSpec((tm, n), lambda i: (i, 0))
    return pl.pallas_call(
        _dil_out_kernel,
        grid=(T // tm,),
        in_specs=[row(DIL_W)] * 6 + [pl.BlockSpec(w.shape, lambda i: (0, 0)), row(D),
                                     pl.BlockSpec((None, 1, D), lambda i: (i // tiles_per_batch, 0, 0))],
        out_specs=row(D),
        out_shape=jax.ShapeDtypeStruct((T, D), F32),
        compiler_params=_cparams(("parallel",)),
        name="dil_out",
    )(*os_, *ls_, w, h, gt)


def _route_kernel(h_ref, g_ref, sc_ref, sh_ref, wr_ref, hn_ref, gates_ref):
    hn = _normmod(h_ref[...], g_ref[...], sc_ref[...], sh_ref[...])
    hn_ref[...] = hn.astype(BF16)
    logits = jnp.dot(hn, wr_ref[...], preferred_element_type=F32, precision=lax.Precision.HIGHEST)
    idx = lax.broadcasted_iota(jnp.int32, logits.shape, 1)
    v1 = jnp.max(logits, axis=1, keepdims=True)
    i1 = jnp.min(jnp.where(logits == v1, idx, N_EXPERTS), axis=1, keepdims=True)
    rest = jnp.where(idx == i1, -jnp.inf, logits)
    v2 = jnp.max(rest, axis=1, keepdims=True)
    i2 = jnp.min(jnp.where(rest == v2, idx, N_EXPERTS), axis=1, keepdims=True)
    e = jnp.exp(v2 - v1)
    g1 = 1.0 / (1.0 + e)
    g2 = e / (1.0 + e)
    gates_ref[...] = jnp.where(idx == i1, g1, 0.0) + jnp.where(idx == i2, g2, 0.0)


def _route(h, g, sc, sh, wr, *, tm, tiles_per_batch):
    T, D = h.shape
    bvec = pl.BlockSpec((None, 1, D), lambda i: (i // tiles_per_batch, 0, 0))
    return pl.pallas_call(
        _route_kernel,
        grid=(T // tm,),
        in_specs=[pl.BlockSpec((tm, D), lambda i: (i, 0)), pl.BlockSpec((1, D), lambda i: (0, 0)), bvec, bvec,
                  pl.BlockSpec(wr.shape, lambda i: (0, 0))],
        out_specs=[pl.BlockSpec((tm, D), lambda i: (i, 0)), pl.BlockSpec((tm, N_EXPERTS), lambda i: (i, 0))],
        out_shape=[jax.ShapeDtypeStruct((T, D), BF16), jax.ShapeDtypeStruct((T, N_EXPERTS), F32)],
        compiler_params=_cparams(("parallel",)),
        name="moe_route",
    )(h, g, sc, sh, wr)


def _moe_kernel(hn_ref, gates_ref, h_ref, gt_ref, gfin_ref, wg_ref, wu_ref, wd_ref, o_ref, acc_sc):
    e = pl.program_id(1)
    f = pl.program_id(2)

    @pl.when((e == 0) & (f == 0))
    def _():
        acc_sc[...] = jnp.zeros(acc_sc.shape, F32)

    hn = hn_ref[...]
    gates = gates_ref[...]
    idx = lax.broadcasted_iota(jnp.int32, gates.shape, 1)
    gate = jnp.sum(jnp.where(idx == e, gates, 0.0), axis=1, keepdims=True)
    a = _silu(_dot(hn, wg_ref[...])) * _dot(hn, wu_ref[...]) * gate
    acc_sc[...] += _dot(a.astype(BF16), wd_ref[...])

    @pl.when((e == pl.num_programs(1) - 1) & (f == pl.num_programs(2) - 1))
    def _():
        y = h_ref[...] + gt_ref[...] * acc_sc[...]
        o_ref[...] = y * _rms_scale(y) * gfin_ref[...]


def _moe(hn, gates, h, gt, gfin, w_gu, w_d, *, tm, tf, tiles_per_batch):
    T, D = h.shape
    nf = D_FF // tf
    return pl.pallas_call(
        _moe_kernel,
        grid=(T // tm, N_EXPERTS, nf),
        in_specs=[pl.BlockSpec((tm, D), lambda i, e, f: (i, 0)),
                  pl.BlockSpec((tm, N_EXPERTS), lambda i, e, f: (i, 0)),
                  pl.BlockSpec((tm, D), lambda i, e, f: (i, 0)),
                  pl.BlockSpec((None, 1, D), lambda i, e, f: (i // tiles_per_batch, 0, 0)),
                  pl.BlockSpec((1, D), lambda i, e, f: (0, 0)),
                  pl.BlockSpec((None, D, tf), lambda i, e, f: (e, 0, f)),
                  pl.BlockSpec((None, D, tf), lambda i, e, f: (e, 0, nf + f)),
                  pl.BlockSpec((None, tf, D), lambda i, e, f: (e, f, 0))],
        out_specs=pl.BlockSpec((tm, D), lambda i, e, f: (i, 0)),
        out_shape=jax.ShapeDtypeStruct((T, D), F32),
        scratch_shapes=[pltpu.VMEM((tm, D), F32)],
        compiler_params=_cparams(("parallel", "arbitrary", "arbitrary")),
        name="moe_experts",
    )(hn, gates, h, gt, gfin, w_gu, w_gu, w_d)


def _mla_weights(w_in, w_q_up, w_kv_up):
    D = w_in.shape[0]
    half = MLA_ROPE // 2
    w_ql = w_in[:, :MLA_Q_RANK]
    w_kvl = w_in[:, MLA_Q_RANK:MLA_Q_RANK + MLA_KV_RANK]
    w_kr = w_in[:, MLA_Q_RANK + MLA_KV_RANK:]
    z64 = jnp.zeros((D, MLA_NOPE), F32)
    z32 = jnp.zeros((D, HEAD_PAD - MLA_QK), F32)
    kr_pad = jnp.concatenate([z64, w_kr, z32], axis=1)
    kr_swap = jnp.concatenate([z64, w_kr[:, half:], w_kr[:, :half], z32], axis=1)
    w1 = jnp.concatenate([w_ql, w_kvl, kr_pad, kr_swap], axis=1).astype(BF16)

    wq = w_q_up.reshape(MLA_Q_RANK, MLA_HEADS, MLA_QK)
    nope, rope = wq[..., :MLA_NOPE], wq[..., MLA_NOPE:]
    zq = jnp.zeros((MLA_Q_RANK, MLA_HEADS, HEAD_PAD - MLA_QK), F32)
    wqa = jnp.concatenate([nope, rope, zq], axis=-1).reshape(MLA_Q_RANK, MLA_HEADS * HEAD_PAD).astype(BF16)
    wqb = jnp.concatenate([jnp.zeros_like(nope), rope[..., half:], rope[..., :half], zq], axis=-1)
    wqb = wqb.reshape(MLA_Q_RANK, MLA_HEADS * HEAD_PAD).astype(BF16)

    wkv = w_kv_up.reshape(MLA_KV_RANK, MLA_HEADS, MLA_NOPE + MLA_V)
    k_nope, v = wkv[..., :MLA_NOPE], wkv[..., MLA_NOPE:]
    zk = jnp.zeros((MLA_KV_RANK, MLA_HEADS, HEAD_PAD - MLA_NOPE), F32)
    wka = jnp.concatenate([k_nope, zk], axis=-1).reshape(MLA_KV_RANK, MLA_HEADS * HEAD_PAD).astype(BF16)
    wv = v.reshape(MLA_KV_RANK, MLA_HEADS * MLA_V).astype(BF16)
    return w1, wqa, wqb, wka, wv


def _rope_tables(positions):
    inv_freq = ROPE_THETA ** (-jnp.arange(0, MLA_ROPE, 2, dtype=F32) / MLA_ROPE)
    ang = positions.astype(F32).reshape(-1, 1) * inv_freq
    cos, sin = jnp.cos(ang), jnp.sin(ang)
    T = ang.shape[0]
    one = jnp.ones((T, MLA_NOPE), F32)
    z64 = jnp.zeros((T, MLA_NOPE), F32)
    z32 = jnp.zeros((T, HEAD_PAD - MLA_QK), F32)
    return (jnp.concatenate([one, cos, cos, z32], axis=1),
            jnp.concatenate([z64, -sin, sin, z32], axis=1))


def kernel(x, c, positions, g_mix, g_ffn, w_ada, b_ada, w_mla_in, g_mla_q, w_mla_q_up, g_mla_kv, w_mla_kv_up,
           w_mla_out, g_kv_b, w_ada_kv, b_ada_kv, w_kv_b, rpb_table, w_q_b, w_o_b, w_ffn_gu, w_ffn_down,
           w_router, w_exp_gu, w_exp_down, g_final):
    B, S, D = x.shape
    T = B * S
    h = x.reshape(T, D)

    c8 = jnp.zeros((8, D), F32).at[:B].set(c)
    mod = _ada(c8, w_ada, b_ada[:, None, :])[:, :B]
    mod = mod.reshape(2, B, N_MOD, 1, D)
    mod_kv = _ada(c8, w_ada_kv[None], b_ada_kv[None, None, :])[0, :B].reshape(B, 2, 1, D)
    sh_kv, sc_kv = mod_kv[:, 0], mod_kv[:, 1]

    def mods(layer):
        return [mod[layer, :, k] for k in range(N_MOD)]

    row = lambda v: v.reshape(1, -1)

    sh_m, sc_m, gt_m, sh_f, sc_f, gt_f = mods(0)
    w1, wqa, wqb, wka, wv = _mla_weights(w_mla_in[0], w_mla_q_up[0], w_mla_kv_up[0])
    cos_t, sin_t = _rope_tables(positions)
    tm = 256
    q, k, v = _mla_proj(h, row(g_mix[0]), sc_m, sh_m, cos_t, sin_t, w1, row(g_mla_q[0]), row(g_mla_kv[0]),
                        wqa, wqb, wka, wv, tm=tm, tiles_per_batch=S // tm)
    o = _mla_attn(q.reshape(B, S, -1), k.reshape(B, S, -1), v.reshape(B, S, -1), batch=B, seq=S, tq=512)
    tm = 512
    h = _proj_res(o.reshape(T, -1), w_mla_out[0].astype(BF16), h, gt_m, tm=tm, tiles_per_batch=S // tm)

    tm = 1024
    h = _ffn(h, row(g_ffn[0]), sc_f, sh_f, gt_f, w_ffn_gu[0].astype(BF16), w_ffn_down[0].astype(BF16),
             tm=tm, tf=256, tiles_per_batch=S // tm)

    sh_m, sc_m, gt_m, sh_f, sc_f, gt_f = mods(1)
    tm = 512
    kv, q = _l1_proj(h, row(g_kv_b), sc_kv, sh_kv, row(g_mix[1]), sc_m, sh_m,
                     w_kv_b.astype(BF16), w_q_b[0].astype(BF16), tm=tm, tiles_per_batch=S // tm)

    outs, lses = [], []
    for g, (window, dil) in enumerate(DIL_PATTERNS):
        bias = _dil_bias(rpb_table[:, g * DIL_HEADS:(g + 1) * DIL_HEADS], window // dil, dil)
        o_g, lse_g = _dil_attn(q, kv, bias, group=g, dil=dil, batch=B, seq=S)
        outs.append(o_g)
        lses.append(lse_g)
    h = _dil_out(outs, lses, w_o_b[0].astype(BF16), h, gt_m, tm=tm, tiles_per_batch=S // tm)

    hn, gates = _route(h, row(g_ffn[1]), sc_f, sh_f, w_router[0], tm=tm, tiles_per_batch=S // tm)
    tm = 1024
    out = _moe(hn, gates, h, gt_f, row(g_final), w_exp_gu[0].astype(BF16), w_exp_down[0].astype(BF16),
               tm=tm, tf=256, tiles_per_batch=S // tm)
    return out.reshape(B, S, D)
```

```python
import functools
import math

import numpy as np
import jax
import jax.numpy as jnp
from jax import lax
from jax.experimental import pallas as pl
from jax.experimental.pallas import tpu as pltpu

D_MODEL = 1024
N_MOD = 6
EPS = 1e-6

MLA_HEADS = 16
MLA_Q_RANK = 384
MLA_KV_RANK = 256
MLA_NOPE = 64
MLA_ROPE = 32
MLA_V = 64
MLA_QK = MLA_NOPE + MLA_ROPE
ROPE_THETA = 10000.0
HEAD_PAD = 128

DIL_PATTERNS = ((128, 1), (512, 4), (2048, 16))
DIL_GROUPS = len(DIL_PATTERNS)
DIL_HEADS = 8
DIL_HEAD_DIM = 64
DIL_BLOCK = 128
DIL_W = DIL_HEADS * DIL_HEAD_DIM
RPB_BUCKETS = 32
RPB_MAX_DIST = 2048

D_FF = 2816
N_EXPERTS = 8

MASK_VALUE = -1e30
LOG2E = math.log2(math.e)

F32 = jnp.float32
BF16 = jnp.bfloat16

VMEM_LIMIT = 56 * 1024 * 1024


def _cparams(sem):
    return pltpu.CompilerParams(dimension_semantics=sem, vmem_limit_bytes=VMEM_LIMIT)


def _dot(a, b):
    return jnp.dot(a, b, preferred_element_type=F32)


def _dot_nt(a, b):
    return lax.dot_general(a, b, (((1,), (1,)), ((), ())), preferred_element_type=F32)


def _rms_scale(x):
    return lax.rsqrt(jnp.mean(x * x, axis=-1, keepdims=True) + EPS)


def _normmod(x, g, sc, sh):
    return (x * _rms_scale(x)) * (g * (1.0 + sc)) + sh


def _silu(x):
    return x * (1.0 / (1.0 + jnp.exp(-x)))


def _ada_kernel(c_ref, w_ref, b_ref, o_ref):
    c = c_ref[...]
    o_ref[...] = jnp.dot(_silu(c), w_ref[...], preferred_element_type=F32,
                         precision=lax.Precision.HIGHEST) + b_ref[...]


def _ada(c8, w, b, tn=512):
    L, D, N = w.shape
    return pl.pallas_call(
        _ada_kernel,
        grid=(L, N // tn),
        in_specs=[pl.BlockSpec((8, D), lambda l, j: (0, 0)),
                  pl.BlockSpec((None, D, tn), lambda l, j: (l, 0, j)),
                  pl.BlockSpec((None, 1, tn), lambda l, j: (l, 0, j))],
        out_specs=pl.BlockSpec((None, 8, tn), lambda l, j: (l, 0, j)),
        out_shape=jax.ShapeDtypeStruct((L, 8, N), F32),
        compiler_params=_cparams(("parallel", "parallel")),
        name="ada_mod",
    )(c8, w, b)


def _mla_proj_kernel(h_ref, g_ref, sc_ref, sh_ref, cos_ref, sin_ref, w1_ref, gq_ref, gkv_ref,
                     wqa_ref, wqb_ref, wka_ref, wv_ref, q_ref, k_ref, v_ref, *, qscale):
    hn = _normmod(h_ref[...], g_ref[...], sc_ref[...], sh_ref[...]).astype(BF16)
    z = _dot(hn, w1_ref[...])
    ql = z[:, :MLA_Q_RANK]
    kvl = z[:, MLA_Q_RANK:MLA_Q_RANK + MLA_KV_RANK]
    kr = z[:, MLA_Q_RANK + MLA_KV_RANK:MLA_Q_RANK + MLA_KV_RANK + HEAD_PAD]
    krs = z[:, MLA_Q_RANK + MLA_KV_RANK + HEAD_PAD:]
    qn = (ql * _rms_scale(ql) * gq_ref[...]).astype(BF16)
    kvn = (kvl * _rms_scale(kvl) * gkv_ref[...]).astype(BF16)
    cos = cos_ref[...]
    sin = sin_ref[...]
    k_rope = kr * cos + krs * sin
    cos2 = jnp.concatenate([cos, cos], axis=1)
    sin2 = jnp.concatenate([sin, sin], axis=1)
    k_rope2 = jnp.concatenate([k_rope, k_rope], axis=1)
    for hp in range(MLA_HEADS // 2):
        sl = slice(2 * HEAD_PAD * hp, 2 * HEAD_PAD * (hp + 1))
        a = _dot(qn, wqa_ref[:, sl])
        b = _dot(qn, wqb_ref[:, sl])
        q_ref[:, sl] = ((a * cos2 + b * sin2) * qscale).astype(BF16)
        k_ref[:, sl] = (_dot(kvn, wka_ref[:, sl]) + k_rope2).astype(BF16)
    v_ref[...] = _dot(kvn, wv_ref[...]).astype(BF16)


def _mla_proj(h, g, sc, sh, cos_t, sin_t, w1, gq, gkv, wqa, wqb, wka, wv, *, tm, tiles_per_batch):
    T, D = h.shape
    HP = MLA_HEADS * HEAD_PAD
    row = lambda n: pl.BlockSpec((tm, n), lambda i: (i, 0))
    bvec = pl.BlockSpec((None, 1, D), lambda i: (i // tiles_per_batch, 0, 0))
    full = lambda a: pl.BlockSpec(a.shape, lambda i: (0,) * a.ndim)
    return pl.pallas_call(
        functools.partial(_mla_proj_kernel, qscale=(MLA_QK ** -0.5) * LOG2E),
        grid=(T // tm,),
        in_specs=[row(D), full(g), bvec, bvec, row(HEAD_PAD), row(HEAD_PAD), full(w1), full(gq), full(gkv),
                  full(wqa), full(wqb), full(wka), full(wv)],
        out_specs=[row(HP), row(HP), row(MLA_HEADS * MLA_V)],
        out_shape=[jax.ShapeDtypeStruct((T, HP), BF16), jax.ShapeDtypeStruct((T, HP), BF16),
                   jax.ShapeDtypeStruct((T, MLA_HEADS * MLA_V), BF16)],
        compiler_params=_cparams(("parallel",)),
        name="mla_proj",
    )(h, g, sc, sh, cos_t, sin_t, w1, gq, gkv, wqa, wqb, wka, wv)


def _mla_attn_kernel(q_ref, k_ref, v_ref, o_ref, m_sc, acc_sc, p_sc, alpha_sc, *, tq):
    qi = pl.program_id(2)
    m_sc[...] = jnp.full(m_sc.shape, -jnp.inf, F32)
    acc_sc[...] = jnp.zeros(acc_sc.shape, F32)
    nrep = tq // HEAD_PAD

    def scores(j, diagonal):
        off = pl.multiple_of(j * tq, tq)
        ss = []
        for hh in range(2):
            q = q_ref[:, HEAD_PAD * hh:HEAD_PAD * (hh + 1)]
            k = k_ref[pl.ds(off, tq), HEAD_PAD * hh:HEAD_PAD * (hh + 1)]
            s = _dot_nt(q, k)
            if diagonal:
                r = lax.broadcasted_iota(jnp.int32, (tq, tq), 0)
                c = lax.broadcasted_iota(jnp.int32, (tq, tq), 1)
                s = jnp.where(r >= c, s, -jnp.inf)
            ss.append(s)
        return ss

    def softmax(ss):
        for hh in range(2):
            m_prev = m_sc[hh]
            m_new = jnp.maximum(m_prev, jnp.max(ss[hh], axis=1, keepdims=True))
            alpha_sc[hh] = jnp.exp2(m_prev - m_new)
            p_sc[hh] = jnp.exp2(ss[hh] - jnp.tile(m_new, (1, nrep))).astype(BF16)
            m_sc[hh] = m_new

    def apply_values(j):
        off = pl.multiple_of(j * tq, tq)
        v = v_ref[pl.ds(off, tq), :]
        v1 = jnp.concatenate([v, jnp.ones_like(v)], axis=1)
        for hh in range(2):
            acc_sc[hh] = jnp.tile(alpha_sc[hh], (1, 2)) * acc_sc[hh] + _dot(p_sc[hh], v1)

    @pl.when(qi == 0)
    def _():
        softmax(scores(0, True))

    @pl.when(qi > 0)
    def _():
        softmax(scores(0, False))

        def body(j, carry):
            ss = scores(j, False)
            apply_values(j - 1)
            softmax(ss)
            return carry

        lax.fori_loop(1, qi, body, 0)
        ss = scores(qi, True)
        apply_values(qi - 1)
        softmax(ss)

    apply_values(qi)
    lane =lax.broadcasted_iota(jnp.int32, (1, HEAD_PAD), 1)
    o0 = acc_sc[0, :, :HEAD_PAD] / acc_sc[0, :, HEAD_PAD:]
    o1 = acc_sc[1, :, :HEAD_PAD] / acc_sc[1, :, HEAD_PAD:]
    o_ref[...] = jnp.where(lane < MLA_V, o0, o1).astype(BF16)


def _mla_attn(q, k, v, *, batch, seq, tq):
    return pl.pallas_call(
        functools.partial(_mla_attn_kernel, tq=tq),
        grid=(batch, MLA_HEADS // 2, seq // tq),
        in_specs=[pl.BlockSpec((None, tq, 2 * HEAD_PAD), lambda b, hp, i: (b, i, hp)),
                  pl.BlockSpec((None, seq, 2 * HEAD_PAD), lambda b, hp, i: (b, 0, hp)),
                  pl.BlockSpec((None, seq, 2 * MLA_V), lambda b, hp, i: (b, 0, hp))],
        out_specs=pl.BlockSpec((None, tq, 2 * MLA_V), lambda b, hp, i: (b, i, hp)),
        out_shape=jax.ShapeDtypeStruct((batch, seq, MLA_HEADS * MLA_V), BF16),
        scratch_shapes=[pltpu.VMEM((2, tq, HEAD_PAD), F32), pltpu.VMEM((2, tq, 2 * HEAD_PAD), F32),
                        pltpu.VMEM((2, tq, tq), BF16), pltpu.VMEM((2, tq, HEAD_PAD), F32)],
        compiler_params=_cparams(("parallel", "parallel", "arbitrary")),
        name="mla_attn",
    )(q, k, v)


def _proj_res_kernel(a_ref, w_ref, h_ref, gt_ref, o_ref):
    o_ref[...] = h_ref[...] + gt_ref[...] * _dot(a_ref[...], w_ref[...])


def _proj_res(a, w, h, gt, *, tm, tiles_per_batch):
    T, D = h.shape
    K = a.shape[1]
    return pl.pallas_call(
        _proj_res_kernel,
        grid=(T // tm,),
        in_specs=[pl.BlockSpec((tm, K), lambda i: (i, 0)),
                  pl.BlockSpec((K, D), lambda i: (0, 0)),
                  pl.BlockSpec((tm, D), lambda i: (i, 0)),
                  pl.BlockSpec((None, 1, D), lambda i: (i // tiles_per_batch, 0, 0))],
        out_specs=pl.BlockSpec((tm, D), lambda i: (i, 0)),
        out_shape=jax.ShapeDtypeStruct((T, D), F32),
        compiler_params=_cparams(("parallel",)),
        name="proj_res",
    )(a, w, h, gt)


def _ffn_kernel(h_ref, g_ref, sc_ref, sh_ref, gt_ref, wg_ref, wu_ref, wd_ref, o_ref, hn_sc, acc_sc):
    f = pl.program_id(1)

    @pl.when(f == 0)
    def _():
        hn_sc[...] = _normmod(h_ref[...], g_ref[...], sc_ref[...], sh_ref[...]).astype(BF16)
        acc_sc[...] = jnp.zeros(acc_sc.shape, F32)

    hn = hn_sc[...]
    a = _silu(_dot(hn, wg_ref[...])) * _dot(hn, wu_ref[...])
    acc_sc[...] += _dot(a.astype(BF16), wd_ref[...])

    @pl.when(f == pl.num_programs(1) - 1)
    def _():
        o_ref[...] = h_ref[...] + gt_ref[...] * acc_sc[...]


def _ffn(h, g, sc, sh, gt, w_gu, w_d, *, tm, tf, tiles_per_batch):
    T, D = h.shape
    nf = D_FF // tf
    bvec = pl.BlockSpec((None, 1, D), lambda i, f: (i // tiles_per_batch, 0, 0))
    return pl.pallas_call(
        _ffn_kernel,
        grid=(T // tm, nf),
        in_specs=[pl.BlockSpec((tm, D), lambda i, f: (i, 0)),
                  pl.BlockSpec((1, D), lambda i, f: (0, 0)), bvec, bvec, bvec,
                  pl.BlockSpec((D, tf), lambda i, f: (0, f)),
                  pl.BlockSpec((D, tf), lambda i, f: (0, nf + f)),
                  pl.BlockSpec((tf, D), lambda i, f: (f, 0))],
        out_specs=pl.BlockSpec((tm, D), lambda i, f: (i, 0)),
        out_shape=jax.ShapeDtypeStruct((T, D), F32),
        scratch_shapes=[pltpu.VMEM((tm, D), BF16), pltpu.VMEM((tm, D), F32)],
        compiler_params=_cparams(("parallel", "arbitrary")),
        name="ffn_dense",
    )(h, g, sc, sh, gt, w_gu, w_gu, w_d)


def _l1_proj_kernel(h_ref, gkv_ref, sckv_ref, shkv_ref, gq_ref, scq_ref, shq_ref, wkv_ref, wq_ref,
                    kv_ref, q_ref):
    x = h_ref[...]
    xh = x * _rms_scale(x)
    hn_kv = (xh * (gkv_ref[...] * (1.0 + sckv_ref[...])) + shkv_ref[...]).astype(BF16)
    hn_q = (xh * (gq_ref[...] * (1.0 + scq_ref[...])) + shq_ref[...]).astype(BF16)
    kv_ref[...] = _dot(hn_kv, wkv_ref[...]).astype(BF16)
    q_ref[...] = (_dot(hn_q, wq_ref[...]) * (DIL_HEAD_DIM ** -0.5)).astype(BF16)


def _l1_proj(h, gkv, sckv, shkv, gq, scq, shq, wkv, wq, *, tm, tiles_per_batch):
    T, D = h.shape
    row = lambda n: pl.BlockSpec((tm, n), lambda i: (i, 0))
    bvec = pl.BlockSpec((None, 1, D), lambda i: (i // tiles_per_batch, 0, 0))
    full = lambda a: pl.BlockSpec(a.shape, lambda i: (0,) * a.ndim)
    return pl.pallas_call(
        _l1_proj_kernel,
        grid=(T // tm,),
        in_specs=[row(D), full(gkv), bvec, bvec, full(gq), bvec, bvec, full(wkv), full(wq)],
        out_specs=[row(wkv.shape[1]), row(wq.shape[1])],
        out_shape=[jax.ShapeDtypeStruct((T, wkv.shape[1]), BF16), jax.ShapeDtypeStruct((T, wq.shape[1]), BF16)],
        compiler_params=_cparams(("parallel",)),
        name="l1_proj",
    )(h, gkv, sckv, shkv, gq, scq, shq, wkv, wq)


def _dil_attn_kernel(q_ref, kc_ref, kp_ref, vc_ref, vp_ref, bias_ref, o_ref, lse_ref):
    i = pl.program_id(2)
    lane = lax.broadcasted_iota(jnp.int32, (1, 2 * DIL_HEAD_DIM), 1)
    lo = lane < DIL_HEAD_DIM
    col = lax.broadcasted_iota(jnp.int32, (1, 2 * DIL_BLOCK), 1)
    edge = jnp.where((col < DIL_BLOCK) & (i == 0), MASK_VALUE, 0.0).astype(F32)
    for hp in range(DIL_HEADS // 2):
        sl = slice(2 * DIL_HEAD_DIM * hp, 2 * DIL_HEAD_DIM * (hp + 1))
        q2 = q_ref[:, sl]
        k2 = jnp.concatenate([kp_ref[:, sl], kc_ref[:, sl]], axis=0)
        v2 = jnp.concatenate([vp_ref[:, sl], vc_ref[:, sl]], axis=0)
        outs, lses = [], []
        for hh in range(2):
            keep = lo if hh == 0 else jnp.logical_not(lo)
            qm = jnp.where(keep, q2, jnp.zeros_like(q2))
            s = _dot_nt(qm, k2) + bias_ref[2 * hp + hh] + edge
            m = jnp.max(s, axis=1, keepdims=True)
            p = jnp.exp(s - m)
            l = jnp.sum(p, axis=1, keepdims=True)
            outs.append(_dot(p.astype(BF16), v2) / l)
            lses.append(m + jnp.log(l))
        o_ref[:, sl] = jnp.where(lo, outs[0], outs[1])
        lse_ref[:, sl] = jnp.where(lo, lses[0], lses[1])


def _dil_attn(q, kv, bias, *, group, dil, batch, seq):
    n = seq // dil
    nb = n // DIL_BLOCK
    qv = q.reshape(batch, n, dil * DIL_GROUPS * DIL_W)
    kvv = kv.reshape(batch, n, dil * 2 * DIL_GROUPS * DIL_W)
    blk = (None, DIL_BLOCK, DIL_W)
    kcol = 2 * group
    out_sd = jax.ShapeDtypeStruct((batch, n, dil * DIL_W), F32)
    o, lse = pl.pallas_call(
        _dil_attn_kernel,
        grid=(batch, dil, nb),
        in_specs=[pl.BlockSpec(blk, lambda b, r, i: (b, i, r * DIL_GROUPS + group)),
                  pl.BlockSpec(blk, lambda b, r, i: (b, i, r * 2 * DIL_GROUPS + kcol)),
                  pl.BlockSpec(blk, lambda b, r, i: (b, jnp.maximum(i - 1, 0), r * 2 * DIL_GROUPS + kcol)),
                  pl.BlockSpec(blk, lambda b, r, i: (b, i, r * 2 * DIL_GROUPS + kcol + 1)),
                  pl.BlockSpec(blk, lambda b, r, i: (b, jnp.maximum(i - 1, 0), r * 2 * DIL_GROUPS + kcol + 1)),
                  pl.BlockSpec(bias.shape, lambda b, r, i: (0, 0, 0))],
        out_specs=[pl.BlockSpec(blk, lambda b, r, i: (b, i, r)),
                   pl.BlockSpec(blk, lambda b, r, i: (b, i, r))],
        out_shape=[out_sd, out_sd],
        compiler_params=_cparams(("parallel", "parallel", "arbitrary")),
        name=f"dil_attn_g{group}",
    )(qv, kvv, kvv, kvv, kvv, bias)
    return o.reshape(batch * seq, DIL_W), lse.reshape(batch * seq, DIL_W)


def _rpb_bucket(dist):
    exact = RPB_BUCKETS // 2
    d = jnp.maximum(dist, 0)
    d_f = jnp.maximum(d, 1).astype(F32)
    large = exact + (jnp.log(d_f / exact) / math.log(RPB_MAX_DIST / exact)
                     * (RPB_BUCKETS - exact)).astype(jnp.int32)
    return jnp.where(d < exact, d, jnp.minimum(large, RPB_BUCKETS - 1))


def _dil_bias(table, span, dil):
    period = 3 * DIL_BLOCK
    j = jnp.arange(period)
    k_minus_q = jnp.where(j < 2 * DIL_BLOCK, j, j - period)
    dist = DIL_BLOCK - k_minus_q
    band = (dist >= 0) & (dist <= span)
    prof = jnp.where(band[:, None], table[_rpb_bucket(dist * dil)].astype(F32), MASK_VALUE).T
    skew = jnp.tile(prof, (1, DIL_BLOCK))[:, :DIL_BLOCK * (period - 1)]
    return skew.reshape(DIL_HEADS, DIL_BLOCK, period - 1)[:, :, :2 * DIL_BLOCK]


def _dil_out_kernel(o0_ref, o1_ref, o2_ref, l0_ref, l1_ref, l2_ref, w_ref, h_ref, gt_ref, out_ref):
    l0, l1, l2 = l0_ref[...], l1_ref[...], l2_ref[...]
    m = jnp.maximum(jnp.maximum(l0, l1), l2)
    w0, w1, w2 = jnp.exp(l0 - m), jnp.exp(l1 - m), jnp.exp(l2 - m)
    o = (o0_ref[...] * w0 + o1_ref[...] * w1 + o2_ref[...] * w2) / (w0 + w1 + w2)
    out_ref[...] = h_ref[...] + gt_ref[...] * _dot(o.astype(BF16), w_ref[...])


def _dil_out(os_, ls_, w, h, gt, *, tm, tiles_per_batch):
    T, D = h.shape
    row = lambda n: pl.BlockSpec((tm, n), lambda i: (i, 0))
    return pl.pallas_call(
        _dil_out_kernel,
        grid=(T // tm,),
        in_specs=[row(DIL_W)] * 6 + [pl.BlockSpec(w.shape, lambda i: (0, 0)), row(D),
                                     pl.BlockSpec((None, 1, D), lambda i: (i // tiles_per_batch, 0, 0))],
        out_specs=row(D),
        out_shape=jax.ShapeDtypeStruct((T, D), F32),
        compiler_params=_cparams(("parallel",)),
        name="dil_out",
    )(*os_, *ls_, w, h, gt)


def _route_kernel(h_ref, g_ref, sc_ref, sh_ref, wr_ref, hn_ref, gates_ref):
    hn = _normmod(h_ref[...], g_ref[...], sc_ref[...], sh_ref[...])
    hn_ref[...] = hn.astype(BF16)
    logits = jnp.dot(hn, wr_ref[...], preferred_element_type=F32, precision=lax.Precision.HIGHEST)
    idx = lax.broadcasted_iota(jnp.int32, logits.shape, 1)
    v1 = jnp.max(logits, axis=1, keepdims=True)
    i1 = jnp.min(jnp.where(logits == v1, idx, N_EXPERTS), axis=1, keepdims=True)
    rest = jnp.where(idx == i1, -jnp.inf, logits)
    v2 = jnp.max(rest, axis=1, keepdims=True)
    i2 = jnp.min(jnp.where(rest == v2, idx, N_EXPERTS), axis=1, keepdims=True)
    e = jnp.exp(v2 - v1)
    g1 = 1.0 / (1.0 + e)
    g2 = e / (1.0 + e)
    gates_ref[...] = jnp.where(idx == i1, g1, 0.0) + jnp.where(idx == i2, g2, 0.0)


def _route(h, g, sc, sh, wr, *, tm, tiles_per_batch):
    T, D = h.shape
    bvec = pl.BlockSpec((None, 1, D), lambda i: (i // tiles_per_batch, 0, 0))
    return pl.pallas_call(
        _route_kernel,
        grid=(T // tm,),
        in_specs=[pl.BlockSpec((tm, D), lambda i: (i, 0)), pl.BlockSpec((1, D), lambda i: (0, 0)), bvec, bvec,
                  pl.BlockSpec(wr.shape, lambda i: (0, 0))],
        out_specs=[pl.BlockSpec((tm, D), lambda i: (i, 0)), pl.BlockSpec((tm, N_EXPERTS), lambda i: (i, 0))],
        out_shape=[jax.ShapeDtypeStruct((T, D), BF16), jax.ShapeDtypeStruct((T, N_EXPERTS), F32)],
        compiler_params=_cparams(("parallel",)),
        name="moe_route",
    )(h, g, sc, sh, wr)


def _moe_kernel(hn_ref, gates_ref, h_ref, gt_ref, gfin_ref, wg_ref, wu_ref, wd_ref, o_ref, acc_sc):
    e = pl.program_id(1)
    f = pl.program_id(2)

    @pl.when((e == 0) & (f == 0))
    def _():
        acc_sc[...] = jnp.zeros(acc_sc.shape, F32)

    hn = hn_ref[...]
    gates = gates_ref[...]
    idx = lax.broadcasted_iota(jnp.int32, gates.shape, 1)
    gate = jnp.sum(jnp.where(idx == e, gates, 0.0), axis=1, keepdims=True)
    a = _silu(_dot(hn, wg_ref[...])) * _dot(hn, wu_ref[...]) * gate
    acc_sc[...] += _dot(a.astype(BF16), wd_ref[...])

    @pl.when((e == pl.num_programs(1) - 1) & (f == pl.num_programs(2) - 1))
    def _():
        y = h_ref[...] + gt_ref[...] * acc_sc[...]
        o_ref[...] = y * _rms_scale(y) * gfin_ref[...]


def _moe(hn, gates, h, gt, gfin, w_gu, w_d, *, tm, tf, tiles_per_batch):
    T, D = h.shape
    nf = D_FF // tf
    return pl.pallas_call(
        _moe_kernel,
        grid=(T // tm, N_EXPERTS, nf),
        in_specs=[pl.BlockSpec((tm, D), lambda i, e, f: (i, 0)),
                  pl.BlockSpec((tm, N_EXPERTS), lambda i, e, f: (i, 0)),
                  pl.BlockSpec((tm, D), lambda i, e, f: (i, 0)),
                  pl.BlockSpec((None, 1, D), lambda i, e, f: (i // tiles_per_batch, 0, 0)),
                  pl.BlockSpec((1, D), lambda i, e, f: (0, 0)),
                  pl.BlockSpec((None, D, tf), lambda i, e, f: (e, 0, f)),
                  pl.BlockSpec((None, D, tf), lambda i, e, f: (e, 0, nf + f)),
                  pl.BlockSpec((None, tf, D), lambda i, e, f: (e, f, 0))],
        out_specs=pl.BlockSpec((tm, D), lambda i, e, f: (i, 0)),
        out_shape=jax.ShapeDtypeStruct((T, D), F32),
        scratch_shapes=[pltpu.VMEM((tm, D), F32)],
        compiler_params=_cparams(("parallel", "arbitrary", "arbitrary")),
        name="moe_experts",
    )(hn, gates, h, gt, gfin, w_gu, w_gu, w_d)


def _mla_weights(w_in, w_q_up, w_kv_up):
    D = w_in.shape[0]
    half = MLA_ROPE // 2
    w_ql = w_in[:, :MLA_Q_RANK]
    w_kvl = w_in[:, MLA_Q_RANK:MLA_Q_RANK + MLA_KV_RANK]
    w_kr = w_in[:, MLA_Q_RANK + MLA_KV_RANK:]
    z64 = jnp.zeros((D, MLA_NOPE), F32)
    z32 = jnp.zeros((D, HEAD_PAD - MLA_QK), F32)
    kr_pad = jnp.concatenate([z64, w_kr, z32], axis=1)
    kr_swap = jnp.concatenate([z64, w_kr[:, half:], w_kr[:, :half], z32], axis=1)
    w1 = jnp.concatenate([w_ql, w_kvl, kr_pad, kr_swap], axis=1).astype(BF16)

    wq = w_q_up.reshape(MLA_Q_RANK, MLA_HEADS, MLA_QK)
    nope, rope = wq[..., :MLA_NOPE], wq[..., MLA_NOPE:]
    zq = jnp.zeros((MLA_Q_RANK, MLA_HEADS, HEAD_PAD - MLA_QK), F32)
    wqa = jnp.concatenate([nope, rope, zq], axis=-1).reshape(MLA_Q_RANK, MLA_HEADS * HEAD_PAD).astype(BF16)
    wqb = jnp.concatenate([jnp.zeros_like(nope), rope[..., half:], rope[..., :half], zq], axis=-1)
    wqb = wqb.reshape(MLA_Q_RANK, MLA_HEADS * HEAD_PAD).astype(BF16)

    wkv = w_kv_up.reshape(MLA_KV_RANK, MLA_HEADS, MLA_NOPE + MLA_V)
    k_nope, v = wkv[..., :MLA_NOPE], wkv[..., MLA_NOPE:]
    zk = jnp.zeros((MLA_KV_RANK, MLA_HEADS, HEAD_PAD - MLA_NOPE), F32)
    wka = jnp.concatenate([k_nope, zk], axis=-1).reshape(MLA_KV_RANK, MLA_HEADS * HEAD_PAD).astype(BF16)
    wv = v.reshape(MLA_KV_RANK, MLA_HEADS * MLA_V).astype(BF16)
    return w1, wqa, wqb, wka, wv


def _rope_tables(positions):
    inv_freq = ROPE_THETA ** (-jnp.arange(0, MLA_ROPE, 2, dtype=F32) / MLA_ROPE)
    ang = positions.astype(F32).reshape(-1, 1) * inv_freq
    cos, sin = jnp.cos(ang), jnp.sin(ang)
    T = ang.shape[0]
    one = jnp.ones((T, MLA_NOPE), F32)
    z64 = jnp.zeros((T, MLA_NOPE), F32)
    z32 = jnp.zeros((T, HEAD_PAD - MLA_QK), F32)
    return (jnp.concatenate([one, cos, cos, z32], axis=1),
            jnp.concatenate([z64, -sin, sin, z32], axis=1))


def kernel(x, c, positions, g_mix, g_ffn, w_ada, b_ada, w_mla_in, g_mla_q, w_mla_q_up, g_mla_kv, w_mla_kv_up,
           w_mla_out, g_kv_b, w_ada_kv, b_ada_kv, w_kv_b, rpb_table, w_q_b, w_o_b, w_ffn_gu, w_ffn_down,
           w_router, w_exp_gu, w_exp_down, g_final):
    B, S, D = x.shape
    T = B * S
    h = x.reshape(T, D)

    c8 = jnp.zeros((8, D), F32).at[:B].set(c)
    mod = _ada(c8, w_ada, b_ada[:, None, :])[:, :B]
    mod = mod.reshape(2, B, N_MOD, 1, D)
    mod_kv = _ada(c8, w_ada_kv[None], b_ada_kv[None, None, :])[0, :B].reshape(B, 2, 1, D)
    sh_kv, sc_kv = mod_kv[:, 0], mod_kv[:, 1]

    def mods(layer):
        return [mod[layer, :, k] for k in range(N_MOD)]

    row = lambda v: v.reshape(1, -1)

    sh_m, sc_m, gt_m, sh_f, sc_f, gt_f = mods(0)
    w1, wqa, wqb, wka, wv = _mla_weights(w_mla_in[0], w_mla_q_up[0], w_mla_kv_up[0])
    cos_t, sin_t = _rope_tables(positions)
    tm = 256
    q, k, v = _mla_proj(h, row(g_mix[0]), sc_m, sh_m, cos_t, sin_t, w1, row(g_mla_q[0]), row(g_mla_kv[0]),
                        wqa, wqb, wka, wv, tm=tm, tiles_per_batch=S // tm)
    o = _mla_attn(q.reshape(B, S, -1), k.reshape(B, S, -1), v.reshape(B, S, -1), batch=B, seq=S, tq=512)
    tm = 512
    h = _proj_res(o.reshape(T, -1), w_mla_out[0].astype(BF16), h, gt_m, tm=tm, tiles_per_batch=S // tm)

    tm = 1024
    h = _ffn(h, row(g_ffn[0]), sc_f, sh_f, gt_f, w_ffn_gu[0].astype(BF16), w_ffn_down[0].astype(BF16),
             tm=tm, tf=256, tiles_per_batch=S // tm)

    sh_m, sc_m, gt_m, sh_f, sc_f, gt_f = mods(1)
    tm = 512
    kv, q = _l1_proj(h, row(g_kv_b), sc_kv, sh_kv, row(g_mix[1]), sc_m, sh_m,
                     w_kv_b.astype(BF16), w_q_b[0].astype(BF16), tm=tm, tiles_per_batch=S // tm)

    outs, lses = [], []
    for g, (window, dil) in enumerate(DIL_PATTERNS):
        bias = _dil_bias(rpb_table[:, g * DIL_HEADS:(g + 1) * DIL_HEADS], window // dil, dil)
        o_g, lse_g = _dil_attn(q, kv, bias, group=g, dil=dil, batch=B, seq=S)
        outs.append(o_g)
        lses.append(lse_g)
    h = _dil_out(outs, lses, w_o_b[0].astype(BF16), h, gt_m, tm=tm, tiles_per_batch=S // tm)

    hn, gates = _route(h, row(g_ffn[1]), sc_f, sh_f, w_router[0], tm=tm, tiles_per_batch=S // tm)
    tm = 1024
    out = _moe(hn, gates, h, gt_f, row(g_final), w_exp_gu[0].astype(BF16), w_exp_down[0].astype(BF16),
               tm=tm, tf=256, tiles_per_batch=S // tm)
    return out.reshape(B, S, D)
```

```python
import functools
import math

import numpy as np
import jax
import jax.numpy as jnp
from jax import lax
from jax.experimental import pallas as pl
from jax.experimental.pallas import tpu as pltpu

D_MODEL = 1024
N_MOD = 6
EPS = 1e-6

MLA_HEADS = 16
MLA_Q_RANK = 384
MLA_KV_RANK = 256
MLA_NOPE = 64
MLA_ROPE = 32
MLA_V = 64
MLA_QK = MLA_NOPE + MLA_ROPE
ROPE_THETA = 10000.0
HEAD_PAD = 128

DIL_PATTERNS = ((128, 1), (512, 4), (2048, 16))
DIL_GROUPS = len(DIL_PATTERNS)
DIL_HEADS = 8
DIL_HEAD_DIM = 64
DIL_BLOCK = 128
DIL_W = DIL_HEADS * DIL_HEAD_DIM
RPB_BUCKETS = 32
RPB_MAX_DIST = 2048

D_FF = 2816
N_EXPERTS = 8
MOE_TM = 512

MASK_VALUE = -1e30
LOG2E = math.log2(math.e)

F32 = jnp.float32
BF16 = jnp.bfloat16

VMEM_LIMIT = 56 * 1024 * 1024


def _cparams(sem):
    return pltpu.CompilerParams(dimension_semantics=sem, vmem_limit_bytes=VMEM_LIMIT)


def _dot(a, b):
    return jnp.dot(a, b, preferred_element_type=F32)


def _dot_nt(a, b):
    return lax.dot_general(a, b, (((1,), (1,)), ((), ())), preferred_element_type=F32)


def _rms_scale(x):
    return lax.rsqrt(jnp.mean(x * x, axis=-1, keepdims=True) + EPS)


def _normmod(x, g, sc, sh):
    return (x * _rms_scale(x)) * (g * (1.0 + sc)) + sh


def _silu(x):
    return x * (1.0 / (1.0 + jnp.exp(-x)))


def _ada_kernel(c_ref, w_ref, b_ref, o_ref):
    c = c_ref[...]
    o_ref[...] = jnp.dot(_silu(c), w_ref[...], preferred_element_type=F32,
                         precision=lax.Precision.HIGHEST) + b_ref[...]


def _ada(c8, w, b, tn=512):
    L, D, N = w.shape
    return pl.pallas_call(
        _ada_kernel,
        grid=(L, N // tn),
        in_specs=[pl.BlockSpec((8, D), lambda l, j: (0, 0)),
                  pl.BlockSpec((None, D, tn), lambda l, j: (l, 0, j)),
                  pl.BlockSpec((None, 1, tn), lambda l, j: (l, 0, j))],
        out_specs=pl.BlockSpec((None, 8, tn), lambda l, j: (l, 0, j)),
        out_shape=jax.ShapeDtypeStruct((L, 8, N), F32),
        compiler_params=_cparams(("parallel", "parallel")),
        name="ada_mod",
    )(c8, w, b)


def _mla_proj_kernel(h_ref, g_ref, sc_ref, sh_ref, cos_ref, sin_ref, w1_ref, gq_ref, gkv_ref,
                     wqa_ref, wqb_ref, wka_ref, wv_ref, q_ref, k_ref, v_ref, *, qscale):
    hn = _normmod(h_ref[...], g_ref[...], sc_ref[...], sh_ref[...]).astype(BF16)
    z = _dot(hn, w1_ref[...])
    ql = z[:, :MLA_Q_RANK]
    kvl = z[:, MLA_Q_RANK:MLA_Q_RANK + MLA_KV_RANK]
    kr = z[:, MLA_Q_RANK + MLA_KV_RANK:MLA_Q_RANK + MLA_KV_RANK + HEAD_PAD]
    krs = z[:, MLA_Q_RANK + MLA_KV_RANK + HEAD_PAD:]
    qn = (ql * _rms_scale(ql) * gq_ref[...]).astype(BF16)
    kvn = (kvl * _rms_scale(kvl) * gkv_ref[...]).astype(BF16)
    cos = cos_ref[...]
    sin = sin_ref[...]
    k_rope = kr * cos + krs * sin
    cos2 = jnp.concatenate([cos, cos], axis=1)
    sin2 = jnp.concatenate([sin, sin], axis=1)
    k_rope2 = jnp.concatenate([k_rope, k_rope], axis=1)
    for hp in range(MLA_HEADS // 2):
        sl = slice(2 * HEAD_PAD * hp, 2 * HEAD_PAD * (hp + 1))
        a = _dot(qn, wqa_ref[:, sl])
        b = _dot(qn, wqb_ref[:, sl])
        q_ref[:, sl] = ((a * cos2 + b * sin2) * qscale).astype(BF16)
        k_ref[:, sl] = (_dot(kvn, wka_ref[:, sl]) + k_rope2).astype(BF16)
    v_ref[...] = _dot(kvn, wv_ref[...]).astype(BF16)


def _mla_proj(h, g, sc, sh, cos_t, sin_t, w1, gq, gkv, wqa, wqb, wka, wv, *, tm, tiles_per_batch):
    T, D = h.shape
    HP = MLA_HEADS * HEAD_PAD
    row = lambda n: pl.BlockSpec((tm, n), lambda i: (i, 0))
    bvec = pl.BlockSpec((None, 1, D), lambda i: (i // tiles_per_batch, 0, 0))
    full = lambda a: pl.BlockSpec(a.shape, lambda i: (0,) * a.ndim)
    return pl.pallas_call(
        functools.partial(_mla_proj_kernel, qscale=(MLA_QK ** -0.5) * LOG2E),
        grid=(T // tm,),
        in_specs=[row(D), full(g), bvec, bvec, row(HEAD_PAD), row(HEAD_PAD), full(w1), full(gq), full(gkv),
                  full(wqa), full(wqb), full(wka), full(wv)],
        out_specs=[row(HP), row(HP), row(MLA_HEADS * MLA_V)],
        out_shape=[jax.ShapeDtypeStruct((T, HP), BF16), jax.ShapeDtypeStruct((T, HP), BF16),
                   jax.ShapeDtypeStruct((T, MLA_HEADS * MLA_V), BF16)],
        compiler_params=_cparams(("parallel",)),
        name="mla_proj",
    )(h, g, sc, sh, cos_t, sin_t, w1, gq, gkv, wqa, wqb, wka, wv)


def _mla_attn_kernel(q_ref, k_ref, v_ref, o_ref, m_sc, acc_sc, p_sc, alpha_sc, *, tq):
    qi = pl.program_id(2)
    m_sc[...] = jnp.full(m_sc.shape, -jnp.inf, F32)
    acc_sc[...] = jnp.zeros(acc_sc.shape, F32)
    nrep = tq // HEAD_PAD

    def scores(j, diagonal):
        off = pl.multiple_of(j * tq, tq)
        ss = []
        for hh in range(2):
            q = q_ref[:, HEAD_PAD * hh:HEAD_PAD * (hh + 1)]
            k = k_ref[pl.ds(off, tq), HEAD_PAD * hh:HEAD_PAD * (hh + 1)]
            s = _dot_nt(q, k)
            if diagonal:
                r = lax.broadcasted_iota(jnp.int32, (tq, tq), 0)
                c = lax.broadcasted_iota(jnp.int32, (tq, tq), 1)
                s = jnp.where(r >= c, s, -jnp.inf)
            ss.append(s)
        return ss

    def softmax(ss):
        for hh in range(2):
            m_prev = m_sc[hh]
            m_new = jnp.maximum(m_prev, jnp.max(ss[hh], axis=1, keepdims=True))
            alpha_sc[hh] = jnp.exp2(m_prev - m_new)
            p_sc[hh] = jnp.exp2(ss[hh] - jnp.tile(m_new, (1, nrep))).astype(BF16)
            m_sc[hh] = m_new

    def apply_values(j):
        off = pl.multiple_of(j * tq, tq)
        v = v_ref[pl.ds(off, tq), :]
        v1 = jnp.concatenate([v, jnp.ones_like(v)], axis=1)
        for hh in range(2):
            acc_sc[hh] = jnp.tile(alpha_sc[hh], (1, 2)) * acc_sc[hh] + _dot(p_sc[hh], v1)

    @pl.when(qi == 0)
    def _():
        softmax(scores(0, True))

    @pl.when(qi > 0)
    def _():
        softmax(scores(0, False))

        def body(j, carry):
            ss = scores(j, False)
            apply_values(j - 1)
            softmax(ss)
            return carry

        lax.fori_loop(1, qi, body, 0)
        ss = scores(qi, True)
        apply_values(qi - 1)
        softmax(ss)

    apply_values(qi)
    lane =lax.broadcasted_iota(jnp.int32, (1, HEAD_PAD), 1)
    o0 = acc_sc[0, :, :HEAD_PAD] / acc_sc[0, :, HEAD_PAD:]
    o1 = acc_sc[1, :, :HEAD_PAD] / acc_sc[1, :, HEAD_PAD:]
    o_ref[...] = jnp.where(lane < MLA_V, o0, o1).astype(BF16)


def _mla_attn(q, k, v, *, batch, seq, tq):
    return pl.pallas_call(
        functools.partial(_mla_attn_kernel, tq=tq),
        grid=(batch, MLA_HEADS // 2, seq // tq),
        in_specs=[pl.BlockSpec((None, tq, 2 * HEAD_PAD), lambda b, hp, i: (b, i, hp)),
                  pl.BlockSpec((None, seq, 2 * HEAD_PAD), lambda b, hp, i: (b, 0, hp)),
                  pl.BlockSpec((None, seq, 2 * MLA_V), lambda b, hp, i: (b, 0, hp))],
        out_specs=pl.BlockSpec((None, tq, 2 * MLA_V), lambda b, hp, i: (b, i, hp)),
        out_shape=jax.ShapeDtypeStruct((batch, seq, MLA_HEADS * MLA_V), BF16),
        scratch_shapes=[pltpu.VMEM((2, tq, HEAD_PAD), F32), pltpu.VMEM((2, tq, 2 * HEAD_PAD), F32),
                        pltpu.VMEM((2, tq, tq), BF16), pltpu.VMEM((2, tq, HEAD_PAD), F32)],
        compiler_params=_cparams(("parallel", "parallel", "arbitrary")),
        name="mla_attn",
    )(q, k, v)


def _swiglu_tile(x, wgu_ref, wd_ref, acc_sc, tf):
    for f in range(D_FF // tf):
        g = _dot(x, wgu_ref[:, f * tf:(f + 1) * tf])
        u = _dot(x, wgu_ref[:, D_FF + f * tf:D_FF + (f + 1) * tf])
        part = _dot((_silu(g) * u).astype(BF16), wd_ref[f * tf:(f + 1) * tf, :])
        if f == 0:
            acc_sc[...] = part
        else:
            acc_sc[...] += part


def _ffn_kernel(a_ref, wo_ref, h_ref, gtm_ref, g_ref, sc_ref, sh_ref, gt_ref, wgu_ref, wd_ref, o_ref, acc_sc,
                *, tf):
    h = h_ref[...] + gtm_ref[...] * _dot(a_ref[...], wo_ref[...])
    x = _normmod(h, g_ref[...], sc_ref[...], sh_ref[...]).astype(BF16)
    _swiglu_tile(x, wgu_ref, wd_ref, acc_sc, tf)
    o_ref[...] = h + gt_ref[...] * acc_sc[...]


def _ffn(a, w_o, h, gt_m, g, sc, sh, gt, w_gu, w_d, *, tm, tf, tiles_per_batch):
    T, D = h.shape
    bvec = pl.BlockSpec((None, 1, D), lambda i: (i // tiles_per_batch, 0, 0))
    row = pl.BlockSpec((tm, D), lambda i: (i, 0))
    const = lambda w: pl.BlockSpec(w.shape, lambda i: (0,) * w.ndim, pipeline_mode=pl.Buffered(1))
    return pl.pallas_call(
        functools.partial(_ffn_kernel, tf=tf),
        grid=(T // tm,),
        in_specs=[row, const(w_o), row, bvec, const(g), bvec, bvec, bvec, const(w_gu), const(w_d)],
        out_specs=row,
        out_shape=jax.ShapeDtypeStruct((T, D), F32),
        scratch_shapes=[pltpu.VMEM((tm, D), F32)],
        compiler_params=_cparams(("parallel",)),
        name="ffn_dense",
    )(a, w_o, h, gt_m, g, sc, sh, gt, w_gu, w_d)


def _l1_proj_kernel(h_ref, gkv_ref, sckv_ref, shkv_ref, gq_ref, scq_ref, shq_ref, wkv_ref, wq_ref,
                    kv_ref, q_ref):
    x = h_ref[...]
    xh = x * _rms_scale(x)
    hn_kv = (xh * (gkv_ref[...] * (1.0 + sckv_ref[...])) + shkv_ref[...]).astype(BF16)
    hn_q = (xh * (gq_ref[...] * (1.0 + scq_ref[...])) + shq_ref[...]).astype(BF16)
    kv_ref[...] = _dot(hn_kv, wkv_ref[...]).astype(BF16)
    q_ref[...] = (_dot(hn_q, wq_ref[...]) * (DIL_HEAD_DIM ** -0.5)).astype(BF16)


def _l1_proj(h, gkv, sckv, shkv, gq, scq, shq, wkv, wq, *, tm, tiles_per_batch):
    T, D = h.shape
    row = lambda n: pl.BlockSpec((tm, n), lambda i: (i, 0))
    bvec = pl.BlockSpec((None, 1, D), lambda i: (i // tiles_per_batch, 0, 0))
    full = lambda a: pl.BlockSpec(a.shape, lambda i: (0,) * a.ndim)
    return pl.pallas_call(
        _l1_proj_kernel,
        grid=(T // tm,),
        in_specs=[row(D), full(gkv), bvec, bvec, full(gq), bvec, bvec, full(wkv), full(wq)],
        out_specs=[row(wkv.shape[1]), row(wq.shape[1])],
        out_shape=[jax.ShapeDtypeStruct((T, wkv.shape[1]), BF16), jax.ShapeDtypeStruct((T, wq.shape[1]), BF16)],
        compiler_params=_cparams(("parallel",)),
        name="l1_proj",
    )(h, gkv, sckv, shkv, gq, scq, shq, wkv, wq)


def _dil_attn_kernel(q_ref, kc_ref, kp_ref, vc_ref, vp_ref, bias_ref, o_ref, lse_ref):
    i = pl.program_id(2)
    lane = lax.broadcasted_iota(jnp.int32, (1, 2 * DIL_HEAD_DIM), 1)
    lo = lane < DIL_HEAD_DIM
    col = lax.broadcasted_iota(jnp.int32, (1, 2 * DIL_BLOCK), 1)
    edge = jnp.where((col < DIL_BLOCK) & (i == 0), MASK_VALUE, 0.0).astype(F32)
    for hp in range(DIL_HEADS // 2):
        sl = slice(2 * DIL_HEAD_DIM * hp, 2 * DIL_HEAD_DIM * (hp + 1))
        q2 = q_ref[:, sl]
        k2 = jnp.concatenate([kp_ref[:, sl], kc_ref[:, sl]], axis=0)
        v2 = jnp.concatenate([vp_ref[:, sl], vc_ref[:, sl]], axis=0)
        outs, lses = [], []
        for hh in range(2):
            keep = lo if hh == 0 else jnp.logical_not(lo)
            qm = jnp.where(keep, q2, jnp.zeros_like(q2))
            s = _dot_nt(qm, k2) + bias_ref[2 * hp + hh] + edge
            m = jnp.max(s, axis=1, keepdims=True)
            p = jnp.exp(s - m)
            l = jnp.sum(p, axis=1, keepdims=True)
            outs.append(_dot(p.astype(BF16), v2) / l)
            lses.append(m + jnp.log(l))
        o_ref[:, sl] = jnp.where(lo, outs[0], outs[1])
        lse_ref[:, sl] = jnp.where(lo, lses[0], lses[1])


def _dil_attn(q, kv, bias, *, group, dil, batch, seq):
    n = seq // dil
    nb = n // DIL_BLOCK
    qv = q.reshape(batch, n, dil * DIL_GROUPS * DIL_W)
    kvv = kv.reshape(batch, n, dil * 2 * DIL_GROUPS * DIL_W)
    blk = (None, DIL_BLOCK, DIL_W)
    kcol = 2 * group
    out_sd = jax.ShapeDtypeStruct((batch, n, dil * DIL_W), F32)
    o, lse = pl.pallas_call(
        _dil_attn_kernel,
        grid=(batch, dil, nb),
        in_specs=[pl.BlockSpec(blk, lambda b, r, i: (b, i, r * DIL_GROUPS + group)),
                  pl.BlockSpec(blk, lambda b, r, i: (b, i, r * 2 * DIL_GROUPS + kcol)),
                  pl.BlockSpec(blk, lambda b, r, i: (b, jnp.maximum(i - 1, 0), r * 2 * DIL_GROUPS + kcol)),
                  pl.BlockSpec(blk, lambda b, r, i: (b, i, r * 2 * DIL_GROUPS + kcol + 1)),
                  pl.BlockSpec(blk, lambda b, r, i: (b, jnp.maximum(i - 1, 0), r * 2 * DIL_GROUPS + kcol + 1)),
                  pl.BlockSpec(bias.shape, lambda b, r, i: (0, 0, 0))],
        out_specs=[pl.BlockSpec(blk, lambda b, r, i: (b, i, r)),
                   pl.BlockSpec(blk, lambda b, r, i: (b, i, r))],
        out_shape=[out_sd, out_sd],
        compiler_params=_cparams(("parallel", "parallel", "arbitrary")),
        name=f"dil_attn_g{group}",
    )(qv, kvv, kvv, kvv, kvv, bias)
    return o.reshape(batch * seq, DIL_W), lse.reshape(batch * seq, DIL_W)


def _rpb_bucket(dist):
    exact = RPB_BUCKETS // 2
    d = jnp.maximum(dist, 0)
    d_f = jnp.maximum(d, 1).astype(F32)
    large = exact + (jnp.log(d_f / exact) / math.log(RPB_MAX_DIST / exact)
                     * (RPB_BUCKETS - exact)).astype(jnp.int32)
    return jnp.where(d < exact, d, jnp.minimum(large, RPB_BUCKETS - 1))


def _dil_bias(table, span, dil):
    period = 3 * DIL_BLOCK
    j = jnp.arange(period)
    k_minus_q = jnp.where(j < 2 * DIL_BLOCK, j, j - period)
    dist = DIL_BLOCK - k_minus_q
    band = (dist >= 0) & (dist <= span)
    prof = jnp.where(band[:, None], table[_rpb_bucket(dist * dil)].astype(F32), MASK_VALUE).T
    skew = jnp.tile(prof, (1, DIL_BLOCK))[:, :DIL_BLOCK * (period - 1)]
    return skew.reshape(DIL_HEADS, DIL_BLOCK, period - 1)[:, :, :2 * DIL_BLOCK]


def _dil_out_kernel(o0_ref, o1_ref, o2_ref, l0_ref, l1_ref, l2_ref, w_ref, h_ref, gt_ref, out_ref):
    l0, l1, l2 = l0_ref[...], l1_ref[...], l2_ref[...]
    m = jnp.maximum(jnp.maximum(l0, l1), l2)
    w0, w1, w2 = jnp.exp(l0 - m), jnp.exp(l1 - m), jnp.exp(l2 - m)
    o = (o0_ref[...] * w0 + o1_ref[...] * w1 + o2_ref[...] * w2) / (w0 + w1 + w2)
    out_ref[...] = h_ref[...] + gt_ref[...] * _dot(o.astype(BF16), w_ref[...])


def _dil_out(os_, ls_, w, h, gt, *, tm, tiles_per_batch):
    T, D = h.shape
    row = lambda n: pl.BlockSpec((tm, n), lambda i: (i, 0))
    return pl.pallas_call(
        _dil_out_kernel,
        grid=(T // tm,),
        in_specs=[row(DIL_W)] * 6 + [pl.BlockSpec(w.shape, lambda i: (0, 0)), row(D),
                                     pl.BlockSpec((None, 1, D), lambda i: (i // tiles_per_batch, 0, 0))],
        out_specs=row(D),
        out_shape=jax.ShapeDtypeStruct((T, D), F32),
        compiler_params=_cparams(("parallel",)),
        name="dil_out",
    )(*os_, *ls_, w, h, gt)


def _route_kernel(h_ref, g_ref, sc_ref, sh_ref, wrt_ref, hn_ref, eid_ref, rank_ref, gate_ref, cnt_ref, carry_sc):
    @pl.when(pl.program_id(0) == 0)
    def _():
        carry_sc[...] = jnp.zeros(carry_sc.shape, F32)

    hn = _normmod(h_ref[...], g_ref[...], sc_ref[...], sh_ref[...])
    hn_ref[...] = hn
    tm = hn.shape[0]
    logits = lax.dot_general(wrt_ref[...], hn, (((1,), (1,)), ((), ())), preferred_element_type=F32,
                             precision=lax.Precision.HIGHEST)
    idx = lax.broadcasted_iota(jnp.int32, logits.shape, 0)
    v1 = jnp.max(logits, axis=0, keepdims=True)
    i1 = jnp.min(jnp.where(logits == v1, idx, N_EXPERTS), axis=0, keepdims=True)
    rest = jnp.where(idx == i1, -jnp.inf, logits)
    v2 = jnp.max(rest, axis=0, keepdims=True)
    i2 = jnp.min(jnp.where(rest == v2, idx, N_EXPERTS), axis=0, keepdims=True)
    e = jnp.exp(v2 - v1)
    gate_ref[...] = jnp.concatenate([1.0 / (1.0 + e), e / (1.0 + e)], axis=0)
    eid_ref[...] = jnp.concatenate([i1, i2], axis=0)

    sel = ((idx == i1) | (idx == i2)).astype(BF16)
    before = (lax.broadcasted_iota(jnp.int32, (tm, tm), 0)
              < lax.broadcasted_iota(jnp.int32, (tm, tm), 1)).astype(BF16)
    rank_all = carry_sc[:, :1] + _dot(sel, before)
    r1 = jnp.sum(jnp.where(idx == i1, rank_all, 0.0), axis=0, keepdims=True)
    r2 = jnp.sum(jnp.where(idx == i2, rank_all, 0.0), axis=0, keepdims=True)
    rank_ref[...] = jnp.concatenate([r1, r2], axis=0).astype(jnp.int32)
    carry_sc[...] += jnp.sum(sel.astype(F32), axis=1, keepdims=True)
    cnt_ref[...] = carry_sc[...].astype(jnp.int32)


def _route(h, g, sc, sh, wrt, *, tm, tiles_per_batch):
    T, D = h.shape
    bvec = pl.BlockSpec((None, 1, D), lambda i: (i // tiles_per_batch, 0, 0))
    lane_blk = pl.BlockSpec((2, tm), lambda i: (0, i))
    return pl.pallas_call(
        _route_kernel,
        grid=(T // tm,),
        in_specs=[pl.BlockSpec((tm, D), lambda i: (i, 0)), pl.BlockSpec((1, D), lambda i: (0, 0)), bvec, bvec,
                  pl.BlockSpec(wrt.shape, lambda i: (0, 0))],
        out_specs=[pl.BlockSpec((tm, D), lambda i: (i, 0)), lane_blk, lane_blk, lane_blk,
                   pl.BlockSpec((N_EXPERTS, HEAD_PAD), lambda i: (0, 0))],
        out_shape=[jax.ShapeDtypeStruct((T, D), F32), jax.ShapeDtypeStruct((2, T), jnp.int32),
                   jax.ShapeDtypeStruct((2, T), jnp.int32), jax.ShapeDtypeStruct((2, T), F32),
                   jax.ShapeDtypeStruct((N_EXPERTS, HEAD_PAD), jnp.int32)],
        scratch_shapes=[pltpu.VMEM((N_EXPERTS, HEAD_PAD), F32)],
        compiler_params=_cparams(("arbitrary",)),
        name="moe_route",
    )(h, g, sc, sh, wrt)


def _dispatch_kernel(pad_ref, dest_ref, hn_ref, xs_ref, zero_sc, sem, zsem):
    tm = hn_ref.shape[0]

    def row_copy(r, k):
        return pltpu.make_async_copy(hn_ref.at[pl.ds(r, 1), :], xs_ref.at[pl.ds(dest_ref[k, r], 1), :], sem)

    def issue(r, carry):
        row_copy(r, 0).start()
        row_copy(r, 1).start()
        return carry

    lax.fori_loop(0, tm, issue, 0)

    @pl.when(pl.program_id(0) == pl.num_programs(0) - 1)
    def _():
        zero_sc[...] = jnp.zeros(zero_sc.shape, F32)

        def pad_copy(p):
            return pltpu.make_async_copy(zero_sc.at[pl.ds(0, 1), :], xs_ref.at[pl.ds(p, 1), :], zsem)

        for e in range(N_EXPERTS):
            lo, hi = pad_ref[0, e], pad_ref[1, e]

            def zissue(p, carry):
                pad_copy(p).start()
                return carry

            def zwait(p, carry):
                pad_copy(p).wait()
                return carry

            lax.fori_loop(lo, hi, zissue, 0)
            lax.fori_loop(lo, hi, zwait, 0)

        zrows = zero_sc.shape[0]

        def tail_copy(c):
            return pltpu.make_async_copy(zero_sc, xs_ref.at[pl.ds(pl.multiple_of(c * zrows, zrows), zrows), :], zsem)

        def tissue(c, carry):
            tail_copy(c).start()
            return carry

        def twait(c, carry):
            tail_copy(c).wait()
            return carry

        lo, hi = pad_ref[1, N_EXPERTS - 1] // zrows, xs_ref.shape[0] // zrows
        lax.fori_loop(lo, hi, tissue, 0)
        lax.fori_loop(lo, hi, twait, 0)

    def drain(r, carry):
        row_copy(r, 0).wait()
        row_copy(r, 1).wait()
        return carry

    lax.fori_loop(0, tm, drain, 0)


def _dispatch(pad_rows, dest, hn, *, n_slots, tm):
    T, D = hn.shape
    return pl.pallas_call(
        _dispatch_kernel,
        grid_spec=pltpu.PrefetchScalarGridSpec(
            num_scalar_prefetch=1,
            grid=(T // tm,),
            in_specs=[pl.BlockSpec((2, tm), lambda i, pad: (0, i), memory_space=pltpu.SMEM),
                      pl.BlockSpec((tm, D), lambda i, pad: (i, 0))],
            out_specs=pl.BlockSpec(memory_space=pl.ANY),
            scratch_shapes=[pltpu.VMEM((64, D), F32), pltpu.SemaphoreType.DMA, pltpu.SemaphoreType.DMA]),
        out_shape=jax.ShapeDtypeStruct((n_slots, D), F32),
        compiler_params=_cparams(("arbitrary",)),
        name="moe_dispatch",
    )(pad_rows, dest, hn)


def _experts_kernel(te_ref, na_ref, xs_ref, wgu_ref, wd_ref, y_ref, acc_sc, *, tf):
    @pl.when(pl.program_id(0) < na_ref[0])
    def _():
        _swiglu_tile(xs_ref[...].astype(BF16), wgu_ref, wd_ref, acc_sc, tf)
        y_ref[...] = acc_sc[...]

    @pl.when(pl.program_id(0) >= na_ref[0])
    def _():
        y_ref[...] = jnp.zeros(y_ref.shape, F32)


def _experts(tile_expert, n_active, xs, w_gu, w_d, *, tm, tf):
    P, D = xs.shape

    def tile(j, te, na):
        return jnp.minimum(j, na[0] - 1)

    return pl.pallas_call(
        functools.partial(_experts_kernel, tf=tf),
        grid_spec=pltpu.PrefetchScalarGridSpec(
            num_scalar_prefetch=2,
            grid=(P // tm,),
            in_specs=[pl.BlockSpec((tm, D), lambda j, te, na: (tile(j, te, na), 0)),
                      pl.BlockSpec((None, D, 2 * D_FF), lambda j, te, na: (te[tile(j, te, na)], 0, 0)),
                      pl.BlockSpec((None, D_FF, D), lambda j, te, na: (te[tile(j, te, na)], 0, 0))],
            out_specs=pl.BlockSpec((tm, D), lambda j, te, na: (j, 0)),
            scratch_shapes=[pltpu.VMEM((tm, D), F32)]),
        out_shape=jax.ShapeDtypeStruct((P, D), F32),
        compiler_params=_cparams(("arbitrary",)),
        name="moe_experts",
    )(tile_expert, n_active, xs, w_gu, w_d)


def _combine_kernel(dest_ref, y_ref, gates_ref, h_ref, gt_ref, gfin_ref, o_ref, ybuf, sem):
    tm = h_ref.shape[0]

    def row_copy(r, k):
        return pltpu.make_async_copy(y_ref.at[pl.ds(dest_ref[k, r], 1), :], ybuf.at[k, pl.ds(r, 1), :], sem)

    def issue(r, carry):
        row_copy(r, 0).start()
        row_copy(r, 1).start()
        return carry

    def drain(r, carry):
        row_copy(r, 0).wait()
        row_copy(r, 1).wait()
        return carry

    lax.fori_loop(0, tm, issue, 0)
    lax.fori_loop(0, tm, drain, 0)
    gates = gates_ref[...]
    moe = gates[:, 0:1] * ybuf[0] + gates[:, 1:2] * ybuf[1]
    y = h_ref[...] + gt_ref[...] * moe
    o_ref[...] = y * _rms_scale(y) * gfin_ref[...]


def _combine(dest, y, gates, h, gt, gfin, *, tm, tiles_per_batch):
    T, D = h.shape
    return pl.pallas_call(
        _combine_kernel,
        grid=(T // tm,),
        in_specs=[pl.BlockSpec((2, tm), lambda i: (0, i), memory_space=pltpu.SMEM),
                  pl.BlockSpec(memory_space=pl.ANY),
                  pl.BlockSpec((tm, 2), lambda i: (i, 0)),
                  pl.BlockSpec((tm, D), lambda i: (i, 0)),
                  pl.BlockSpec((None, 1, D), lambda i: (i // tiles_per_batch, 0, 0)),
                  pl.BlockSpec((1, D), lambda i: (0, 0))],
        out_specs=pl.BlockSpec((tm, D), lambda i: (i, 0)),
        out_shape=jax.ShapeDtypeStruct((T, D), F32),
        scratch_shapes=[pltpu.VMEM((2, tm, D), F32), pltpu.SemaphoreType.DMA],
        compiler_params=_cparams(("arbitrary",)),
        name="moe_combine",
    )(dest, y, gates, h, gt, gfin)


def _moe_plan(eids, ranks, counts, *, tm, n_tiles):
    padded = (counts + tm - 1) // tm * tm
    ends = jnp.cumsum(padded)
    starts = ends - padded
    dest = ranks
    for e in range(N_EXPERTS):
        dest = dest + jnp.where(eids == e, starts[e], 0)
    tile_start = jnp.arange(n_tiles, dtype=jnp.int32) * tm
    tile_expert = jnp.minimum(jnp.sum(tile_start[:, None] >= ends[None, :], axis=1), N_EXPERTS - 1)
    n_active = (ends[-1] // tm).reshape(1)
    pad_rows = jnp.stack([starts + counts, ends])
    return dest.astype(jnp.int32), tile_expert.astype(jnp.int32), n_active.astype(jnp.int32), pad_rows.astype(jnp.int32)


def _mla_weights(w_in, w_q_up, w_kv_up):
    D = w_in.shape[0]
    half = MLA_ROPE // 2
    w_ql = w_in[:, :MLA_Q_RANK]
    w_kvl = w_in[:, MLA_Q_RANK:MLA_Q_RANK + MLA_KV_RANK]
    w_kr = w_in[:, MLA_Q_RANK + MLA_KV_RANK:]
    z64 = jnp.zeros((D, MLA_NOPE), F32)
    z32 = jnp.zeros((D, HEAD_PAD - MLA_QK), F32)
    kr_pad = jnp.concatenate([z64, w_kr, z32], axis=1)
    kr_swap = jnp.concatenate([z64, w_kr[:, half:], w_kr[:, :half], z32], axis=1)
    w1 = jnp.concatenate([w_ql, w_kvl, kr_pad, kr_swap], axis=1).astype(BF16)

    wq = w_q_up.reshape(MLA_Q_RANK, MLA_HEADS, MLA_QK)
    nope, rope = wq[..., :MLA_NOPE], wq[..., MLA_NOPE:]
    zq = jnp.zeros((MLA_Q_RANK, MLA_HEADS, HEAD_PAD - MLA_QK), F32)
    wqa = jnp.concatenate([nope, rope, zq], axis=-1).reshape(MLA_Q_RANK, MLA_HEADS * HEAD_PAD).astype(BF16)
    wqb = jnp.concatenate([jnp.zeros_like(nope), rope[..., half:], rope[..., :half], zq], axis=-1)
    wqb = wqb.reshape(MLA_Q_RANK, MLA_HEADS * HEAD_PAD).astype(BF16)

    wkv = w_kv_up.reshape(MLA_KV_RANK, MLA_HEADS, MLA_NOPE + MLA_V)
    k_nope, v = wkv[..., :MLA_NOPE], wkv[..., MLA_NOPE:]
    zk = jnp.zeros((MLA_KV_RANK, MLA_HEADS, HEAD_PAD - MLA_NOPE), F32)
    wka = jnp.concatenate([k_nope, zk], axis=-1).reshape(MLA_KV_RANK, MLA_HEADS * HEAD_PAD).astype(BF16)
    wv = v.reshape(MLA_KV_RANK, MLA_HEADS * MLA_V).astype(BF16)
    return w1, wqa, wqb, wka, wv


def _rope_tables(positions):
    inv_freq = ROPE_THETA ** (-jnp.arange(0, MLA_ROPE, 2, dtype=F32) / MLA_ROPE)
    ang = positions.astype(F32).reshape(-1, 1) * inv_freq
    cos, sin = jnp.cos(ang), jnp.sin(ang)
    T = ang.shape[0]
    one = jnp.ones((T, MLA_NOPE), F32)
    z64 = jnp.zeros((T, MLA_NOPE), F32)
    z32 = jnp.zeros((T, HEAD_PAD - MLA_QK), F32)
    return (jnp.concatenate([one, cos, cos, z32], axis=1),
            jnp.concatenate([z64, -sin, sin, z32], axis=1))


def kernel(x, c, positions, g_mix, g_ffn, w_ada, b_ada, w_mla_in, g_mla_q, w_mla_q_up, g_mla_kv, w_mla_kv_up,
           w_mla_out, g_kv_b, w_ada_kv, b_ada_kv, w_kv_b, rpb_table, w_q_b, w_o_b, w_ffn_gu, w_ffn_down,
           w_router, w_exp_gu, w_exp_down, g_final):
    B, S, D = x.shape
    T = B * S
    h = x.reshape(T, D)

    c8 = jnp.zeros((8, D), F32).at[:B].set(c)
    mod = _ada(c8, w_ada, b_ada[:, None, :])[:, :B]
    mod = mod.reshape(2, B, N_MOD, 1, D)
    mod_kv = _ada(c8, w_ada_kv[None], b_ada_kv[None, None, :])[0, :B].reshape(B, 2, 1, D)
    sh_kv, sc_kv = mod_kv[:, 0], mod_kv[:, 1]

    def mods(layer):
        return [mod[layer, :, k] for k in range(N_MOD)]

    row = lambda v: v.reshape(1, -1)

    sh_m, sc_m, gt_m, sh_f, sc_f, gt_f = mods(0)
    w1, wqa, wqb, wka, wv = _mla_weights(w_mla_in[0], w_mla_q_up[0], w_mla_kv_up[0])
    cos_t, sin_t = _rope_tables(positions)
    tm = 256
    q, k, v = _mla_proj(h, row(g_mix[0]), sc_m, sh_m, cos_t, sin_t, w1, row(g_mla_q[0]), row(g_mla_kv[0]),
                        wqa, wqb, wka, wv, tm=tm, tiles_per_batch=S // tm)
    o = _mla_attn(q.reshape(B, S, -1), k.reshape(B, S, -1), v.reshape(B, S, -1), batch=B, seq=S, tq=512)

    tm = 512
    h = _ffn(o.reshape(T, -1), w_mla_out[0].astype(BF16), h, gt_m, row(g_ffn[0]), sc_f, sh_f, gt_f,
             w_ffn_gu[0].astype(BF16), w_ffn_down[0].astype(BF16), tm=tm, tf=256, tiles_per_batch=S // tm)

    sh_m, sc_m, gt_m, sh_f, sc_f, gt_f = mods(1)
    tm = 512
    kv, q = _l1_proj(h, row(g_kv_b), sc_kv, sh_kv, row(g_mix[1]), sc_m, sh_m,
                     w_kv_b.astype(BF16), w_q_b[0].astype(BF16), tm=tm, tiles_per_batch=S // tm)

    outs, lses = [], []
    for g, (window, dil) in enumerate(DIL_PATTERNS):
        bias = _dil_bias(rpb_table[:, g * DIL_HEADS:(g + 1) * DIL_HEADS], window // dil, dil)
        o_g, lse_g = _dil_attn(q, kv, bias, group=g, dil=dil, batch=B, seq=S)
        outs.append(o_g)
        lses.append(lse_g)
    h = _dil_out(outs, lses, w_o_b[0].astype(BF16), h, gt_m, tm=tm, tiles_per_batch=S // tm)

    hn, eids, ranks, gates, counts = _route(h, row(g_ffn[1]), sc_f, sh_f, w_router[0].T, tm=tm,
                                            tiles_per_batch=S // tm)
    n_tiles = (T * 2) // MOE_TM + N_EXPERTS
    dest, tile_expert, n_active, pad_rows = _moe_plan(eids, ranks, counts[:, 0], tm=MOE_TM, n_tiles=n_tiles)
    xs = _dispatch(pad_rows, dest, hn, n_slots=n_tiles * MOE_TM, tm=tm)
    y = _experts(tile_expert, n_active, xs, w_exp_gu[0].astype(BF16), w_exp_down[0].astype(BF16),
                 tm=MOE_TM, tf=256)
    tm = 256
    out = _combine(dest, y, gates.T, h, gt_f, row(g_final), tm=tm, tiles_per_batch=S // tm)
    return out.reshape(B, S, D)
```

```python
import functools
import math

import numpy as np
import jax
import jax.numpy as jnp
from jax import lax
from jax.experimental import pallas as pl
from jax.experimental.pallas import tpu as pltpu

D_MODEL = 1024
N_MOD = 6
EPS = 1e-6

MLA_HEADS = 16
MLA_Q_RANK = 384
MLA_KV_RANK = 256
MLA_NOPE = 64
MLA_ROPE = 32
MLA_V = 64
MLA_QK = MLA_NOPE + MLA_ROPE
ROPE_THETA = 10000.0
LANES = 128
HEAD_PAD = 128

DIL_PATTERNS = ((128, 1), (512, 4), (2048, 16))
DIL_GROUPS = len(DIL_PATTERNS)
DIL_HEADS = 8
DIL_HEAD_DIM = 64
DIL_BLOCK = 128
DIL_W = DIL_HEADS * DIL_HEAD_DIM
RPB_BUCKETS = 32
RPB_MAX_DIST = 2048

D_FF = 2816
N_EXPERTS = 8
MOE_TM = 512

MASK_VALUE = -1e30
LOG2E = math.log2(math.e)

F32 = jnp.float32
BF16 = jnp.bfloat16

VMEM_LIMIT = 56 * 1024 * 1024


def _cparams(sem):
    return pltpu.CompilerParams(dimension_semantics=sem, vmem_limit_bytes=VMEM_LIMIT)


def _dot(a, b):
    return jnp.dot(a, b, preferred_element_type=F32)


def _dot_nt(a, b):
    return lax.dot_general(a, b, (((1,), (1,)), ((), ())), preferred_element_type=F32)


def _rms_scale(x):
    return lax.rsqrt(jnp.mean(x * x, axis=-1, keepdims=True) + EPS)


def _normmod(x, g, sc, sh):
    return (x * _rms_scale(x)) * (g * (1.0 + sc)) + sh


def _silu(x):
    return x * (1.0 / (1.0 + jnp.exp(-x)))


def _ada_kernel(c_ref, w_ref, b_ref, o_ref):
    c = c_ref[...]
    o_ref[...] = jnp.dot(_silu(c), w_ref[...], preferred_element_type=F32,
                         precision=lax.Precision.HIGHEST) + b_ref[...]


def _ada(c8, w, b, tn=512):
    L, D, N = w.shape
    return pl.pallas_call(
        _ada_kernel,
        grid=(L, N // tn),
        in_specs=[pl.BlockSpec((8, D), lambda l, j: (0, 0)),
                  pl.BlockSpec((None, D, tn), lambda l, j: (l, 0, j)),
                  pl.BlockSpec((None, 1, tn), lambda l, j: (l, 0, j))],
        out_specs=pl.BlockSpec((None, 8, tn), lambda l, j: (l, 0, j)),
        out_shape=jax.ShapeDtypeStruct((L, 8, N), F32),
        compiler_params=_cparams(("parallel", "parallel")),
        name="ada_mod",
    )(c8, w, b)


def _mla_proj_kernel(h_ref, g_ref, sc_ref, sh_ref, cos_ref, sin_ref, w1_ref, gq_ref, gkv_ref,
                     wqa_ref, wqb_ref, wka_ref, wv_ref, q_ref, k_ref, v_ref, *, qscale):
    hn = _normmod(h_ref[...], g_ref[...], sc_ref[...], sh_ref[...]).astype(BF16)
    z = _dot(hn, w1_ref[...])
    ql = z[:, :MLA_Q_RANK]
    kvl = z[:, MLA_Q_RANK:MLA_Q_RANK + MLA_KV_RANK]
    kr = z[:, MLA_Q_RANK + MLA_KV_RANK:MLA_Q_RANK + MLA_KV_RANK + HEAD_PAD]
    krs = z[:, MLA_Q_RANK + MLA_KV_RANK + HEAD_PAD:]
    qn = (ql * _rms_scale(ql) * gq_ref[...]).astype(BF16)
    kvn = (kvl * _rms_scale(kvl) * gkv_ref[...]).astype(BF16)
    cos = cos_ref[...]
    sin = sin_ref[...]
    k_rope = kr * cos + krs * sin
    cos2 = jnp.concatenate([cos, cos], axis=1)
    sin2 = jnp.concatenate([sin, sin], axis=1)
    k_rope2 = jnp.concatenate([k_rope, k_rope], axis=1)
    for hp in range(MLA_HEADS // 2):
        sl = slice(2 * HEAD_PAD * hp, 2 * HEAD_PAD * (hp + 1))
        a = _dot(qn, wqa_ref[:, sl])
        b = _dot(qn, wqb_ref[:, sl])
        q_ref[:, sl] = ((a * cos2 + b * sin2) * qscale).astype(BF16)
        k_ref[:, sl] = (_dot(kvn, wka_ref[:, sl]) + k_rope2).astype(BF16)
    v_ref[...] = _dot(kvn, wv_ref[...]).astype(BF16)


def _mla_proj(h, g, sc, sh, cos_t, sin_t, w1, gq, gkv, wqa, wqb, wka, wv, *, tm, tiles_per_batch):
    T, D = h.shape
    HP = MLA_HEADS * HEAD_PAD
    row = lambda n: pl.BlockSpec((tm, n), lambda i: (i, 0))
    bvec = pl.BlockSpec((None, 1, D), lambda i: (i // tiles_per_batch, 0, 0))
    full = lambda a: pl.BlockSpec(a.shape, lambda i: (0,) * a.ndim)
    return pl.pallas_call(
        functools.partial(_mla_proj_kernel, qscale=(MLA_QK ** -0.5) * LOG2E),
        grid=(T // tm,),
        in_specs=[row(D), full(g), bvec, bvec, row(HEAD_PAD), row(HEAD_PAD), full(w1), full(gq), full(gkv),
                  full(wqa), full(wqb), full(wka), full(wv)],
        out_specs=[row(HP), row(HP), row(MLA_HEADS * MLA_V)],
        out_shape=[jax.ShapeDtypeStruct((T, HP), BF16), jax.ShapeDtypeStruct((T, HP), BF16),
                   jax.ShapeDtypeStruct((T, MLA_HEADS * MLA_V), BF16)],
        compiler_params=_cparams(("parallel",)),
        name="mla_proj",
    )(h, g, sc, sh, cos_t, sin_t, w1, gq, gkv, wqa, wqb, wka, wv)


def _mla_attn_kernel(q_ref, k_ref, v_ref, o_ref, m_sc, acc_sc, p_sc, alpha_sc, *, tq):
    qi = pl.program_id(2)
    m_sc[...] = jnp.full(m_sc.shape, -jnp.inf, F32)
    acc_sc[...] = jnp.zeros(acc_sc.shape, F32)
    nrep = tq // HEAD_PAD

    def scores(j, diagonal):
        off = pl.multiple_of(j * tq, tq)
        ss = []
        for hh in range(2):
            q = q_ref[:, HEAD_PAD * hh:HEAD_PAD * (hh + 1)]
            k = k_ref[pl.ds(off, tq), HEAD_PAD * hh:HEAD_PAD * (hh + 1)]
            s = _dot_nt(q, k)
            if diagonal:
                r = lax.broadcasted_iota(jnp.int32, (tq, tq), 0)
                c = lax.broadcasted_iota(jnp.int32, (tq, tq), 1)
                s = jnp.where(r >= c, s, -jnp.inf)
            ss.append(s)
        return ss

    def softmax(ss):
        for hh in range(2):
            m_prev = m_sc[hh]
            m_new = jnp.maximum(m_prev, jnp.max(ss[hh], axis=1, keepdims=True))
            alpha_sc[hh] = jnp.exp2(m_prev - m_new)
            p_sc[hh] = jnp.exp2(ss[hh] - jnp.tile(m_new, (1, nrep))).astype(BF16)
            m_sc[hh] = m_new

    def apply_values(j):
        off = pl.multiple_of(j * tq, tq)
        v = v_ref[pl.ds(off, tq), :]
        v1 = jnp.concatenate([v, jnp.ones_like(v)], axis=1)
        for hh in range(2):
            acc_sc[hh] = jnp.tile(alpha_sc[hh], (1, 2)) * acc_sc[hh] + _dot(p_sc[hh], v1)

    @pl.when(qi == 0)
    def _():
        softmax(scores(0, True))

    @pl.when(qi > 0)
    def _():
        softmax(scores(0, False))

        def body(j, carry):
            ss = scores(j, False)
            apply_values(j - 1)
            softmax(ss)
            return carry

        lax.fori_loop(1, qi, body, 0)
        ss = scores(qi, True)
        apply_values(qi - 1)
        softmax(ss)

    apply_values(qi)
    lane =lax.broadcasted_iota(jnp.int32, (1, HEAD_PAD), 1)
    o0 = acc_sc[0, :, :HEAD_PAD] / acc_sc[0, :, HEAD_PAD:]
    o1 = acc_sc[1, :, :HEAD_PAD] / acc_sc[1, :, HEAD_PAD:]
    o_ref[...] = jnp.where(lane < MLA_V, o0, o1).astype(BF16)


def _mla_attn(q, k, v, *, batch, seq, tq):
    return pl.pallas_call(
        functools.partial(_mla_attn_kernel, tq=tq),
        grid=(batch, MLA_HEADS // 2, seq // tq),
        in_specs=[pl.BlockSpec((None, tq, 2 * HEAD_PAD), lambda b, hp, i: (b, i, hp)),
                  pl.BlockSpec((None, seq, 2 * HEAD_PAD), lambda b, hp, i: (b, 0, hp)),
                  pl.BlockSpec((None, seq, 2 * MLA_V), lambda b, hp, i: (b, 0, hp))],
        out_specs=pl.BlockSpec((None, tq, 2 * MLA_V), lambda b, hp, i: (b, i, hp)),
        out_shape=jax.ShapeDtypeStruct((batch, seq, MLA_HEADS * MLA_V), BF16),
        scratch_shapes=[pltpu.VMEM((2, tq, HEAD_PAD), F32), pltpu.VMEM((2, tq, 2 * HEAD_PAD), F32),
                        pltpu.VMEM((2, tq, tq), BF16), pltpu.VMEM((2, tq, HEAD_PAD), F32)],
        compiler_params=_cparams(("parallel", "parallel", "arbitrary")),
        name="mla_attn",
    )(q, k, v)


def _swiglu_tile(x, wgu_ref, wd_ref, acc_sc, tf):
    for f in range(D_FF // tf):
        g = _dot(x, wgu_ref[:, f * tf:(f + 1) * tf])
        u = _dot(x, wgu_ref[:, D_FF + f * tf:D_FF + (f + 1) * tf])
        part = _dot((_silu(g) * u).astype(BF16), wd_ref[f * tf:(f + 1) * tf, :])
        if f == 0:
            acc_sc[...] = part
        else:
            acc_sc[...] += part


def _ffn_kernel(a_ref, wo_ref, h_ref, gtm_ref, g_ref, sc_ref, sh_ref, gt_ref, wgu_ref, wd_ref, o_ref, acc_sc,
                *, tf):
    h = h_ref[...] + gtm_ref[...] * _dot(a_ref[...], wo_ref[...])
    x = _normmod(h, g_ref[...], sc_ref[...], sh_ref[...]).astype(BF16)
    _swiglu_tile(x, wgu_ref, wd_ref, acc_sc, tf)
    o_ref[...] = h + gt_ref[...] * acc_sc[...]


def _ffn(a, w_o, h, gt_m, g, sc, sh, gt, w_gu, w_d, *, tm, tf, tiles_per_batch):
    T, D = h.shape
    bvec = pl.BlockSpec((None, 1, D), lambda i: (i // tiles_per_batch, 0, 0))
    row = pl.BlockSpec((tm, D), lambda i: (i, 0))
    const = lambda w: pl.BlockSpec(w.shape, lambda i: (0,) * w.ndim, pipeline_mode=pl.Buffered(1))
    return pl.pallas_call(
        functools.partial(_ffn_kernel, tf=tf),
        grid=(T // tm,),
        in_specs=[row, const(w_o), row, bvec, const(g), bvec, bvec, bvec, const(w_gu), const(w_d)],
        out_specs=row,
        out_shape=jax.ShapeDtypeStruct((T, D), F32),
        scratch_shapes=[pltpu.VMEM((tm, D), F32)],
        compiler_params=_cparams(("parallel",)),
        name="ffn_dense",
    )(a, w_o, h, gt_m, g, sc, sh, gt, w_gu, w_d)


def _l1_proj_kernel(h_ref, gkv_ref, sckv_ref, shkv_ref, gq_ref, scq_ref, shq_ref, wkv_ref, wq_ref, *refs):
    out_refs, stage_sc = refs[:-1], refs[-1]
    tm = h_ref.shape[0]
    x = h_ref[...]
    xh = x * _rms_scale(x)
    hn_kv = (xh * (gkv_ref[...] * (1.0 + sckv_ref[...])) + shkv_ref[...]).astype(BF16)
    hn_q = (xh * (gq_ref[...] * (1.0 + scq_ref[...])) + shq_ref[...]).astype(BF16)
    for g, (_, dil) in enumerate(DIL_PATTERNS):
        ys = (_dot(hn_q, wq_ref[:, g * DIL_W:(g + 1) * DIL_W]) * (DIL_HEAD_DIM ** -0.5),
              _dot(hn_kv, wkv_ref[:, 2 * g * DIL_W:(2 * g + 1) * DIL_W]),
              _dot(hn_kv, wkv_ref[:, (2 * g + 1) * DIL_W:(2 * g + 2) * DIL_W]))
        for y, out_ref in zip(ys, out_refs[3 * g:3 * g + 3]):
            if dil == 1:
                out_ref[...] = y.astype(BF16)
            else:
                for c in range(DIL_W // LANES):
                    stage_sc[c] = y[:, c * LANES:(c + 1) * LANES]
                for r in range(dil):
                    for c in range(DIL_W // LANES):
                        col = r * DIL_W + c * LANES
                        out_ref[:, col:col + LANES] = stage_sc[c, pl.ds(r, tm // dil, stride=dil), :].astype(BF16)


def _l1_proj(h, gkv, sckv, shkv, gq, scq, shq, wkv, wq, *, tm, tiles_per_batch):
    T, D = h.shape
    bvec = pl.BlockSpec((None, 1, D), lambda i: (i // tiles_per_batch, 0, 0))
    full = lambda a: pl.BlockSpec(a.shape, lambda i: (0,) * a.ndim)
    out_specs, out_shape = [], []
    for _, dil in DIL_PATTERNS:
        for _ in range(3):
            out_specs.append(pl.BlockSpec((tm // dil, dil * DIL_W), lambda i: (i, 0)))
            out_shape.append(jax.ShapeDtypeStruct((T // dil, dil * DIL_W), BF16))
    return pl.pallas_call(
        _l1_proj_kernel,
        grid=(T // tm,),
        in_specs=[pl.BlockSpec((tm, D), lambda i: (i, 0)), full(gkv), bvec, bvec, full(gq), bvec, bvec,
                  full(wkv), full(wq)],
        out_specs=out_specs,
        out_shape=out_shape,
        scratch_shapes=[pltpu.VMEM((DIL_W // LANES, tm, LANES), F32)],
        compiler_params=_cparams(("parallel",)),
        name="l1_proj",
    )(h, gkv, sckv, shkv, gq, scq, shq, wkv, wq)


def _dil_attn_kernel(q_ref, kc_ref, kp_ref, vc_ref, vp_ref, bias_ref, o_ref, lse_ref):
    i = pl.program_id(2)
    lane = lax.broadcasted_iota(jnp.int32, (1, 2 * DIL_HEAD_DIM), 1)
    lo = lane < DIL_HEAD_DIM
    col = lax.broadcasted_iota(jnp.int32, (1, 2 * DIL_BLOCK), 1)
    edge = jnp.where((col < DIL_BLOCK) & (i == 0), MASK_VALUE, 0.0).astype(F32)
    for hp in range(DIL_HEADS // 2):
        sl = slice(2 * DIL_HEAD_DIM * hp, 2 * DIL_HEAD_DIM * (hp + 1))
        q2 = q_ref[:, sl]
        k2 = jnp.concatenate([kp_ref[:, sl], kc_ref[:, sl]], axis=0)
        v2 = jnp.concatenate([vp_ref[:, sl], vc_ref[:, sl]], axis=0)
        outs, lses = [], []
        for hh in range(2):
            keep = lo if hh == 0 else jnp.logical_not(lo)
            qm = jnp.where(keep, q2, jnp.zeros_like(q2))
            s = _dot_nt(qm, k2) + bias_ref[2 * hp + hh] + edge
            m = jnp.max(s, axis=1, keepdims=True)
            p = jnp.exp(s - m)
            l = jnp.sum(p, axis=1, keepdims=True)
            outs.append(_dot(p.astype(BF16), v2) / l)
            lses.append(m + jnp.log(l))
        o_ref[:, sl] = jnp.where(lo, outs[0], outs[1])
        lse_ref[:, sl] = jnp.where(lo, lses[0], lses[1])


def _dil_attn(q, k, v, bias, *, group, dil, batch, seq):
    n = seq // dil
    nb = n // DIL_BLOCK
    q, k, v = (a.reshape(batch, n, dil * DIL_W) for a in (q, k, v))
    blk = (None, DIL_BLOCK, DIL_W)
    cur = pl.BlockSpec(blk, lambda b, r, i: (b, i, r))
    prev = pl.BlockSpec(blk, lambda b, r, i: (b, jnp.maximum(i - 1, 0), r))
    out_sd = jax.ShapeDtypeStruct((batch, n, dil * DIL_W), F32)
    o, lse = pl.pallas_call(
        _dil_attn_kernel,
        grid=(batch, dil, nb),
        in_specs=[cur, cur, prev, cur, prev, pl.BlockSpec(bias.shape, lambda b, r, i: (0, 0, 0))],
        out_specs=[cur, cur],
        out_shape=[out_sd, out_sd],
        compiler_params=_cparams(("parallel", "parallel", "arbitrary")),
        name=f"dil_attn_g{group}",
    )(q, k, k, v, v, bias)
    return o.reshape(batch * n, dil * DIL_W), lse.reshape(batch * n, dil * DIL_W)


def _rpb_bucket(dist):
    exact = RPB_BUCKETS // 2
    d = jnp.maximum(dist, 0)
    d_f = jnp.maximum(d, 1).astype(F32)
    large = exact + (jnp.log(d_f / exact) / math.log(RPB_MAX_DIST / exact)
                     * (RPB_BUCKETS - exact)).astype(jnp.int32)
    return jnp.where(d < exact, d, jnp.minimum(large, RPB_BUCKETS - 1))


def _dil_bias(table, span, dil):
    period = 3 * DIL_BLOCK
    j = jnp.arange(period)
    k_minus_q = jnp.where(j < 2 * DIL_BLOCK, j, j - period)
    dist = DIL_BLOCK - k_minus_q
    band = (dist >= 0) & (dist <= span)
    prof = jnp.where(band[:, None], table[_rpb_bucket(dist * dil)].astype(F32), MASK_VALUE).T
    skew = jnp.tile(prof, (1, DIL_BLOCK))[:, :DIL_BLOCK * (period - 1)]
    return skew.reshape(DIL_HEADS, DIL_BLOCK, period - 1)[:, :, :2 * DIL_BLOCK]


def _dil_out_kernel(o0_ref, o1_ref, o2_ref, l0_ref, l1_ref, l2_ref, w_ref, h_ref, gt_ref, out_ref, *stages):
    tm = h_ref.shape[0]
    stages = list(stages)

    def token_major(ref, dil):
        if dil == 1:
            return ref[...]
        stage = stages.pop()
        for r in range(dil):
            for c in range(DIL_W // LANES):
                col = r * DIL_W + c * LANES
                stage[c, pl.ds(r, tm // dil, stride=dil), :] = ref[:, col:col + LANES]
        return jnp.concatenate([stage[c] for c in range(DIL_W // LANES)], axis=1)

    dils = [dil for _, dil in DIL_PATTERNS]
    l0, l1, l2 = (token_major(ref, d) for ref, d in zip((l0_ref, l1_ref, l2_ref), dils))
    o0, o1, o2 = (token_major(ref, d) for ref, d in zip((o0_ref, o1_ref, o2_ref), dils))
    m = jnp.maximum(jnp.maximum(l0, l1), l2)
    w0, w1, w2 = jnp.exp(l0 - m), jnp.exp(l1 - m), jnp.exp(l2 - m)
    o = (o0 * w0 + o1 * w1 + o2 * w2) / (w0 + w1 + w2)
    out_ref[...] = h_ref[...] + gt_ref[...] * _dot(o.astype(BF16), w_ref[...])


def _dil_out(os_, ls_, w, h, gt, *, tm, tiles_per_batch):
    T, D = h.shape
    row = lambda n: pl.BlockSpec((tm, n), lambda i: (i, 0))
    grp = [pl.BlockSpec((tm // dil, dil * DIL_W), lambda i: (i, 0)) for _, dil in DIL_PATTERNS]
    n_stage = 2 * sum(1 for _, dil in DIL_PATTERNS if dil > 1)
    return pl.pallas_call(
        _dil_out_kernel,
        grid=(T // tm,),
        in_specs=grp + grp + [pl.BlockSpec(w.shape, lambda i: (0, 0)), row(D),
                              pl.BlockSpec((None, 1, D), lambda i: (i // tiles_per_batch, 0, 0))],
        out_specs=row(D),
        out_shape=jax.ShapeDtypeStruct((T, D), F32),
        scratch_shapes=[pltpu.VMEM((DIL_W // LANES, tm, LANES), F32)] * n_stage,
        compiler_params=_cparams(("parallel",)),
        name="dil_out",
    )(*os_, *ls_, w, h, gt)


def _route_kernel(h_ref, g_ref, sc_ref, sh_ref, wrt_ref, hn_ref, eid_ref, rank_ref, gate_ref, cnt_ref, carry_sc):
    @pl.when(pl.program_id(0) == 0)
    def _():
        carry_sc[...] = jnp.zeros(carry_sc.shape, F32)

    hn = _normmod(h_ref[...], g_ref[...], sc_ref[...], sh_ref[...])
    hn_ref[...] = hn
    tm = hn.shape[0]
    logits = lax.dot_general(wrt_ref[...], hn, (((1,), (1,)), ((), ())), preferred_element_type=F32,
                             precision=lax.Precision.HIGHEST)
    idx = lax.broadcasted_iota(jnp.int32, logits.shape, 0)
    v1 = jnp.max(logits, axis=0, keepdims=True)
    i1 = jnp.min(jnp.where(logits == v1, idx, N_EXPERTS), axis=0, keepdims=True)
    rest = jnp.where(idx == i1, -jnp.inf, logits)
    v2 = jnp.max(rest, axis=0, keepdims=True)
    i2 = jnp.min(jnp.where(rest == v2, idx, N_EXPERTS), axis=0, keepdims=True)
    e = jnp.exp(v2 - v1)
    gate_ref[...] = jnp.concatenate([1.0 / (1.0 + e), e / (1.0 + e)], axis=0)
    eid_ref[...] = jnp.concatenate([i1, i2], axis=0)

    sel = ((idx == i1) | (idx == i2)).astype(BF16)
    before = (lax.broadcasted_iota(jnp.int32, (tm, tm), 0)
              < lax.broadcasted_iota(jnp.int32, (tm, tm), 1)).astype(BF16)
    rank_all = carry_sc[:, :1] + _dot(sel, before)
    r1 = jnp.sum(jnp.where(idx == i1, rank_all, 0.0), axis=0, keepdims=True)
    r2 = jnp.sum(jnp.where(idx == i2, rank_all, 0.0), axis=0, keepdims=True)
    rank_ref[...] = jnp.concatenate([r1, r2], axis=0).astype(jnp.int32)
    carry_sc[...] += jnp.sum(sel.astype(F32), axis=1, keepdims=True)
    cnt_ref[...] = carry_sc[...].astype(jnp.int32)


def _route(h, g, sc, sh, wrt, *, tm, tiles_per_batch):
    T, D = h.shape
    bvec = pl.BlockSpec((None, 1, D), lambda i: (i // tiles_per_batch, 0, 0))
    lane_blk = pl.BlockSpec((2, tm), lambda i: (0, i))
    return pl.pallas_call(
        _route_kernel,
        grid=(T // tm,),
        in_specs=[pl.BlockSpec((tm, D), lambda i: (i, 0)), pl.BlockSpec((1, D), lambda i: (0, 0)), bvec, bvec,
                  pl.BlockSpec(wrt.shape, lambda i: (0, 0))],
        out_specs=[pl.BlockSpec((tm, D), lambda i: (i, 0)), lane_blk, lane_blk, lane_blk,
                   pl.BlockSpec((N_EXPERTS, HEAD_PAD), lambda i: (0, 0))],
        out_shape=[jax.ShapeDtypeStruct((T, D), F32), jax.ShapeDtypeStruct((2, T), jnp.int32),
                   jax.ShapeDtypeStruct((2, T), jnp.int32), jax.ShapeDtypeStruct((2, T), F32),
                   jax.ShapeDtypeStruct((N_EXPERTS, HEAD_PAD), jnp.int32)],
        scratch_shapes=[pltpu.VMEM((N_EXPERTS, HEAD_PAD), F32)],
        compiler_params=_cparams(("arbitrary",)),
        name="moe_route",
    )(h, g, sc, sh, wrt)


def _dispatch_kernel(pad_ref, dest_ref, hn_ref, xs_ref, zero_sc, sem, zsem):
    tm = hn_ref.shape[0]

    def row_copy(r, k):
        return pltpu.make_async_copy(hn_ref.at[pl.ds(r, 1), :], xs_ref.at[pl.ds(dest_ref[k, r], 1), :], sem)

    def issue(r, carry):
        row_copy(r, 0).start()
        row_copy(r, 1).start()
        return carry

    lax.fori_loop(0, tm, issue, 0)

    @pl.when(pl.program_id(0) == pl.num_programs(0) - 1)
    def _():
        zero_sc[...] = jnp.zeros(zero_sc.shape, F32)

        def pad_copy(p):
            return pltpu.make_async_copy(zero_sc.at[pl.ds(0, 1), :], xs_ref.at[pl.ds(p, 1), :], zsem)

        for e in range(N_EXPERTS):
            lo, hi = pad_ref[0, e], pad_ref[1, e]

            def zissue(p, carry):
                pad_copy(p).start()
                return carry

            def zwait(p, carry):
                pad_copy(p).wait()
                return carry

            lax.fori_loop(lo, hi, zissue, 0)
            lax.fori_loop(lo, hi, zwait, 0)

        zrows = zero_sc.shape[0]

        def tail_copy(c):
            return pltpu.make_async_copy(zero_sc, xs_ref.at[pl.ds(pl.multiple_of(c * zrows, zrows), zrows), :], zsem)

        def tissue(c, carry):
            tail_copy(c).start()
            return carry

        def twait(c, carry):
            tail_copy(c).wait()
            return carry

        lo, hi = pad_ref[1, N_EXPERTS - 1] // zrows, xs_ref.shape[0] // zrows
        lax.fori_loop(lo, hi, tissue, 0)
        lax.fori_loop(lo, hi, twait, 0)

    for _ in range(2):
        pltpu.make_async_copy(hn_ref, xs_ref.at[pl.ds(0, tm), :], sem).wait()


def _dispatch(pad_rows, dest, hn, *, n_slots, tm):
    T, D = hn.shape
    return pl.pallas_call(
        _dispatch_kernel,
        grid_spec=pltpu.PrefetchScalarGridSpec(
            num_scalar_prefetch=1,
            grid=(T // tm,),
            in_specs=[pl.BlockSpec((2, tm), lambda i, pad: (0, i), memory_space=pltpu.SMEM),
                      pl.BlockSpec((tm, D), lambda i, pad: (i, 0))],
            out_specs=pl.BlockSpec(memory_space=pl.ANY),
            scratch_shapes=[pltpu.VMEM((64, D), F32), pltpu.SemaphoreType.DMA, pltpu.SemaphoreType.DMA]),
        out_shape=jax.ShapeDtypeStruct((n_slots, D), F32),
        compiler_params=_cparams(("arbitrary",)),
        name="moe_dispatch",
    )(pad_rows, dest, hn)


def _experts_kernel(te_ref, na_ref, xs_ref, wgu_ref, wd_ref, y_ref, acc_sc, *, tf):
    @pl.when(pl.program_id(0) < na_ref[0])
    def _():
        _swiglu_tile(xs_ref[...].astype(BF16), wgu_ref, wd_ref, acc_sc, tf)
        y_ref[...] = acc_sc[...]

    @pl.when(pl.program_id(0) >= na_ref[0])
    def _():
        y_ref[...] = jnp.zeros(y_ref.shape, F32)


def _experts(tile_expert, n_active, xs, w_gu, w_d, *, tm, tf):
    P, D = xs.shape

    def tile(j, te, na):
        return jnp.minimum(j, na[0] - 1)

    return pl.pallas_call(
        functools.partial(_experts_kernel, tf=tf),
        grid_spec=pltpu.PrefetchScalarGridSpec(
            num_scalar_prefetch=2,
            grid=(P // tm,),
            in_specs=[pl.BlockSpec((tm, D), lambda j, te, na: (tile(j, te, na), 0)),
                      pl.BlockSpec((None, D, 2 * D_FF), lambda j, te, na: (te[tile(j, te, na)], 0, 0)),
                      pl.BlockSpec((None, D_FF, D), lambda j, te, na: (te[tile(j, te, na)], 0, 0))],
            out_specs=pl.BlockSpec((tm, D), lambda j, te, na: (j, 0)),
            scratch_shapes=[pltpu.VMEM((tm, D), F32)]),
        out_shape=jax.ShapeDtypeStruct((P, D), F32),
        compiler_params=_cparams(("arbitrary",)),
        name="moe_experts",
    )(tile_expert, n_active, xs, w_gu, w_d)


def _combine_kernel(dest_ref, y_ref, gates_ref, h_ref, gt_ref, gfin_ref, o_ref, ybuf, sem):
    tm = h_ref.shape[0]

    def row_copy(r, k):
        return pltpu.make_async_copy(y_ref.at[pl.ds(dest_ref[k, r], 1), :], ybuf.at[k, pl.ds(r, 1), :], sem)

    def issue(r, carry):
        row_copy(r, 0).start()
        row_copy(r, 1).start()
        return carry

    lax.fori_loop(0, tm, issue, 0)
    for k in range(2):
        pltpu.make_async_copy(y_ref.at[pl.ds(0, tm), :], ybuf.at[k], sem).wait()
    gates = gates_ref[...]
    moe = gates[:, 0:1] * ybuf[0] + gates[:, 1:2] * ybuf[1]
    y = h_ref[...] + gt_ref[...] * moe
    o_ref[...] = y * _rms_scale(y) * gfin_ref[...]


def _combine(dest, y, gates, h, gt, gfin, *, tm, tiles_per_batch):
    T, D = h.shape
    return pl.pallas_call(
        _combine_kernel,
        grid=(T // tm,),
        in_specs=[pl.BlockSpec((2, tm), lambda i: (0, i), memory_space=pltpu.SMEM),
                  pl.BlockSpec(memory_space=pl.ANY),
                  pl.BlockSpec((tm, 2), lambda i: (i, 0)),
                  pl.BlockSpec((tm, D), lambda i: (i, 0)),
                  pl.BlockSpec((None, 1, D), lambda i: (i // tiles_per_batch, 0, 0)),
                  pl.BlockSpec((1, D), lambda i: (0, 0))],
        out_specs=pl.BlockSpec((tm, D), lambda i: (i, 0)),
        out_shape=jax.ShapeDtypeStruct((T, D), F32),
        scratch_shapes=[pltpu.VMEM((2, tm, D), F32), pltpu.SemaphoreType.DMA],
        compiler_params=_cparams(("arbitrary",)),
        name="moe_combine",
    )(dest, y, gates, h, gt, gfin)


def _moe_plan(eids, ranks, counts, *, tm, n_tiles):
    padded = (counts + tm - 1) // tm * tm
    ends = jnp.cumsum(padded)
    starts = ends - padded
    dest = ranks
    for e in range(N_EXPERTS):
        dest = dest + jnp.where(eids == e, starts[e], 0)
    tile_start = jnp.arange(n_tiles, dtype=jnp.int32) * tm
    tile_expert = jnp.minimum(jnp.sum(tile_start[:, None] >= ends[None, :], axis=1), N_EXPERTS - 1)
    n_active = (ends[-1] // tm).reshape(1)
    pad_rows = jnp.stack([starts + counts, ends])
    return dest.astype(jnp.int32), tile_expert.astype(jnp.int32), n_active.astype(jnp.int32), pad_rows.astype(jnp.int32)


def _mla_weights(w_in, w_q_up, w_kv_up):
    D = w_in.shape[0]
    half = MLA_ROPE // 2
    w_ql = w_in[:, :MLA_Q_RANK]
    w_kvl = w_in[:, MLA_Q_RANK:MLA_Q_RANK + MLA_KV_RANK]
    w_kr = w_in[:, MLA_Q_RANK + MLA_KV_RANK:]
    z64 = jnp.zeros((D, MLA_NOPE), F32)
    z32 = jnp.zeros((D, HEAD_PAD - MLA_QK), F32)
    kr_pad = jnp.concatenate([z64, w_kr, z32], axis=1)
    kr_swap = jnp.concatenate([z64, w_kr[:, half:], w_kr[:, :half], z32], axis=1)
    w1 = jnp.concatenate([w_ql, w_kvl, kr_pad, kr_swap], axis=1).astype(BF16)

    wq = w_q_up.reshape(MLA_Q_RANK, MLA_HEADS, MLA_QK)
    nope, rope = wq[..., :MLA_NOPE], wq[..., MLA_NOPE:]
    zq = jnp.zeros((MLA_Q_RANK, MLA_HEADS, HEAD_PAD - MLA_QK), F32)
    wqa = jnp.concatenate([nope, rope, zq], axis=-1).reshape(MLA_Q_RANK, MLA_HEADS * HEAD_PAD).astype(BF16)
    wqb = jnp.concatenate([jnp.zeros_like(nope), rope[..., half:], rope[..., :half], zq], axis=-1)
    wqb = wqb.reshape(MLA_Q_RANK, MLA_HEADS * HEAD_PAD).astype(BF16)

    wkv = w_kv_up.reshape(MLA_KV_RANK, MLA_HEADS, MLA_NOPE + MLA_V)
    k_nope, v = wkv[..., :MLA_NOPE], wkv[..., MLA_NOPE:]
    zk = jnp.zeros((MLA_KV_RANK, MLA_HEADS, HEAD_PAD - MLA_NOPE), F32)
    wka = jnp.concatenate([k_nope, zk], axis=-1).reshape(MLA_KV_RANK, MLA_HEADS * HEAD_PAD).astype(BF16)
    wv = v.reshape(MLA_KV_RANK, MLA_HEADS * MLA_V).astype(BF16)
    return w1, wqa, wqb, wka, wv


def _rope_tables(positions):
    inv_freq = ROPE_THETA ** (-jnp.arange(0, MLA_ROPE, 2, dtype=F32) / MLA_ROPE)
    ang = positions.astype(F32).reshape(-1, 1) * inv_freq
    cos, sin = jnp.cos(ang), jnp.sin(ang)
    T = ang.shape[0]
    one = jnp.ones((T, MLA_NOPE), F32)
    z64 = jnp.zeros((T, MLA_NOPE), F32)
    z32 = jnp.zeros((T, HEAD_PAD - MLA_QK), F32)
    return (jnp.concatenate([one, cos, cos, z32], axis=1),
            jnp.concatenate([z64, -sin, sin, z32], axis=1))


def kernel(x, c, positions, g_mix, g_ffn, w_ada, b_ada, w_mla_in, g_mla_q, w_mla_q_up, g_mla_kv, w_mla_kv_up,
           w_mla_out, g_kv_b, w_ada_kv, b_ada_kv, w_kv_b, rpb_table, w_q_b, w_o_b, w_ffn_gu, w_ffn_down,
           w_router, w_exp_gu, w_exp_down, g_final):
    B, S, D = x.shape
    T = B * S
    h = x.reshape(T, D)

    c8 = jnp.zeros((8, D), F32).at[:B].set(c)
    mod = _ada(c8, w_ada, b_ada[:, None, :])[:, :B]
    mod = mod.reshape(2, B, N_MOD, 1, D)
    mod_kv = _ada(c8, w_ada_kv[None], b_ada_kv[None, None, :])[0, :B].reshape(B, 2, 1, D)
    sh_kv, sc_kv = mod_kv[:, 0], mod_kv[:, 1]

    def mods(layer):
        return [mod[layer, :, k] for k in range(N_MOD)]

    row = lambda v: v.reshape(1, -1)

    sh_m, sc_m, gt_m, sh_f, sc_f, gt_f = mods(0)
    w1, wqa, wqb, wka, wv = _mla_weights(w_mla_in[0], w_mla_q_up[0], w_mla_kv_up[0])
    cos_t, sin_t = _rope_tables(positions)
    tm = 256
    q, k, v = _mla_proj(h, row(g_mix[0]), sc_m, sh_m, cos_t, sin_t, w1, row(g_mla_q[0]), row(g_mla_kv[0]),
                        wqa, wqb, wka, wv, tm=tm, tiles_per_batch=S // tm)
    o = _mla_attn(q.reshape(B, S, -1), k.reshape(B, S, -1), v.reshape(B, S, -1), batch=B, seq=S, tq=512)

    tm = 512
    h = _ffn(o.reshape(T, -1), w_mla_out[0].astype(BF16), h, gt_m, row(g_ffn[0]), sc_f, sh_f, gt_f,
             w_ffn_gu[0].astype(BF16), w_ffn_down[0].astype(BF16), tm=tm, tf=256, tiles_per_batch=S // tm)

    sh_m, sc_m, gt_m, sh_f, sc_f, gt_f = mods(1)
    tm = 512
    qkv = _l1_proj(h, row(g_kv_b), sc_kv, sh_kv, row(g_mix[1]), sc_m, sh_m,
                   w_kv_b.astype(BF16), w_q_b[0].astype(BF16), tm=tm, tiles_per_batch=S // tm)

    outs, lses = [], []
    for g, (window, dil) in enumerate(DIL_PATTERNS):
        bias = _dil_bias(rpb_table[:, g * DIL_HEADS:(g + 1) * DIL_HEADS], window // dil, dil)
        o_g, lse_g = _dil_attn(*qkv[3 * g:3 * g + 3], bias, group=g, dil=dil, batch=B, seq=S)
        outs.append(o_g)
        lses.append(lse_g)
    h = _dil_out(outs, lses, w_o_b[0].astype(BF16), h, gt_m, tm=tm, tiles_per_batch=S // tm)

    hn, eids, ranks, gates, counts = _route(h, row(g_ffn[1]), sc_f, sh_f, w_router[0].T, tm=tm,
                                            tiles_per_batch=S // tm)
    n_tiles = (T * 2) // MOE_TM + N_EXPERTS
    dest, tile_expert, n_active, pad_rows = _moe_plan(eids, ranks, counts[:, 0], tm=MOE_TM, n_tiles=n_tiles)
    xs = _dispatch(pad_rows, dest, hn, n_slots=n_tiles * MOE_TM, tm=tm)
    y = _experts(tile_expert, n_active, xs, w_exp_gu[0].astype(BF16), w_exp_down[0].astype(BF16),
                 tm=MOE_TM, tf=256)
    tm = 256
    out = _combine(dest, y, gates.T, h, gt_f, row(g_final), tm=tm, tiles_per_batch=S // tm)
    return out.reshape(B, S, D)
```

```python
import functools
import math

import numpy as np
import jax
import jax.numpy as jnp
from jax import lax
from jax.experimental import pallas as pl
from jax.experimental.pallas import tpu as pltpu

D_MODEL = 1024
N_MOD = 6
EPS = 1e-6

MLA_HEADS = 16
MLA_Q_RANK = 384
MLA_KV_RANK = 256
MLA_NOPE = 64
MLA_ROPE = 32
MLA_V = 64
MLA_QK = MLA_NOPE + MLA_ROPE
ROPE_THETA = 10000.0
LANES = 128
HEAD_PAD = 128

DIL_PATTERNS = ((128, 1), (512, 4), (2048, 16))
DIL_GROUPS = len(DIL_PATTERNS)
DIL_HEADS = 8
DIL_HEAD_DIM = 64
DIL_BLOCK = 128
DIL_W = DIL_HEADS * DIL_HEAD_DIM
RPB_BUCKETS = 32
RPB_MAX_DIST = 2048

D_FF = 2816
N_EXPERTS = 8
MOE_TM = 512

MASK_VALUE = -1e30
LOG2E = math.log2(math.e)

F32 = jnp.float32
BF16 = jnp.bfloat16

VMEM_LIMIT = 56 * 1024 * 1024


def _cparams(sem):
    return pltpu.CompilerParams(dimension_semantics=sem, vmem_limit_bytes=VMEM_LIMIT)


def _dot(a, b):
    return jnp.dot(a, b, preferred_element_type=F32)


def _dot_nt(a, b):
    return lax.dot_general(a, b, (((1,), (1,)), ((), ())), preferred_element_type=F32)


def _rms_scale(x):
    return lax.rsqrt(jnp.mean(x * x, axis=-1, keepdims=True) + EPS)


def _normmod(x, g, sc, sh):
    return (x * _rms_scale(x)) * (g * (1.0 + sc)) + sh


def _silu(x):
    return x * (1.0 / (1.0 + jnp.exp(-x)))


def _ada_kernel(c_ref, w_ref, b_ref, o_ref):
    c = c_ref[...]
    o_ref[...] = jnp.dot(_silu(c), w_ref[...], preferred_element_type=F32,
                         precision=lax.Precision.HIGHEST) + b_ref[...]


def _ada(c8, w, b, tn=512):
    L, D, N = w.shape
    return pl.pallas_call(
        _ada_kernel,
        grid=(L, N // tn),
        in_specs=[pl.BlockSpec((8, D), lambda l, j: (0, 0)),
                  pl.BlockSpec((None, D, tn), lambda l, j: (l, 0, j)),
                  pl.BlockSpec((None, 1, tn), lambda l, j: (l, 0, j))],
        out_specs=pl.BlockSpec((None, 8, tn), lambda l, j: (l, 0, j)),
        out_shape=jax.ShapeDtypeStruct((L, 8, N), F32),
        compiler_params=_cparams(("parallel", "parallel")),
        name="ada_mod",
    )(c8, w, b)


def _mla_proj_kernel(h_ref, g_ref, sc_ref, sh_ref, cos_ref, sin_ref, w1_ref, gq_ref, gkv_ref,
                     wqa_ref, wqb_ref, wka_ref, wvt_ref, q_ref, k_ref, vt_ref, *, qscale):
    hn = _normmod(h_ref[...], g_ref[...], sc_ref[...], sh_ref[...]).astype(BF16)
    z = _dot(hn, w1_ref[...])
    ql = z[:, :MLA_Q_RANK]
    kvl = z[:, MLA_Q_RANK:MLA_Q_RANK + MLA_KV_RANK]
    kr = z[:, MLA_Q_RANK + MLA_KV_RANK:MLA_Q_RANK + MLA_KV_RANK + HEAD_PAD]
    krs = z[:, MLA_Q_RANK + MLA_KV_RANK + HEAD_PAD:]
    qn = (ql * _rms_scale(ql) * gq_ref[...]).astype(BF16)
    kvn = (kvl * _rms_scale(kvl) * gkv_ref[...]).astype(BF16)
    cos = cos_ref[...]
    sin = sin_ref[...]
    k_rope = kr * cos + krs * sin
    cos2 = jnp.concatenate([cos, cos], axis=1)
    sin2 = jnp.concatenate([sin, sin], axis=1)
    k_rope2 = jnp.concatenate([k_rope, k_rope], axis=1)
    for hp in range(MLA_HEADS // 2):
        sl = slice(2 * HEAD_PAD * hp, 2 * HEAD_PAD * (hp + 1))
        a = _dot(qn, wqa_ref[:, sl])
        b = _dot(qn, wqb_ref[:, sl])
        q_ref[:, sl] = ((a * cos2 + b * sin2) * qscale).astype(BF16)
        k_ref[:, sl] = (_dot(kvn, wka_ref[:, sl]) + k_rope2).astype(BF16)
    vt_ref[...] = _dot_nt(wvt_ref[...], kvn).astype(BF16)


def _mla_proj(h, g, sc, sh, cos_t, sin_t, w1, gq, gkv, wqa, wqb, wka, wvt, *, tm, batch, seq):
    T, D = h.shape
    HP = MLA_HEADS * HEAD_PAD
    tpb = seq // tm
    row = lambda n: pl.BlockSpec((tm, n), lambda i: (i, 0))
    bvec = pl.BlockSpec((None, 1, D), lambda i: (i // tpb, 0, 0))
    full = lambda a: pl.BlockSpec(a.shape, lambda i: (0,) * a.ndim)
    return pl.pallas_call(
        functools.partial(_mla_proj_kernel, qscale=(MLA_QK ** -0.5) * LOG2E),
        grid=(T // tm,),
        in_specs=[row(D), full(g), bvec, bvec, row(HEAD_PAD), row(HEAD_PAD), full(w1), full(gq), full(gkv),
                  full(wqa), full(wqb), full(wka), full(wvt)],
        out_specs=[row(HP), row(HP),
                   pl.BlockSpec((None, MLA_HEADS * MLA_V, tm), lambda i: (i // tpb, 0, i % tpb))],
        out_shape=[jax.ShapeDtypeStruct((T, HP), BF16), jax.ShapeDtypeStruct((T, HP), BF16),
                   jax.ShapeDtypeStruct((batch, MLA_HEADS * MLA_V, seq), BF16)],
        compiler_params=_cparams(("parallel",)),
        name="mla_proj",
    )(h, g, sc, sh, cos_t, sin_t, w1, gq, gkv, wqa, wqb, wka, wvt)


def _mla_attn_kernel(q_ref, k_ref, vt_ref, o_ref, m_sc, l_sc, acc_sc, p_sc, alpha_sc, s_sc, *, tq):
    qi = pl.program_id(2)
    tk = tq
    m_sc[...] = jnp.full(m_sc.shape, -jnp.inf, F32)
    l_sc[...] = jnp.zeros(l_sc.shape, F32)
    acc_sc[...] = jnp.zeros(acc_sc.shape, F32)

    def scores(j, diagonal, slot):
        off = pl.multiple_of(j * tk, tk)
        for hh in range(2):
            q = q_ref[:, HEAD_PAD * hh:HEAD_PAD * (hh + 1)]
            k = k_ref[pl.ds(off, tk), HEAD_PAD * hh:HEAD_PAD * (hh + 1)]
            s = _dot_nt(k, q)
            if diagonal:
                key = lax.broadcasted_iota(jnp.int32, (tk, tq), 0)
                qry = lax.broadcasted_iota(jnp.int32, (tk, tq), 1)
                s = jnp.where(key <= qry, s, -jnp.inf)
            s_sc[slot, hh] = s

    def softmax(slot):
        for hh in range(2):
            s3 = s_sc[slot, hh].reshape(tk // 8, 8, tq)
            m_prev = m_sc[hh]
            m_blk = jnp.max(jnp.max(s3, axis=0), axis=0, keepdims=True)
            m_new = jnp.maximum(m_prev, m_blk)
            alpha = jnp.exp2(m_prev - m_new)
            p3 = jnp.exp2(s3 - m_new[None])
            l_blk = jnp.sum(jnp.sum(p3, axis=0), axis=0, keepdims=True)
            l_sc[hh] = alpha * l_sc[hh] + l_blk
            alpha_sc[hh] = alpha
            p_sc[hh] = p3.reshape(tk, tq).astype(BF16)
            m_sc[hh] = m_new

    def apply_values(j):
        off = pl.multiple_of(j * tk, tk)
        for hh in range(2):
            vt = vt_ref[MLA_V * hh:MLA_V * (hh + 1), pl.ds(off, tk)]
            acc_sc[hh] = alpha_sc[hh][:1] * acc_sc[hh] + _dot(vt, p_sc[hh])

    @pl.when(qi == 0)
    def _():
        scores(0, True, 0)
        softmax(0)

    def step(j, even, diagonal_next=False, has_next=True):
        cur, nxt = (0, 1) if even else (1, 0)
        if has_next:
            scores(j + 1, diagonal_next, nxt)
        apply_values(jnp.maximum(j - 1, 0))
        softmax(cur)

    @pl.when(qi > 0)
    def _():
        p_sc[...] = jnp.zeros(p_sc.shape, BF16)
        alpha_sc[...] = jnp.ones(alpha_sc.shape, F32)
        scores(0, False, 0)

        def pair(t, carry):
            step(2 * t, True)
            step(2 * t + 1, False)
            return carry

        lax.fori_loop(0, (qi - 1) // 2, pair, 0)

    @pl.when(qi % 2 == 1)
    def _():
        step(qi - 1, True, diagonal_next=True)
        step(qi, False, has_next=False)

    @pl.when((qi % 2 == 0) & (qi > 0))
    def _():
        step(qi - 2, True)
        step(qi - 1, False, diagonal_next=True)
        step(qi, True, has_next=False)

    apply_values(qi)
    o_t = jnp.concatenate([acc_sc[0] / l_sc[0][:1], acc_sc[1] / l_sc[1][:1]], axis=0)
    o_ref[...] = o_t.T.astype(BF16)


def _mla_attn(q, k, vt, *, batch, seq, tq):
    return pl.pallas_call(
        functools.partial(_mla_attn_kernel, tq=tq),
        grid=(batch, MLA_HEADS // 2, seq // tq),
        in_specs=[pl.BlockSpec((None, tq, 2 * HEAD_PAD), lambda b, hp, i: (b, i, hp)),
                  pl.BlockSpec((None, seq, 2 * HEAD_PAD), lambda b, hp, i: (b, 0, hp)),
                  pl.BlockSpec((None, 2 * MLA_V, seq), lambda b, hp, i: (b, hp, 0))],
        out_specs=pl.BlockSpec((None, tq, 2 * MLA_V), lambda b, hp, i: (b, i, hp)),
        out_shape=jax.ShapeDtypeStruct((batch, seq, MLA_HEADS * MLA_V), BF16),
        scratch_shapes=[pltpu.VMEM((2, 8, tq), F32), pltpu.VMEM((2, 8, tq), F32),
                        pltpu.VMEM((2, MLA_V, tq), F32), pltpu.VMEM((2, tq, tq), BF16),
                        pltpu.VMEM((2, 8, tq), F32), pltpu.VMEM((2, 2, tq, tq), F32)],
        compiler_params=_cparams(("parallel", "parallel", "arbitrary")),
        name="mla_attn",
    )(q, k, vt)


def _swiglu_tile(x, wgu_ref, wd_ref, acc_sc, tf):
    for f in range(D_FF // tf):
        g = _dot(x, wgu_ref[:, f * tf:(f + 1) * tf])
        u = _dot(x, wgu_ref[:, D_FF + f * tf:D_FF + (f + 1) * tf])
        part = _dot((_silu(g) * u).astype(BF16), wd_ref[f * tf:(f + 1) * tf, :])
        if f == 0:
            acc_sc[...] = part
        else:
            acc_sc[...] += part


def _ffn_kernel(a_ref, wo_ref, h_ref, gtm_ref, g_ref, sc_ref, sh_ref, gt_ref, wgu_ref, wd_ref, o_ref, acc_sc,
                *, tf):
    h = h_ref[...] + gtm_ref[...] * _dot(a_ref[...], wo_ref[...])
    x = _normmod(h, g_ref[...], sc_ref[...], sh_ref[...]).astype(BF16)
    _swiglu_tile(x, wgu_ref, wd_ref, acc_sc, tf)
    o_ref[...] = h + gt_ref[...] * acc_sc[...]


def _ffn(a, w_o, h, gt_m, g, sc, sh, gt, w_gu, w_d, *, tm, tf, tiles_per_batch):
    T, D = h.shape
    bvec = pl.BlockSpec((None, 1, D), lambda i: (i // tiles_per_batch, 0, 0))
    row = pl.BlockSpec((tm, D), lambda i: (i, 0))
    const = lambda w: pl.BlockSpec(w.shape, lambda i: (0,) * w.ndim, pipeline_mode=pl.Buffered(1))
    return pl.pallas_call(
        functools.partial(_ffn_kernel, tf=tf),
        grid=(T // tm,),
        in_specs=[row, const(w_o), row, bvec, const(g), bvec, bvec, bvec, const(w_gu), const(w_d)],
        out_specs=row,
        out_shape=jax.ShapeDtypeStruct((T, D), F32),
        scratch_shapes=[pltpu.VMEM((tm, D), F32)],
        compiler_params=_cparams(("parallel",)),
        name="ffn_dense",
    )(a, w_o, h, gt_m, g, sc, sh, gt, w_gu, w_d)


def _l1_proj_kernel(h_ref, gkv_ref, sckv_ref, shkv_ref, gq_ref, scq_ref, shq_ref, wkv_ref, wq_ref, *refs):
    out_refs, stage_sc = refs[:-1], refs[-1]
    tm = h_ref.shape[0]
    x = h_ref[...]
    xh = x * _rms_scale(x)
    hn_kv = (xh * (gkv_ref[...] * (1.0 + sckv_ref[...])) + shkv_ref[...]).astype(BF16)
    hn_q = (xh * (gq_ref[...] * (1.0 + scq_ref[...])) + shq_ref[...]).astype(BF16)
    for g, (_, dil) in enumerate(DIL_PATTERNS):
        ys = (_dot(hn_q, wq_ref[:, g * DIL_W:(g + 1) * DIL_W]) * (DIL_HEAD_DIM ** -0.5),
              _dot(hn_kv, wkv_ref[:, 2 * g * DIL_W:(2 * g + 1) * DIL_W]),
              _dot(hn_kv, wkv_ref[:, (2 * g + 1) * DIL_W:(2 * g + 2) * DIL_W]))
        for y, out_ref in zip(ys, out_refs[3 * g:3 * g + 3]):
            if dil == 1:
                out_ref[...] = y.astype(BF16)
            else:
                for c in range(DIL_W // LANES):
                    stage_sc[c] = y[:, c * LANES:(c + 1) * LANES]
                for r in range(dil):
                    for c in range(DIL_W // LANES):
                        col = r * DIL_W + c * LANES
                        out_ref[:, col:col + LANES] = stage_sc[c, pl.ds(r, tm // dil, stride=dil), :].astype(BF16)


def _l1_proj(h, gkv, sckv, shkv, gq, scq, shq, wkv, wq, *, tm, tiles_per_batch):
    T, D = h.shape
    bvec = pl.BlockSpec((None, 1, D), lambda i: (i // tiles_per_batch, 0, 0))
    full = lambda a: pl.BlockSpec(a.shape, lambda i: (0,) * a.ndim)
    out_specs, out_shape = [], []
    for _, dil in DIL_PATTERNS:
        for _ in range(3):
            out_specs.append(pl.BlockSpec((tm // dil, dil * DIL_W), lambda i: (i, 0)))
            out_shape.append(jax.ShapeDtypeStruct((T // dil, dil * DIL_W), BF16))
    return pl.pallas_call(
        _l1_proj_kernel,
        grid=(T // tm,),
        in_specs=[pl.BlockSpec((tm, D), lambda i: (i, 0)), full(gkv), bvec, bvec, full(gq), bvec, bvec,
                  full(wkv), full(wq)],
        out_specs=out_specs,
        out_shape=out_shape,
        scratch_shapes=[pltpu.VMEM((DIL_W // LANES, tm, LANES), F32)],
        compiler_params=_cparams(("parallel",)),
        name="l1_proj",
    )(h, gkv, sckv, shkv, gq, scq, shq, wkv, wq)


def _dil_attn_kernel(q_ref, kc_ref, kp_ref, vc_ref, vp_ref, bias_ref, o_ref, lse_ref):
    i = pl.program_id(2)
    lane = lax.broadcasted_iota(jnp.int32, (1, 2 * DIL_HEAD_DIM), 1)
    lo = lane < DIL_HEAD_DIM
    col = lax.broadcasted_iota(jnp.int32, (1, 2 * DIL_BLOCK), 1)
    edge = jnp.where((col < DIL_BLOCK) & (i == 0), MASK_VALUE, 0.0).astype(F32)
    sls = [slice(2 * DIL_HEAD_DIM * hp, 2 * DIL_HEAD_DIM * (hp + 1)) for hp in range(DIL_HEADS // 2)]
    scores = []
    for hp, sl in enumerate(sls):
        q2 = q_ref[:, sl]
        zero = jnp.zeros_like(q2)
        qs = jnp.concatenate([jnp.where(lo, q2, zero), jnp.where(lo, zero, q2)], axis=0)
        k2 = jnp.concatenate([kp_ref[:, sl], kc_ref[:, sl]], axis=0)
        bias = jnp.concatenate([bias_ref[2 * hp], bias_ref[2 * hp + 1]], axis=0)
        scores.append(_dot_nt(qs, k2) + bias + edge)
    probs, stats = [], []
    for s in scores:
        m = jnp.max(s, axis=1, keepdims=True)
        p = jnp.exp(s - m)
        probs.append(p.astype(BF16))
        stats.append((m, jnp.sum(p, axis=1, keepdims=True)))
    for sl, p, (m, l) in zip(sls, probs, stats):
        v2 = jnp.concatenate([vp_ref[:, sl], vc_ref[:, sl]], axis=0)
        o = _dot(p, v2) / l
        lse = jnp.broadcast_to(m + jnp.log(l), o.shape)
        o_ref[:, sl] = jnp.where(lo, o[:DIL_BLOCK], o[DIL_BLOCK:])
        lse_ref[:, sl] = jnp.where(lo, lse[:DIL_BLOCK], lse[DIL_BLOCK:])


def _dil_attn(q, k, v, bias, *, group, dil, batch, seq):
    n = seq // dil
    nb = n // DIL_BLOCK
    q, k, v = (a.reshape(batch, n, dil * DIL_W) for a in (q, k, v))
    blk = (None, DIL_BLOCK, DIL_W)
    cur = pl.BlockSpec(blk, lambda b, r, i: (b, i, r))
    prev = pl.BlockSpec(blk, lambda b, r, i: (b, jnp.maximum(i - 1, 0), r))
    out_sd = jax.ShapeDtypeStruct((batch, n, dil * DIL_W), F32)
    o, lse = pl.pallas_call(
        _dil_attn_kernel,
        grid=(batch, dil, nb),
        in_specs=[cur, cur, prev, cur, prev, pl.BlockSpec(bias.shape, lambda b, r, i: (0, 0, 0))],
        out_specs=[cur, cur],
        out_shape=[out_sd, out_sd],
        compiler_params=_cparams(("parallel", "parallel", "arbitrary")),
        name=f"dil_attn_g{group}",
    )(q, k, k, v, v, bias)
    return o.reshape(batch * n, dil * DIL_W), lse.reshape(batch * n, dil * DIL_W)


def _rpb_bucket(dist):
    exact = RPB_BUCKETS // 2
    d = jnp.maximum(dist, 0)
    d_f = jnp.maximum(d, 1).astype(F32)
    large = exact + (jnp.log(d_f / exact) / math.log(RPB_MAX_DIST / exact)
                     * (RPB_BUCKETS - exact)).astype(jnp.int32)
    return jnp.where(d < exact, d, jnp.minimum(large, RPB_BUCKETS - 1))


def _dil_bias(table, span, dil):
    period = 3 * DIL_BLOCK
    j = jnp.arange(period)
    k_minus_q = jnp.where(j < 2 * DIL_BLOCK, j, j - period)
    dist = DIL_BLOCK - k_minus_q
    band = (dist >= 0) & (dist <= span)
    prof = jnp.where(band[:, None], table[_rpb_bucket(dist * dil)].astype(F32), MASK_VALUE).T
    skew = jnp.tile(prof, (1, DIL_BLOCK))[:, :DIL_BLOCK * (period - 1)]
    return skew.reshape(DIL_HEADS, DIL_BLOCK, period - 1)[:, :, :2 * DIL_BLOCK]


def _dil_out_kernel(o0_ref, o1_ref, o2_ref, l0_ref, l1_ref, l2_ref, w_ref, h_ref, gt_ref, out_ref, *stages):
    tm = h_ref.shape[0]
    stages = list(stages)

    def token_major(ref, dil):
        if dil == 1:
            return ref[...]
        stage = stages.pop()
        for r in range(dil):
            for c in range(DIL_W // LANES):
                col = r * DIL_W + c * LANES
                stage[c, pl.ds(r, tm // dil, stride=dil), :] = ref[:, col:col + LANES]
        return jnp.concatenate([stage[c] for c in range(DIL_W // LANES)], axis=1)

    dils = [dil for _, dil in DIL_PATTERNS]
    l0, l1, l2 = (token_major(ref, d) for ref, d in zip((l0_ref, l1_ref, l2_ref), dils))
    o0, o1, o2 = (token_major(ref, d) for ref, d in zip((o0_ref, o1_ref, o2_ref), dils))
    m = jnp.maximum(jnp.maximum(l0, l1), l2)
    w0, w1, w2 = jnp.exp(l0 - m), jnp.exp(l1 - m), jnp.exp(l2 - m)
    o = (o0 * w0 + o1 * w1 + o2 * w2) / (w0 + w1 + w2)
    out_ref[...] = h_ref[...] + gt_ref[...] * _dot(o.astype(BF16), w_ref[...])


def _dil_out(os_, ls_, w, h, gt, *, tm, tiles_per_batch):
    T, D = h.shape
    row = lambda n: pl.BlockSpec((tm, n), lambda i: (i, 0))
    grp = [pl.BlockSpec((tm // dil, dil * DIL_W), lambda i: (i, 0)) for _, dil in DIL_PATTERNS]
    n_stage = 2 * sum(1 for _, dil in DIL_PATTERNS if dil > 1)
    return pl.pallas_call(
        _dil_out_kernel,
        grid=(T // tm,),
        in_specs=grp + grp + [pl.BlockSpec(w.shape, lambda i: (0, 0)), row(D),
                              pl.BlockSpec((None, 1, D), lambda i: (i // tiles_per_batch, 0, 0))],
        out_specs=row(D),
        out_shape=jax.ShapeDtypeStruct((T, D), F32),
        scratch_shapes=[pltpu.VMEM((DIL_W // LANES, tm, LANES), F32)] * n_stage,
        compiler_params=_cparams(("parallel",)),
        name="dil_out",
    )(*os_, *ls_, w, h, gt)


def _route_kernel(h_ref, g_ref, sc_ref, sh_ref, wrt_ref, hn_ref, eid_ref, rank_ref, gate_ref, cnt_ref, carry_sc):
    @pl.when(pl.program_id(0) == 0)
    def _():
        carry_sc[...] = jnp.zeros(carry_sc.shape, F32)

    hn = _normmod(h_ref[...], g_ref[...], sc_ref[...], sh_ref[...])
    hn_ref[...] = hn
    tm = hn.shape[0]
    logits = lax.dot_general(wrt_ref[...], hn, (((1,), (1,)), ((), ())), preferred_element_type=F32,
                             precision=lax.Precision.HIGHEST)
    idx = lax.broadcasted_iota(jnp.int32, logits.shape, 0)
    v1 = jnp.max(logits, axis=0, keepdims=True)
    i1 = jnp.min(jnp.where(logits == v1, idx, N_EXPERTS), axis=0, keepdims=True)
    rest = jnp.where(idx == i1, -jnp.inf, logits)
    v2 = jnp.max(rest, axis=0, keepdims=True)
    i2 = jnp.min(jnp.where(rest == v2, idx, N_EXPERTS), axis=0, keepdims=True)
    e = jnp.exp(v2 - v1)
    gate_ref[...] = jnp.concatenate([1.0 / (1.0 + e), e / (1.0 + e)], axis=0)
    eid_ref[...] = jnp.concatenate([i1, i2], axis=0)

    sel = ((idx == i1) | (idx == i2)).astype(BF16)
    before = (lax.broadcasted_iota(jnp.int32, (tm, tm), 0)
              < lax.broadcasted_iota(jnp.int32, (tm, tm), 1)).astype(BF16)
    rank_all = carry_sc[:, :1] + _dot(sel, before)
    r1 = jnp.sum(jnp.where(idx == i1, rank_all, 0.0), axis=0, keepdims=True)
    r2 = jnp.sum(jnp.where(idx == i2, rank_all, 0.0), axis=0, keepdims=True)
    rank_ref[...] = jnp.concatenate([r1, r2], axis=0).astype(jnp.int32)
    carry_sc[...] += jnp.sum(sel.astype(F32), axis=1, keepdims=True)
    cnt_ref[...] = carry_sc[...].astype(jnp.int32)


def _route(h, g, sc, sh, wrt, *, tm, tiles_per_batch):
    T, D = h.shape
    bvec = pl.BlockSpec((None, 1, D), lambda i: (i // tiles_per_batch, 0, 0))
    lane_blk = pl.BlockSpec((2, tm), lambda i: (0, i))
    return pl.pallas_call(
        _route_kernel,
        grid=(T // tm,),
        in_specs=[pl.BlockSpec((tm, D), lambda i: (i, 0)), pl.BlockSpec((1, D), lambda i: (0, 0)), bvec, bvec,
                  pl.BlockSpec(wrt.shape, lambda i: (0, 0))],
        out_specs=[pl.BlockSpec((tm, D), lambda i: (i, 0)), lane_blk, lane_blk, lane_blk,
                   pl.BlockSpec((N_EXPERTS, HEAD_PAD), lambda i: (0, 0))],
        out_shape=[jax.ShapeDtypeStruct((T, D), F32), jax.ShapeDtypeStruct((2, T), jnp.int32),
                   jax.ShapeDtypeStruct((2, T), jnp.int32), jax.ShapeDtypeStruct((2, T), F32),
                   jax.ShapeDtypeStruct((N_EXPERTS, HEAD_PAD), jnp.int32)],
        scratch_shapes=[pltpu.VMEM((N_EXPERTS, HEAD_PAD), F32)],
        compiler_params=_cparams(("arbitrary",)),
        name="moe_route",
    )(h, g, sc, sh, wrt)


def _dispatch_kernel(pad_ref, dest_ref, hn_ref, xs_ref, zero_sc, sem, zsem):
    tm = hn_ref.shape[0]

    def row_copy(r, k):
        return pltpu.make_async_copy(hn_ref.at[pl.ds(r, 1), :], xs_ref.at[pl.ds(dest_ref[k, r], 1), :], sem)

    def issue(r, carry):
        row_copy(r, 0).start()
        row_copy(r, 1).start()
        return carry

    lax.fori_loop(0, tm, issue, 0, unroll=8)

    @pl.when(pl.program_id(0) == pl.num_programs(0) - 1)
    def _():
        zero_sc[...] = jnp.zeros(zero_sc.shape, F32)

        def pad_copy(p):
            return pltpu.make_async_copy(zero_sc.at[pl.ds(0, 1), :], xs_ref.at[pl.ds(p, 1), :], zsem)

        for e in range(N_EXPERTS):
            lo, hi = pad_ref[0, e], pad_ref[1, e]

            def zissue(p, carry):
                pad_copy(p).start()
                return carry

            def zwait(p, carry):
                pad_copy(p).wait()
                return carry

            lax.fori_loop(lo, hi, zissue, 0)
            lax.fori_loop(lo, hi, zwait, 0)

        zrows = zero_sc.shape[0]

        def tail_copy(c):
            return pltpu.make_async_copy(zero_sc, xs_ref.at[pl.ds(pl.multiple_of(c * zrows, zrows), zrows), :], zsem)

        def tissue(c, carry):
            tail_copy(c).start()
            return carry

        def twait(c, carry):
            tail_copy(c).wait()
            return carry

        lo, hi = pad_ref[1, N_EXPERTS - 1] // zrows, xs_ref.shape[0] // zrows
        lax.fori_loop(lo, hi, tissue, 0)
        lax.fori_loop(lo, hi, twait, 0)

    for _ in range(2):
        pltpu.make_async_copy(hn_ref, xs_ref.at[pl.ds(0, tm), :], sem).wait()


def _dispatch(pad_rows, dest, hn, *, n_slots, tm):
    T, D = hn.shape
    return pl.pallas_call(
        _dispatch_kernel,
        grid_spec=pltpu.PrefetchScalarGridSpec(
            num_scalar_prefetch=1,
            grid=(T // tm,),
            in_specs=[pl.BlockSpec((2, tm), lambda i, pad: (0, i), memory_space=pltpu.SMEM),
                      pl.BlockSpec((tm, D), lambda i, pad: (i, 0))],
            out_specs=pl.BlockSpec(memory_space=pl.ANY),
            scratch_shapes=[pltpu.VMEM((64, D), F32), pltpu.SemaphoreType.DMA, pltpu.SemaphoreType.DMA]),
        out_shape=jax.ShapeDtypeStruct((n_slots, D), F32),
        compiler_params=_cparams(("arbitrary",)),
        name="moe_dispatch",
    )(pad_rows, dest, hn)


def _experts_kernel(te_ref, na_ref, xs_ref, wgu_ref, wd_ref, y_ref, acc_sc, *, tf):
    @pl.when(pl.program_id(0) < na_ref[0])
    def _():
        _swiglu_tile(xs_ref[...].astype(BF16), wgu_ref, wd_ref, acc_sc, tf)
        y_ref[...] = acc_sc[...]

    @pl.when(pl.program_id(0) >= na_ref[0])
    def _():
        y_ref[...] = jnp.zeros(y_ref.shape, F32)


def _experts(tile_expert, n_active, xs, w_gu, w_d, *, tm, tf):
    P, D = xs.shape

    def tile(j, te, na):
        return jnp.minimum(j, na[0] - 1)

    return pl.pallas_call(
        functools.partial(_experts_kernel, tf=tf),
        grid_spec=pltpu.PrefetchScalarGridSpec(
            num_scalar_prefetch=2,
            grid=(P // tm,),
            in_specs=[pl.BlockSpec((tm, D), lambda j, te, na: (tile(j, te, na), 0)),
                      pl.BlockSpec((None, D, 2 * D_FF), lambda j, te, na: (te[tile(j, te, na)], 0, 0)),
                      pl.BlockSpec((None, D_FF, D), lambda j, te, na: (te[tile(j, te, na)], 0, 0))],
            out_specs=pl.BlockSpec((tm, D), lambda j, te, na: (j, 0)),
            scratch_shapes=[pltpu.VMEM((tm, D), F32)]),
        out_shape=jax.ShapeDtypeStruct((P, D), F32),
        compiler_params=_cparams(("arbitrary",)),
        name="moe_experts",
    )(tile_expert, n_active, xs, w_gu, w_d)


def _combine_kernel(dest_ref, y_ref, gates_ref, h_ref, gt_ref, gfin_ref, o_ref, ybuf, sem):
    tm = h_ref.shape[0]

    def row_copy(r, k):
        return pltpu.make_async_copy(y_ref.at[pl.ds(dest_ref[k, r], 1), :], ybuf.at[k, pl.ds(r, 1), :], sem)

    def issue(r, carry):
        row_copy(r, 0).start()
        row_copy(r, 1).start()
        return carry

    lax.fori_loop(0, tm, issue, 0, unroll=8)
    for k in range(2):
        pltpu.make_async_copy(y_ref.at[pl.ds(0, tm), :], ybuf.at[k], sem).wait()
    gates = gates_ref[...]
    moe = gates[:, 0:1] * ybuf[0] + gates[:, 1:2] * ybuf[1]
    y = h_ref[...] + gt_ref[...] * moe
    o_ref[...] = y * _rms_scale(y) * gfin_ref[...]


def _combine(dest, y, gates, h, gt, gfin, *, tm, tiles_per_batch):
    T, D = h.shape
    return pl.pallas_call(
        _combine_kernel,
        grid=(T // tm,),
        in_specs=[pl.BlockSpec((2, tm), lambda i: (0, i), memory_space=pltpu.SMEM),
                  pl.BlockSpec(memory_space=pl.ANY),
                  pl.BlockSpec((tm, 2), lambda i: (i, 0)),
                  pl.BlockSpec((tm, D), lambda i: (i, 0)),
                  pl.BlockSpec((None, 1, D), lambda i: (i // tiles_per_batch, 0, 0)),
                  pl.BlockSpec((1, D), lambda i: (0, 0))],
        out_specs=pl.BlockSpec((tm, D), lambda i: (i, 0)),
        out_shape=jax.ShapeDtypeStruct((T, D), F32),
        scratch_shapes=[pltpu.VMEM((2, tm, D), F32), pltpu.SemaphoreType.DMA],
        compiler_params=_cparams(("arbitrary",)),
        name="moe_combine",
    )(dest, y, gates, h, gt, gfin)


def _moe_plan(eids, ranks, counts, *, tm, n_tiles):
    padded = (counts + tm - 1) // tm * tm
    ends = jnp.cumsum(padded)
    starts = ends - padded
    dest = ranks
    for e in range(N_EXPERTS):
        dest = dest + jnp.where(eids == e, starts[e], 0)
    tile_start = jnp.arange(n_tiles, dtype=jnp.int32) * tm
    tile_expert = jnp.minimum(jnp.sum(tile_start[:, None] >= ends[None, :], axis=1), N_EXPERTS - 1)
    n_active = (ends[-1] // tm).reshape(1)
    pad_rows = jnp.stack([starts + counts, ends])
    return dest.astype(jnp.int32), tile_expert.astype(jnp.int32), n_active.astype(jnp.int32), pad_rows.astype(jnp.int32)


def _mla_weights(w_in, w_q_up, w_kv_up):
    D = w_in.shape[0]
    half = MLA_ROPE // 2
    w_ql = w_in[:, :MLA_Q_RANK]
    w_kvl = w_in[:, MLA_Q_RANK:MLA_Q_RANK + MLA_KV_RANK]
    w_kr = w_in[:, MLA_Q_RANK + MLA_KV_RANK:]
    z64 = jnp.zeros((D, MLA_NOPE), F32)
    z32 = jnp.zeros((D, HEAD_PAD - MLA_QK), F32)
    kr_pad = jnp.concatenate([z64, w_kr, z32], axis=1)
    kr_swap = jnp.concatenate([z64, w_kr[:, half:], w_kr[:, :half], z32], axis=1)
    w1 = jnp.concatenate([w_ql, w_kvl, kr_pad, kr_swap], axis=1).astype(BF16)

    wq = w_q_up.reshape(MLA_Q_RANK, MLA_HEADS, MLA_QK)
    nope, rope = wq[..., :MLA_NOPE], wq[..., MLA_NOPE:]
    zq = jnp.zeros((MLA_Q_RANK, MLA_HEADS, HEAD_PAD - MLA_QK), F32)
    wqa = jnp.concatenate([nope, rope, zq], axis=-1).reshape(MLA_Q_RANK, MLA_HEADS * HEAD_PAD).astype(BF16)
    wqb = jnp.concatenate([jnp.zeros_like(nope), rope[..., half:], rope[..., :half], zq], axis=-1)
    wqb = wqb.reshape(MLA_Q_RANK, MLA_HEADS * HEAD_PAD).astype(BF16)

    wkv = w_kv_up.reshape(MLA_KV_RANK, MLA_HEADS, MLA_NOPE + MLA_V)
    k_nope, v = wkv[..., :MLA_NOPE], wkv[..., MLA_NOPE:]
    zk = jnp.zeros((MLA_KV_RANK, MLA_HEADS, HEAD_PAD - MLA_NOPE), F32)
    wka = jnp.concatenate([k_nope, zk], axis=-1).reshape(MLA_KV_RANK, MLA_HEADS * HEAD_PAD).astype(BF16)
    wvt = v.reshape(MLA_KV_RANK, MLA_HEADS * MLA_V).T.astype(BF16)
    return w1, wqa, wqb, wka, wvt


def _rope_tables(positions):
    inv_freq = ROPE_THETA ** (-jnp.arange(0, MLA_ROPE, 2, dtype=F32) / MLA_ROPE)
    ang = positions.astype(F32).reshape(-1, 1) * inv_freq
    cos, sin = jnp.cos(ang), jnp.sin(ang)
    T = ang.shape[0]
    one = jnp.ones((T, MLA_NOPE), F32)
    z64 = jnp.zeros((T, MLA_NOPE), F32)
    z32 = jnp.zeros((T, HEAD_PAD - MLA_QK), F32)
    return (jnp.concatenate([one, cos, cos, z32], axis=1),
            jnp.concatenate([z64, -sin, sin, z32], axis=1))


def kernel(x, c, positions, g_mix, g_ffn, w_ada, b_ada, w_mla_in, g_mla_q, w_mla_q_up, g_mla_kv, w_mla_kv_up,
           w_mla_out, g_kv_b, w_ada_kv, b_ada_kv, w_kv_b, rpb_table, w_q_b, w_o_b, w_ffn_gu, w_ffn_down,
           w_router, w_exp_gu, w_exp_down, g_final):
    B, S, D = x.shape
    T = B * S
    h = x.reshape(T, D)

    c8 = jnp.zeros((8, D), F32).at[:B].set(c)
    mod = _ada(c8, w_ada, b_ada[:, None, :])[:, :B]
    mod = mod.reshape(2, B, N_MOD, 1, D)
    mod_kv = _ada(c8, w_ada_kv[None], b_ada_kv[None, None, :])[0, :B].reshape(B, 2, 1, D)
    sh_kv, sc_kv = mod_kv[:, 0], mod_kv[:, 1]

    def mods(layer):
        return [mod[layer, :, k] for k in range(N_MOD)]

    row = lambda v: v.reshape(1, -1)

    sh_m, sc_m, gt_m, sh_f, sc_f, gt_f = mods(0)
    w1, wqa, wqb, wka, wvt = _mla_weights(w_mla_in[0], w_mla_q_up[0], w_mla_kv_up[0])
    cos_t, sin_t = _rope_tables(positions)
    q, k, vt = _mla_proj(h, row(g_mix[0]), sc_m, sh_m, cos_t, sin_t, w1, row(g_mla_q[0]), row(g_mla_kv[0]),
                         wqa, wqb, wka, wvt, tm=256, batch=B, seq=S)
    o = _mla_attn(q.reshape(B, S, -1), k.reshape(B, S, -1), vt, batch=B, seq=S, tq=512)

    tm = 512
    h = _ffn(o.reshape(T, -1), w_mla_out[0].astype(BF16), h, gt_m, row(g_ffn[0]), sc_f, sh_f, gt_f,
             w_ffn_gu[0].astype(BF16), w_ffn_down[0].astype(BF16), tm=tm, tf=256, tiles_per_batch=S // tm)

    sh_m, sc_m, gt_m, sh_f, sc_f, gt_f = mods(1)
    tm = 512
    qkv = _l1_proj(h, row(g_kv_b), sc_kv, sh_kv, row(g_mix[1]), sc_m, sh_m,
                   w_kv_b.astype(BF16), w_q_b[0].astype(BF16), tm=tm, tiles_per_batch=S // tm)

    outs, lses = [], []
    for g, (window, dil) in enumerate(DIL_PATTERNS):
        bias = _dil_bias(rpb_table[:, g * DIL_HEADS:(g + 1) * DIL_HEADS], window // dil, dil)
        o_g, lse_g = _dil_attn(*qkv[3 * g:3 * g + 3], bias, group=g, dil=dil, batch=B, seq=S)
        outs.append(o_g)
        lses.append(lse_g)
    h = _dil_out(outs, lses, w_o_b[0].astype(BF16), h, gt_m, tm=tm, tiles_per_batch=S // tm)

    hn, eids, ranks, gates, counts = _route(h, row(g_ffn[1]), sc_f, sh_f, w_router[0].T, tm=tm,
                                            tiles_per_batch=S // tm)
    n_tiles = (T * 2) // MOE_TM + N_EXPERTS
    dest, tile_expert, n_active, pad_rows = _moe_plan(eids, ranks, counts[:, 0], tm=MOE_TM, n_tiles=n_tiles)
    xs = _dispatch(pad_rows, dest, hn, n_slots=n_tiles * MOE_TM, tm=tm)
    y = _experts(tile_expert, n_active, xs, w_exp_gu[0].astype(BF16), w_exp_down[0].astype(BF16),
                 tm=MOE_TM, tf=256)
    tm = 256
    out = _combine(dest, y, gates.T, h, gt_f, row(g_final), tm=tm, tiles_per_batch=S // tm)
    return out.reshape(B, S, D)
```

```python
import functools
import math

import numpy as np
import jax
import jax.numpy as jnp
from jax import lax
from jax.experimental import pallas as pl
from jax.experimental.pallas import tpu as pltpu

D_MODEL = 1024
N_MOD = 6
EPS = 1e-6

MLA_HEADS = 16
MLA_Q_RANK = 384
MLA_KV_RANK = 256
MLA_NOPE = 64
MLA_ROPE = 32
MLA_V = 64
MLA_QK = MLA_NOPE + MLA_ROPE
ROPE_THETA = 10000.0
LANES = 128
ONES_ROWS = 16
HEAD_PAD = 128

DIL_PATTERNS = ((128, 1), (512, 4), (2048, 16))
DIL_GROUPS = len(DIL_PATTERNS)
DIL_HEADS = 8
DIL_HEAD_DIM = 64
DIL_BLOCK = 128
DIL_W = DIL_HEADS * DIL_HEAD_DIM
RPB_BUCKETS = 32
RPB_MAX_DIST = 2048

D_FF = 2816
N_EXPERTS = 8
MOE_TM = 512

MASK_VALUE = -1e30
LOG2E = math.log2(math.e)

F32 = jnp.float32
BF16 = jnp.bfloat16

VMEM_LIMIT = 56 * 1024 * 1024


def _cparams(sem):
    return pltpu.CompilerParams(dimension_semantics=sem, vmem_limit_bytes=VMEM_LIMIT)


def _dot(a, b):
    return jnp.dot(a, b, preferred_element_type=F32)


def _dot_nt(a, b):
    return lax.dot_general(a, b, (((1,), (1,)), ((), ())), preferred_element_type=F32)


def _rms_scale(x):
    return lax.rsqrt(jnp.mean(x * x, axis=-1, keepdims=True) + EPS)


def _normmod(x, g, sc, sh):
    return (x * _rms_scale(x)) * (g * (1.0 + sc)) + sh


def _silu(x):
    return x * (1.0 / (1.0 + jnp.exp(-x)))


def _ada_kernel(c_ref, w_ref, b_ref, o_ref):
    c = c_ref[...]
    o_ref[...] = jnp.dot(_silu(c), w_ref[...], preferred_element_type=F32,
                         precision=lax.Precision.HIGHEST) + b_ref[...]


def _ada(c8, w, b, tn=512):
    L, D, N = w.shape
    return pl.pallas_call(
        _ada_kernel,
        grid=(L, N // tn),
        in_specs=[pl.BlockSpec((8, D), lambda l, j: (0, 0)),
                  pl.BlockSpec((None, D, tn), lambda l, j: (l, 0, j)),
                  pl.BlockSpec((None, 1, tn), lambda l, j: (l, 0, j))],
        out_specs=pl.BlockSpec((None, 8, tn), lambda l, j: (l, 0, j)),
        out_shape=jax.ShapeDtypeStruct((L, 8, N), F32),
        compiler_params=_cparams(("parallel", "parallel")),
        name="ada_mod",
    )(c8, w, b)


def _mla_proj_kernel(h_ref, g_ref, sc_ref, sh_ref, cos_ref, sin_ref, w1_ref, gq_ref, gkv_ref,
                     wqa_ref, wqb_ref, wka_ref, wvt_ref, q_ref, k_ref, vt_ref, *, qscale):
    hn = _normmod(h_ref[...], g_ref[...], sc_ref[...], sh_ref[...]).astype(BF16)
    z = _dot(hn, w1_ref[...])
    ql = z[:, :MLA_Q_RANK]
    kvl = z[:, MLA_Q_RANK:MLA_Q_RANK + MLA_KV_RANK]
    kr = z[:, MLA_Q_RANK + MLA_KV_RANK:MLA_Q_RANK + MLA_KV_RANK + HEAD_PAD]
    krs = z[:, MLA_Q_RANK + MLA_KV_RANK + HEAD_PAD:]
    qn = (ql * _rms_scale(ql) * gq_ref[...]).astype(BF16)
    kvn = (kvl * _rms_scale(kvl) * gkv_ref[...]).astype(BF16)
    cos = cos_ref[...]
    sin = sin_ref[...]
    k_rope = kr * cos + krs * sin
    cos2 = jnp.concatenate([cos, cos], axis=1)
    sin2 = jnp.concatenate([sin, sin], axis=1)
    k_rope2 = jnp.concatenate([k_rope, k_rope], axis=1)
    for hp in range(MLA_HEADS // 2):
        sl = slice(2 * HEAD_PAD * hp, 2 * HEAD_PAD * (hp + 1))
        a = _dot(qn, wqa_ref[:, sl])
        b = _dot(qn, wqb_ref[:, sl])
        q_ref[:, sl] = ((a * cos2 + b * sin2) * qscale).astype(BF16)
        k_ref[:, sl] = (_dot(kvn, wka_ref[:, sl]) + k_rope2).astype(BF16)
    vt_ref[...] = _dot_nt(wvt_ref[...], kvn).astype(BF16)


def _mla_proj(h, g, sc, sh, cos_t, sin_t, w1, gq, gkv, wqa, wqb, wka, wvt, *, tm, batch, seq):
    T, D = h.shape
    HP = MLA_HEADS * HEAD_PAD
    tpb = seq // tm
    row = lambda n: pl.BlockSpec((tm, n), lambda i: (i, 0))
    bvec = pl.BlockSpec((None, 1, D), lambda i: (i // tpb, 0, 0))
    full = lambda a: pl.BlockSpec(a.shape, lambda i: (0,) * a.ndim)
    return pl.pallas_call(
        functools.partial(_mla_proj_kernel, qscale=(MLA_QK ** -0.5) * LOG2E),
        grid=(T // tm,),
        in_specs=[row(D), full(g), bvec, bvec, row(HEAD_PAD), row(HEAD_PAD), full(w1), full(gq), full(gkv),
                  full(wqa), full(wqb), full(wka), full(wvt)],
        out_specs=[row(HP), row(HP),
                   pl.BlockSpec((None, MLA_HEADS * MLA_V, tm), lambda i: (i // tpb, 0, i % tpb))],
        out_shape=[jax.ShapeDtypeStruct((T, HP), BF16), jax.ShapeDtypeStruct((T, HP), BF16),
                   jax.ShapeDtypeStruct((batch, MLA_HEADS * MLA_V, seq), BF16)],
        compiler_params=_cparams(("parallel",)),
        name="mla_proj",
    )(h, g, sc, sh, cos_t, sin_t, w1, gq, gkv, wqa, wqb, wka, wvt)


def _mla_attn_kernel(q_ref, k_ref, vt_ref, o_ref, m_sc, acc_sc, p_sc, alpha_sc, s_sc, *, tq):
    qi = pl.program_id(2)
    tk = tq
    m_sc[...] = jnp.full(m_sc.shape, -jnp.inf, F32)
    acc_sc[...] = jnp.zeros(acc_sc.shape, F32)

    def scores(j, diagonal, slot):
        off = pl.multiple_of(j * tk, tk)
        for hh in range(2):
            q = q_ref[:, HEAD_PAD * hh:HEAD_PAD * (hh + 1)]
            k = k_ref[pl.ds(off, tk), HEAD_PAD * hh:HEAD_PAD * (hh + 1)]
            s = _dot_nt(k, q)
            if diagonal:
                key = lax.broadcasted_iota(jnp.int32, (tk, tq), 0)
                qry = lax.broadcasted_iota(jnp.int32, (tk, tq), 1)
                s = jnp.where(key <= qry, s, -jnp.inf)
            s_sc[slot, hh] = s

    def softmax(slot):
        for hh in range(2):
            s3 = s_sc[slot, hh].reshape(tk // 8, 8, tq)
            m_prev = m_sc[hh]
            m_blk = jnp.max(jnp.max(s3, axis=0), axis=0, keepdims=True)
            m_new = jnp.maximum(m_prev, m_blk)
            alpha = jnp.exp2(m_prev - m_new)
            p3 = jnp.exp2(s3 - m_new[None])
            alpha_sc[hh] = alpha
            p_sc[hh] = p3.reshape(tk, tq).astype(BF16)
            m_sc[hh] = m_new

    def apply_values(j):
        off = pl.multiple_of(j * tk, tk)
        ones = jnp.ones((ONES_ROWS, tk), BF16)
        for hh in range(2):
            vt = vt_ref[MLA_V * hh:MLA_V * (hh + 1), pl.ds(off, tk)]
            vt1 = jnp.concatenate([vt, ones], axis=0)
            acc_sc[hh] = alpha_sc[hh][:1] * acc_sc[hh] + _dot(vt1, p_sc[hh])

    @pl.when(qi == 0)
    def _():
        scores(0, True, 0)
        softmax(0)

    def step(j, even, diagonal_next=False, has_next=True):
        cur, nxt = (0, 1) if even else (1, 0)
        if has_next:
            scores(j + 1, diagonal_next, nxt)
        apply_values(jnp.maximum(j - 1, 0))
        softmax(cur)

    @pl.when(qi > 0)
    def _():
        p_sc[...] = jnp.zeros(p_sc.shape, BF16)
        alpha_sc[...] = jnp.ones(alpha_sc.shape, F32)
        scores(0, False, 0)

        def pair(t, carry):
            step(2 * t, True)
            step(2 * t + 1, False)
            return carry

        lax.fori_loop(0, (qi - 1) // 2, pair, 0)

    @pl.when(qi % 2 == 1)
    def _():
        step(qi - 1, True, diagonal_next=True)
        step(qi, False, has_next=False)

    @pl.when((qi % 2 == 0) & (qi > 0))
    def _():
        step(qi - 2, True)
        step(qi - 1, False, diagonal_next=True)
        step(qi, True, has_next=False)

    apply_values(qi)
    o_t = jnp.concatenate([acc_sc[hh, :MLA_V] / acc_sc[hh, MLA_V:MLA_V + 1] for hh in range(2)], axis=0)
    o_ref[...] = o_t.T.astype(BF16)


def _mla_attn(q, k, vt, *, batch, seq, tq):
    return pl.pallas_call(
        functools.partial(_mla_attn_kernel, tq=tq),
        grid=(batch, MLA_HEADS // 2, seq // tq),
        in_specs=[pl.BlockSpec((None, tq, 2 * HEAD_PAD), lambda b, hp, i: (b, i, hp)),
                  pl.BlockSpec((None, seq, 2 * HEAD_PAD), lambda b, hp, i: (b, 0, hp)),
                  pl.BlockSpec((None, 2 * MLA_V, seq), lambda b, hp, i: (b, hp, 0))],
        out_specs=pl.BlockSpec((None, tq, 2 * MLA_V), lambda b, hp, i: (b, i, hp)),
        out_shape=jax.ShapeDtypeStruct((batch, seq, MLA_HEADS * MLA_V), BF16),
        scratch_shapes=[pltpu.VMEM((2, 8, tq), F32), pltpu.VMEM((2, MLA_V + ONES_ROWS, tq), F32), pltpu.VMEM((2, tq, tq), BF16),
                        pltpu.VMEM((2, 8, tq), F32), pltpu.VMEM((2, 2, tq, tq), F32)],
        compiler_params=_cparams(("parallel", "parallel", "arbitrary")),
        name="mla_attn",
    )(q, k, vt)


def _swiglu_tile(x, wgu_ref, wd_ref, acc_sc, tf):
    for f in range(D_FF // tf):
        g = _dot(x, wgu_ref[:, f * tf:(f + 1) * tf])
        u = _dot(x, wgu_ref[:, D_FF + f * tf:D_FF + (f + 1) * tf])
        part = _dot((_silu(g) * u).astype(BF16), wd_ref[f * tf:(f + 1) * tf, :])
        if f == 0:
            acc_sc[...] = part
        else:
            acc_sc[...] += part


def _ffn_kernel(a_ref, wo_ref, h_ref, gtm_ref, g_ref, sc_ref, sh_ref, gt_ref, wgu_ref, wd_ref, o_ref, acc_sc,
                *, tf):
    h = h_ref[...] + gtm_ref[...] * _dot(a_ref[...], wo_ref[...])
    x = _normmod(h, g_ref[...], sc_ref[...], sh_ref[...]).astype(BF16)
    _swiglu_tile(x, wgu_ref, wd_ref, acc_sc, tf)
    o_ref[...] = h + gt_ref[...] * acc_sc[...]


def _ffn(a, w_o, h, gt_m, g, sc, sh, gt, w_gu, w_d, *, tm, tf, tiles_per_batch):
    T, D = h.shape
    bvec = pl.BlockSpec((None, 1, D), lambda i: (i // tiles_per_batch, 0, 0))
    row = pl.BlockSpec((tm, D), lambda i: (i, 0))
    const = lambda w: pl.BlockSpec(w.shape, lambda i: (0,) * w.ndim, pipeline_mode=pl.Buffered(1))
    return pl.pallas_call(
        functools.partial(_ffn_kernel, tf=tf),
        grid=(T // tm,),
        in_specs=[row, const(w_o), row, bvec, const(g), bvec, bvec, bvec, const(w_gu), const(w_d)],
        out_specs=row,
        out_shape=jax.ShapeDtypeStruct((T, D), F32),
        scratch_shapes=[pltpu.VMEM((tm, D), F32)],
        compiler_params=_cparams(("parallel",)),
        name="ffn_dense",
    )(a, w_o, h, gt_m, g, sc, sh, gt, w_gu, w_d)


def _l1_proj_kernel(h_ref, gkv_ref, sckv_ref, shkv_ref, gq_ref, scq_ref, shq_ref, wkv_ref, wq_ref, *refs):
    out_refs, stage_sc = refs[:-1], refs[-1]
    tm = h_ref.shape[0]
    x = h_ref[...]
    xh = x * _rms_scale(x)
    hn_kv = (xh * (gkv_ref[...] * (1.0 + sckv_ref[...])) + shkv_ref[...]).astype(BF16)
    hn_q = (xh * (gq_ref[...] * (1.0 + scq_ref[...])) + shq_ref[...]).astype(BF16)
    for g, (_, dil) in enumerate(DIL_PATTERNS):
        ys = (_dot(hn_q, wq_ref[:, g * DIL_W:(g + 1) * DIL_W]) * (DIL_HEAD_DIM ** -0.5),
              _dot(hn_kv, wkv_ref[:, 2 * g * DIL_W:(2 * g + 1) * DIL_W]),
              _dot(hn_kv, wkv_ref[:, (2 * g + 1) * DIL_W:(2 * g + 2) * DIL_W]))
        for y, out_ref in zip(ys, out_refs[3 * g:3 * g + 3]):
            if dil == 1:
                out_ref[...] = y.astype(BF16)
            else:
                for c in range(DIL_W // LANES):
                    stage_sc[c] = y[:, c * LANES:(c + 1) * LANES]
                for r in range(dil):
                    for c in range(DIL_W // LANES):
                        col = r * DIL_W + c * LANES
                        out_ref[:, col:col + LANES] = stage_sc[c, pl.ds(r, tm // dil, stride=dil), :].astype(BF16)


def _l1_proj(h, gkv, sckv, shkv, gq, scq, shq, wkv, wq, *, tm, tiles_per_batch):
    T, D = h.shape
    bvec = pl.BlockSpec((None, 1, D), lambda i: (i // tiles_per_batch, 0, 0))
    full = lambda a: pl.BlockSpec(a.shape, lambda i: (0,) * a.ndim)
    out_specs, out_shape = [], []
    for _, dil in DIL_PATTERNS:
        for _ in range(3):
            out_specs.append(pl.BlockSpec((tm // dil, dil * DIL_W), lambda i: (i, 0)))
            out_shape.append(jax.ShapeDtypeStruct((T // dil, dil * DIL_W), BF16))
    return pl.pallas_call(
        _l1_proj_kernel,
        grid=(T // tm,),
        in_specs=[pl.BlockSpec((tm, D), lambda i: (i, 0)), full(gkv), bvec, bvec, full(gq), bvec, bvec,
                  full(wkv), full(wq)],
        out_specs=out_specs,
        out_shape=out_shape,
        scratch_shapes=[pltpu.VMEM((DIL_W // LANES, tm, LANES), F32)],
        compiler_params=_cparams(("parallel",)),
        name="l1_proj",
    )(h, gkv, sckv, shkv, gq, scq, shq, wkv, wq)


def _dil_attn_kernel(q_ref, kc_ref, kp_ref, vc_ref, vp_ref, bias_ref, o_ref, lse_ref):
    i = pl.program_id(2)
    lane = lax.broadcasted_iota(jnp.int32, (1, 2 * DIL_HEAD_DIM), 1)
    lo = lane < DIL_HEAD_DIM
    col = lax.broadcasted_iota(jnp.int32, (1, 2 * DIL_BLOCK), 1)
    edge = jnp.where((col < DIL_BLOCK) & (i == 0), MASK_VALUE, 0.0).astype(F32)
    sls = [slice(2 * DIL_HEAD_DIM * hp, 2 * DIL_HEAD_DIM * (hp + 1)) for hp in range(DIL_HEADS // 2)]
    scores = []
    for hp, sl in enumerate(sls):
        q2 = q_ref[:, sl]
        zero = jnp.zeros_like(q2)
        qs = jnp.concatenate([jnp.where(lo, q2, zero), jnp.where(lo, zero, q2)], axis=0)
        k2 = jnp.concatenate([kp_ref[:, sl], kc_ref[:, sl]], axis=0)
        bias = jnp.concatenate([bias_ref[2 * hp], bias_ref[2 * hp + 1]], axis=0)
        scores.append(_dot_nt(qs, k2) + bias + edge)
    probs, stats = [], []
    for s in scores:
        m = jnp.max(s, axis=1, keepdims=True)
        p = jnp.exp(s - m)
        probs.append(p.astype(BF16))
        stats.append((m, jnp.sum(p, axis=1, keepdims=True)))
    for sl, p, (m, l) in zip(sls, probs, stats):
        v2 = jnp.concatenate([vp_ref[:, sl], vc_ref[:, sl]], axis=0)
        o = _dot(p, v2) / l
        lse = jnp.broadcast_to(m + jnp.log(l), o.shape)
        o_ref[:, sl] = jnp.where(lo, o[:DIL_BLOCK], o[DIL_BLOCK:])
        lse_ref[:, sl] = jnp.where(lo, lse[:DIL_BLOCK], lse[DIL_BLOCK:])


def _dil_attn(q, k, v, bias, *, group, dil, batch, seq):
    n = seq // dil
    nb = n // DIL_BLOCK
    q, k, v = (a.reshape(batch, n, dil * DIL_W) for a in (q, k, v))
    blk = (None, DIL_BLOCK, DIL_W)
    cur = pl.BlockSpec(blk, lambda b, r, i: (b, i, r))
    prev = pl.BlockSpec(blk, lambda b, r, i: (b, jnp.maximum(i - 1, 0), r))
    out_sd = jax.ShapeDtypeStruct((batch, n, dil * DIL_W), F32)
    o, lse = pl.pallas_call(
        _dil_attn_kernel,
        grid=(batch, dil, nb),
        in_specs=[cur, cur, prev, cur, prev, pl.BlockSpec(bias.shape, lambda b, r, i: (0, 0, 0))],
        out_specs=[cur, cur],
        out_shape=[out_sd, out_sd],
        compiler_params=_cparams(("parallel", "parallel", "arbitrary")),
        name=f"dil_attn_g{group}",
    )(q, k, k, v, v, bias)
    return o.reshape(batch * n, dil * DIL_W), lse.reshape(batch * n, dil * DIL_W)


def _rpb_bucket(dist):
    exact = RPB_BUCKETS // 2
    d = jnp.maximum(dist, 0)
    d_f = jnp.maximum(d, 1).astype(F32)
    large = exact + (jnp.log(d_f / exact) / math.log(RPB_MAX_DIST / exact)
                     * (RPB_BUCKETS - exact)).astype(jnp.int32)
    return jnp.where(d < exact, d, jnp.minimum(large, RPB_BUCKETS - 1))


def _dil_bias(table, span, dil):
    period = 3 * DIL_BLOCK
    j = jnp.arange(period)
    k_minus_q = jnp.where(j < 2 * DIL_BLOCK, j, j - period)
    dist = DIL_BLOCK - k_minus_q
    band = (dist >= 0) & (dist <= span)
    prof = jnp.where(band[:, None], table[_rpb_bucket(dist * dil)].astype(F32), MASK_VALUE).T
    skew = jnp.tile(prof, (1, DIL_BLOCK))[:, :DIL_BLOCK * (period - 1)]
    return skew.reshape(DIL_HEADS, DIL_BLOCK, period - 1)[:, :, :2 * DIL_BLOCK]


def _dil_out_kernel(o0_ref, o1_ref, o2_ref, l0_ref, l1_ref, l2_ref, w_ref, h_ref, gt_ref, out_ref, *stages):
    tm = h_ref.shape[0]
    stages = list(stages)

    def token_major(ref, dil):
        if dil == 1:
            return ref[...]
        stage = stages.pop()
        for r in range(dil):
            for c in range(DIL_W // LANES):
                col = r * DIL_W + c * LANES
                stage[c, pl.ds(r, tm // dil, stride=dil), :] = ref[:, col:col + LANES]
        return jnp.concatenate([stage[c] for c in range(DIL_W // LANES)], axis=1)

    dils = [dil for _, dil in DIL_PATTERNS]
    l0, l1, l2 = (token_major(ref, d) for ref, d in zip((l0_ref, l1_ref, l2_ref), dils))
    o0, o1, o2 = (token_major(ref, d) for ref, d in zip((o0_ref, o1_ref, o2_ref), dils))
    m = jnp.maximum(jnp.maximum(l0, l1), l2)
    w0, w1, w2 = jnp.exp(l0 - m), jnp.exp(l1 - m), jnp.exp(l2 - m)
    o = (o0 * w0 + o1 * w1 + o2 * w2) / (w0 + w1 + w2)
    out_ref[...] = h_ref[...] + gt_ref[...] * _dot(o.astype(BF16), w_ref[...])


def _dil_out(os_, ls_, w, h, gt, *, tm, tiles_per_batch):
    T, D = h.shape
    row = lambda n: pl.BlockSpec((tm, n), lambda i: (i, 0))
    grp = [pl.BlockSpec((tm // dil, dil * DIL_W), lambda i: (i, 0)) for _, dil in DIL_PATTERNS]
    n_stage = 2 * sum(1 for _, dil in DIL_PATTERNS if dil > 1)
    return pl.pallas_call(
        _dil_out_kernel,
        grid=(T // tm,),
        in_specs=grp + grp + [pl.BlockSpec(w.shape, lambda i: (0, 0)), row(D),
                              pl.BlockSpec((None, 1, D), lambda i: (i // tiles_per_batch, 0, 0))],
        out_specs=row(D),
        out_shape=jax.ShapeDtypeStruct((T, D), F32),
        scratch_shapes=[pltpu.VMEM((DIL_W // LANES, tm, LANES), F32)] * n_stage,
        compiler_params=_cparams(("parallel",)),
        name="dil_out",
    )(*os_, *ls_, w, h, gt)


def _route_kernel(h_ref, g_ref, sc_ref, sh_ref, wrt_ref, hn_ref, eid_ref, rank_ref, gate_ref, cnt_ref, carry_sc):
    @pl.when(pl.program_id(0) == 0)
    def _():
        carry_sc[...] = jnp.zeros(carry_sc.shape, F32)

    hn = _normmod(h_ref[...], g_ref[...], sc_ref[...], sh_ref[...])
    hn_ref[...] = hn
    tm = hn.shape[0]
    logits = lax.dot_general(wrt_ref[...], hn, (((1,), (1,)), ((), ())), preferred_element_type=F32,
                             precision=lax.Precision.HIGHEST)
    idx = lax.broadcasted_iota(jnp.int32, logits.shape, 0)
    v1 = jnp.max(logits, axis=0, keepdims=True)
    i1 = jnp.min(jnp.where(logits == v1, idx, N_EXPERTS), axis=0, keepdims=True)
    rest = jnp.where(idx == i1, -jnp.inf, logits)
    v2 = jnp.max(rest, axis=0, keepdims=True)
    i2 = jnp.min(jnp.where(rest == v2, idx, N_EXPERTS), axis=0, keepdims=True)
    e = jnp.exp(v2 - v1)
    gate_ref[...] = jnp.concatenate([1.0 / (1.0 + e), e / (1.0 + e)], axis=0)
    eid_ref[...] = jnp.concatenate([i1, i2], axis=0)

    sel = ((idx == i1) | (idx == i2)).astype(BF16)
    before = (lax.broadcasted_iota(jnp.int32, (tm, tm), 0)
              < lax.broadcasted_iota(jnp.int32, (tm, tm), 1)).astype(BF16)
    rank_all = carry_sc[:, :1] + _dot(sel, before)
    r1 = jnp.sum(jnp.where(idx == i1, rank_all, 0.0), axis=0, keepdims=True)
    r2 = jnp.sum(jnp.where(idx == i2, rank_all, 0.0), axis=0, keepdims=True)
    rank_ref[...] = jnp.concatenate([r1, r2], axis=0).astype(jnp.int32)
    carry_sc[...] += jnp.sum(sel.astype(F32), axis=1, keepdims=True)
    cnt_ref[...] = carry_sc[...].astype(jnp.int32)


def _route(h, g, sc, sh, wrt, *, tm, tiles_per_batch):
    T, D = h.shape
    bvec = pl.BlockSpec((None, 1, D), lambda i: (i // tiles_per_batch, 0, 0))
    lane_blk = pl.BlockSpec((2, tm), lambda i: (0, i))
    return pl.pallas_call(
        _route_kernel,
        grid=(T // tm,),
        in_specs=[pl.BlockSpec((tm, D), lambda i: (i, 0)), pl.BlockSpec((1, D), lambda i: (0, 0)), bvec, bvec,
                  pl.BlockSpec(wrt.shape, lambda i: (0, 0))],
        out_specs=[pl.BlockSpec((tm, D), lambda i: (i, 0)), lane_blk, lane_blk, lane_blk,
                   pl.BlockSpec((N_EXPERTS, HEAD_PAD), lambda i: (0, 0))],
        out_shape=[jax.ShapeDtypeStruct((T, D), F32), jax.ShapeDtypeStruct((2, T), jnp.int32),
                   jax.ShapeDtypeStruct((2, T), jnp.int32), jax.ShapeDtypeStruct((2, T), F32),
                   jax.ShapeDtypeStruct((N_EXPERTS, HEAD_PAD), jnp.int32)],
        scratch_shapes=[pltpu.VMEM((N_EXPERTS, HEAD_PAD), F32)],
        compiler_params=_cparams(("arbitrary",)),
        name="moe_route",
    )(h, g, sc, sh, wrt)


def _dispatch_kernel(pad_ref, dest_ref, hn_ref, xs_ref, zero_sc, sem, zsem):
    tm = hn_ref.shape[0]

    def row_copy(r, k):
        return pltpu.make_async_copy(hn_ref.at[pl.ds(r, 1), :], xs_ref.at[pl.ds(dest_ref[k, r], 1), :], sem)

    def issue(r, carry):
        row_copy(r, 0).start()
        row_copy(r, 1).start()
        return carry

    for r in range(tm):
        issue(r, 0)

    @pl.when(pl.program_id(0) == pl.num_programs(0) - 1)
    def _():
        zero_sc[...] = jnp.zeros(zero_sc.shape, F32)

        def pad_copy(p):
            return pltpu.make_async_copy(zero_sc.at[pl.ds(0, 1), :], xs_ref.at[pl.ds(p, 1), :], zsem)

        for e in range(N_EXPERTS):
            lo, hi = pad_ref[0, e], pad_ref[1, e]

            def zissue(p, carry):
                pad_copy(p).start()
                return carry

            def zwait(p, carry):
                pad_copy(p).wait()
                return carry

            lax.fori_loop(lo, hi, zissue, 0)
            lax.fori_loop(lo, hi, zwait, 0)

        zrows = zero_sc.shape[0]

        def tail_copy(c):
            return pltpu.make_async_copy(zero_sc, xs_ref.at[pl.ds(pl.multiple_of(c * zrows, zrows), zrows), :], zsem)

        def tissue(c, carry):
            tail_copy(c).start()
            return carry

        def twait(c, carry):
            tail_copy(c).wait()
            return carry

        lo, hi = pad_ref[1, N_EXPERTS - 1] // zrows, xs_ref.shape[0] // zrows
        lax.fori_loop(lo, hi, tissue, 0)
        lax.fori_loop(lo, hi, twait, 0)

    for _ in range(2):
        pltpu.make_async_copy(hn_ref, xs_ref.at[pl.ds(0, tm), :], sem).wait()


def _dispatch(pad_rows, dest, hn, *, n_slots, tm):
    T, D = hn.shape
    return pl.pallas_call(
        _dispatch_kernel,
        grid_spec=pltpu.PrefetchScalarGridSpec(
            num_scalar_prefetch=1,
            grid=(T // tm,),
            in_specs=[pl.BlockSpec((2, tm), lambda i, pad: (0, i), memory_space=pltpu.SMEM),
                      pl.BlockSpec((tm, D), lambda i, pad: (i, 0))],
            out_specs=pl.BlockSpec(memory_space=pl.ANY),
            scratch_shapes=[pltpu.VMEM((64, D), F32), pltpu.SemaphoreType.DMA, pltpu.SemaphoreType.DMA]),
        out_shape=jax.ShapeDtypeStruct((n_slots, D), F32),
        compiler_params=_cparams(("arbitrary",)),
        name="moe_dispatch",
    )(pad_rows, dest, hn)


def _experts_kernel(te_ref, na_ref, xs_ref, wgu_ref, wd_ref, y_ref, acc_sc, *, tf):
    @pl.when(pl.program_id(0) < na_ref[0])
    def _():
        _swiglu_tile(xs_ref[...].astype(BF16), wgu_ref, wd_ref, acc_sc, tf)
        y_ref[...] = acc_sc[...]

    @pl.when(pl.program_id(0) >= na_ref[0])
    def _():
        y_ref[...] = jnp.zeros(y_ref.shape, F32)


def _experts(tile_expert, n_active, xs, w_gu, w_d, *, tm, tf):
    P, D = xs.shape

    def tile(j, te, na):
        return jnp.minimum(j, na[0] - 1)

    return pl.pallas_call(
        functools.partial(_experts_kernel, tf=tf),
        grid_spec=pltpu.PrefetchScalarGridSpec(
            num_scalar_prefetch=2,
            grid=(P // tm,),
            in_specs=[pl.BlockSpec((tm, D), lambda j, te, na: (tile(j, te, na), 0)),
                      pl.BlockSpec((None, D, 2 * D_FF), lambda j, te, na: (te[tile(j, te, na)], 0, 0)),
                      pl.BlockSpec((None, D_FF, D), lambda j, te, na: (te[tile(j, te, na)], 0, 0))],
            out_specs=pl.BlockSpec((tm, D), lambda j, te, na: (j, 0)),
            scratch_shapes=[pltpu.VMEM((tm, D), F32)]),
        out_shape=jax.ShapeDtypeStruct((P, D), F32),
        compiler_params=_cparams(("arbitrary",)),
        name="moe_experts",
    )(tile_expert, n_active, xs, w_gu, w_d)


def _combine_kernel(dest_ref, y_ref, gates_ref, h_ref, gt_ref, gfin_ref, o_ref, ybuf, sem):
    tm = h_ref.shape[0]

    def row_copy(r, k):
        return pltpu.make_async_copy(y_ref.at[pl.ds(dest_ref[k, r], 1), :], ybuf.at[k, pl.ds(r, 1), :], sem)

    def issue(r, carry):
        row_copy(r, 0).start()
        row_copy(r, 1).start()
        return carry

    for r in range(tm):
        issue(r, 0)
    for k in range(2):
        pltpu.make_async_copy(y_ref.at[pl.ds(0, tm), :], ybuf.at[k], sem).wait()
    gates = gates_ref[...]
    moe = gates[:, 0:1] * ybuf[0] + gates[:, 1:2] * ybuf[1]
    y = h_ref[...] + gt_ref[...] * moe
    o_ref[...] = y * _rms_scale(y) * gfin_ref[...]


def _combine(dest, y, gates, h, gt, gfin, *, tm, tiles_per_batch):
    T, D = h.shape
    return pl.pallas_call(
        _combine_kernel,
        grid=(T // tm,),
        in_specs=[pl.BlockSpec((2, tm), lambda i: (0, i), memory_space=pltpu.SMEM),
                  pl.BlockSpec(memory_space=pl.ANY),
                  pl.BlockSpec((tm, 2), lambda i: (i, 0)),
                  pl.BlockSpec((tm, D), lambda i: (i, 0)),
                  pl.BlockSpec((None, 1, D), lambda i: (i // tiles_per_batch, 0, 0)),
                  pl.BlockSpec((1, D), lambda i: (0, 0))],
        out_specs=pl.BlockSpec((tm, D), lambda i: (i, 0)),
        out_shape=jax.ShapeDtypeStruct((T, D), F32),
        scratch_shapes=[pltpu.VMEM((2, tm, D), F32), pltpu.SemaphoreType.DMA],
        compiler_params=_cparams(("arbitrary",)),
        name="moe_combine",
    )(dest, y, gates, h, gt, gfin)


def _moe_plan(eids, ranks, counts, *, tm, n_tiles):
    padded = (counts + tm - 1) // tm * tm
    ends = jnp.cumsum(padded)
    starts = ends - padded
    dest = ranks
    for e in range(N_EXPERTS):
        dest = dest + jnp.where(eids == e, starts[e], 0)
    tile_start = jnp.arange(n_tiles, dtype=jnp.int32) * tm
    tile_expert = jnp.minimum(jnp.sum(tile_start[:, None] >= ends[None, :], axis=1), N_EXPERTS - 1)
    n_active = (ends[-1] // tm).reshape(1)
    pad_rows = jnp.stack([starts + counts, ends])
    return dest.astype(jnp.int32), tile_expert.astype(jnp.int32), n_active.astype(jnp.int32), pad_rows.astype(jnp.int32)


def _mla_weights(w_in, w_q_up, w_kv_up):
    D = w_in.shape[0]
    half = MLA_ROPE // 2
    w_ql = w_in[:, :MLA_Q_RANK]
    w_kvl = w_in[:, MLA_Q_RANK:MLA_Q_RANK + MLA_KV_RANK]
    w_kr = w_in[:, MLA_Q_RANK + MLA_KV_RANK:]
    z64 = jnp.zeros((D, MLA_NOPE), F32)
    z32 = jnp.zeros((D, HEAD_PAD - MLA_QK), F32)
    kr_pad = jnp.concatenate([z64, w_kr, z32], axis=1)
    kr_swap = jnp.concatenate([z64, w_kr[:, half:], w_kr[:, :half], z32], axis=1)
    w1 = jnp.concatenate([w_ql, w_kvl, kr_pad, kr_swap], axis=1).astype(BF16)

    wq = w_q_up.reshape(MLA_Q_RANK, MLA_HEADS, MLA_QK)
    nope, rope = wq[..., :MLA_NOPE], wq[..., MLA_NOPE:]
    zq = jnp.zeros((MLA_Q_RANK, MLA_HEADS, HEAD_PAD - MLA_QK), F32)
    wqa = jnp.concatenate([nope, rope, zq], axis=-1).reshape(MLA_Q_RANK, MLA_HEADS * HEAD_PAD).astype(BF16)
    wqb = jnp.concatenate([jnp.zeros_like(nope), rope[..., half:], rope[..., :half], zq], axis=-1)
    wqb = wqb.reshape(MLA_Q_RANK, MLA_HEADS * HEAD_PAD).astype(BF16)

    wkv = w_kv_up.reshape(MLA_KV_RANK, MLA_HEADS, MLA_NOPE + MLA_V)
    k_nope, v = wkv[..., :MLA_NOPE], wkv[..., MLA_NOPE:]
    zk = jnp.zeros((MLA_KV_RANK, MLA_HEADS, HEAD_PAD - MLA_NOPE), F32)
    wka = jnp.concatenate([k_nope, zk], axis=-1).reshape(MLA_KV_RANK, MLA_HEADS * HEAD_PAD).astype(BF16)
    wvt = v.reshape(MLA_KV_RANK, MLA_HEADS * MLA_V).T.astype(BF16)
    return w1, wqa, wqb, wka, wvt


def _rope_tables(positions):
    inv_freq = ROPE_THETA ** (-jnp.arange(0, MLA_ROPE, 2, dtype=F32) / MLA_ROPE)
    ang = positions.astype(F32).reshape(-1, 1) * inv_freq
    cos, sin = jnp.cos(ang), jnp.sin(ang)
    T = ang.shape[0]
    one = jnp.ones((T, MLA_NOPE), F32)
    z64 = jnp.zeros((T, MLA_NOPE), F32)
    z32 = jnp.zeros((T, HEAD_PAD - MLA_QK), F32)
    return (jnp.concatenate([one, cos, cos, z32], axis=1),
            jnp.concatenate([z64, -sin, sin, z32], axis=1))


def kernel(x, c, positions, g_mix, g_ffn, w_ada, b_ada, w_mla_in, g_mla_q, w_mla_q_up, g_mla_kv, w_mla_kv_up,
           w_mla_out, g_kv_b, w_ada_kv, b_ada_kv, w_kv_b, rpb_table, w_q_b, w_o_b, w_ffn_gu, w_ffn_down,
           w_router, w_exp_gu, w_exp_down, g_final):
    B, S, D = x.shape
    T = B * S
    h = x.reshape(T, D)

    c8 = jnp.zeros((8, D), F32).at[:B].set(c)
    mod = _ada(c8, w_ada, b_ada[:, None, :])[:, :B]
    mod = mod.reshape(2, B, N_MOD, 1, D)
    mod_kv = _ada(c8, w_ada_kv[None], b_ada_kv[None, None, :])[0, :B].reshape(B, 2, 1, D)
    sh_kv, sc_kv = mod_kv[:, 0], mod_kv[:, 1]

    def mods(layer):
        return [mod[layer, :, k] for k in range(N_MOD)]

    row = lambda v: v.reshape(1, -1)

    sh_m, sc_m, gt_m, sh_f, sc_f, gt_f = mods(0)
    w1, wqa, wqb, wka, wvt = _mla_weights(w_mla_in[0], w_mla_q_up[0], w_mla_kv_up[0])
    cos_t, sin_t = _rope_tables(positions)
    q, k, vt = _mla_proj(h, row(g_mix[0]), sc_m, sh_m, cos_t, sin_t, w1, row(g_mla_q[0]), row(g_mla_kv[0]),
                         wqa, wqb, wka, wvt, tm=256, batch=B, seq=S)
    o = _mla_attn(q.reshape(B, S, -1), k.reshape(B, S, -1), vt, batch=B, seq=S, tq=512)

    tm = 512
    h = _ffn(o.reshape(T, -1), w_mla_out[0].astype(BF16), h, gt_m, row(g_ffn[0]), sc_f, sh_f, gt_f,
             w_ffn_gu[0].astype(BF16), w_ffn_down[0].astype(BF16), tm=tm, tf=256, tiles_per_batch=S // tm)

    sh_m, sc_m, gt_m, sh_f, sc_f, gt_f = mods(1)
    tm = 512
    qkv = _l1_proj(h, row(g_kv_b), sc_kv, sh_kv, row(g_mix[1]), sc_m, sh_m,
                   w_kv_b.astype(BF16), w_q_b[0].astype(BF16), tm=tm, tiles_per_batch=S // tm)

    outs, lses = [], []
    for g, (window, dil) in enumerate(DIL_PATTERNS):
        bias = _dil_bias(rpb_table[:, g * DIL_HEADS:(g + 1) * DIL_HEADS], window // dil, dil)
        o_g, lse_g = _dil_attn(*qkv[3 * g:3 * g + 3], bias, group=g, dil=dil, batch=B, seq=S)
        outs.append(o_g)
        lses.append(lse_g)
    h = _dil_out(outs, lses, w_o_b[0].astype(BF16), h, gt_m, tm=tm, tiles_per_batch=S // tm)

    hn, eids, ranks, gates, counts = _route(h, row(g_ffn[1]), sc_f, sh_f, w_router[0].T, tm=tm,
                                            tiles_per_batch=S // tm)
    n_tiles = (T * 2) // MOE_TM + N_EXPERTS
    dest, tile_expert, n_active, pad_rows = _moe_plan(eids, ranks, counts[:, 0], tm=MOE_TM, n_tiles=n_tiles)
    xs = _dispatch(pad_rows, dest, hn, n_slots=n_tiles * MOE_TM, tm=tm)
    y = _experts(tile_expert, n_active, xs, w_exp_gu[0].astype(BF16), w_exp_down[0].astype(BF16),
                 tm=MOE_TM, tf=256)
    tm = 256
    out = _combine(dest, y, gates.T, h, gt_f, row(g_final), tm=tm, tiles_per_batch=S // tm)
    return out.reshape(B, S, D)
```

```python
import functools
import math

import numpy as np
import jax
import jax.numpy as jnp
from jax import lax
from jax.experimental import pallas as pl
from jax.experimental.pallas import tpu as pltpu

D_MODEL = 1024
N_MOD = 6
EPS = 1e-6

MLA_HEADS = 16
MLA_Q_RANK = 384
MLA_KV_RANK = 256
MLA_NOPE = 64
MLA_ROPE = 32
MLA_V = 64
MLA_QK = MLA_NOPE + MLA_ROPE
ROPE_THETA = 10000.0
LANES = 128
ONES_ROWS = 16
HEAD_PAD = 128

DIL_PATTERNS = ((128, 1), (512, 4), (2048, 16))
DIL_GROUPS = len(DIL_PATTERNS)
DIL_HEADS = 8
DIL_HEAD_DIM = 64
DIL_BLOCK = 128
DIL_W = DIL_HEADS * DIL_HEAD_DIM
RPB_BUCKETS = 32
RPB_MAX_DIST = 2048

D_FF = 2816
N_EXPERTS = 8
MOE_TM = 512

MASK_VALUE = -1e30
LOG2E = math.log2(math.e)

F32 = jnp.float32
BF16 = jnp.bfloat16

VMEM_LIMIT = 56 * 1024 * 1024


def _cparams(sem):
    return pltpu.CompilerParams(dimension_semantics=sem, vmem_limit_bytes=VMEM_LIMIT)


def _dot(a, b):
    return jnp.dot(a, b, preferred_element_type=F32)


def _dot_nt(a, b):
    return lax.dot_general(a, b, (((1,), (1,)), ((), ())), preferred_element_type=F32)


def _rms_scale(x):
    return lax.rsqrt(jnp.mean(x * x, axis=-1, keepdims=True) + EPS)


def _normmod(x, g, sc, sh):
    return (x * _rms_scale(x)) * (g * (1.0 + sc)) + sh


def _silu(x):
    return x * (1.0 / (1.0 + jnp.exp(-x)))


def _ada_kernel(c_ref, w_ref, b_ref, o_ref):
    c = c_ref[...]
    o_ref[...] = jnp.dot(_silu(c), w_ref[...], preferred_element_type=F32,
                         precision=lax.Precision.HIGHEST) + b_ref[...]


def _ada(c8, w, b, tn=512):
    L, D, N = w.shape
    return pl.pallas_call(
        _ada_kernel,
        grid=(L, N // tn),
        in_specs=[pl.BlockSpec((8, D), lambda l, j: (0, 0)),
                  pl.BlockSpec((None, D, tn), lambda l, j: (l, 0, j)),
                  pl.BlockSpec((None, 1, tn), lambda l, j: (l, 0, j))],
        out_specs=pl.BlockSpec((None, 8, tn), lambda l, j: (l, 0, j)),
        out_shape=jax.ShapeDtypeStruct((L, 8, N), F32),
        compiler_params=_cparams(("parallel", "parallel")),
        name="ada_mod",
    )(c8, w, b)


def _mla_proj_kernel(h_ref, g_ref, sc_ref, sh_ref, cos_ref, sin_ref, w1_ref, gq_ref, gkv_ref,
                     wqa_ref, wqb_ref, wka_ref, wvt_ref, q_ref, k_ref, vt_ref, *, qscale):
    hn = _normmod(h_ref[...], g_ref[...], sc_ref[...], sh_ref[...]).astype(BF16)
    z = _dot(hn, w1_ref[...])
    ql = z[:, :MLA_Q_RANK]
    kvl = z[:, MLA_Q_RANK:MLA_Q_RANK + MLA_KV_RANK]
    kr = z[:, MLA_Q_RANK + MLA_KV_RANK:MLA_Q_RANK + MLA_KV_RANK + HEAD_PAD]
    krs = z[:, MLA_Q_RANK + MLA_KV_RANK + HEAD_PAD:]
    qn = (ql * _rms_scale(ql) * gq_ref[...]).astype(BF16)
    kvn = (kvl * _rms_scale(kvl) * gkv_ref[...]).astype(BF16)
    cos = cos_ref[...]
    sin = sin_ref[...]
    k_rope = kr * cos + krs * sin
    cos2 = jnp.concatenate([cos, cos], axis=1)
    sin2 = jnp.concatenate([sin, sin], axis=1)
    k_rope2 = jnp.concatenate([k_rope, k_rope], axis=1)
    for hp in range(MLA_HEADS // 2):
        sl = slice(2 * HEAD_PAD * hp, 2 * HEAD_PAD * (hp + 1))
        a = _dot(qn, wqa_ref[:, sl])
        b = _dot(qn, wqb_ref[:, sl])
        q_ref[:, sl] = ((a * cos2 + b * sin2) * qscale).astype(BF16)
        k_ref[:, sl] = (_dot(kvn, wka_ref[:, sl]) + k_rope2).astype(BF16)
    vt_ref[...] = _dot_nt(wvt_ref[...], kvn).astype(BF16)


def _mla_proj(h, g, sc, sh, cos_t, sin_t, w1, gq, gkv, wqa, wqb, wka, wvt, *, tm, batch, seq):
    T, D = h.shape
    HP = MLA_HEADS * HEAD_PAD
    tpb = seq // tm
    row = lambda n: pl.BlockSpec((tm, n), lambda i: (i, 0))
    bvec = pl.BlockSpec((None, 1, D), lambda i: (i // tpb, 0, 0))
    full = lambda a: pl.BlockSpec(a.shape, lambda i: (0,) * a.ndim)
    return pl.pallas_call(
        functools.partial(_mla_proj_kernel, qscale=(MLA_QK ** -0.5) * LOG2E),
        grid=(T // tm,),
        in_specs=[row(D), full(g), bvec, bvec, row(HEAD_PAD), row(HEAD_PAD), full(w1), full(gq), full(gkv),
                  full(wqa), full(wqb), full(wka), full(wvt)],
        out_specs=[row(HP), row(HP),
                   pl.BlockSpec((None, MLA_HEADS * MLA_V, tm), lambda i: (i // tpb, 0, i % tpb))],
        out_shape=[jax.ShapeDtypeStruct((T, HP), BF16), jax.ShapeDtypeStruct((T, HP), BF16),
                   jax.ShapeDtypeStruct((batch, MLA_HEADS * MLA_V, seq), BF16)],
        compiler_params=_cparams(("parallel",)),
        name="mla_proj",
    )(h, g, sc, sh, cos_t, sin_t, w1, gq, gkv, wqa, wqb, wka, wvt)


def _mla_attn_kernel(q_ref, k_ref, vt_ref, o_ref, m_sc, acc_sc, p_sc, alpha_sc, s_sc, mb_sc, *, tq):
    qi = pl.program_id(2)
    tk = tq
    m_sc[...] = jnp.full(m_sc.shape, -jnp.inf, F32)
    acc_sc[...] = jnp.zeros(acc_sc.shape, F32)

    def scores(j, diagonal, slot):
        off = pl.multiple_of(j * tk, tk)
        for hh in range(2):
            q = q_ref[:, HEAD_PAD * hh:HEAD_PAD * (hh + 1)]
            k = k_ref[pl.ds(off, tk), HEAD_PAD * hh:HEAD_PAD * (hh + 1)]
            s = _dot_nt(k, q)
            if diagonal:
                key = lax.broadcasted_iota(jnp.int32, (tk, tq), 0)
                qry = lax.broadcasted_iota(jnp.int32, (tk, tq), 1)
                s = jnp.where(key <= qry, s, -jnp.inf)
            s_sc[slot, hh] = s
            m_blk = jnp.max(jnp.max(s.reshape(tk // 8, 8, tq), axis=0), axis=0, keepdims=True)
            mb_sc[slot, hh] = jnp.broadcast_to(m_blk, (8, tq))

    def softmax(slot):
        for hh in range(2):
            s3 = s_sc[slot, hh].reshape(tk // 8, 8, tq)
            m_prev = m_sc[hh]
            m_new = jnp.maximum(m_prev, mb_sc[slot, hh])
            alpha = jnp.exp2(m_prev - m_new)
            p3 = jnp.exp2(s3 - m_new[None])
            alpha_sc[hh] = alpha
            p_sc[hh] = p3.reshape(tk, tq).astype(BF16)
            m_sc[hh] = m_new

    def apply_values(j):
        off = pl.multiple_of(j * tk, tk)
        ones = jnp.ones((ONES_ROWS, tk), BF16)
        for hh in range(2):
            vt = vt_ref[MLA_V * hh:MLA_V * (hh + 1), pl.ds(off, tk)]
            vt1 = jnp.concatenate([vt, ones], axis=0)
            acc_sc[hh] = alpha_sc[hh][:1] * acc_sc[hh] + _dot(vt1, p_sc[hh])

    @pl.when(qi == 0)
    def _():
        scores(0, True, 0)
        softmax(0)

    def step(j, even, diagonal_next=False, has_next=True):
        cur, nxt = (0, 1) if even else (1, 0)
        if has_next:
            scores(j + 1, diagonal_next, nxt)
        apply_values(jnp.maximum(j - 1, 0))
        softmax(cur)

    @pl.when(qi > 0)
    def _():
        p_sc[...] = jnp.zeros(p_sc.shape, BF16)
        alpha_sc[...] = jnp.ones(alpha_sc.shape, F32)
        scores(0, False, 0)

        def pair(t, carry):
            step(2 * t, True)
            step(2 * t + 1, False)
            return carry

        lax.fori_loop(0, (qi - 1) // 2, pair, 0)

    @pl.when(qi % 2 == 1)
    def _():
        step(qi - 1, True, diagonal_next=True)
        step(qi, False, has_next=False)

    @pl.when((qi % 2 == 0) & (qi > 0))
    def _():
        step(qi - 2, True)
        step(qi - 1, False, diagonal_next=True)
        step(qi, True, has_next=False)

    apply_values(qi)
    o_t = jnp.concatenate([acc_sc[hh, :MLA_V] / acc_sc[hh, MLA_V:MLA_V + 1] for hh in range(2)], axis=0)
    o_ref[...] = o_t.T.astype(BF16)


def _mla_attn(q, k, vt, *, batch, seq, tq):
    return pl.pallas_call(
        functools.partial(_mla_attn_kernel, tq=tq),
        grid=(batch, MLA_HEADS // 2, seq // tq),
        in_specs=[pl.BlockSpec((None, tq, 2 * HEAD_PAD), lambda b, hp, i: (b, i, hp)),
                  pl.BlockSpec((None, seq, 2 * HEAD_PAD), lambda b, hp, i: (b, 0, hp)),
                  pl.BlockSpec((None, 2 * MLA_V, seq), lambda b, hp, i: (b, hp, 0))],
        out_specs=pl.BlockSpec((None, tq, 2 * MLA_V), lambda b, hp, i: (b, i, hp)),
        out_shape=jax.ShapeDtypeStruct((batch, seq, MLA_HEADS * MLA_V), BF16),
        scratch_shapes=[pltpu.VMEM((2, 8, tq), F32), pltpu.VMEM((2, MLA_V + ONES_ROWS, tq), F32), pltpu.VMEM((2, tq, tq), BF16),
                        pltpu.VMEM((2, 8, tq), F32), pltpu.VMEM((2, 2, tq, tq), F32),
                        pltpu.VMEM((2, 2, 8, tq), F32)],
        compiler_params=_cparams(("parallel", "parallel", "arbitrary")),
        name="mla_attn",
    )(q, k, vt)


def _swiglu_tile(x, wgu_ref, wd_ref, acc_sc, tf):
    for f in range(D_FF // tf):
        g = _dot(x, wgu_ref[:, f * tf:(f + 1) * tf])
        u = _dot(x, wgu_ref[:, D_FF + f * tf:D_FF + (f + 1) * tf])
        part = _dot((_silu(g) * u).astype(BF16), wd_ref[f * tf:(f + 1) * tf, :])
        if f == 0:
            acc_sc[...] = part
        else:
            acc_sc[...] += part


def _ffn_kernel(a_ref, wo_ref, h_ref, gtm_ref, g_ref, sc_ref, sh_ref, gt_ref, wgu_ref, wd_ref, o_ref, acc_sc,
                *, tf):
    h = h_ref[...] + gtm_ref[...] * _dot(a_ref[...], wo_ref[...])
    x = _normmod(h, g_ref[...], sc_ref[...], sh_ref[...]).astype(BF16)
    _swiglu_tile(x, wgu_ref, wd_ref, acc_sc, tf)
    o_ref[...] = h + gt_ref[...] * acc_sc[...]


def _ffn(a, w_o, h, gt_m, g, sc, sh, gt, w_gu, w_d, *, tm, tf, tiles_per_batch):
    T, D = h.shape
    bvec = pl.BlockSpec((None, 1, D), lambda i: (i // tiles_per_batch, 0, 0))
    row = pl.BlockSpec((tm, D), lambda i: (i, 0))
    const = lambda w: pl.BlockSpec(w.shape, lambda i: (0,) * w.ndim, pipeline_mode=pl.Buffered(1))
    return pl.pallas_call(
        functools.partial(_ffn_kernel, tf=tf),
        grid=(T // tm,),
        in_specs=[row, const(w_o), row, bvec, const(g), bvec, bvec, bvec, const(w_gu), const(w_d)],
        out_specs=row,
        out_shape=jax.ShapeDtypeStruct((T, D), F32),
        scratch_shapes=[pltpu.VMEM((tm, D), F32)],
        compiler_params=_cparams(("parallel",)),
        name="ffn_dense",
    )(a, w_o, h, gt_m, g, sc, sh, gt, w_gu, w_d)


def _l1_proj_kernel(h_ref, gkv_ref, sckv_ref, shkv_ref, gq_ref, scq_ref, shq_ref, wkv_ref, wq_ref, *refs):
    out_refs, stage_sc = refs[:-1], refs[-1]
    tm = h_ref.shape[0]
    x = h_ref[...]
    xh = x * _rms_scale(x)
    hn_kv = (xh * (gkv_ref[...] * (1.0 + sckv_ref[...])) + shkv_ref[...]).astype(BF16)
    hn_q = (xh * (gq_ref[...] * (1.0 + scq_ref[...])) + shq_ref[...]).astype(BF16)
    for g, (_, dil) in enumerate(DIL_PATTERNS):
        ys = (_dot(hn_q, wq_ref[:, g * DIL_W:(g + 1) * DIL_W]) * (DIL_HEAD_DIM ** -0.5),
              _dot(hn_kv, wkv_ref[:, 2 * g * DIL_W:(2 * g + 1) * DIL_W]),
              _dot(hn_kv, wkv_ref[:, (2 * g + 1) * DIL_W:(2 * g + 2) * DIL_W]))
        for y, out_ref in zip(ys, out_refs[3 * g:3 * g + 3]):
            if dil == 1:
                out_ref[...] = y.astype(BF16)
            else:
                for c in range(DIL_W // LANES):
                    stage_sc[c] = y[:, c * LANES:(c + 1) * LANES]
                for r in range(dil):
                    for c in range(DIL_W // LANES):
                        col = r * DIL_W + c * LANES
                        out_ref[:, col:col + LANES] = stage_sc[c, pl.ds(r, tm // dil, stride=dil), :].astype(BF16)


def _l1_proj(h, gkv, sckv, shkv, gq, scq, shq, wkv, wq, *, tm, tiles_per_batch):
    T, D = h.shape
    bvec = pl.BlockSpec((None, 1, D), lambda i: (i // tiles_per_batch, 0, 0))
    full = lambda a: pl.BlockSpec(a.shape, lambda i: (0,) * a.ndim)
    out_specs, out_shape = [], []
    for _, dil in DIL_PATTERNS:
        for _ in range(3):
            out_specs.append(pl.BlockSpec((tm // dil, dil * DIL_W), lambda i: (i, 0)))
            out_shape.append(jax.ShapeDtypeStruct((T // dil, dil * DIL_W), BF16))
    return pl.pallas_call(
        _l1_proj_kernel,
        grid=(T // tm,),
        in_specs=[pl.BlockSpec((tm, D), lambda i: (i, 0)), full(gkv), bvec, bvec, full(gq), bvec, bvec,
                  full(wkv), full(wq)],
        out_specs=out_specs,
        out_shape=out_shape,
        scratch_shapes=[pltpu.VMEM((DIL_W // LANES, tm, LANES), F32)],
        compiler_params=_cparams(("parallel",)),
        name="l1_proj",
    )(h, gkv, sckv, shkv, gq, scq, shq, wkv, wq)


def _dil_attn_kernel(q_ref, kc_ref, kp_ref, vc_ref, vp_ref, bias_ref, o_ref, lse_ref):
    i = pl.program_id(2)
    lane = lax.broadcasted_iota(jnp.int32, (1, 2 * DIL_HEAD_DIM), 1)
    lo = lane < DIL_HEAD_DIM
    col = lax.broadcasted_iota(jnp.int32, (1, 2 * DIL_BLOCK), 1)
    edge = jnp.where((col < DIL_BLOCK) & (i == 0), MASK_VALUE, 0.0).astype(F32)
    sls = [slice(2 * DIL_HEAD_DIM * hp, 2 * DIL_HEAD_DIM * (hp + 1)) for hp in range(DIL_HEADS // 2)]
    scores = []
    for hp, sl in enumerate(sls):
        q2 = q_ref[:, sl]
        zero = jnp.zeros_like(q2)
        qs = jnp.concatenate([jnp.where(lo, q2, zero), jnp.where(lo, zero, q2)], axis=0)
        k2 = jnp.concatenate([kp_ref[:, sl], kc_ref[:, sl]], axis=0)
        bias = jnp.concatenate([bias_ref[2 * hp], bias_ref[2 * hp + 1]], axis=0)
        scores.append(_dot_nt(qs, k2) + bias + edge)
    probs, stats = [], []
    for s in scores:
        m = jnp.max(s, axis=1, keepdims=True)
        p = jnp.exp(s - m)
        probs.append(p.astype(BF16))
        stats.append((m, jnp.sum(p, axis=1, keepdims=True)))
    for sl, p, (m, l) in zip(sls, probs, stats):
        v2 = jnp.concatenate([vp_ref[:, sl], vc_ref[:, sl]], axis=0)
        o = _dot(p, v2) / l
        lse = jnp.broadcast_to(m + jnp.log(l), o.shape)
        o_ref[:, sl] = jnp.where(lo, o[:DIL_BLOCK], o[DIL_BLOCK:])
        lse_ref[:, sl] = jnp.where(lo, lse[:DIL_BLOCK], lse[DIL_BLOCK:])


def _dil_attn(q, k, v, bias, *, group, dil, batch, seq):
    n = seq // dil
    nb = n // DIL_BLOCK
    q, k, v = (a.reshape(batch, n, dil * DIL_W) for a in (q, k, v))
    blk = (None, DIL_BLOCK, DIL_W)
    cur = pl.BlockSpec(blk, lambda b, r, i: (b, i, r))
    prev = pl.BlockSpec(blk, lambda b, r, i: (b, jnp.maximum(i - 1, 0), r))
    out_sd = jax.ShapeDtypeStruct((batch, n, dil * DIL_W), F32)
    o, lse = pl.pallas_call(
        _dil_attn_kernel,
        grid=(batch, dil, nb),
        in_specs=[cur, cur, prev, cur, prev, pl.BlockSpec(bias.shape, lambda b, r, i: (0, 0, 0))],
        out_specs=[cur, cur],
        out_shape=[out_sd, out_sd],
        compiler_params=_cparams(("parallel", "parallel", "arbitrary")),
        name=f"dil_attn_g{group}",
    )(q, k, k, v, v, bias)
    return o.reshape(batch * n, dil * DIL_W), lse.reshape(batch * n, dil * DIL_W)


def _rpb_bucket(dist):
    exact = RPB_BUCKETS // 2
    d = jnp.maximum(dist, 0)
    d_f = jnp.maximum(d, 1).astype(F32)
    large = exact + (jnp.log(d_f / exact) / math.log(RPB_MAX_DIST / exact)
                     * (RPB_BUCKETS - exact)).astype(jnp.int32)
    return jnp.where(d < exact, d, jnp.minimum(large, RPB_BUCKETS - 1))


def _dil_bias(table, span, dil):
    period = 3 * DIL_BLOCK
    j = jnp.arange(period)
    k_minus_q = jnp.where(j < 2 * DIL_BLOCK, j, j - period)
    dist = DIL_BLOCK - k_minus_q
    band = (dist >= 0) & (dist <= span)
    prof = jnp.where(band[:, None], table[_rpb_bucket(dist * dil)].astype(F32), MASK_VALUE).T
    skew = jnp.tile(prof, (1, DIL_BLOCK))[:, :DIL_BLOCK * (period - 1)]
    return skew.reshape(DIL_HEADS, DIL_BLOCK, period - 1)[:, :, :2 * DIL_BLOCK]


def _dil_out_kernel(o0_ref, o1_ref, o2_ref, l0_ref, l1_ref, l2_ref, w_ref, h_ref, gt_ref, out_ref, *stages):
    tm = h_ref.shape[0]
    stages = list(stages)

    def token_major(ref, dil):
        if dil == 1:
            return ref[...]
        stage = stages.pop()
        for r in range(dil):
            for c in range(DIL_W // LANES):
                col = r * DIL_W + c * LANES
                stage[c, pl.ds(r, tm // dil, stride=dil), :] = ref[:, col:col + LANES]
        return jnp.concatenate([stage[c] for c in range(DIL_W // LANES)], axis=1)

    dils = [dil for _, dil in DIL_PATTERNS]
    l0, l1, l2 = (token_major(ref, d) for ref, d in zip((l0_ref, l1_ref, l2_ref), dils))
    o0, o1, o2 = (token_major(ref, d) for ref, d in zip((o0_ref, o1_ref, o2_ref), dils))
    m = jnp.maximum(jnp.maximum(l0, l1), l2)
    w0, w1, w2 = jnp.exp(l0 - m), jnp.exp(l1 - m), jnp.exp(l2 - m)
    o = (o0 * w0 + o1 * w1 + o2 * w2) / (w0 + w1 + w2)
    out_ref[...] = h_ref[...] + gt_ref[...] * _dot(o.astype(BF16), w_ref[...])


def _dil_out(os_, ls_, w, h, gt, *, tm, tiles_per_batch):
    T, D = h.shape
    row = lambda n: pl.BlockSpec((tm, n), lambda i: (i, 0))
    grp = [pl.BlockSpec((tm // dil, dil * DIL_W), lambda i: (i, 0)) for _, dil in DIL_PATTERNS]
    n_stage = 2 * sum(1 for _, dil in DIL_PATTERNS if dil > 1)
    return pl.pallas_call(
        _dil_out_kernel,
        grid=(T // tm,),
        in_specs=grp + grp + [pl.BlockSpec(w.shape, lambda i: (0, 0)), row(D),
                              pl.BlockSpec((None, 1, D), lambda i: (i // tiles_per_batch, 0, 0))],
        out_specs=row(D),
        out_shape=jax.ShapeDtypeStruct((T, D), F32),
        scratch_shapes=[pltpu.VMEM((DIL_W // LANES, tm, LANES), F32)] * n_stage,
        compiler_params=_cparams(("parallel",)),
        name="dil_out",
    )(*os_, *ls_, w, h, gt)


def _route_kernel(h_ref, g_ref, sc_ref, sh_ref, wrt_ref, hn_ref, eid_ref, rank_ref, gate_ref, cnt_ref, carry_sc):
    @pl.when(pl.program_id(0) == 0)
    def _():
        carry_sc[...] = jnp.zeros(carry_sc.shape, F32)

    hn = _normmod(h_ref[...], g_ref[...], sc_ref[...], sh_ref[...])
    hn_ref[...] = hn
    tm = hn.shape[0]
    logits = lax.dot_general(wrt_ref[...], hn, (((1,), (1,)), ((), ())), preferred_element_type=F32,
                             precision=lax.Precision.HIGHEST)
    idx = lax.broadcasted_iota(jnp.int32, logits.shape, 0)
    v1 = jnp.max(logits, axis=0, keepdims=True)
    i1 = jnp.min(jnp.where(logits == v1, idx, N_EXPERTS), axis=0, keepdims=True)
    rest = jnp.where(idx == i1, -jnp.inf, logits)
    v2 = jnp.max(rest, axis=0, keepdims=True)
    i2 = jnp.min(jnp.where(rest == v2, idx, N_EXPERTS), axis=0, keepdims=True)
    e = jnp.exp(v2 - v1)
    gate_ref[...] = jnp.concatenate([1.0 / (1.0 + e), e / (1.0 + e)], axis=0)
    eid_ref[...] = jnp.concatenate([i1, i2], axis=0)

    sel = ((idx == i1) | (idx == i2)).astype(BF16)
    before = (lax.broadcasted_iota(jnp.int32, (tm, tm), 0)
              < lax.broadcasted_iota(jnp.int32, (tm, tm), 1)).astype(BF16)
    rank_all = carry_sc[:, :1] + _dot(sel, before)
    r1 = jnp.sum(jnp.where(idx == i1, rank_all, 0.0), axis=0, keepdims=True)
    r2 = jnp.sum(jnp.where(idx == i2, rank_all, 0.0), axis=0, keepdims=True)
    rank_ref[...] = jnp.concatenate([r1, r2], axis=0).astype(jnp.int32)
    carry_sc[...] += jnp.sum(sel.astype(F32), axis=1, keepdims=True)
    cnt_ref[...] = carry_sc[...].astype(jnp.int32)


def _route(h, g, sc, sh, wrt, *, tm, tiles_per_batch):
    T, D = h.shape
    bvec = pl.BlockSpec((None, 1, D), lambda i: (i // tiles_per_batch, 0, 0))
    lane_blk = pl.BlockSpec((2, tm), lambda i: (0, i))
    return pl.pallas_call(
        _route_kernel,
        grid=(T // tm,),
        in_specs=[pl.BlockSpec((tm, D), lambda i: (i, 0)), pl.BlockSpec((1, D), lambda i: (0, 0)), bvec, bvec,
                  pl.BlockSpec(wrt.shape, lambda i: (0, 0))],
        out_specs=[pl.BlockSpec((tm, D), lambda i: (i, 0)), lane_blk, lane_blk, lane_blk,
                   pl.BlockSpec((N_EXPERTS, HEAD_PAD), lambda i: (0, 0))],
        out_shape=[jax.ShapeDtypeStruct((T, D), F32), jax.ShapeDtypeStruct((2, T), jnp.int32),
                   jax.ShapeDtypeStruct((2, T), jnp.int32), jax.ShapeDtypeStruct((2, T), F32),
                   jax.ShapeDtypeStruct((N_EXPERTS, HEAD_PAD), jnp.int32)],
        scratch_shapes=[pltpu.VMEM((N_EXPERTS, HEAD_PAD), F32)],
        compiler_params=_cparams(("arbitrary",)),
        name="moe_route",
    )(h, g, sc, sh, wrt)


def _dispatch_kernel(pad_ref, dest_ref, hn_ref, xs_ref, zero_sc, sem, zsem):
    tm = hn_ref.shape[0]

    def row_copy(r, k):
        return pltpu.make_async_copy(hn_ref.at[pl.ds(r, 1), :], xs_ref.at[pl.ds(dest_ref[k, r], 1), :], sem)

    def issue(r, carry):
        row_copy(r, 0).start(priority=0)
        row_copy(r, 1).start(priority=1)
        return carry

    for r in range(tm):
        issue(r, 0)

    @pl.when(pl.program_id(0) == pl.num_programs(0) - 1)
    def _():
        zero_sc[...] = jnp.zeros(zero_sc.shape, F32)

        def pad_copy(p):
            return pltpu.make_async_copy(zero_sc.at[pl.ds(0, 1), :], xs_ref.at[pl.ds(p, 1), :], zsem)

        for e in range(N_EXPERTS):
            lo, hi = pad_ref[0, e], pad_ref[1, e]

            def zissue(p, carry):
                pad_copy(p).start()
                return carry

            def zwait(p, carry):
                pad_copy(p).wait()
                return carry

            lax.fori_loop(lo, hi, zissue, 0)
            lax.fori_loop(lo, hi, zwait, 0)

        zrows = zero_sc.shape[0]

        def tail_copy(c):
            return pltpu.make_async_copy(zero_sc, xs_ref.at[pl.ds(pl.multiple_of(c * zrows, zrows), zrows), :], zsem)

        def tissue(c, carry):
            tail_copy(c).start()
            return carry

        def twait(c, carry):
            tail_copy(c).wait()
            return carry

        lo, hi = pad_ref[1, N_EXPERTS - 1] // zrows, xs_ref.shape[0] // zrows
        lax.fori_loop(lo, hi, tissue, 0)
        lax.fori_loop(lo, hi, twait, 0)

    for _ in range(2):
        pltpu.make_async_copy(hn_ref, xs_ref.at[pl.ds(0, tm), :], sem).wait()


def _dispatch(pad_rows, dest, hn, *, n_slots, tm):
    T, D = hn.shape
    return pl.pallas_call(
        _dispatch_kernel,
        grid_spec=pltpu.PrefetchScalarGridSpec(
            num_scalar_prefetch=1,
            grid=(T // tm,),
            in_specs=[pl.BlockSpec((2, tm), lambda i, pad: (0, i), memory_space=pltpu.SMEM),
                      pl.BlockSpec((tm, D), lambda i, pad: (i, 0))],
            out_specs=pl.BlockSpec(memory_space=pl.ANY),
            scratch_shapes=[pltpu.VMEM((64, D), F32), pltpu.SemaphoreType.DMA, pltpu.SemaphoreType.DMA]),
        out_shape=jax.ShapeDtypeStruct((n_slots, D), F32),
        compiler_params=_cparams(("arbitrary",)),
        name="moe_dispatch",
    )(pad_rows, dest, hn)


def _experts_kernel(te_ref, na_ref, xs_ref, wgu_ref, wd_ref, y_ref, acc_sc, *, tf):
    @pl.when(pl.program_id(0) < na_ref[0])
    def _():
        _swiglu_tile(xs_ref[...].astype(BF16), wgu_ref, wd_ref, acc_sc, tf)
        y_ref[...] = acc_sc[...]

    @pl.when(pl.program_id(0) >= na_ref[0])
    def _():
        y_ref[...] = jnp.zeros(y_ref.shape, F32)


def _experts(tile_expert, n_active, xs, w_gu, w_d, *, tm, tf):
    P, D = xs.shape

    def tile(j, te, na):
        return jnp.minimum(j, na[0] - 1)

    return pl.pallas_call(
        functools.partial(_experts_kernel, tf=tf),
        grid_spec=pltpu.PrefetchScalarGridSpec(
            num_scalar_prefetch=2,
            grid=(P // tm,),
            in_specs=[pl.BlockSpec((tm, D), lambda j, te, na: (tile(j, te, na), 0)),
                      pl.BlockSpec((None, D, 2 * D_FF), lambda j, te, na: (te[tile(j, te, na)], 0, 0)),
                      pl.BlockSpec((None, D_FF, D), lambda j, te, na: (te[tile(j, te, na)], 0, 0))],
            out_specs=pl.BlockSpec((tm, D), lambda j, te, na: (j, 0)),
            scratch_shapes=[pltpu.VMEM((tm, D), F32)]),
        out_shape=jax.ShapeDtypeStruct((P, D), F32),
        compiler_params=_cparams(("arbitrary",)),
        name="moe_experts",
    )(tile_expert, n_active, xs, w_gu, w_d)


def _combine_kernel(dest_ref, y_ref, gates_ref, h_ref, gt_ref, gfin_ref, o_ref, ybuf, sem):
    tm = h_ref.shape[0]

    def row_copy(r, k):
        return pltpu.make_async_copy(y_ref.at[pl.ds(dest_ref[k, r], 1), :], ybuf.at[k, pl.ds(r, 1), :], sem)

    def issue(r, carry):
        row_copy(r, 0).start(priority=0)
        row_copy(r, 1).start(priority=1)
        return carry

    for r in range(tm):
        issue(r, 0)
    for k in range(2):
        pltpu.make_async_copy(y_ref.at[pl.ds(0, tm), :], ybuf.at[k], sem).wait()
    gates = gates_ref[...]
    moe = gates[:, 0:1] * ybuf[0] + gates[:, 1:2] * ybuf[1]
    y = h_ref[...] + gt_ref[...] * moe
    o_ref[...] = y * _rms_scale(y) * gfin_ref[...]


def _combine(dest, y, gates, h, gt, gfin, *, tm, tiles_per_batch):
    T, D = h.shape
    return pl.pallas_call(
        _combine_kernel,
        grid=(T // tm,),
        in_specs=[pl.BlockSpec((2, tm), lambda i: (0, i), memory_space=pltpu.SMEM),
                  pl.BlockSpec(memory_space=pl.ANY),
                  pl.BlockSpec((tm, 2), lambda i: (i, 0)),
                  pl.BlockSpec((tm, D), lambda i: (i, 0)),
                  pl.BlockSpec((None, 1, D), lambda i: (i // tiles_per_batch, 0, 0)),
                  pl.BlockSpec((1, D), lambda i: (0, 0))],
        out_specs=pl.BlockSpec((tm, D), lambda i: (i, 0)),
        out_shape=jax.ShapeDtypeStruct((T, D), F32),
        scratch_shapes=[pltpu.VMEM((2, tm, D), F32), pltpu.SemaphoreType.DMA],
        compiler_params=_cparams(("arbitrary",)),
        name="moe_combine",
    )(dest, y, gates, h, gt, gfin)


def _moe_plan(eids, ranks, counts, *, tm, n_tiles):
    padded = (counts + tm - 1) // tm * tm
    ends = jnp.cumsum(padded)
    starts = ends - padded
    dest = ranks
    for e in range(N_EXPERTS):
        dest = dest + jnp.where(eids == e, starts[e], 0)
    tile_start = jnp.arange(n_tiles, dtype=jnp.int32) * tm
    tile_expert = jnp.minimum(jnp.sum(tile_start[:, None] >= ends[None, :], axis=1), N_EXPERTS - 1)
    n_active = (ends[-1] // tm).reshape(1)
    pad_rows = jnp.stack([starts + counts, ends])
    return dest.astype(jnp.int32), tile_expert.astype(jnp.int32), n_active.astype(jnp.int32), pad_rows.astype(jnp.int32)


def _mla_weights(w_in, w_q_up, w_kv_up):
    D = w_in.shape[0]
    half = MLA_ROPE // 2
    w_ql = w_in[:, :MLA_Q_RANK]
    w_kvl = w_in[:, MLA_Q_RANK:MLA_Q_RANK + MLA_KV_RANK]
    w_kr = w_in[:, MLA_Q_RANK + MLA_KV_RANK:]
    z64 = jnp.zeros((D, MLA_NOPE), F32)
    z32 = jnp.zeros((D, HEAD_PAD - MLA_QK), F32)
    kr_pad = jnp.concatenate([z64, w_kr, z32], axis=1)
    kr_swap = jnp.concatenate([z64, w_kr[:, half:], w_kr[:, :half], z32], axis=1)
    w1 = jnp.concatenate([w_ql, w_kvl, kr_pad, kr_swap], axis=1).astype(BF16)

    wq = w_q_up.reshape(MLA_Q_RANK, MLA_HEADS, MLA_QK)
    nope, rope = wq[..., :MLA_NOPE], wq[..., MLA_NOPE:]
    zq = jnp.zeros((MLA_Q_RANK, MLA_HEADS, HEAD_PAD - MLA_QK), F32)
    wqa = jnp.concatenate([nope, rope, zq], axis=-1).reshape(MLA_Q_RANK, MLA_HEADS * HEAD_PAD).astype(BF16)
    wqb = jnp.concatenate([jnp.zeros_like(nope), rope[..., half:], rope[..., :half], zq], axis=-1)
    wqb = wqb.reshape(MLA_Q_RANK, MLA_HEADS * HEAD_PAD).astype(BF16)

    wkv = w_kv_up.reshape(MLA_KV_RANK, MLA_HEADS, MLA_NOPE + MLA_V)
    k_nope, v = wkv[..., :MLA_NOPE], wkv[..., MLA_NOPE:]
    zk = jnp.zeros((MLA_KV_RANK, MLA_HEADS, HEAD_PAD - MLA_NOPE), F32)
    wka = jnp.concatenate([k_nope, zk], axis=-1).reshape(MLA_KV_RANK, MLA_HEADS * HEAD_PAD).astype(BF16)
    wvt = v.reshape(MLA_KV_RANK, MLA_HEADS * MLA_V).T.astype(BF16)
    return w1, wqa, wqb, wka, wvt


def _rope_tables(positions):
    inv_freq = ROPE_THETA ** (-jnp.arange(0, MLA_ROPE, 2, dtype=F32) / MLA_ROPE)
    ang = positions.astype(F32).reshape(-1, 1) * inv_freq
    cos, sin = jnp.cos(ang), jnp.sin(ang)
    T = ang.shape[0]
    one = jnp.ones((T, MLA_NOPE), F32)
    z64 = jnp.zeros((T, MLA_NOPE), F32)
    z32 = jnp.zeros((T, HEAD_PAD - MLA_QK), F32)
    return (jnp.concatenate([one, cos, cos, z32], axis=1),
            jnp.concatenate([z64, -sin, sin, z32], axis=1))


def kernel(x, c, positions, g_mix, g_ffn, w_ada, b_ada, w_mla_in, g_mla_q, w_mla_q_up, g_mla_kv, w_mla_kv_up,
           w_mla_out, g_kv_b, w_ada_kv, b_ada_kv, w_kv_b, rpb_table, w_q_b, w_o_b, w_ffn_gu, w_ffn_down,
           w_router, w_exp_gu, w_exp_down, g_final):
    B, S, D = x.shape
    T = B * S
    h = x.reshape(T, D)

    c8 = jnp.zeros((8, D), F32).at[:B].set(c)
    mod = _ada(c8, w_ada, b_ada[:, None, :])[:, :B]
    mod = mod.reshape(2, B, N_MOD, 1, D)
    mod_kv = _ada(c8, w_ada_kv[None], b_ada_kv[None, None, :])[0, :B].reshape(B, 2, 1, D)
    sh_kv, sc_kv = mod_kv[:, 0], mod_kv[:, 1]

    def mods(layer):
        return [mod[layer, :, k] for k in range(N_MOD)]

    row = lambda v: v.reshape(1, -1)

    sh_m, sc_m, gt_m, sh_f, sc_f, gt_f = mods(0)
    w1, wqa, wqb, wka, wvt = _mla_weights(w_mla_in[0], w_mla_q_up[0], w_mla_kv_up[0])
    cos_t, sin_t = _rope_tables(positions)
    q, k, vt = _mla_proj(h, row(g_mix[0]), sc_m, sh_m, cos_t, sin_t, w1, row(g_mla_q[0]), row(g_mla_kv[0]),
                         wqa, wqb, wka, wvt, tm=256, batch=B, seq=S)
    o = _mla_attn(q.reshape(B, S, -1), k.reshape(B, S, -1), vt, batch=B, seq=S, tq=512)

    tm = 512
    h = _ffn(o.reshape(T, -1), w_mla_out[0].astype(BF16), h, gt_m, row(g_ffn[0]), sc_f, sh_f, gt_f,
             w_ffn_gu[0].astype(BF16), w_ffn_down[0].astype(BF16), tm=tm, tf=256, tiles_per_batch=S // tm)

    sh_m, sc_m, gt_m, sh_f, sc_f, gt_f = mods(1)
    tm = 512
    qkv = _l1_proj(h, row(g_kv_b), sc_kv, sh_kv, row(g_mix[1]), sc_m, sh_m,
                   w_kv_b.astype(BF16), w_q_b[0].astype(BF16), tm=tm, tiles_per_batch=S // tm)

    outs, lses = [], []
    for g, (window, dil) in enumerate(DIL_PATTERNS):
        bias = _dil_bias(rpb_table[:, g * DIL_HEADS:(g + 1) * DIL_HEADS], window // dil, dil)
        o_g, lse_g = _dil_attn(*qkv[3 * g:3 * g + 3], bias, group=g, dil=dil, batch=B, seq=S)
        outs.append(o_g)
        lses.append(lse_g)
    h = _dil_out(outs, lses, w_o_b[0].astype(BF16), h, gt_m, tm=tm, tiles_per_batch=S // tm)

    hn, eids, ranks, gates, counts = _route(h, row(g_ffn[1]), sc_f, sh_f, w_router[0].T, tm=tm,
                                            tiles_per_batch=S // tm)
    n_tiles = (T * 2) // MOE_TM + N_EXPERTS
    dest, tile_expert, n_active, pad_rows = _moe_plan(eids, ranks, counts[:, 0], tm=MOE_TM, n_tiles=n_tiles)
    xs = _dispatch(pad_rows, dest, hn, n_slots=n_tiles * MOE_TM, tm=tm)
    y = _experts(tile_expert, n_active, xs, w_exp_gu[0].astype(BF16), w_exp_down[0].astype(BF16),
                 tm=MOE_TM, tf=256)
    tm = 256
    out = _combine(dest, y, gates.T, h, gt_f, row(g_final), tm=tm, tiles_per_batch=S // tm)
    return out.reshape(B, S, D)
```

```python
import functools
import math

import numpy as np
import jax
import jax.numpy as jnp
from jax import lax
from jax.experimental import pallas as pl
from jax.experimental.pallas import tpu as pltpu

D_MODEL = 1024
N_MOD = 6
EPS = 1e-6

MLA_HEADS = 16
MLA_Q_RANK = 384
MLA_KV_RANK = 256
MLA_NOPE = 64
MLA_ROPE = 32
MLA_V = 64
MLA_QK = MLA_NOPE + MLA_ROPE
ROPE_THETA = 10000.0
LANES = 128
ONES_ROWS = 16
HEAD_PAD = 128

DIL_PATTERNS = ((128, 1), (512, 4), (2048, 16))
DIL_GROUPS = len(DIL_PATTERNS)
DIL_HEADS = 8
DIL_HEAD_DIM = 64
DIL_BLOCK = 128
DIL_W = DIL_HEADS * DIL_HEAD_DIM
RPB_BUCKETS = 32
RPB_MAX_DIST = 2048

D_FF = 2816
N_EXPERTS = 8
MOE_TM = 512

MASK_VALUE = -1e30
LOG2E = math.log2(math.e)

F32 = jnp.float32
BF16 = jnp.bfloat16

VMEM_LIMIT = 56 * 1024 * 1024


def _cparams(sem):
    return pltpu.CompilerParams(dimension_semantics=sem, vmem_limit_bytes=VMEM_LIMIT)


def _dot(a, b):
    return jnp.dot(a, b, preferred_element_type=F32)


def _dot_nt(a, b):
    return lax.dot_general(a, b, (((1,), (1,)), ((), ())), preferred_element_type=F32)


def _rms_scale(x):
    return lax.rsqrt(jnp.mean(x * x, axis=-1, keepdims=True) + EPS)


def _normmod(x, g, sc, sh):
    return (x * _rms_scale(x)) * (g * (1.0 + sc)) + sh


def _silu(x):
    return x * (1.0 / (1.0 + jnp.exp(-x)))


def _ada_kernel(c_ref, w_ref, b_ref, o_ref):
    c = c_ref[...]
    o_ref[...] = jnp.dot(_silu(c), w_ref[...], preferred_element_type=F32,
                         precision=lax.Precision.HIGHEST) + b_ref[...]


def _ada(c8, w, b, tn=512):
    L, D, N = w.shape
    return pl.pallas_call(
        _ada_kernel,
        grid=(L, N // tn),
        in_specs=[pl.BlockSpec((8, D), lambda l, j: (0, 0)),
                  pl.BlockSpec((None, D, tn), lambda l, j: (l, 0, j)),
                  pl.BlockSpec((None, 1, tn), lambda l, j: (l, 0, j))],
        out_specs=pl.BlockSpec((None, 8, tn), lambda l, j: (l, 0, j)),
        out_shape=jax.ShapeDtypeStruct((L, 8, N), F32),
        compiler_params=_cparams(("parallel", "parallel")),
        name="ada_mod",
    )(c8, w, b)


def _mla_proj_kernel(h_ref, g_ref, sc_ref, sh_ref, cos_ref, sin_ref, w1_ref, gq_ref, gkv_ref,
                     wqa_ref, wqb_ref, wka_ref, wvt_ref, q_ref, k_ref, vt_ref, *, qscale):
    hn = _normmod(h_ref[...], g_ref[...], sc_ref[...], sh_ref[...]).astype(BF16)
    z = _dot(hn, w1_ref[...])
    ql = z[:, :MLA_Q_RANK]
    kvl = z[:, MLA_Q_RANK:MLA_Q_RANK + MLA_KV_RANK]
    kr = z[:, MLA_Q_RANK + MLA_KV_RANK:MLA_Q_RANK + MLA_KV_RANK + HEAD_PAD]
    krs = z[:, MLA_Q_RANK + MLA_KV_RANK + HEAD_PAD:]
    qn = (ql * _rms_scale(ql) * gq_ref[...]).astype(BF16)
    kvn = (kvl * _rms_scale(kvl) * gkv_ref[...]).astype(BF16)
    cos = cos_ref[...]
    sin = sin_ref[...]
    k_rope = kr * cos + krs * sin
    cos2 = jnp.concatenate([cos, cos], axis=1)
    sin2 = jnp.concatenate([sin, sin], axis=1)
    k_rope2 = jnp.concatenate([k_rope, k_rope], axis=1)
    for hp in range(MLA_HEADS // 2):
        sl = slice(2 * HEAD_PAD * hp, 2 * HEAD_PAD * (hp + 1))
        a = _dot(qn, wqa_ref[:, sl])
        b = _dot(qn, wqb_ref[:, sl])
        q_ref[:, sl] = ((a * cos2 + b * sin2) * qscale).astype(BF16)
        k_ref[:, sl] = (_dot(kvn, wka_ref[:, sl]) + k_rope2).astype(BF16)
    vt_ref[...] = _dot_nt(wvt_ref[...], kvn).astype(BF16)


def _mla_proj(h, g, sc, sh, cos_t, sin_t, w1, gq, gkv, wqa, wqb, wka, wvt, *, tm, batch, seq):
    T, D = h.shape
    HP = MLA_HEADS * HEAD_PAD
    tpb = seq // tm
    row = lambda n: pl.BlockSpec((tm, n), lambda i: (i, 0))
    bvec = pl.BlockSpec((None, 1, D), lambda i: (i // tpb, 0, 0))
    full = lambda a: pl.BlockSpec(a.shape, lambda i: (0,) * a.ndim)
    return pl.pallas_call(
        functools.partial(_mla_proj_kernel, qscale=(MLA_QK ** -0.5) * LOG2E),
        grid=(T // tm,),
        in_specs=[row(D), full(g), bvec, bvec, row(HEAD_PAD), row(HEAD_PAD), full(w1), full(gq), full(gkv),
                  full(wqa), full(wqb), full(wka), full(wvt)],
        out_specs=[row(HP), row(HP),
                   pl.BlockSpec((None, MLA_HEADS * MLA_V, tm), lambda i: (i // tpb, 0, i % tpb))],
        out_shape=[jax.ShapeDtypeStruct((T, HP), BF16), jax.ShapeDtypeStruct((T, HP), BF16),
                   jax.ShapeDtypeStruct((batch, MLA_HEADS * MLA_V, seq), BF16)],
        compiler_params=_cparams(("parallel",)),
        name="mla_proj",
    )(h, g, sc, sh, cos_t, sin_t, w1, gq, gkv, wqa, wqb, wka, wvt)


def _mla_attn_kernel(q_ref, k_ref, vt_ref, o_ref, m_sc, acc_sc, s_sc, mb_sc, *, tq):
    qi = pl.program_id(2)
    tk = tq
    m_sc[...] = jnp.full(m_sc.shape, -jnp.inf, F32)
    acc_sc[...] = jnp.zeros(acc_sc.shape, F32)

    def scores(j, diagonal, slot):
        off = pl.multiple_of(j * tk, tk)
        for hh in range(2):
            q = q_ref[:, HEAD_PAD * hh:HEAD_PAD * (hh + 1)]
            k = k_ref[pl.ds(off, tk), HEAD_PAD * hh:HEAD_PAD * (hh + 1)]
            s = _dot_nt(k, q)
            if diagonal:
                key = lax.broadcasted_iota(jnp.int32, (tk, tq), 0)
                qry = lax.broadcasted_iota(jnp.int32, (tk, tq), 1)
                s = jnp.where(key <= qry, s, -jnp.inf)
            s_sc[slot, hh] = s
            m_blk = jnp.max(jnp.max(s.reshape(tk // 8, 8, tq), axis=0), axis=0, keepdims=True)
            mb_sc[slot, hh] = jnp.broadcast_to(m_blk, (8, tq))

    def softmax_values(j, slot):
        off = pl.multiple_of(j * tk, tk)
        ones = jnp.ones((ONES_ROWS, tk), BF16)
        for hh in range(2):
            s3 = s_sc[slot, hh].reshape(tk // 8, 8, tq)
            m_prev = m_sc[hh]
            m_new = jnp.maximum(m_prev, mb_sc[slot, hh])
            alpha = jnp.exp2(m_prev - m_new)
            p = jnp.exp2(s3 - m_new[None]).reshape(tk, tq).astype(BF16)
            vt = vt_ref[MLA_V * hh:MLA_V * (hh + 1), pl.ds(off, tk)]
            vt1 = jnp.concatenate([vt, ones], axis=0)
            acc_sc[hh] = alpha[:1] * acc_sc[hh] + _dot(vt1, p)
            m_sc[hh] = m_new

    def step(j, even, diagonal_next=False, has_next=True):
        cur, nxt = (0, 1) if even else (1, 0)
        if has_next:
            scores(j + 1, diagonal_next, nxt)
        softmax_values(j, cur)

    @pl.when(qi == 0)
    def _():
        scores(0, True, 0)

    @pl.when(qi > 0)
    def _():
        scores(0, False, 0)

        def pair(t, carry):
            step(2 * t, True)
            step(2 * t + 1, False)
            return carry

        lax.fori_loop(0, (qi - 1) // 2, pair, 0)

    @pl.when(qi % 2 == 1)
    def _():
        step(qi - 1, True, diagonal_next=True)
        step(qi, False, has_next=False)

    @pl.when((qi % 2 == 0) & (qi > 0))
    def _():
        step(qi - 2, True)
        step(qi - 1, False, diagonal_next=True)

    @pl.when(qi % 2 == 0)
    def _():
        step(qi, True, has_next=False)

    o_t = jnp.concatenate([acc_sc[hh, :MLA_V] / acc_sc[hh, MLA_V:MLA_V + 1] for hh in range(2)], axis=0)
    o_ref[...] = o_t.T.astype(BF16)


def _mla_attn(q, k, vt, *, batch, seq, tq):
    return pl.pallas_call(
        functools.partial(_mla_attn_kernel, tq=tq),
        grid=(batch, MLA_HEADS // 2, seq // tq),
        in_specs=[pl.BlockSpec((None, tq, 2 * HEAD_PAD), lambda b, hp, i: (b, i, hp)),
                  pl.BlockSpec((None, seq, 2 * HEAD_PAD), lambda b, hp, i: (b, 0, hp)),
                  pl.BlockSpec((None, 2 * MLA_V, seq), lambda b, hp, i: (b, hp, 0))],
        out_specs=pl.BlockSpec((None, tq, 2 * MLA_V), lambda b, hp, i: (b, i, hp)),
        out_shape=jax.ShapeDtypeStruct((batch, seq, MLA_HEADS * MLA_V), BF16),
        scratch_shapes=[pltpu.VMEM((2, 8, tq), F32), pltpu.VMEM((2, MLA_V + ONES_ROWS, tq), F32),
                        pltpu.VMEM((2, 2, tq, tq), F32),
                        pltpu.VMEM((2, 2, 8, tq), F32)],
        compiler_params=_cparams(("parallel", "parallel", "arbitrary")),
        name="mla_attn",
    )(q, k, vt)


def _swiglu_tile(x, wgu_ref, wd_ref, acc_sc, tf):
    for f in range(D_FF // tf):
        g = _dot(x, wgu_ref[:, f * tf:(f + 1) * tf])
        u = _dot(x, wgu_ref[:, D_FF + f * tf:D_FF + (f + 1) * tf])
        part = _dot((_silu(g) * u).astype(BF16), wd_ref[f * tf:(f + 1) * tf, :])
        if f == 0:
            acc_sc[...] = part
        else:
            acc_sc[...] += part


def _ffn_kernel(a_ref, wo_ref, h_ref, gtm_ref, g_ref, sc_ref, sh_ref, gt_ref, wgu_ref, wd_ref, o_ref, acc_sc,
                *, tf):
    h = h_ref[...] + gtm_ref[...] * _dot(a_ref[...], wo_ref[...])
    x = _normmod(h, g_ref[...], sc_ref[...], sh_ref[...]).astype(BF16)
    _swiglu_tile(x, wgu_ref, wd_ref, acc_sc, tf)
    o_ref[...] = h + gt_ref[...] * acc_sc[...]


def _ffn(a, w_o, h, gt_m, g, sc, sh, gt, w_gu, w_d, *, tm, tf, tiles_per_batch):
    T, D = h.shape
    bvec = pl.BlockSpec((None, 1, D), lambda i: (i // tiles_per_batch, 0, 0))
    row = pl.BlockSpec((tm, D), lambda i: (i, 0))
    const = lambda w: pl.BlockSpec(w.shape, lambda i: (0,) * w.ndim, pipeline_mode=pl.Buffered(1))
    return pl.pallas_call(
        functools.partial(_ffn_kernel, tf=tf),
        grid=(T // tm,),
        in_specs=[row, const(w_o), row, bvec, const(g), bvec, bvec, bvec, const(w_gu), const(w_d)],
        out_specs=row,
        out_shape=jax.ShapeDtypeStruct((T, D), F32),
        scratch_shapes=[pltpu.VMEM((tm, D), F32)],
        compiler_params=_cparams(("parallel",)),
        name="ffn_dense",
    )(a, w_o, h, gt_m, g, sc, sh, gt, w_gu, w_d)


def _l1_proj_kernel(h_ref, gkv_ref, sckv_ref, shkv_ref, gq_ref, scq_ref, shq_ref, wkv_ref, wq_ref, *refs):
    out_refs, stage_sc = refs[:-1], refs[-1]
    tm = h_ref.shape[0]
    x = h_ref[...]
    xh = x * _rms_scale(x)
    hn_kv = (xh * (gkv_ref[...] * (1.0 + sckv_ref[...])) + shkv_ref[...]).astype(BF16)
    hn_q = (xh * (gq_ref[...] * (1.0 + scq_ref[...])) + shq_ref[...]).astype(BF16)
    for g, (_, dil) in enumerate(DIL_PATTERNS):
        ys = (_dot(hn_q, wq_ref[:, g * DIL_W:(g + 1) * DIL_W]) * (DIL_HEAD_DIM ** -0.5),
              _dot(hn_kv, wkv_ref[:, 2 * g * DIL_W:(2 * g + 1) * DIL_W]),
              _dot(hn_kv, wkv_ref[:, (2 * g + 1) * DIL_W:(2 * g + 2) * DIL_W]))
        for y, out_ref in zip(ys, out_refs[3 * g:3 * g + 3]):
            if dil == 1:
                out_ref[...] = y.astype(BF16)
            else:
                for c in range(DIL_W // LANES):
                    stage_sc[c] = y[:, c * LANES:(c + 1) * LANES]
                for r in range(dil):
                    for c in range(DIL_W // LANES):
                        col = r * DIL_W + c * LANES
                        out_ref[:, col:col + LANES] = stage_sc[c, pl.ds(r, tm // dil, stride=dil), :].astype(BF16)


def _l1_proj(h, gkv, sckv, shkv, gq, scq, shq, wkv, wq, *, tm, tiles_per_batch):
    T, D = h.shape
    bvec = pl.BlockSpec((None, 1, D), lambda i: (i // tiles_per_batch, 0, 0))
    full = lambda a: pl.BlockSpec(a.shape, lambda i: (0,) * a.ndim)
    out_specs, out_shape = [], []
    for _, dil in DIL_PATTERNS:
        for _ in range(3):
            out_specs.append(pl.BlockSpec((tm // dil, dil * DIL_W), lambda i: (i, 0)))
            out_shape.append(jax.ShapeDtypeStruct((T // dil, dil * DIL_W), BF16))
    return pl.pallas_call(
        _l1_proj_kernel,
        grid=(T // tm,),
        in_specs=[pl.BlockSpec((tm, D), lambda i: (i, 0)), full(gkv), bvec, bvec, full(gq), bvec, bvec,
                  full(wkv), full(wq)],
        out_specs=out_specs,
        out_shape=out_shape,
        scratch_shapes=[pltpu.VMEM((DIL_W // LANES, tm, LANES), F32)],
        compiler_params=_cparams(("parallel",)),
        name="l1_proj",
    )(h, gkv, sckv, shkv, gq, scq, shq, wkv, wq)


def _dil_attn_kernel(q_ref, kc_ref, kp_ref, vc_ref, vp_ref, bias_ref, o_ref, lse_ref):
    i = pl.program_id(2)
    lane = lax.broadcasted_iota(jnp.int32, (1, 2 * DIL_HEAD_DIM), 1)
    lo = lane < DIL_HEAD_DIM
    col = lax.broadcasted_iota(jnp.int32, (1, 2 * DIL_BLOCK), 1)
    edge = jnp.where((col < DIL_BLOCK) & (i == 0), MASK_VALUE, 0.0).astype(F32)
    sls = [slice(2 * DIL_HEAD_DIM * hp, 2 * DIL_HEAD_DIM * (hp + 1)) for hp in range(DIL_HEADS // 2)]
    scores = []
    for hp, sl in enumerate(sls):
        q2 = q_ref[:, sl]
        zero = jnp.zeros_like(q2)
        qs = jnp.concatenate([jnp.where(lo, q2, zero), jnp.where(lo, zero, q2)], axis=0)
        k2 = jnp.concatenate([kp_ref[:, sl], kc_ref[:, sl]], axis=0)
        bias = jnp.concatenate([bias_ref[2 * hp], bias_ref[2 * hp + 1]], axis=0)
        scores.append(_dot_nt(qs, k2) + bias + edge)
    probs, stats = [], []
    for s in scores:
        m = jnp.max(s, axis=1, keepdims=True)
        p = jnp.exp(s - m)
        probs.append(p.astype(BF16))
        stats.append((m, jnp.sum(p, axis=1, keepdims=True)))
    for sl, p, (m, l) in zip(sls, probs, stats):
        v2 = jnp.concatenate([vp_ref[:, sl], vc_ref[:, sl]], axis=0)
        o = _dot(p, v2) / l
        lse = jnp.broadcast_to(m + jnp.log(l), o.shape)
        o_ref[:, sl] = jnp.where(lo, o[:DIL_BLOCK], o[DIL_BLOCK:])
        lse_ref[:, sl] = jnp.where(lo, lse[:DIL_BLOCK], lse[DIL_BLOCK:])


def _dil_attn(q, k, v, bias, *, group, dil, batch, seq):
    n = seq // dil
    nb = n // DIL_BLOCK
    q, k, v = (a.reshape(batch, n, dil * DIL_W) for a in (q, k, v))
    blk = (None, DIL_BLOCK, DIL_W)
    cur = pl.BlockSpec(blk, lambda b, r, i: (b, i, r))
    prev = pl.BlockSpec(blk, lambda b, r, i: (b, jnp.maximum(i - 1, 0), r))
    out_sd = jax.ShapeDtypeStruct((batch, n, dil * DIL_W), F32)
    o, lse = pl.pallas_call(
        _dil_attn_kernel,
        grid=(batch, dil, nb),
        in_specs=[cur, cur, prev, cur, prev, pl.BlockSpec(bias.shape, lambda b, r, i: (0, 0, 0))],
        out_specs=[cur, cur],
        out_shape=[out_sd, out_sd],
        compiler_params=_cparams(("parallel", "parallel", "arbitrary")),
        name=f"dil_attn_g{group}",
    )(q, k, k, v, v, bias)
    return o.reshape(batch * n, dil * DIL_W), lse.reshape(batch * n, dil * DIL_W)


def _rpb_bucket(dist):
    exact = RPB_BUCKETS // 2
    d = jnp.maximum(dist, 0)
    d_f = jnp.maximum(d, 1).astype(F32)
    large = exact + (jnp.log(d_f / exact) / math.log(RPB_MAX_DIST / exact)
                     * (RPB_BUCKETS - exact)).astype(jnp.int32)
    return jnp.where(d < exact, d, jnp.minimum(large, RPB_BUCKETS - 1))


def _dil_bias(table, span, dil):
    period = 3 * DIL_BLOCK
    j = jnp.arange(period)
    k_minus_q = jnp.where(j < 2 * DIL_BLOCK, j, j - period)
    dist = DIL_BLOCK - k_minus_q
    band = (dist >= 0) & (dist <= span)
    prof = jnp.where(band[:, None], table[_rpb_bucket(dist * dil)].astype(F32), MASK_VALUE).T
    skew = jnp.tile(prof, (1, DIL_BLOCK))[:, :DIL_BLOCK * (period - 1)]
    return skew.reshape(DIL_HEADS, DIL_BLOCK, period - 1)[:, :, :2 * DIL_BLOCK]


def _dil_out_kernel(o0_ref, o1_ref, o2_ref, l0_ref, l1_ref, l2_ref, w_ref, h_ref, gt_ref, out_ref, *stages):
    tm = h_ref.shape[0]
    stages = list(stages)

    def token_major(ref, dil):
        if dil == 1:
            return ref[...]
        stage = stages.pop()
        for r in range(dil):
            for c in range(DIL_W // LANES):
                col = r * DIL_W + c * LANES
                stage[c, pl.ds(r, tm // dil, stride=dil), :] = ref[:, col:col + LANES]
        return jnp.concatenate([stage[c] for c in range(DIL_W // LANES)], axis=1)

    dils = [dil for _, dil in DIL_PATTERNS]
    l0, l1, l2 = (token_major(ref, d) for ref, d in zip((l0_ref, l1_ref, l2_ref), dils))
    o0, o1, o2 = (token_major(ref, d) for ref, d in zip((o0_ref, o1_ref, o2_ref), dils))
    m = jnp.maximum(jnp.maximum(l0, l1), l2)
    w0, w1, w2 = jnp.exp(l0 - m), jnp.exp(l1 - m), jnp.exp(l2 - m)
    o = (o0 * w0 + o1 * w1 + o2 * w2) / (w0 + w1 + w2)
    out_ref[...] = h_ref[...] + gt_ref[...] * _dot(o.astype(BF16), w_ref[...])


def _dil_out(os_, ls_, w, h, gt, *, tm, tiles_per_batch):
    T, D = h.shape
    row = lambda n: pl.BlockSpec((tm, n), lambda i: (i, 0))
    grp = [pl.BlockSpec((tm // dil, dil * DIL_W), lambda i: (i, 0)) for _, dil in DIL_PATTERNS]
    n_stage = 2 * sum(1 for _, dil in DIL_PATTERNS if dil > 1)
    return pl.pallas_call(
        _dil_out_kernel,
        grid=(T // tm,),
        in_specs=grp + grp + [pl.BlockSpec(w.shape, lambda i: (0, 0)), row(D),
                              pl.BlockSpec((None, 1, D), lambda i: (i // tiles_per_batch, 0, 0))],
        out_specs=row(D),
        out_shape=jax.ShapeDtypeStruct((T, D), F32),
        scratch_shapes=[pltpu.VMEM((DIL_W // LANES, tm, LANES), F32)] * n_stage,
        compiler_params=_cparams(("parallel",)),
        name="dil_out",
    )(*os_, *ls_, w, h, gt)


def _route_kernel(h_ref, g_ref, sc_ref, sh_ref, wrt_ref, hn_ref, eid_ref, rank_ref, gate_ref, cnt_ref, carry_sc):
    @pl.when(pl.program_id(0) == 0)
    def _():
        carry_sc[...] = jnp.zeros(carry_sc.shape, F32)

    hn = _normmod(h_ref[...], g_ref[...], sc_ref[...], sh_ref[...])
    hn_ref[...] = hn
    tm = hn.shape[0]
    logits = lax.dot_general(wrt_ref[...], hn, (((1,), (1,)), ((), ())), preferred_element_type=F32,
                             precision=lax.Precision.HIGHEST)
    idx = lax.broadcasted_iota(jnp.int32, logits.shape, 0)
    v1 = jnp.max(logits, axis=0, keepdims=True)
    i1 = jnp.min(jnp.where(logits == v1, idx, N_EXPERTS), axis=0, keepdims=True)
    rest = jnp.where(idx == i1, -jnp.inf, logits)
    v2 = jnp.max(rest, axis=0, keepdims=True)
    i2 = jnp.min(jnp.where(rest == v2, idx, N_EXPERTS), axis=0, keepdims=True)
    e = jnp.exp(v2 - v1)
    gate_ref[...] = jnp.concatenate([1.0 / (1.0 + e), e / (1.0 + e)], axis=0)
    eid_ref[...] = jnp.concatenate([i1, i2], axis=0)

    sel = ((idx == i1) | (idx == i2)).astype(BF16)
    before = (lax.broadcasted_iota(jnp.int32, (tm, tm), 0)
              < lax.broadcasted_iota(jnp.int32, (tm, tm), 1)).astype(BF16)
    rank_all = carry_sc[:, :1] + _dot(sel, before)
    r1 = jnp.sum(jnp.where(idx == i1, rank_all, 0.0), axis=0, keepdims=True)
    r2 = jnp.sum(jnp.where(idx == i2, rank_all, 0.0), axis=0, keepdims=True)
    rank_ref[...] = jnp.concatenate([r1, r2], axis=0).astype(jnp.int32)
    carry_sc[...] += jnp.sum(sel.astype(F32), axis=1, keepdims=True)
    cnt_ref[...] = carry_sc[...].astype(jnp.int32)


def _route(h, g, sc, sh, wrt, *, tm, tiles_per_batch):
    T, D = h.shape
    bvec = pl.BlockSpec((None, 1, D), lambda i: (i // tiles_per_batch, 0, 0))
    lane_blk = pl.BlockSpec((2, tm), lambda i: (0, i))
    return pl.pallas_call(
        _route_kernel,
        grid=(T // tm,),
        in_specs=[pl.BlockSpec((tm, D), lambda i: (i, 0)), pl.BlockSpec((1, D), lambda i: (0, 0)), bvec, bvec,
                  pl.BlockSpec(wrt.shape, lambda i: (0, 0))],
        out_specs=[pl.BlockSpec((tm, D), lambda i: (i, 0)), lane_blk, lane_blk, lane_blk,
                   pl.BlockSpec((N_EXPERTS, HEAD_PAD), lambda i: (0, 0))],
        out_shape=[jax.ShapeDtypeStruct((T, D), F32), jax.ShapeDtypeStruct((2, T), jnp.int32),
                   jax.ShapeDtypeStruct((2, T), jnp.int32), jax.ShapeDtypeStruct((2, T), F32),
                   jax.ShapeDtypeStruct((N_EXPERTS, HEAD_PAD), jnp.int32)],
        scratch_shapes=[pltpu.VMEM((N_EXPERTS, HEAD_PAD), F32)],
        compiler_params=_cparams(("arbitrary",)),
        name="moe_route",
    )(h, g, sc, sh, wrt)


def _dispatch_kernel(pad_ref, dest_ref, hn_ref, xs_ref, zero_sc, sem, zsem):
    tm = hn_ref.shape[0]

    def row_copy(r, k):
        return pltpu.make_async_copy(hn_ref.at[pl.ds(r, 1), :], xs_ref.at[pl.ds(dest_ref[k, r], 1), :], sem)

    def issue(r, carry):
        row_copy(r, 0).start(priority=0)
        row_copy(r, 1).start(priority=1)
        return carry

    for r in range(tm):
        issue(r, 0)

    @pl.when(pl.program_id(0) == pl.num_programs(0) - 1)
    def _():
        zero_sc[...] = jnp.zeros(zero_sc.shape, F32)

        def pad_copy(p):
            return pltpu.make_async_copy(zero_sc.at[pl.ds(0, 1), :], xs_ref.at[pl.ds(p, 1), :], zsem)

        for e in range(N_EXPERTS):
            lo, hi = pad_ref[0, e], pad_ref[1, e]

            def zissue(p, carry):
                pad_copy(p).start()
                return carry

            def zwait(p, carry):
                pad_copy(p).wait()
                return carry

            lax.fori_loop(lo, hi, zissue, 0)
            lax.fori_loop(lo, hi, zwait, 0)

        zrows = zero_sc.shape[0]

        def tail_copy(c):
            return pltpu.make_async_copy(zero_sc, xs_ref.at[pl.ds(pl.multiple_of(c * zrows, zrows), zrows), :], zsem)

        def tissue(c, carry):
            tail_copy(c).start()
            return carry

        def twait(c, carry):
            tail_copy(c).wait()
            return carry

        lo, hi = pad_ref[1, N_EXPERTS - 1] // zrows, xs_ref.shape[0] // zrows
        lax.fori_loop(lo, hi, tissue, 0)
        lax.fori_loop(lo, hi, twait, 0)

    for _ in range(2):
        pltpu.make_async_copy(hn_ref, xs_ref.at[pl.ds(0, tm), :], sem).wait()


def _dispatch(pad_rows, dest, hn, *, n_slots, tm):
    T, D = hn.shape
    return pl.pallas_call(
        _dispatch_kernel,
        grid_spec=pltpu.PrefetchScalarGridSpec(
            num_scalar_prefetch=1,
            grid=(T // tm,),
            in_specs=[pl.BlockSpec((2, tm), lambda i, pad: (0, i), memory_space=pltpu.SMEM),
                      pl.BlockSpec((tm, D), lambda i, pad: (i, 0))],
            out_specs=pl.BlockSpec(memory_space=pl.ANY),
            scratch_shapes=[pltpu.VMEM((64, D), F32), pltpu.SemaphoreType.DMA, pltpu.SemaphoreType.DMA]),
        out_shape=jax.ShapeDtypeStruct((n_slots, D), F32),
        compiler_params=_cparams(("arbitrary",)),
        name="moe_dispatch",
    )(pad_rows, dest, hn)


def _experts_kernel(te_ref, na_ref, xs_ref, wgu_ref, wd_ref, y_ref, acc_sc, *, tf):
    @pl.when(pl.program_id(0) < na_ref[0])
    def _():
        _swiglu_tile(xs_ref[...].astype(BF16), wgu_ref, wd_ref, acc_sc, tf)
        y_ref[...] = acc_sc[...]

    @pl.when(pl.program_id(0) >= na_ref[0])
    def _():
        y_ref[...] = jnp.zeros(y_ref.shape, F32)


def _experts(tile_expert, n_active, xs, w_gu, w_d, *, tm, tf):
    P, D = xs.shape

    def tile(j, te, na):
        return jnp.minimum(j, na[0] - 1)

    return pl.pallas_call(
        functools.partial(_experts_kernel, tf=tf),
        grid_spec=pltpu.PrefetchScalarGridSpec(
            num_scalar_prefetch=2,
            grid=(P // tm,),
            in_specs=[pl.BlockSpec((tm, D), lambda j, te, na: (tile(j, te, na), 0)),
                      pl.BlockSpec((None, D, 2 * D_FF), lambda j, te, na: (te[tile(j, te, na)], 0, 0)),
                      pl.BlockSpec((None, D_FF, D), lambda j, te, na: (te[tile(j, te, na)], 0, 0))],
            out_specs=pl.BlockSpec((tm, D), lambda j, te, na: (j, 0)),
            scratch_shapes=[pltpu.VMEM((tm, D), F32)]),
        out_shape=jax.ShapeDtypeStruct((P, D), F32),
        compiler_params=_cparams(("arbitrary",)),
        name="moe_experts",
    )(tile_expert, n_active, xs, w_gu, w_d)


def _combine_kernel(dest_ref, y_ref, gates_ref, h_ref, gt_ref, gfin_ref, o_ref, ybuf, sem):
    tm = h_ref.shape[0]

    def row_copy(r, k):
        return pltpu.make_async_copy(y_ref.at[pl.ds(dest_ref[k, r], 1), :], ybuf.at[k, pl.ds(r, 1), :], sem)

    def issue(r, carry):
        row_copy(r, 0).start(priority=0)
        row_copy(r, 1).start(priority=1)
        return carry

    for r in range(tm):
        issue(r, 0)
    for k in range(2):
        pltpu.make_async_copy(y_ref.at[pl.ds(0, tm), :], ybuf.at[k], sem).wait()
    gates = gates_ref[...]
    moe = gates[:, 0:1] * ybuf[0] + gates[:, 1:2] * ybuf[1]
    y = h_ref[...] + gt_ref[...] * moe
    o_ref[...] = y * _rms_scale(y) * gfin_ref[...]


def _combine(dest, y, gates, h, gt, gfin, *, tm, tiles_per_batch):
    T, D = h.shape
    return pl.pallas_call(
        _combine_kernel,
        grid=(T // tm,),
        in_specs=[pl.BlockSpec((2, tm), lambda i: (0, i), memory_space=pltpu.SMEM),
                  pl.BlockSpec(memory_space=pl.ANY),
                  pl.BlockSpec((tm, 2), lambda i: (i, 0)),
                  pl.BlockSpec((tm, D), lambda i: (i, 0)),
                  pl.BlockSpec((None, 1, D), lambda i: (i // tiles_per_batch, 0, 0)),
                  pl.BlockSpec((1, D), lambda i: (0, 0))],
        out_specs=pl.BlockSpec((tm, D), lambda i: (i, 0)),
        out_shape=jax.ShapeDtypeStruct((T, D), F32),
        scratch_shapes=[pltpu.VMEM((2, tm, D), F32), pltpu.SemaphoreType.DMA],
        compiler_params=_cparams(("arbitrary",)),
        name="moe_combine",
    )(dest, y, gates, h, gt, gfin)


def _moe_plan(eids, ranks, counts, *, tm, n_tiles):
    padded = (counts + tm - 1) // tm * tm
    ends = jnp.cumsum(padded)
    starts = ends - padded
    dest = ranks
    for e in range(N_EXPERTS):
        dest = dest + jnp.where(eids == e, starts[e], 0)
    tile_start = jnp.arange(n_tiles, dtype=jnp.int32) * tm
    tile_expert = jnp.minimum(jnp.sum(tile_start[:, None] >= ends[None, :], axis=1), N_EXPERTS - 1)
    n_active = (ends[-1] // tm).reshape(1)
    pad_rows = jnp.stack([starts + counts, ends])
    return dest.astype(jnp.int32), tile_expert.astype(jnp.int32), n_active.astype(jnp.int32), pad_rows.astype(jnp.int32)


def _mla_weights(w_in, w_q_up, w_kv_up):
    D = w_in.shape[0]
    half = MLA_ROPE // 2
    w_ql = w_in[:, :MLA_Q_RANK]
    w_kvl = w_in[:, MLA_Q_RANK:MLA_Q_RANK + MLA_KV_RANK]
    w_kr = w_in[:, MLA_Q_RANK + MLA_KV_RANK:]
    z64 = jnp.zeros((D, MLA_NOPE), F32)
    z32 = jnp.zeros((D, HEAD_PAD - MLA_QK), F32)
    kr_pad = jnp.concatenate([z64, w_kr, z32], axis=1)
    kr_swap = jnp.concatenate([z64, w_kr[:, half:], w_kr[:, :half], z32], axis=1)
    w1 = jnp.concatenate([w_ql, w_kvl, kr_pad, kr_swap], axis=1).astype(BF16)

    wq = w_q_up.reshape(MLA_Q_RANK, MLA_HEADS, MLA_QK)
    nope, rope = wq[..., :MLA_NOPE], wq[..., MLA_NOPE:]
    zq = jnp.zeros((MLA_Q_RANK, MLA_HEADS, HEAD_PAD - MLA_QK), F32)
    wqa = jnp.concatenate([nope, rope, zq], axis=-1).reshape(MLA_Q_RANK, MLA_HEADS * HEAD_PAD).astype(BF16)
    wqb = jnp.concatenate([jnp.zeros_like(nope), rope[..., half:], rope[..., :half], zq], axis=-1)
    wqb = wqb.reshape(MLA_Q_RANK, MLA_HEADS * HEAD_PAD).astype(BF16)

    wkv = w_kv_up.reshape(MLA_KV_RANK, MLA_HEADS, MLA_NOPE + MLA_V)
    k_nope, v = wkv[..., :MLA_NOPE], wkv[..., MLA_NOPE:]
    zk = jnp.zeros((MLA_KV_RANK, MLA_HEADS, HEAD_PAD - MLA_NOPE), F32)
    wka = jnp.concatenate([k_nope, zk], axis=-1).reshape(MLA_KV_RANK, MLA_HEADS * HEAD_PAD).astype(BF16)
    wvt = v.reshape(MLA_KV_RANK, MLA_HEADS * MLA_V).T.astype(BF16)
    return w1, wqa, wqb, wka, wvt


def _rope_tables(positions):
    inv_freq = ROPE_THETA ** (-jnp.arange(0, MLA_ROPE, 2, dtype=F32) / MLA_ROPE)
    ang = positions.astype(F32).reshape(-1, 1) * inv_freq
    cos, sin = jnp.cos(ang), jnp.sin(ang)
    T = ang.shape[0]
    one = jnp.ones((T, MLA_NOPE), F32)
    z64 = jnp.zeros((T, MLA_NOPE), F32)
    z32 = jnp.zeros((T, HEAD_PAD - MLA_QK), F32)
    return (jnp.concatenate([one, cos, cos, z32], axis=1),
            jnp.concatenate([z64, -sin, sin, z32], axis=1))


def kernel(x, c, positions, g_mix, g_ffn, w_ada, b_ada, w_mla_in, g_mla_q, w_mla_q_up, g_mla_kv, w_mla_kv_up,
           w_mla_out, g_kv_b, w_ada_kv, b_ada_kv, w_kv_b, rpb_table, w_q_b, w_o_b, w_ffn_gu, w_ffn_down,
           w_router, w_exp_gu, w_exp_down, g_final):
    B, S, D = x.shape
    T = B * S
    h = x.reshape(T, D)

    c8 = jnp.zeros((8, D), F32).at[:B].set(c)
    mod = _ada(c8, w_ada, b_ada[:, None, :])[:, :B]
    mod = mod.reshape(2, B, N_MOD, 1, D)
    mod_kv = _ada(c8, w_ada_kv[None], b_ada_kv[None, None, :])[0, :B].reshape(B, 2, 1, D)
    sh_kv, sc_kv = mod_kv[:, 0], mod_kv[:, 1]

    def mods(layer):
        return [mod[layer, :, k] for k in range(N_MOD)]

    row = lambda v: v.reshape(1, -1)

    sh_m, sc_m, gt_m, sh_f, sc_f, gt_f = mods(0)
    w1, wqa, wqb, wka, wvt = _mla_weights(w_mla_in[0], w_mla_q_up[0], w_mla_kv_up[0])
    cos_t, sin_t = _rope_tables(positions)
    q, k, vt = _mla_proj(h, row(g_mix[0]), sc_m, sh_m, cos_t, sin_t, w1, row(g_mla_q[0]), row(g_mla_kv[0]),
                         wqa, wqb, wka, wvt, tm=256, batch=B, seq=S)
    o = _mla_attn(q.reshape(B, S, -1), k.reshape(B, S, -1), vt, batch=B, seq=S, tq=1024)

    tm = 512
    h = _ffn(o.reshape(T, -1), w_mla_out[0].astype(BF16), h, gt_m, row(g_ffn[0]), sc_f, sh_f, gt_f,
             w_ffn_gu[0].astype(BF16), w_ffn_down[0].astype(BF16), tm=tm, tf=256, tiles_per_batch=S // tm)

    sh_m, sc_m, gt_m, sh_f, sc_f, gt_f = mods(1)
    tm = 512
    qkv = _l1_proj(h, row(g_kv_b), sc_kv, sh_kv, row(g_mix[1]), sc_m, sh_m,
                   w_kv_b.astype(BF16), w_q_b[0].astype(BF16), tm=tm, tiles_per_batch=S // tm)

    outs, lses = [], []
    for g, (window, dil) in enumerate(DIL_PATTERNS):
        bias = _dil_bias(rpb_table[:, g * DIL_HEADS:(g + 1) * DIL_HEADS], window // dil, dil)
        o_g, lse_g = _dil_attn(*qkv[3 * g:3 * g + 3], bias, group=g, dil=dil, batch=B, seq=S)
        outs.append(o_g)
        lses.append(lse_g)
    h = _dil_out(outs, lses, w_o_b[0].astype(BF16), h, gt_m, tm=tm, tiles_per_batch=S // tm)

    hn, eids, ranks, gates, counts = _route(h, row(g_ffn[1]), sc_f, sh_f, w_router[0].T, tm=tm,
                                            tiles_per_batch=S // tm)
    n_tiles = (T * 2) // MOE_TM + N_EXPERTS
    dest, tile_expert, n_active, pad_rows = _moe_plan(eids, ranks, counts[:, 0], tm=MOE_TM, n_tiles=n_tiles)
    xs = _dispatch(pad_rows, dest, hn, n_slots=n_tiles * MOE_TM, tm=tm)
    y = _experts(tile_expert, n_active, xs, w_exp_gu[0].astype(BF16), w_exp_down[0].astype(BF16),
                 tm=MOE_TM, tf=256)
    tm = 256
    out = _combine(dest, y, gates.T, h, gt_f, row(g_final), tm=tm, tiles_per_batch=S // tm)
    return out.reshape(B, S, D)
```

```python
import functools
import math

import numpy as np
import jax
import jax.numpy as jnp
from jax import lax
from jax.experimental import pallas as pl
from jax.experimental.pallas import tpu as pltpu

D_MODEL = 1024
N_MOD = 6
EPS = 1e-6

MLA_HEADS = 16
MLA_Q_RANK = 384
MLA_KV_RANK = 256
MLA_NOPE = 64
MLA_ROPE = 32
MLA_V = 64
MLA_QK = MLA_NOPE + MLA_ROPE
ROPE_THETA = 10000.0
LANES = 128
ONES_ROWS = 16
HEAD_PAD = 128

DIL_PATTERNS = ((128, 1), (512, 4), (2048, 16))
DIL_GROUPS = len(DIL_PATTERNS)
DIL_HEADS = 8
DIL_HEAD_DIM = 64
DIL_BLOCK = 128
DIL_SUBBLOCKS = 8
DIL_W = DIL_HEADS * DIL_HEAD_DIM
RPB_BUCKETS = 32
RPB_MAX_DIST = 2048

D_FF = 2816
N_EXPERTS = 8
MOE_TM = 512

TM_MLA_PROJ = 1024
TM_FFN = 512
TM_L1_PROJ = 1024
TM_DIL_OUT = 1024
TM_ROUTE = 1024
TM_DISPATCH = 1024
TM_COMBINE = 512
MLA_TQ = 1024
FF_CHUNK = 256

MASK_VALUE = -1e30
LOG2E = math.log2(math.e)

F32 = jnp.float32
BF16 = jnp.bfloat16

VMEM_LIMIT = 56 * 1024 * 1024


def _cparams(sem):
    return pltpu.CompilerParams(dimension_semantics=sem, vmem_limit_bytes=VMEM_LIMIT)


def _dot(a, b):
    return jnp.dot(a, b, preferred_element_type=F32)


def _dot_nt(a, b):
    return lax.dot_general(a, b, (((1,), (1,)), ((), ())), preferred_element_type=F32)


def _rms_scale(x):
    return lax.rsqrt(jnp.mean(x * x, axis=-1, keepdims=True) + EPS)


def _normmod(x, g, sc, sh):
    return (x * _rms_scale(x)) * (g * (1.0 + sc)) + sh


def _silu(x):
    return x * (1.0 / (1.0 + jnp.exp(-x)))


def _ada_kernel(c_ref, w_ref, b_ref, o_ref):
    c = c_ref[...]
    o_ref[...] = jnp.dot(_silu(c), w_ref[...], preferred_element_type=F32,
                         precision=lax.Precision.HIGHEST) + b_ref[...]


def _ada(c8, w, b, tn=512):
    L, D, N = w.shape
    return pl.pallas_call(
        _ada_kernel,
        grid=(L, N // tn),
        in_specs=[pl.BlockSpec((8, D), lambda l, j: (0, 0)),
                  pl.BlockSpec((None, D, tn), lambda l, j: (l, 0, j)),
                  pl.BlockSpec((None, 1, tn), lambda l, j: (l, 0, j))],
        out_specs=pl.BlockSpec((None, 8, tn), lambda l, j: (l, 0, j)),
        out_shape=jax.ShapeDtypeStruct((L, 8, N), F32),
        compiler_params=_cparams(("parallel", "parallel")),
        name="ada_mod",
    )(c8, w, b)


def _mla_proj_kernel(h_ref, g_ref, sc_ref, sh_ref, cos_ref, sin_ref, w1_ref, gq_ref, gkv_ref,
                     wqa_ref, wqb_ref, wka_ref, wvt_ref, q_ref, k_ref, vt_ref, *, qscale):
    hn = _normmod(h_ref[...], g_ref[...], sc_ref[...], sh_ref[...]).astype(BF16)
    z = _dot(hn, w1_ref[...])
    ql = z[:, :MLA_Q_RANK]
    kvl = z[:, MLA_Q_RANK:MLA_Q_RANK + MLA_KV_RANK]
    kr = z[:, MLA_Q_RANK + MLA_KV_RANK:MLA_Q_RANK + MLA_KV_RANK + HEAD_PAD]
    krs = z[:, MLA_Q_RANK + MLA_KV_RANK + HEAD_PAD:]
    qn = (ql * _rms_scale(ql) * gq_ref[...]).astype(BF16)
    kvn = (kvl * _rms_scale(kvl) * gkv_ref[...]).astype(BF16)
    cos = cos_ref[...]
    sin = sin_ref[...]
    k_rope = kr * cos + krs * sin
    cos2 = jnp.concatenate([cos, cos], axis=1)
    sin2 = jnp.concatenate([sin, sin], axis=1)
    k_rope2 = jnp.concatenate([k_rope, k_rope], axis=1)
    for hp in range(MLA_HEADS // 2):
        sl = slice(2 * HEAD_PAD * hp, 2 * HEAD_PAD * (hp + 1))
        a = _dot(qn, wqa_ref[:, sl])
        b = _dot(qn, wqb_ref[:, sl])
        q_ref[:, sl] = ((a * cos2 + b * sin2) * qscale).astype(BF16)
        k_ref[:, sl] = (_dot(kvn, wka_ref[:, sl]) + k_rope2).astype(BF16)
    vt_ref[...] = _dot_nt(wvt_ref[...], kvn).astype(BF16)


def _mla_proj(h, g, sc, sh, cos_t, sin_t, w1, gq, gkv, wqa, wqb, wka, wvt, *, tm, batch, seq):
    T, D = h.shape
    HP = MLA_HEADS * HEAD_PAD
    tpb = seq // tm
    row = lambda n: pl.BlockSpec((tm, n), lambda i: (i, 0))
    bvec = pl.BlockSpec((None, 1, D), lambda i: (i // tpb, 0, 0))
    full = lambda a: pl.BlockSpec(a.shape, lambda i: (0,) * a.ndim)
    return pl.pallas_call(
        functools.partial(_mla_proj_kernel, qscale=(MLA_QK ** -0.5) * LOG2E),
        grid=(T // tm,),
        in_specs=[row(D), full(g), bvec, bvec, row(HEAD_PAD), row(HEAD_PAD), full(w1), full(gq), full(gkv),
                  full(wqa), full(wqb), full(wka), full(wvt)],
        out_specs=[row(HP), row(HP),
                   pl.BlockSpec((None, MLA_HEADS * MLA_V, tm), lambda i: (i // tpb, 0, i % tpb))],
        out_shape=[jax.ShapeDtypeStruct((T, HP), BF16), jax.ShapeDtypeStruct((T, HP), BF16),
                   jax.ShapeDtypeStruct((batch, MLA_HEADS * MLA_V, seq), BF16)],
        compiler_params=_cparams(("parallel",)),
        name="mla_proj",
    )(h, g, sc, sh, cos_t, sin_t, w1, gq, gkv, wqa, wqb, wka, wvt)


def _mla_attn_kernel(q_ref, k_ref, vt_ref, o_ref, m_sc, acc_sc, s_sc, mb_sc, *, tq):
    qi = pl.program_id(2)
    tk = tq
    m_sc[...] = jnp.full(m_sc.shape, -jnp.inf, F32)
    acc_sc[...] = jnp.zeros(acc_sc.shape, F32)

    def scores(j, diagonal, slot):
        off = pl.multiple_of(j * tk, tk)
        for hh in range(2):
            q = q_ref[:, HEAD_PAD * hh:HEAD_PAD * (hh + 1)]
            k = k_ref[pl.ds(off, tk), HEAD_PAD * hh:HEAD_PAD * (hh + 1)]
            s = _dot_nt(k, q)
            if diagonal:
                key = lax.broadcasted_iota(jnp.int32, (tk, tq), 0)
                qry = lax.broadcasted_iota(jnp.int32, (tk, tq), 1)
                s = jnp.where(key <= qry, s, -jnp.inf)
            s_sc[slot, hh] = s
            m_blk = jnp.max(jnp.max(s.reshape(tk // 8, 8, tq), axis=0), axis=0, keepdims=True)
            mb_sc[slot, hh] = jnp.broadcast_to(m_blk, (8, tq))

    def softmax_values(j, slot):
        off = pl.multiple_of(j * tk, tk)
        ones = jnp.ones((ONES_ROWS, tk), BF16)
        for hh in range(2):
            s3 = s_sc[slot, hh].reshape(tk // 8, 8, tq)
            m_prev = m_sc[hh]
            m_new = jnp.maximum(m_prev, mb_sc[slot, hh])
            alpha = jnp.exp2(m_prev - m_new)
            p = jnp.exp2(s3 - m_new[None]).reshape(tk, tq).astype(BF16)
            vt = vt_ref[MLA_V * hh:MLA_V * (hh + 1), pl.ds(off, tk)]
            vt1 = jnp.concatenate([vt, ones], axis=0)
            acc_sc[hh] = alpha[:1] * acc_sc[hh] + _dot(vt1, p)
            m_sc[hh] = m_new

    def step(j, even, diagonal_next=False, has_next=True):
        cur, nxt = (0, 1) if even else (1, 0)
        if has_next:
            scores(j + 1, diagonal_next, nxt)
        softmax_values(j, cur)

    @pl.when(qi == 0)
    def _():
        scores(0, True, 0)

    @pl.when(qi > 0)
    def _():
        scores(0, False, 0)

        def pair(t, carry):
            step(2 * t, True)
            step(2 * t + 1, False)
            return carry

        lax.fori_loop(0, (qi - 1) // 2, pair, 0)

    @pl.when(qi % 2 == 1)
    def _():
        step(qi - 1, True, diagonal_next=True)
        step(qi, False, has_next=False)

    @pl.when((qi % 2 == 0) & (qi > 0))
    def _():
        step(qi - 2, True)
        step(qi - 1, False, diagonal_next=True)

    @pl.when(qi % 2 == 0)
    def _():
        step(qi, True, has_next=False)

    o_t = jnp.concatenate([acc_sc[hh, :MLA_V] / acc_sc[hh, MLA_V:MLA_V + 1] for hh in range(2)], axis=0)
    o_ref[...] = o_t.T.astype(BF16)


def _mla_attn(q, k, vt, *, batch, seq, tq):
    return pl.pallas_call(
        functools.partial(_mla_attn_kernel, tq=tq),
        grid=(batch, MLA_HEADS // 2, seq // tq),
        in_specs=[pl.BlockSpec((None, tq, 2 * HEAD_PAD), lambda b, hp, i: (b, i, hp)),
                  pl.BlockSpec((None, seq, 2 * HEAD_PAD), lambda b, hp, i: (b, 0, hp)),
                  pl.BlockSpec((None, 2 * MLA_V, seq), lambda b, hp, i: (b, hp, 0))],
        out_specs=pl.BlockSpec((None, tq, 2 * MLA_V), lambda b, hp, i: (b, i, hp)),
        out_shape=jax.ShapeDtypeStruct((batch, seq, MLA_HEADS * MLA_V), BF16),
        scratch_shapes=[pltpu.VMEM((2, 8, tq), F32), pltpu.VMEM((2, MLA_V + ONES_ROWS, tq), F32),
                        pltpu.VMEM((2, 2, tq, tq), F32),
                        pltpu.VMEM((2, 2, 8, tq), F32)],
        compiler_params=_cparams(("parallel", "parallel", "arbitrary")),
        name="mla_attn",
    )(q, k, vt)


def _swiglu_tile(x, wgu_ref, wd_ref, acc_sc, tf):
    for f in range(D_FF // tf):
        g = _dot(x, wgu_ref[:, f * tf:(f + 1) * tf])
        u = _dot(x, wgu_ref[:, D_FF + f * tf:D_FF + (f + 1) * tf])
        part = _dot((_silu(g) * u).astype(BF16), wd_ref[f * tf:(f + 1) * tf, :])
        if f == 0:
            acc_sc[...] = part
        else:
            acc_sc[...] += part


def _ffn_kernel(a_ref, wo_ref, h_ref, gtm_ref, g_ref, sc_ref, sh_ref, gt_ref, wgu_ref, wd_ref, o_ref, acc_sc,
                *, tf):
    h = h_ref[...] + gtm_ref[...] * _dot(a_ref[...], wo_ref[...])
    x = _normmod(h, g_ref[...], sc_ref[...], sh_ref[...]).astype(BF16)
    _swiglu_tile(x, wgu_ref, wd_ref, acc_sc, tf)
    o_ref[...] = h + gt_ref[...] * acc_sc[...]


def _ffn(a, w_o, h, gt_m, g, sc, sh, gt, w_gu, w_d, *, tm, tf, tiles_per_batch):
    T, D = h.shape
    bvec = pl.BlockSpec((None, 1, D), lambda i: (i // tiles_per_batch, 0, 0))
    row = pl.BlockSpec((tm, D), lambda i: (i, 0))
    const = lambda w: pl.BlockSpec(w.shape, lambda i: (0,) * w.ndim, pipeline_mode=pl.Buffered(1))
    return pl.pallas_call(
        functools.partial(_ffn_kernel, tf=tf),
        grid=(T // tm,),
        in_specs=[row, const(w_o), row, bvec, const(g), bvec, bvec, bvec, const(w_gu), const(w_d)],
        out_specs=row,
        out_shape=jax.ShapeDtypeStruct((T, D), F32),
        scratch_shapes=[pltpu.VMEM((tm, D), F32)],
        compiler_params=_cparams(("parallel",)),
        name="ffn_dense",
    )(a, w_o, h, gt_m, g, sc, sh, gt, w_gu, w_d)


def _l1_proj_kernel(h_ref, gkv_ref, sckv_ref, shkv_ref, gq_ref, scq_ref, shq_ref, wkv_ref, wq_ref, *refs):
    out_refs, stage_sc = refs[:-1], refs[-1]
    tm = h_ref.shape[0]
    x = h_ref[...]
    xh = x * _rms_scale(x)
    hn_kv = (xh * (gkv_ref[...] * (1.0 + sckv_ref[...])) + shkv_ref[...]).astype(BF16)
    hn_q = (xh * (gq_ref[...] * (1.0 + scq_ref[...])) + shq_ref[...]).astype(BF16)
    for g, (_, dil) in enumerate(DIL_PATTERNS):
        ys = (_dot(hn_q, wq_ref[:, g * DIL_W:(g + 1) * DIL_W]) * (DIL_HEAD_DIM ** -0.5),
              _dot(hn_kv, wkv_ref[:, 2 * g * DIL_W:(2 * g + 1) * DIL_W]),
              _dot(hn_kv, wkv_ref[:, (2 * g + 1) * DIL_W:(2 * g + 2) * DIL_W]))
        for y, out_ref in zip(ys, out_refs[3 * g:3 * g + 3]):
            if dil == 1:
                out_ref[...] = y.astype(BF16)
            else:
                for c in range(DIL_W // LANES):
                    stage_sc[c] = y[:, c * LANES:(c + 1) * LANES]
                for r in range(dil):
                    for c in range(DIL_W // LANES):
                        col = r * DIL_W + c * LANES
                        out_ref[:, col:col + LANES] = stage_sc[c, pl.ds(r, tm // dil, stride=dil), :].astype(BF16)


def _l1_proj(h, gkv, sckv, shkv, gq, scq, shq, wkv, wq, *, tm, tiles_per_batch):
    T, D = h.shape
    bvec = pl.BlockSpec((None, 1, D), lambda i: (i // tiles_per_batch, 0, 0))
    full = lambda a: pl.BlockSpec(a.shape, lambda i: (0,) * a.ndim)
    out_specs, out_shape = [], []
    for _, dil in DIL_PATTERNS:
        for _ in range(3):
            out_specs.append(pl.BlockSpec((tm // dil, dil * DIL_W), lambda i: (i, 0)))
            out_shape.append(jax.ShapeDtypeStruct((T // dil, dil * DIL_W), BF16))
    return pl.pallas_call(
        _l1_proj_kernel,
        grid=(T // tm,),
        in_specs=[pl.BlockSpec((tm, D), lambda i: (i, 0)), full(gkv), bvec, bvec, full(gq), bvec, bvec,
                  full(wkv), full(wq)],
        out_specs=out_specs,
        out_shape=out_shape,
        scratch_shapes=[pltpu.VMEM((DIL_W // LANES, tm, LANES), F32)],
        compiler_params=_cparams(("parallel",)),
        name="l1_proj",
    )(h, gkv, sckv, shkv, gq, scq, shq, wkv, wq)


def _dil_attn_kernel(q_ref, kc_ref, kp_ref, vc_ref, vp_ref, bias_ref, o_ref, lse_ref):
    i = pl.program_id(2)
    nsub = q_ref.shape[0] // DIL_BLOCK
    lane = lax.broadcasted_iota(jnp.int32, (1, 2 * DIL_HEAD_DIM), 1)
    lo = lane < DIL_HEAD_DIM
    col = lax.broadcasted_iota(jnp.int32, (1, 2 * DIL_BLOCK), 1)
    edge = jnp.where((col < DIL_BLOCK) & (i == 0), MASK_VALUE, 0.0).astype(F32)
    sls = [slice(2 * DIL_HEAD_DIM * hp, 2 * DIL_HEAD_DIM * (hp + 1)) for hp in range(DIL_HEADS // 2)]
    for sub in range(nsub):
        rows = slice(DIL_BLOCK * sub, DIL_BLOCK * (sub + 1))
        before = slice(DIL_BLOCK * (sub - 1), DIL_BLOCK * sub)
        scores = []
        for hp, sl in enumerate(sls):
            q2 = q_ref[rows, sl]
            zero = jnp.zeros_like(q2)
            qs = jnp.concatenate([jnp.where(lo, q2, zero), jnp.where(lo, zero, q2)], axis=0)
            k_prev = kp_ref[:, sl] if sub == 0 else kc_ref[before, sl]
            k2 = jnp.concatenate([k_prev, kc_ref[rows, sl]], axis=0)
            bias = jnp.concatenate([bias_ref[2 * hp], bias_ref[2 * hp + 1]], axis=0)
            s = _dot_nt(qs, k2) + bias
            scores.append(s + edge if sub == 0 else s)
        probs, stats = [], []
        for s in scores:
            m = jnp.max(s, axis=1, keepdims=True)
            p = jnp.exp(s - m)
            probs.append(p.astype(BF16))
            stats.append((m, jnp.sum(p, axis=1, keepdims=True)))
        for sl, p, (m, l) in zip(sls, probs, stats):
            v_prev = vp_ref[:, sl] if sub == 0 else vc_ref[before, sl]
            v2 = jnp.concatenate([v_prev, vc_ref[rows, sl]], axis=0)
            o = _dot(p, v2) / l
            lse = jnp.broadcast_to(m + jnp.log(l), o.shape)
            o_ref[rows, sl] = jnp.where(lo, o[:DIL_BLOCK], o[DIL_BLOCK:])
            lse_ref[rows, sl] = jnp.where(lo, lse[:DIL_BLOCK], lse[DIL_BLOCK:])


def _dil_attn(q, k, v, bias, *, group, dil, batch, seq):
    n = seq // dil
    nsub = min(DIL_SUBBLOCKS, n // DIL_BLOCK)
    run = nsub * DIL_BLOCK
    assert n % run == 0, (n, run)
    q, k, v = (a.reshape(batch, n, dil * DIL_W) for a in (q, k, v))
    cur = pl.BlockSpec((None, run, DIL_W), lambda b, r, i: (b, i, r))
    prev = pl.BlockSpec((None, DIL_BLOCK, DIL_W), lambda b, r, i: (b, jnp.maximum(i * nsub - 1, 0), r))
    out_sd = jax.ShapeDtypeStruct((batch, n, dil * DIL_W), F32)
    o, lse = pl.pallas_call(
        _dil_attn_kernel,
        grid=(batch, dil, n // run),
        in_specs=[cur, cur, prev, cur, prev, pl.BlockSpec(bias.shape, lambda b, r, i: (0, 0, 0))],
        out_specs=[cur, cur],
        out_shape=[out_sd, out_sd],
        compiler_params=_cparams(("parallel", "parallel", "arbitrary")),
        name=f"dil_attn_g{group}",
    )(q, k, k, v, v, bias)
    return o.reshape(batch * n, dil * DIL_W), lse.reshape(batch * n, dil * DIL_W)


def _rpb_bucket(dist):
    exact = RPB_BUCKETS // 2
    d = jnp.maximum(dist, 0)
    d_f = jnp.maximum(d, 1).astype(F32)
    large = exact + (jnp.log(d_f / exact) / math.log(RPB_MAX_DIST / exact)
                     * (RPB_BUCKETS - exact)).astype(jnp.int32)
    return jnp.where(d < exact, d, jnp.minimum(large, RPB_BUCKETS - 1))


def _dil_bias(table, span, dil):
    period = 3 * DIL_BLOCK
    j = jnp.arange(period)
    k_minus_q = jnp.where(j < 2 * DIL_BLOCK, j, j - period)
    dist = DIL_BLOCK - k_minus_q
    band = (dist >= 0) & (dist <= span)
    prof = jnp.where(band[:, None], table[_rpb_bucket(dist * dil)].astype(F32), MASK_VALUE).T
    skew = jnp.tile(prof, (1, DIL_BLOCK))[:, :DIL_BLOCK * (period - 1)]
    return skew.reshape(DIL_HEADS, DIL_BLOCK, period - 1)[:, :, :2 * DIL_BLOCK]


def _dil_out_kernel(o0_ref, o1_ref, o2_ref, l0_ref, l1_ref, l2_ref, w_ref, h_ref, gt_ref, out_ref, *stages):
    tm = h_ref.shape[0]
    stages = list(stages)

    def token_major(ref, dil):
        if dil == 1:
            return ref[...]
        stage = stages.pop()
        for r in range(dil):
            for c in range(DIL_W // LANES):
                col = r * DIL_W + c * LANES
                stage[c, pl.ds(r, tm // dil, stride=dil), :] = ref[:, col:col + LANES]
        return jnp.concatenate([stage[c] for c in range(DIL_W // LANES)], axis=1)

    dils = [dil for _, dil in DIL_PATTERNS]
    l0, l1, l2 = (token_major(ref, d) for ref, d in zip((l0_ref, l1_ref, l2_ref), dils))
    o0, o1, o2 = (token_major(ref, d) for ref, d in zip((o0_ref, o1_ref, o2_ref), dils))
    m = jnp.maximum(jnp.maximum(l0, l1), l2)
    w0, w1, w2 = jnp.exp(l0 - m), jnp.exp(l1 - m), jnp.exp(l2 - m)
    o = (o0 * w0 + o1 * w1 + o2 * w2) / (w0 + w1 + w2)
    out_ref[...] = h_ref[...] + gt_ref[...] * _dot(o.astype(BF16), w_ref[...])


def _dil_out(os_, ls_, w, h, gt, *, tm, tiles_per_batch):
    T, D = h.shape
    row = lambda n: pl.BlockSpec((tm, n), lambda i: (i, 0))
    grp = [pl.BlockSpec((tm // dil, dil * DIL_W), lambda i: (i, 0)) for _, dil in DIL_PATTERNS]
    n_stage = 2 * sum(1 for _, dil in DIL_PATTERNS if dil > 1)
    return pl.pallas_call(
        _dil_out_kernel,
        grid=(T // tm,),
        in_specs=grp + grp + [pl.BlockSpec(w.shape, lambda i: (0, 0)), row(D),
                              pl.BlockSpec((None, 1, D), lambda i: (i // tiles_per_batch, 0, 0))],
        out_specs=row(D),
        out_shape=jax.ShapeDtypeStruct((T, D), F32),
        scratch_shapes=[pltpu.VMEM((DIL_W // LANES, tm, LANES), F32)] * n_stage,
        compiler_params=_cparams(("parallel",)),
        name="dil_out",
    )(*os_, *ls_, w, h, gt)


def _route_kernel(h_ref, g_ref, sc_ref, sh_ref, wrt_ref, hn_ref, eid_ref, rank_ref, gate_ref, cnt_ref, carry_sc):
    @pl.when(pl.program_id(0) == 0)
    def _():
        carry_sc[...] = jnp.zeros(carry_sc.shape, F32)

    hn = _normmod(h_ref[...], g_ref[...], sc_ref[...], sh_ref[...])
    hn_ref[...] = hn
    tm = hn.shape[0]
    logits = lax.dot_general(wrt_ref[...], hn, (((1,), (1,)), ((), ())), preferred_element_type=F32,
                             precision=lax.Precision.HIGHEST)
    idx = lax.broadcasted_iota(jnp.int32, logits.shape, 0)
    v1 = jnp.max(logits, axis=0, keepdims=True)
    i1 = jnp.min(jnp.where(logits == v1, idx, N_EXPERTS), axis=0, keepdims=True)
    rest = jnp.where(idx == i1, -jnp.inf, logits)
    v2 = jnp.max(rest, axis=0, keepdims=True)
    i2 = jnp.min(jnp.where(rest == v2, idx, N_EXPERTS), axis=0, keepdims=True)
    e = jnp.exp(v2 - v1)
    gate_ref[...] = jnp.concatenate([1.0 / (1.0 + e), e / (1.0 + e)], axis=0)
    eid_ref[...] = jnp.concatenate([i1, i2], axis=0)

    sel = ((idx == i1) | (idx == i2)).astype(BF16)
    before = (lax.broadcasted_iota(jnp.int32, (tm, tm), 0)
              < lax.broadcasted_iota(jnp.int32, (tm, tm), 1)).astype(BF16)
    rank_all = carry_sc[:, :1] + _dot(sel, before)
    r1 = jnp.sum(jnp.where(idx == i1, rank_all, 0.0), axis=0, keepdims=True)
    r2 = jnp.sum(jnp.where(idx == i2, rank_all, 0.0), axis=0, keepdims=True)
    rank_ref[...] = jnp.concatenate([r1, r2], axis=0).astype(jnp.int32)
    carry_sc[...] += jnp.sum(sel.astype(F32), axis=1, keepdims=True)
    cnt_ref[...] = carry_sc[...].astype(jnp.int32)


def _route(h, g, sc, sh, wrt, *, tm, tiles_per_batch):
    T, D = h.shape
    bvec = pl.BlockSpec((None, 1, D), lambda i: (i // tiles_per_batch, 0, 0))
    lane_blk = pl.BlockSpec((2, tm), lambda i: (0, i))
    return pl.pallas_call(
        _route_kernel,
        grid=(T // tm,),
        in_specs=[pl.BlockSpec((tm, D), lambda i: (i, 0)), pl.BlockSpec((1, D), lambda i: (0, 0)), bvec, bvec,
                  pl.BlockSpec(wrt.shape, lambda i: (0, 0))],
        out_specs=[pl.BlockSpec((tm, D), lambda i: (i, 0)), lane_blk, lane_blk, lane_blk,
                   pl.BlockSpec((N_EXPERTS, HEAD_PAD), lambda i: (0, 0))],
        out_shape=[jax.ShapeDtypeStruct((T, D), F32), jax.ShapeDtypeStruct((2, T), jnp.int32),
                   jax.ShapeDtypeStruct((2, T), jnp.int32), jax.ShapeDtypeStruct((2, T), F32),
                   jax.ShapeDtypeStruct((N_EXPERTS, HEAD_PAD), jnp.int32)],
        scratch_shapes=[pltpu.VMEM((N_EXPERTS, HEAD_PAD), F32)],
        compiler_params=_cparams(("arbitrary",)),
        name="moe_route",
    )(h, g, sc, sh, wrt)


def _dispatch_kernel(pad_ref, dest_ref, hn_ref, xs_ref, zero_sc, sem, zsem):
    tm = hn_ref.shape[0]

    def row_copy(r, k):
        return pltpu.make_async_copy(hn_ref.at[pl.ds(r, 1), :], xs_ref.at[pl.ds(dest_ref[k, r], 1), :], sem)

    def issue(r, carry):
        row_copy(r, 0).start(priority=0)
        row_copy(r, 1).start(priority=1)
        return carry

    for r in range(tm):
        issue(r, 0)

    @pl.when(pl.program_id(0) == pl.num_programs(0) - 1)
    def _():
        zero_sc[...] = jnp.zeros(zero_sc.shape, F32)

        def pad_copy(p):
            return pltpu.make_async_copy(zero_sc.at[pl.ds(0, 1), :], xs_ref.at[pl.ds(p, 1), :], zsem)

        for e in range(N_EXPERTS):
            lo, hi = pad_ref[0, e], pad_ref[1, e]

            def zissue(p, carry):
                pad_copy(p).start()
                return carry

            def zwait(p, carry):
                pad_copy(p).wait()
                return carry

            lax.fori_loop(lo, hi, zissue, 0)
            lax.fori_loop(lo, hi, zwait, 0)

        zrows = zero_sc.shape[0]

        def tail_copy(c):
            return pltpu.make_async_copy(zero_sc, xs_ref.at[pl.ds(pl.multiple_of(c * zrows, zrows), zrows), :], zsem)

        def tissue(c, carry):
            tail_copy(c).start()
            return carry

        def twait(c, carry):
            tail_copy(c).wait()
            return carry

        lo, hi = pad_ref[1, N_EXPERTS - 1] // zrows, xs_ref.shape[0] // zrows
        lax.fori_loop(lo, hi, tissue, 0)
        lax.fori_loop(lo, hi, twait, 0)

    for _ in range(2):
        pltpu.make_async_copy(hn_ref, xs_ref.at[pl.ds(0, tm), :], sem).wait()


def _dispatch(pad_rows, dest, hn, *, n_slots, tm):
    T, D = hn.shape
    return pl.pallas_call(
        _dispatch_kernel,
        grid_spec=pltpu.PrefetchScalarGridSpec(
            num_scalar_prefetch=1,
            grid=(T // tm,),
            in_specs=[pl.BlockSpec((2, tm), lambda i, pad: (0, i), memory_space=pltpu.SMEM),
                      pl.BlockSpec((tm, D), lambda i, pad: (i, 0))],
            out_specs=pl.BlockSpec(memory_space=pl.ANY),
            scratch_shapes=[pltpu.VMEM((64, D), F32), pltpu.SemaphoreType.DMA, pltpu.SemaphoreType.DMA]),
        out_shape=jax.ShapeDtypeStruct((n_slots, D), F32),
        compiler_params=_cparams(("arbitrary",)),
        name="moe_dispatch",
    )(pad_rows, dest, hn)


def _experts_kernel(te_ref, na_ref, xs_ref, wgu_ref, wd_ref, y_ref, acc_sc, *, tf):
    @pl.when(pl.program_id(0) < na_ref[0])
    def _():
        _swiglu_tile(xs_ref[...].astype(BF16), wgu_ref, wd_ref, acc_sc, tf)
        y_ref[...] = acc_sc[...]

    @pl.when(pl.program_id(0) >= na_ref[0])
    def _():
        y_ref[...] = jnp.zeros(y_ref.shape, F32)


def _experts(tile_expert, n_active, xs, w_gu, w_d, *, tm, tf):
    P, D = xs.shape

    def tile(j, te, na):
        return jnp.minimum(j, na[0] - 1)

    return pl.pallas_call(
        functools.partial(_experts_kernel, tf=tf),
        grid_spec=pltpu.PrefetchScalarGridSpec(
            num_scalar_prefetch=2,
            grid=(P // tm,),
            in_specs=[pl.BlockSpec((tm, D), lambda j, te, na: (tile(j, te, na), 0)),
                      pl.BlockSpec((None, D, 2 * D_FF), lambda j, te, na: (te[tile(j, te, na)], 0, 0)),
                      pl.BlockSpec((None, D_FF, D), lambda j, te, na: (te[tile(j, te, na)], 0, 0))],
            out_specs=pl.BlockSpec((tm, D), lambda j, te, na: (j, 0)),
            scratch_shapes=[pltpu.VMEM((tm, D), F32)]),
        out_shape=jax.ShapeDtypeStruct((P, D), F32),
        compiler_params=_cparams(("arbitrary",)),
        name="moe_experts",
    )(tile_expert, n_active, xs, w_gu, w_d)


def _combine_kernel(dest_ref, y_ref, gates_ref, h_ref, gt_ref, gfin_ref, o_ref, ybuf, sem):
    tm = h_ref.shape[0]

    def row_copy(r, k):
        return pltpu.make_async_copy(y_ref.at[pl.ds(dest_ref[k, r], 1), :], ybuf.at[k, pl.ds(r, 1), :], sem)

    def issue(r, carry):
        row_copy(r, 0).start(priority=0)
        row_copy(r, 1).start(priority=1)
        return carry

    for r in range(tm):
        issue(r, 0)
    for k in range(2):
        pltpu.make_async_copy(y_ref.at[pl.ds(0, tm), :], ybuf.at[k], sem).wait()
    gates = gates_ref[...]
    moe = gates[:, 0:1] * ybuf[0] + gates[:, 1:2] * ybuf[1]
    y = h_ref[...] + gt_ref[...] * moe
    o_ref[...] = y * _rms_scale(y) * gfin_ref[...]


def _combine(dest, y, gates, h, gt, gfin, *, tm, tiles_per_batch):
    T, D = h.shape
    return pl.pallas_call(
        _combine_kernel,
        grid=(T // tm,),
        in_specs=[pl.BlockSpec((2, tm), lambda i: (0, i), memory_space=pltpu.SMEM),
                  pl.BlockSpec(memory_space=pl.ANY),
                  pl.BlockSpec((tm, 2), lambda i: (i, 0)),
                  pl.BlockSpec((tm, D), lambda i: (i, 0)),
                  pl.BlockSpec((None, 1, D), lambda i: (i // tiles_per_batch, 0, 0)),
                  pl.BlockSpec((1, D), lambda i: (0, 0))],
        out_specs=pl.BlockSpec((tm, D), lambda i: (i, 0)),
        out_shape=jax.ShapeDtypeStruct((T, D), F32),
        scratch_shapes=[pltpu.VMEM((2, tm, D), F32), pltpu.SemaphoreType.DMA],
        compiler_params=_cparams(("arbitrary",)),
        name="moe_combine",
    )(dest, y, gates, h, gt, gfin)


def _moe_plan(eids, ranks, counts, *, tm, n_tiles):
    padded = (counts + tm - 1) // tm * tm
    ends = jnp.cumsum(padded)
    starts = ends - padded
    dest = ranks
    for e in range(N_EXPERTS):
        dest = dest + jnp.where(eids == e, starts[e], 0)
    tile_start = jnp.arange(n_tiles, dtype=jnp.int32) * tm
    tile_expert = jnp.minimum(jnp.sum(tile_start[:, None] >= ends[None, :], axis=1), N_EXPERTS - 1)
    n_active = (ends[-1] // tm).reshape(1)
    pad_rows = jnp.stack([starts + counts, ends])
    return dest.astype(jnp.int32), tile_expert.astype(jnp.int32), n_active.astype(jnp.int32), pad_rows.astype(jnp.int32)


def _mla_weights(w_in, w_q_up, w_kv_up):
    D = w_in.shape[0]
    half = MLA_ROPE // 2
    w_ql = w_in[:, :MLA_Q_RANK]
    w_kvl = w_in[:, MLA_Q_RANK:MLA_Q_RANK + MLA_KV_RANK]
    w_kr = w_in[:, MLA_Q_RANK + MLA_KV_RANK:]
    z64 = jnp.zeros((D, MLA_NOPE), F32)
    z32 = jnp.zeros((D, HEAD_PAD - MLA_QK), F32)
    kr_pad = jnp.concatenate([z64, w_kr, z32], axis=1)
    kr_swap = jnp.concatenate([z64, w_kr[:, half:], w_kr[:, :half], z32], axis=1)
    w1 = jnp.concatenate([w_ql, w_kvl, kr_pad, kr_swap], axis=1).astype(BF16)

    wq = w_q_up.reshape(MLA_Q_RANK, MLA_HEADS, MLA_QK)
    nope, rope = wq[..., :MLA_NOPE], wq[..., MLA_NOPE:]
    zq = jnp.zeros((MLA_Q_RANK, MLA_HEADS, HEAD_PAD - MLA_QK), F32)
    wqa = jnp.concatenate([nope, rope, zq], axis=-1).reshape(MLA_Q_RANK, MLA_HEADS * HEAD_PAD).astype(BF16)
    wqb = jnp.concatenate([jnp.zeros_like(nope), rope[..., half:], rope[..., :half], zq], axis=-1)
    wqb = wqb.reshape(MLA_Q_RANK, MLA_HEADS * HEAD_PAD).astype(BF16)

    wkv = w_kv_up.reshape(MLA_KV_RANK, MLA_HEADS, MLA_NOPE + MLA_V)
    k_nope, v = wkv[..., :MLA_NOPE], wkv[..., MLA_NOPE:]
    zk = jnp.zeros((MLA_KV_RANK, MLA_HEADS, HEAD_PAD - MLA_NOPE), F32)
    wka = jnp.concatenate([k_nope, zk], axis=-1).reshape(MLA_KV_RANK, MLA_HEADS * HEAD_PAD).astype(BF16)
    wvt = v.reshape(MLA_KV_RANK, MLA_HEADS * MLA_V).T.astype(BF16)
    return w1, wqa, wqb, wka, wvt


def _rope_tables(positions):
    inv_freq = ROPE_THETA ** (-jnp.arange(0, MLA_ROPE, 2, dtype=F32) / MLA_ROPE)
    ang = positions.astype(F32).reshape(-1, 1) * inv_freq
    cos, sin = jnp.cos(ang), jnp.sin(ang)
    T = ang.shape[0]
    one = jnp.ones((T, MLA_NOPE), F32)
    z64 = jnp.zeros((T, MLA_NOPE), F32)
    z32 = jnp.zeros((T, HEAD_PAD - MLA_QK), F32)
    return (jnp.concatenate([one, cos, cos, z32], axis=1),
            jnp.concatenate([z64, -sin, sin, z32], axis=1))


def kernel(x, c, positions, g_mix, g_ffn, w_ada, b_ada, w_mla_in, g_mla_q, w_mla_q_up, g_mla_kv, w_mla_kv_up,
           w_mla_out, g_kv_b, w_ada_kv, b_ada_kv, w_kv_b, rpb_table, w_q_b, w_o_b, w_ffn_gu, w_ffn_down,
           w_router, w_exp_gu, w_exp_down, g_final):
    B, S, D = x.shape
    T = B * S
    h = x.reshape(T, D)

    c8 = jnp.zeros((8, D), F32).at[:B].set(c)
    mod = _ada(c8, w_ada, b_ada[:, None, :])[:, :B]
    mod = mod.reshape(2, B, N_MOD, 1, D)
    mod_kv = _ada(c8, w_ada_kv[None], b_ada_kv[None, None, :])[0, :B].reshape(B, 2, 1, D)
    sh_kv, sc_kv = mod_kv[:, 0], mod_kv[:, 1]

    def mods(layer):
        return [mod[layer, :, k] for k in range(N_MOD)]

    row = lambda v: v.reshape(1, -1)

    sh_m, sc_m, gt_m, sh_f, sc_f, gt_f = mods(0)
    w1, wqa, wqb, wka, wvt = _mla_weights(w_mla_in[0], w_mla_q_up[0], w_mla_kv_up[0])
    cos_t, sin_t = _rope_tables(positions)
    q, k, vt = _mla_proj(h, row(g_mix[0]), sc_m, sh_m, cos_t, sin_t, w1, row(g_mla_q[0]), row(g_mla_kv[0]),
                         wqa, wqb, wka, wvt, tm=TM_MLA_PROJ, batch=B, seq=S)
    o = _mla_attn(q.reshape(B, S, -1), k.reshape(B, S, -1), vt, batch=B, seq=S, tq=MLA_TQ)

    h = _ffn(o.reshape(T, -1), w_mla_out[0].astype(BF16), h, gt_m, row(g_ffn[0]), sc_f, sh_f, gt_f,
             w_ffn_gu[0].astype(BF16), w_ffn_down[0].astype(BF16), tm=TM_FFN, tf=FF_CHUNK,
             tiles_per_batch=S // TM_FFN)

    sh_m, sc_m, gt_m, sh_f, sc_f, gt_f = mods(1)
    qkv = _l1_proj(h, row(g_kv_b), sc_kv, sh_kv, row(g_mix[1]), sc_m, sh_m,
                   w_kv_b.astype(BF16), w_q_b[0].astype(BF16), tm=TM_L1_PROJ, tiles_per_batch=S // TM_L1_PROJ)

    outs, lses = [], []
    for g, (window, dil) in enumerate(DIL_PATTERNS):
        bias = _dil_bias(rpb_table[:, g * DIL_HEADS:(g + 1) * DIL_HEADS], window // dil, dil)
        o_g, lse_g = _dil_attn(*qkv[3 * g:3 * g + 3], bias, group=g, dil=dil, batch=B, seq=S)
        outs.append(o_g)
        lses.append(lse_g)
    h = _dil_out(outs, lses, w_o_b[0].astype(BF16), h, gt_m, tm=TM_DIL_OUT, tiles_per_batch=S // TM_DIL_OUT)

    hn, eids, ranks, gates, counts = _route(h, row(g_ffn[1]), sc_f, sh_f, w_router[0].T, tm=TM_ROUTE,
                                            tiles_per_batch=S // TM_ROUTE)
    n_tiles = (T * 2) // MOE_TM + N_EXPERTS
    dest, tile_expert, n_active, pad_rows = _moe_plan(eids, ranks, counts[:, 0], tm=MOE_TM, n_tiles=n_tiles)
    xs = _dispatch(pad_rows, dest, hn, n_slots=n_tiles * MOE_TM, tm=TM_DISPATCH)
    y = _experts(tile_expert, n_active, xs, w_exp_gu[0].astype(BF16), w_exp_down[0].astype(BF16),
                 tm=MOE_TM, tf=FF_CHUNK)
    out = _combine(dest, y, gates.T, h, gt_f, row(g_final), tm=TM_COMBINE, tiles_per_batch=S // TM_COMBINE)
    return out.reshape(B, S, D)
```

```python
import functools
import math

import numpy as np
import jax
import jax.numpy as jnp
from jax import lax
from jax.experimental import pallas as pl
from jax.experimental.pallas import tpu as pltpu

D_MODEL = 1024
N_MOD = 6
EPS = 1e-6

MLA_HEADS = 16
MLA_Q_RANK = 384
MLA_KV_RANK = 256
MLA_NOPE = 64
MLA_ROPE = 32
MLA_V = 64
MLA_QK = MLA_NOPE + MLA_ROPE
ROPE_THETA = 10000.0
LANES = 128
ONES_ROWS = 16
STRIP = 256
HEAD_PAD = 128

DIL_PATTERNS = ((128, 1), (512, 4), (2048, 16))
DIL_GROUPS = len(DIL_PATTERNS)
DIL_HEADS = 8
DIL_HEAD_DIM = 64
DIL_BLOCK = 128
DIL_SUBBLOCKS = 8
DIL_W = DIL_HEADS * DIL_HEAD_DIM
RPB_BUCKETS = 32
RPB_MAX_DIST = 2048

D_FF = 2816
N_EXPERTS = 8
MOE_TM = 512

TM_MLA_PROJ = 1024
TM_FFN = 512
TM_L1_PROJ = 1024
TM_DIL_OUT = 1024
TM_ROUTE = 1024
TM_DISPATCH = 1024
TM_COMBINE = 512
MLA_TQ = 1024
FF_CHUNK = 256

MASK_VALUE = -1e30
LOG2E = math.log2(math.e)

F32 = jnp.float32
BF16 = jnp.bfloat16

VMEM_LIMIT = 56 * 1024 * 1024


def _cparams(sem):
    return pltpu.CompilerParams(dimension_semantics=sem, vmem_limit_bytes=VMEM_LIMIT)


def _dot(a, b):
    return jnp.dot(a, b, preferred_element_type=F32)


def _dot_nt(a, b):
    return lax.dot_general(a, b, (((1,), (1,)), ((), ())), preferred_element_type=F32)


def _rms_scale(x):
    return lax.rsqrt(jnp.mean(x * x, axis=-1, keepdims=True) + EPS)


def _normmod(x, g, sc, sh):
    return (x * _rms_scale(x)) * (g * (1.0 + sc)) + sh


def _silu(x):
    return x * (1.0 / (1.0 + jnp.exp(-x)))


def _ada_kernel(c_ref, w_ref, b_ref, o_ref):
    c = c_ref[...]
    o_ref[...] = jnp.dot(_silu(c), w_ref[...], preferred_element_type=F32,
                         precision=lax.Precision.HIGHEST) + b_ref[...]


def _ada(c8, w, b, tn=512):
    L, D, N = w.shape
    return pl.pallas_call(
        _ada_kernel,
        grid=(L, N // tn),
        in_specs=[pl.BlockSpec((8, D), lambda l, j: (0, 0)),
                  pl.BlockSpec((None, D, tn), lambda l, j: (l, 0, j)),
                  pl.BlockSpec((None, 1, tn), lambda l, j: (l, 0, j))],
        out_specs=pl.BlockSpec((None, 8, tn), lambda l, j: (l, 0, j)),
        out_shape=jax.ShapeDtypeStruct((L, 8, N), F32),
        compiler_params=_cparams(("parallel", "parallel")),
        name="ada_mod",
    )(c8, w, b)


def _mla_proj_kernel(h_ref, g_ref, sc_ref, sh_ref, cos_ref, sin_ref, w1_ref, gq_ref, gkv_ref,
                     wqa_ref, wqb_ref, wka_ref, wvt_ref, q_ref, k_ref, vt_ref, *, qscale):
    hn = _normmod(h_ref[...], g_ref[...], sc_ref[...], sh_ref[...]).astype(BF16)
    z = _dot(hn, w1_ref[...])
    ql = z[:, :MLA_Q_RANK]
    kvl = z[:, MLA_Q_RANK:MLA_Q_RANK + MLA_KV_RANK]
    kr = z[:, MLA_Q_RANK + MLA_KV_RANK:MLA_Q_RANK + MLA_KV_RANK + HEAD_PAD]
    krs = z[:, MLA_Q_RANK + MLA_KV_RANK + HEAD_PAD:]
    qn = (ql * _rms_scale(ql) * gq_ref[...]).astype(BF16)
    kvn = (kvl * _rms_scale(kvl) * gkv_ref[...]).astype(BF16)
    cos = cos_ref[...]
    sin = sin_ref[...]
    k_rope = kr * cos + krs * sin
    cos2 = jnp.concatenate([cos, cos], axis=1)
    sin2 = jnp.concatenate([sin, sin], axis=1)
    k_rope2 = jnp.concatenate([k_rope, k_rope], axis=1)
    for hp in range(MLA_HEADS // 2):
        sl = slice(2 * HEAD_PAD * hp, 2 * HEAD_PAD * (hp + 1))
        a = _dot(qn, wqa_ref[:, sl])
        b = _dot(qn, wqb_ref[:, sl])
        q_ref[:, sl] = ((a * cos2 + b * sin2) * qscale).astype(BF16)
        k_ref[:, sl] = (_dot(kvn, wka_ref[:, sl]) + k_rope2).astype(BF16)
    vt_ref[...] = _dot_nt(wvt_ref[...], kvn).astype(BF16)


def _mla_proj(h, g, sc, sh, cos_t, sin_t, w1, gq, gkv, wqa, wqb, wka, wvt, *, tm, batch, seq):
    T, D = h.shape
    HP = MLA_HEADS * HEAD_PAD
    tpb = seq // tm
    row = lambda n: pl.BlockSpec((tm, n), lambda i: (i, 0))
    bvec = pl.BlockSpec((None, 1, D), lambda i: (i // tpb, 0, 0))
    full = lambda a: pl.BlockSpec(a.shape, lambda i: (0,) * a.ndim)
    return pl.pallas_call(
        functools.partial(_mla_proj_kernel, qscale=(MLA_QK ** -0.5) * LOG2E),
        grid=(T // tm,),
        in_specs=[row(D), full(g), bvec, bvec, row(HEAD_PAD), row(HEAD_PAD), full(w1), full(gq), full(gkv),
                  full(wqa), full(wqb), full(wka), full(wvt)],
        out_specs=[row(HP), row(HP),
                   pl.BlockSpec((None, MLA_HEADS * MLA_V, tm), lambda i: (i // tpb, 0, i % tpb))],
        out_shape=[jax.ShapeDtypeStruct((T, HP), BF16), jax.ShapeDtypeStruct((T, HP), BF16),
                   jax.ShapeDtypeStruct((batch, MLA_HEADS * MLA_V, seq), BF16)],
        compiler_params=_cparams(("parallel",)),
        name="mla_proj",
    )(h, g, sc, sh, cos_t, sin_t, w1, gq, gkv, wqa, wqb, wka, wvt)


def _mla_attn_kernel(q_ref, k_ref, vt_ref, o_ref, m_sc, acc_sc, s_sc, mb_sc, *, tq):
    qi = pl.program_id(2)
    tk = tq
    m_sc[...] = jnp.full(m_sc.shape, -jnp.inf, F32)
    acc_sc[...] = jnp.zeros(acc_sc.shape, F32)

    def scores(j, diagonal, slot, hh, n):
        rows = (n + 1) * STRIP if diagonal else tk
        off = pl.multiple_of(j * tk, tk)
        cols = slice(n * STRIP, (n + 1) * STRIP)
        q = q_ref[cols, HEAD_PAD * hh:HEAD_PAD * (hh + 1)]
        k = k_ref[pl.ds(off, rows), HEAD_PAD * hh:HEAD_PAD * (hh + 1)]
        s = _dot_nt(k, q)
        if diagonal:
            key = lax.broadcasted_iota(jnp.int32, (rows, STRIP), 0)
            qry = lax.broadcasted_iota(jnp.int32, (rows, STRIP), 1) + n * STRIP
            s = jnp.where(key <= qry, s, -jnp.inf)
        s_sc[slot, hh, :rows, cols] = s
        m_blk = jnp.max(jnp.max(s.reshape(rows // 8, 8, STRIP), axis=0), axis=0, keepdims=True)
        mb_sc[slot, hh, :, cols] = jnp.broadcast_to(m_blk, (8, STRIP))

    def softmax_values(j, diagonal, slot, hh, n):
        rows = (n + 1) * STRIP if diagonal else tk
        off = pl.multiple_of(j * tk, tk)
        cols = slice(n * STRIP, (n + 1) * STRIP)
        s3 = s_sc[slot, hh, :rows, cols].reshape(rows // 8, 8, STRIP)
        m_prev = m_sc[hh, :, cols]
        m_new = jnp.maximum(m_prev, mb_sc[slot, hh, :, cols])
        alpha = jnp.exp2(m_prev - m_new)
        p = jnp.exp2(s3 - m_new[None]).reshape(rows, STRIP).astype(BF16)
        vt = vt_ref[MLA_V * hh:MLA_V * (hh + 1), pl.ds(off, rows)]
        vt1 = jnp.concatenate([vt, jnp.ones((ONES_ROWS, rows), BF16)], axis=0)
        acc_sc[hh, :, cols] = alpha[:1] * acc_sc[hh, :, cols] + _dot(vt1, p)
        m_sc[hh, :, cols] = m_new

    strips = [(hh, n) for hh in range(2) for n in range(tq // STRIP)]

    def step(j, even, diagonal_next=False, has_next=True):
        cur, nxt = (0, 1) if even else (1, 0)
        for hh, n in strips:
            if has_next:
                scores(j + 1, diagonal_next, nxt, hh, n)
            softmax_values(j, not has_next, cur, hh, n)

    def first_scores(diagonal):
        for hh, n in strips:
            scores(0, diagonal, 0, hh, n)

    @pl.when(qi == 0)
    def _():
        first_scores(True)

    @pl.when(qi > 0)
    def _():
        first_scores(False)

        def pair(t, carry):
            step(2 * t, True)
            step(2 * t + 1, False)
            return carry

        lax.fori_loop(0, (qi - 1) // 2, pair, 0)

    @pl.when(qi % 2 == 1)
    def _():
        step(qi - 1, True, diagonal_next=True)
        step(qi, False, has_next=False)

    @pl.when((qi % 2 == 0) & (qi > 0))
    def _():
        step(qi - 2, True)
        step(qi - 1, False, diagonal_next=True)

    @pl.when(qi % 2 == 0)
    def _():
        step(qi, True, has_next=False)

    o_t = jnp.concatenate([acc_sc[hh, :MLA_V] / acc_sc[hh, MLA_V:MLA_V + 1] for hh in range(2)], axis=0)
    o_ref[...] = o_t.T.astype(BF16)


def _mla_attn(q, k, vt, *, batch, seq, tq):
    return pl.pallas_call(
        functools.partial(_mla_attn_kernel, tq=tq),
        grid=(batch, MLA_HEADS // 2, seq // tq),
        in_specs=[pl.BlockSpec((None, tq, 2 * HEAD_PAD), lambda b, hp, i: (b, i, hp)),
                  pl.BlockSpec((None, seq, 2 * HEAD_PAD), lambda b, hp, i: (b, 0, hp)),
                  pl.BlockSpec((None, 2 * MLA_V, seq), lambda b, hp, i: (b, hp, 0))],
        out_specs=pl.BlockSpec((None, tq, 2 * MLA_V), lambda b, hp, i: (b, i, hp)),
        out_shape=jax.ShapeDtypeStruct((batch, seq, MLA_HEADS * MLA_V), BF16),
        scratch_shapes=[pltpu.VMEM((2, 8, tq), F32), pltpu.VMEM((2, MLA_V + ONES_ROWS, tq), F32),
                        pltpu.VMEM((2, 2, tq, tq), F32),
                        pltpu.VMEM((2, 2, 8, tq), F32)],
        compiler_params=_cparams(("parallel", "parallel", "arbitrary")),
        name="mla_attn",
    )(q, k, vt)


def _swiglu_tile(x, wgu_ref, wd_ref, acc_sc, tf):
    for f in range(D_FF // tf):
        g = _dot(x, wgu_ref[:, f * tf:(f + 1) * tf])
        u = _dot(x, wgu_ref[:, D_FF + f * tf:D_FF + (f + 1) * tf])
        part = _dot((_silu(g) * u).astype(BF16), wd_ref[f * tf:(f + 1) * tf, :])
        if f == 0:
            acc_sc[...] = part
        else:
            acc_sc[...] += part


def _ffn_kernel(a_ref, wo_ref, h_ref, gtm_ref, g_ref, sc_ref, sh_ref, gt_ref, wgu_ref, wd_ref, o_ref, acc_sc,
                *, tf):
    h = h_ref[...] + gtm_ref[...] * _dot(a_ref[...], wo_ref[...])
    x = _normmod(h, g_ref[...], sc_ref[...], sh_ref[...]).astype(BF16)
    _swiglu_tile(x, wgu_ref, wd_ref, acc_sc, tf)
    o_ref[...] = h + gt_ref[...] * acc_sc[...]


def _ffn(a, w_o, h, gt_m, g, sc, sh, gt, w_gu, w_d, *, tm, tf, tiles_per_batch):
    T, D = h.shape
    bvec = pl.BlockSpec((None, 1, D), lambda i: (i // tiles_per_batch, 0, 0))
    row = pl.BlockSpec((tm, D), lambda i: (i, 0))
    const = lambda w: pl.BlockSpec(w.shape, lambda i: (0,) * w.ndim, pipeline_mode=pl.Buffered(1))
    return pl.pallas_call(
        functools.partial(_ffn_kernel, tf=tf),
        grid=(T // tm,),
        in_specs=[row, const(w_o), row, bvec, const(g), bvec, bvec, bvec, const(w_gu), const(w_d)],
        out_specs=row,
        out_shape=jax.ShapeDtypeStruct((T, D), F32),
        scratch_shapes=[pltpu.VMEM((tm, D), F32)],
        compiler_params=_cparams(("parallel",)),
        name="ffn_dense",
    )(a, w_o, h, gt_m, g, sc, sh, gt, w_gu, w_d)


def _l1_proj_kernel(h_ref, gkv_ref, sckv_ref, shkv_ref, gq_ref, scq_ref, shq_ref, wkv_ref, wq_ref, *refs):
    out_refs, stage_sc = refs[:-1], refs[-1]
    tm = h_ref.shape[0]
    x = h_ref[...]
    xh = x * _rms_scale(x)
    hn_kv = (xh * (gkv_ref[...] * (1.0 + sckv_ref[...])) + shkv_ref[...]).astype(BF16)
    hn_q = (xh * (gq_ref[...] * (1.0 + scq_ref[...])) + shq_ref[...]).astype(BF16)
    for g, (_, dil) in enumerate(DIL_PATTERNS):
        ys = (_dot(hn_q, wq_ref[:, g * DIL_W:(g + 1) * DIL_W]) * (DIL_HEAD_DIM ** -0.5),
              _dot(hn_kv, wkv_ref[:, 2 * g * DIL_W:(2 * g + 1) * DIL_W]),
              _dot(hn_kv, wkv_ref[:, (2 * g + 1) * DIL_W:(2 * g + 2) * DIL_W]))
        for y, out_ref in zip(ys, out_refs[3 * g:3 * g + 3]):
            if dil == 1:
                out_ref[...] = y.astype(BF16)
            else:
                for c in range(DIL_W // LANES):
                    stage_sc[c] = y[:, c * LANES:(c + 1) * LANES]
                for r in range(dil):
                    for c in range(DIL_W // LANES):
                        col = r * DIL_W + c * LANES
                        out_ref[:, col:col + LANES] = stage_sc[c, pl.ds(r, tm // dil, stride=dil), :].astype(BF16)


def _l1_proj(h, gkv, sckv, shkv, gq, scq, shq, wkv, wq, *, tm, tiles_per_batch):
    T, D = h.shape
    bvec = pl.BlockSpec((None, 1, D), lambda i: (i // tiles_per_batch, 0, 0))
    full = lambda a: pl.BlockSpec(a.shape, lambda i: (0,) * a.ndim)
    out_specs, out_shape = [], []
    for _, dil in DIL_PATTERNS:
        for _ in range(3):
            out_specs.append(pl.BlockSpec((tm // dil, dil * DIL_W), lambda i: (i, 0)))
            out_shape.append(jax.ShapeDtypeStruct((T // dil, dil * DIL_W), BF16))
    return pl.pallas_call(
        _l1_proj_kernel,
        grid=(T // tm,),
        in_specs=[pl.BlockSpec((tm, D), lambda i: (i, 0)), full(gkv), bvec, bvec, full(gq), bvec, bvec,
                  full(wkv), full(wq)],
        out_specs=out_specs,
        out_shape=out_shape,
        scratch_shapes=[pltpu.VMEM((DIL_W // LANES, tm, LANES), F32)],
        compiler_params=_cparams(("parallel",)),
        name="l1_proj",
    )(h, gkv, sckv, shkv, gq, scq, shq, wkv, wq)


def _dil_attn_kernel(q_ref, kc_ref, kp_ref, vc_ref, vp_ref, bias_ref, o_ref, lse_ref):
    i = pl.program_id(2)
    nsub = q_ref.shape[0] // DIL_BLOCK
    lane = lax.broadcasted_iota(jnp.int32, (1, 2 * DIL_HEAD_DIM), 1)
    lo = lane < DIL_HEAD_DIM
    col = lax.broadcasted_iota(jnp.int32, (1, 2 * DIL_BLOCK), 1)
    edge = jnp.where((col < DIL_BLOCK) & (i == 0), MASK_VALUE, 0.0).astype(F32)
    sls = [slice(2 * DIL_HEAD_DIM * hp, 2 * DIL_HEAD_DIM * (hp + 1)) for hp in range(DIL_HEADS // 2)]
    for sub in range(nsub):
        rows = slice(DIL_BLOCK * sub, DIL_BLOCK * (sub + 1))
        before = slice(DIL_BLOCK * (sub - 1), DIL_BLOCK * sub)
        scores = []
        for hp, sl in enumerate(sls):
            q2 = q_ref[rows, sl]
            zero = jnp.zeros_like(q2)
            qs = jnp.concatenate([jnp.where(lo, q2, zero), jnp.where(lo, zero, q2)], axis=0)
            k_prev = kp_ref[:, sl] if sub == 0 else kc_ref[before, sl]
            k2 = jnp.concatenate([k_prev, kc_ref[rows, sl]], axis=0)
            bias = jnp.concatenate([bias_ref[2 * hp], bias_ref[2 * hp + 1]], axis=0)
            s = _dot_nt(qs, k2) + bias
            scores.append(s + edge if sub == 0 else s)
        probs, stats = [], []
        for s in scores:
            m = jnp.max(s, axis=1, keepdims=True)
            p = jnp.exp(s - m)
            probs.append(p.astype(BF16))
            stats.append((m, jnp.sum(p, axis=1, keepdims=True)))
        for sl, p, (m, l) in zip(sls, probs, stats):
            v_prev = vp_ref[:, sl] if sub == 0 else vc_ref[before, sl]
            v2 = jnp.concatenate([v_prev, vc_ref[rows, sl]], axis=0)
            o = _dot(p, v2) / l
            lse = jnp.broadcast_to(m + jnp.log(l), o.shape)
            o_ref[rows, sl] = jnp.where(lo, o[:DIL_BLOCK], o[DIL_BLOCK:])
            lse_ref[rows, sl] = jnp.where(lo, lse[:DIL_BLOCK], lse[DIL_BLOCK:])


def _dil_attn(q, k, v, bias, *, group, dil, batch, seq):
    n = seq // dil
    nsub = min(DIL_SUBBLOCKS, n // DIL_BLOCK)
    run = nsub * DIL_BLOCK
    assert n % run == 0, (n, run)
    q, k, v = (a.reshape(batch, n, dil * DIL_W) for a in (q, k, v))
    cur = pl.BlockSpec((None, run, DIL_W), lambda b, r, i: (b, i, r))
    prev = pl.BlockSpec((None, DIL_BLOCK, DIL_W), lambda b, r, i: (b, jnp.maximum(i * nsub - 1, 0), r))
    out_sd = jax.ShapeDtypeStruct((batch, n, dil * DIL_W), F32)
    o, lse = pl.pallas_call(
        _dil_attn_kernel,
        grid=(batch, dil, n // run),
        in_specs=[cur, cur, prev, cur, prev, pl.BlockSpec(bias.shape, lambda b, r, i: (0, 0, 0))],
        out_specs=[cur, cur],
        out_shape=[out_sd, out_sd],
        compiler_params=_cparams(("parallel", "parallel", "arbitrary")),
        name=f"dil_attn_g{group}",
    )(q, k, k, v, v, bias)
    return o.reshape(batch * n, dil * DIL_W), lse.reshape(batch * n, dil * DIL_W)


def _rpb_bucket(dist):
    exact = RPB_BUCKETS // 2
    d = jnp.maximum(dist, 0)
    d_f = jnp.maximum(d, 1).astype(F32)
    large = exact + (jnp.log(d_f / exact) / math.log(RPB_MAX_DIST / exact)
                     * (RPB_BUCKETS - exact)).astype(jnp.int32)
    return jnp.where(d < exact, d, jnp.minimum(large, RPB_BUCKETS - 1))


def _dil_bias(table, span, dil):
    period = 3 * DIL_BLOCK
    j = jnp.arange(period)
    k_minus_q = jnp.where(j < 2 * DIL_BLOCK, j, j - period)
    dist = DIL_BLOCK - k_minus_q
    band = (dist >= 0) & (dist <= span)
    prof = jnp.where(band[:, None], table[_rpb_bucket(dist * dil)].astype(F32), MASK_VALUE).T
    skew = jnp.tile(prof, (1, DIL_BLOCK))[:, :DIL_BLOCK * (period - 1)]
    return skew.reshape(DIL_HEADS, DIL_BLOCK, period - 1)[:, :, :2 * DIL_BLOCK]


def _dil_out_kernel(o0_ref, o1_ref, o2_ref, l0_ref, l1_ref, l2_ref, w_ref, h_ref, gt_ref, out_ref, *stages):
    tm = h_ref.shape[0]
    stages = list(stages)

    def token_major(ref, dil):
        if dil == 1:
            return ref[...]
        stage = stages.pop()
        for r in range(dil):
            for c in range(DIL_W // LANES):
                col = r * DIL_W + c * LANES
                stage[c, pl.ds(r, tm // dil, stride=dil), :] = ref[:, col:col + LANES]
        return jnp.concatenate([stage[c] for c in range(DIL_W // LANES)], axis=1)

    dils = [dil for _, dil in DIL_PATTERNS]
    l0, l1, l2 = (token_major(ref, d) for ref, d in zip((l0_ref, l1_ref, l2_ref), dils))
    o0, o1, o2 = (token_major(ref, d) for ref, d in zip((o0_ref, o1_ref, o2_ref), dils))
    m = jnp.maximum(jnp.maximum(l0, l1), l2)
    w0, w1, w2 = jnp.exp(l0 - m), jnp.exp(l1 - m), jnp.exp(l2 - m)
    o = (o0 * w0 + o1 * w1 + o2 * w2) / (w0 + w1 + w2)
    out_ref[...] = h_ref[...] + gt_ref[...] * _dot(o.astype(BF16), w_ref[...])


def _dil_out(os_, ls_, w, h, gt, *, tm, tiles_per_batch):
    T, D = h.shape
    row = lambda n: pl.BlockSpec((tm, n), lambda i: (i, 0))
    grp = [pl.BlockSpec((tm // dil, dil * DIL_W), lambda i: (i, 0)) for _, dil in DIL_PATTERNS]
    n_stage = 2 * sum(1 for _, dil in DIL_PATTERNS if dil > 1)
    return pl.pallas_call(
        _dil_out_kernel,
        grid=(T // tm,),
        in_specs=grp + grp + [pl.BlockSpec(w.shape, lambda i: (0, 0)), row(D),
                              pl.BlockSpec((None, 1, D), lambda i: (i // tiles_per_batch, 0, 0))],
        out_specs=row(D),
        out_shape=jax.ShapeDtypeStruct((T, D), F32),
        scratch_shapes=[pltpu.VMEM((DIL_W // LANES, tm, LANES), F32)] * n_stage,
        compiler_params=_cparams(("parallel",)),
        name="dil_out",
    )(*os_, *ls_, w, h, gt)


def _route_kernel(h_ref, g_ref, sc_ref, sh_ref, wrt_ref, hn_ref, eid_ref, rank_ref, gate_ref, cnt_ref, carry_sc):
    @pl.when(pl.program_id(0) == 0)
    def _():
        carry_sc[...] = jnp.zeros(carry_sc.shape, F32)

    hn = _normmod(h_ref[...], g_ref[...], sc_ref[...], sh_ref[...])
    hn_ref[...] = hn
    tm = hn.shape[0]
    logits = lax.dot_general(wrt_ref[...], hn, (((1,), (1,)), ((), ())), preferred_element_type=F32,
                             precision=lax.Precision.HIGHEST)
    idx = lax.broadcasted_iota(jnp.int32, logits.shape, 0)
    v1 = jnp.max(logits, axis=0, keepdims=True)
    i1 = jnp.min(jnp.where(logits == v1, idx, N_EXPERTS), axis=0, keepdims=True)
    rest = jnp.where(idx == i1, -jnp.inf, logits)
    v2 = jnp.max(rest, axis=0, keepdims=True)
    i2 = jnp.min(jnp.where(rest == v2, idx, N_EXPERTS), axis=0, keepdims=True)
    e = jnp.exp(v2 - v1)
    gate_ref[...] = jnp.concatenate([1.0 / (1.0 + e), e / (1.0 + e)], axis=0)
    eid_ref[...] = jnp.concatenate([i1, i2], axis=0)

    sel = ((idx == i1) | (idx == i2)).astype(BF16)
    before = (lax.broadcasted_iota(jnp.int32, (tm, tm), 0)
              < lax.broadcasted_iota(jnp.int32, (tm, tm), 1)).astype(BF16)
    rank_all = carry_sc[:, :1] + _dot(sel, before)
    r1 = jnp.sum(jnp.where(idx == i1, rank_all, 0.0), axis=0, keepdims=True)
    r2 = jnp.sum(jnp.where(idx == i2, rank_all, 0.0), axis=0, keepdims=True)
    rank_ref[...] = jnp.concatenate([r1, r2], axis=0).astype(jnp.int32)
    carry_sc[...] += jnp.sum(sel.astype(F32), axis=1, keepdims=True)
    cnt_ref[...] = carry_sc[...].astype(jnp.int32)


def _route(h, g, sc, sh, wrt, *, tm, tiles_per_batch):
    T, D = h.shape
    bvec = pl.BlockSpec((None, 1, D), lambda i: (i // tiles_per_batch, 0, 0))
    lane_blk = pl.BlockSpec((2, tm), lambda i: (0, i))
    return pl.pallas_call(
        _route_kernel,
        grid=(T // tm,),
        in_specs=[pl.BlockSpec((tm, D), lambda i: (i, 0)), pl.BlockSpec((1, D), lambda i: (0, 0)), bvec, bvec,
                  pl.BlockSpec(wrt.shape, lambda i: (0, 0))],
        out_specs=[pl.BlockSpec((tm, D), lambda i: (i, 0)), lane_blk, lane_blk, lane_blk,
                   pl.BlockSpec((N_EXPERTS, HEAD_PAD), lambda i: (0, 0))],
        out_shape=[jax.ShapeDtypeStruct((T, D), F32), jax.ShapeDtypeStruct((2, T), jnp.int32),
                   jax.ShapeDtypeStruct((2, T), jnp.int32), jax.ShapeDtypeStruct((2, T), F32),
                   jax.ShapeDtypeStruct((N_EXPERTS, HEAD_PAD), jnp.int32)],
        scratch_shapes=[pltpu.VMEM((N_EXPERTS, HEAD_PAD), F32)],
        compiler_params=_cparams(("arbitrary",)),
        name="moe_route",
    )(h, g, sc, sh, wrt)


def _dispatch_kernel(pad_ref, dest_ref, hn_ref, xs_ref, zero_sc, sem, zsem):
    tm = hn_ref.shape[0]

    def row_copy(r, k):
        return pltpu.make_async_copy(hn_ref.at[pl.ds(r, 1), :], xs_ref.at[pl.ds(dest_ref[k, r], 1), :], sem)

    def issue(r, carry):
        row_copy(r, 0).start(priority=0)
        row_copy(r, 1).start(priority=1)
        return carry

    for r in range(tm):
        issue(r, 0)

    @pl.when(pl.program_id(0) == pl.num_programs(0) - 1)
    def _():
        zero_sc[...] = jnp.zeros(zero_sc.shape, F32)

        def pad_copy(p):
            return pltpu.make_async_copy(zero_sc.at[pl.ds(0, 1), :], xs_ref.at[pl.ds(p, 1), :], zsem)

        for e in range(N_EXPERTS):
            lo, hi = pad_ref[0, e], pad_ref[1, e]

            def zissue(p, carry):
                pad_copy(p).start()
                return carry

            def zwait(p, carry):
                pad_copy(p).wait()
                return carry

            lax.fori_loop(lo, hi, zissue, 0)
            lax.fori_loop(lo, hi, zwait, 0)

        zrows = zero_sc.shape[0]

        def tail_copy(c):
            return pltpu.make_async_copy(zero_sc, xs_ref.at[pl.ds(pl.multiple_of(c * zrows, zrows), zrows), :], zsem)

        def tissue(c, carry):
            tail_copy(c).start()
            return carry

        def twait(c, carry):
            tail_copy(c).wait()
            return carry

        lo, hi = pad_ref[1, N_EXPERTS - 1] // zrows, xs_ref.shape[0] // zrows
        lax.fori_loop(lo, hi, tissue, 0)
        lax.fori_loop(lo, hi, twait, 0)

    for _ in range(2):
        pltpu.make_async_copy(hn_ref, xs_ref.at[pl.ds(0, tm), :], sem).wait()


def _dispatch(pad_rows, dest, hn, *, n_slots, tm):
    T, D = hn.shape
    return pl.pallas_call(
        _dispatch_kernel,
        grid_spec=pltpu.PrefetchScalarGridSpec(
            num_scalar_prefetch=1,
            grid=(T // tm,),
            in_specs=[pl.BlockSpec((2, tm), lambda i, pad: (0, i), memory_space=pltpu.SMEM),
                      pl.BlockSpec((tm, D), lambda i, pad: (i, 0))],
            out_specs=pl.BlockSpec(memory_space=pl.ANY),
            scratch_shapes=[pltpu.VMEM((64, D), F32), pltpu.SemaphoreType.DMA, pltpu.SemaphoreType.DMA]),
        out_shape=jax.ShapeDtypeStruct((n_slots, D), F32),
        compiler_params=_cparams(("arbitrary",)),
        name="moe_dispatch",
    )(pad_rows, dest, hn)


def _experts_kernel(te_ref, na_ref, xs_ref, wgu_ref, wd_ref, y_ref, acc_sc, *, tf):
    @pl.when(pl.program_id(0) < na_ref[0])
    def _():
        _swiglu_tile(xs_ref[...].astype(BF16), wgu_ref, wd_ref, acc_sc, tf)
        y_ref[...] = acc_sc[...]

    @pl.when(pl.program_id(0) >= na_ref[0])
    def _():
        y_ref[...] = jnp.zeros(y_ref.shape, F32)


def _experts(tile_expert, n_active, xs, w_gu, w_d, *, tm, tf):
    P, D = xs.shape

    def tile(j, te, na):
        return jnp.minimum(j, na[0] - 1)

    return pl.pallas_call(
        functools.partial(_experts_kernel, tf=tf),
        grid_spec=pltpu.PrefetchScalarGridSpec(
            num_scalar_prefetch=2,
            grid=(P // tm,),
            in_specs=[pl.BlockSpec((tm, D), lambda j, te, na: (tile(j, te, na), 0)),
                      pl.BlockSpec((None, D, 2 * D_FF), lambda j, te, na: (te[tile(j, te, na)], 0, 0)),
                      pl.BlockSpec((None, D_FF, D), lambda j, te, na: (te[tile(j, te, na)], 0, 0))],
            out_specs=pl.BlockSpec((tm, D), lambda j, te, na: (j, 0)),
            scratch_shapes=[pltpu.VMEM((tm, D), F32)]),
        out_shape=jax.ShapeDtypeStruct((P, D), F32),
        compiler_params=_cparams(("arbitrary",)),
        name="moe_experts",
    )(tile_expert, n_active, xs, w_gu, w_d)


def _combine_kernel(dest_ref, y_ref, gates_ref, h_ref, gt_ref, gfin_ref, o_ref, ybuf, sem):
    tm = h_ref.shape[0]

    def row_copy(r, k):
        return pltpu.make_async_copy(y_ref.at[pl.ds(dest_ref[k, r], 1), :], ybuf.at[k, pl.ds(r, 1), :], sem)

    def issue(r, carry):
        row_copy(r, 0).start(priority=0)
        row_copy(r, 1).start(priority=1)
        return carry

    for r in range(tm):
        issue(r, 0)
    for k in range(2):
        pltpu.make_async_copy(y_ref.at[pl.ds(0, tm), :], ybuf.at[k], sem).wait()
    gates = gates_ref[...]
    moe = gates[:, 0:1] * ybuf[0] + gates[:, 1:2] * ybuf[1]
    y = h_ref[...] + gt_ref[...] * moe
    o_ref[...] = y * _rms_scale(y) * gfin_ref[...]


def _combine(dest, y, gates, h, gt, gfin, *, tm, tiles_per_batch):
    T, D = h.shape
    return pl.pallas_call(
        _combine_kernel,
        grid=(T // tm,),
        in_specs=[pl.BlockSpec((2, tm), lambda i: (0, i), memory_space=pltpu.SMEM),
                  pl.BlockSpec(memory_space=pl.ANY),
                  pl.BlockSpec((tm, 2), lambda i: (i, 0)),
                  pl.BlockSpec((tm, D), lambda i: (i, 0)),
                  pl.BlockSpec((None, 1, D), lambda i: (i // tiles_per_batch, 0, 0)),
                  pl.BlockSpec((1, D), lambda i: (0, 0))],
        out_specs=pl.BlockSpec((tm, D), lambda i: (i, 0)),
        out_shape=jax.ShapeDtypeStruct((T, D), F32),
        scratch_shapes=[pltpu.VMEM((2, tm, D), F32), pltpu.SemaphoreType.DMA],
        compiler_params=_cparams(("arbitrary",)),
        name="moe_combine",
    )(dest, y, gates, h, gt, gfin)


def _moe_plan(eids, ranks, counts, *, tm, n_tiles):
    padded = (counts + tm - 1) // tm * tm
    ends = jnp.cumsum(padded)
    starts = ends - padded
    dest = ranks
    for e in range(N_EXPERTS):
        dest = dest + jnp.where(eids == e, starts[e], 0)
    tile_start = jnp.arange(n_tiles, dtype=jnp.int32) * tm
    tile_expert = jnp.minimum(jnp.sum(tile_start[:, None] >= ends[None, :], axis=1), N_EXPERTS - 1)
    n_active = (ends[-1] // tm).reshape(1)
    pad_rows = jnp.stack([starts + counts, ends])
    return dest.astype(jnp.int32), tile_expert.astype(jnp.int32), n_active.astype(jnp.int32), pad_rows.astype(jnp.int32)


def _mla_weights(w_in, w_q_up, w_kv_up):
    D = w_in.shape[0]
    half = MLA_ROPE // 2
    w_ql = w_in[:, :MLA_Q_RANK]
    w_kvl = w_in[:, MLA_Q_RANK:MLA_Q_RANK + MLA_KV_RANK]
    w_kr = w_in[:, MLA_Q_RANK + MLA_KV_RANK:]
    z64 = jnp.zeros((D, MLA_NOPE), F32)
    z32 = jnp.zeros((D, HEAD_PAD - MLA_QK), F32)
    kr_pad = jnp.concatenate([z64, w_kr, z32], axis=1)
    kr_swap = jnp.concatenate([z64, w_kr[:, half:], w_kr[:, :half], z32], axis=1)
    w1 = jnp.concatenate([w_ql, w_kvl, kr_pad, kr_swap], axis=1).astype(BF16)

    wq = w_q_up.reshape(MLA_Q_RANK, MLA_HEADS, MLA_QK)
    nope, rope = wq[..., :MLA_NOPE], wq[..., MLA_NOPE:]
    zq = jnp.zeros((MLA_Q_RANK, MLA_HEADS, HEAD_PAD - MLA_QK), F32)
    wqa = jnp.concatenate([nope, rope, zq], axis=-1).reshape(MLA_Q_RANK, MLA_HEADS * HEAD_PAD).astype(BF16)
    wqb = jnp.concatenate([jnp.zeros_like(nope), rope[..., half:], rope[..., :half], zq], axis=-1)
    wqb = wqb.reshape(MLA_Q_RANK, MLA_HEADS * HEAD_PAD).astype(BF16)

    wkv = w_kv_up.reshape(MLA_KV_RANK, MLA_HEADS, MLA_NOPE + MLA_V)
    k_nope, v = wkv[..., :MLA_NOPE], wkv[..., MLA_NOPE:]
    zk = jnp.zeros((MLA_KV_RANK, MLA_HEADS, HEAD_PAD - MLA_NOPE), F32)
    wka = jnp.concatenate([k_nope, zk], axis=-1).reshape(MLA_KV_RANK, MLA_HEADS * HEAD_PAD).astype(BF16)
    wvt = v.reshape(MLA_KV_RANK, MLA_HEADS * MLA_V).T.astype(BF16)
    return w1, wqa, wqb, wka, wvt


def _rope_tables(positions):
    inv_freq = ROPE_THETA ** (-jnp.arange(0, MLA_ROPE, 2, dtype=F32) / MLA_ROPE)
    ang = positions.astype(F32).reshape(-1, 1) * inv_freq
    cos, sin = jnp.cos(ang), jnp.sin(ang)
    T = ang.shape[0]
    one = jnp.ones((T, MLA_NOPE), F32)
    z64 = jnp.zeros((T, MLA_NOPE), F32)
    z32 = jnp.zeros((T, HEAD_PAD - MLA_QK), F32)
    return (jnp.concatenate([one, cos, cos, z32], axis=1),
            jnp.concatenate([z64, -sin, sin, z32], axis=1))


def kernel(x, c, positions, g_mix, g_ffn, w_ada, b_ada, w_mla_in, g_mla_q, w_mla_q_up, g_mla_kv, w_mla_kv_up,
           w_mla_out, g_kv_b, w_ada_kv, b_ada_kv, w_kv_b, rpb_table, w_q_b, w_o_b, w_ffn_gu, w_ffn_down,
           w_router, w_exp_gu, w_exp_down, g_final):
    B, S, D = x.shape
    T = B * S
    h = x.reshape(T, D)

    c8 = jnp.zeros((8, D), F32).at[:B].set(c)
    mod = _ada(c8, w_ada, b_ada[:, None, :])[:, :B]
    mod = mod.reshape(2, B, N_MOD, 1, D)
    mod_kv = _ada(c8, w_ada_kv[None], b_ada_kv[None, None, :])[0, :B].reshape(B, 2, 1, D)
    sh_kv, sc_kv = mod_kv[:, 0], mod_kv[:, 1]

    def mods(layer):
        return [mod[layer, :, k] for k in range(N_MOD)]

    row = lambda v: v.reshape(1, -1)

    sh_m, sc_m, gt_m, sh_f, sc_f, gt_f = mods(0)
    w1, wqa, wqb, wka, wvt = _mla_weights(w_mla_in[0], w_mla_q_up[0], w_mla_kv_up[0])
    cos_t, sin_t = _rope_tables(positions)
    q, k, vt = _mla_proj(h, row(g_mix[0]), sc_m, sh_m, cos_t, sin_t, w1, row(g_mla_q[0]), row(g_mla_kv[0]),
                         wqa, wqb, wka, wvt, tm=TM_MLA_PROJ, batch=B, seq=S)
    o = _mla_attn(q.reshape(B, S, -1), k.reshape(B, S, -1), vt, batch=B, seq=S, tq=MLA_TQ)

    h = _ffn(o.reshape(T, -1), w_mla_out[0].astype(BF16), h, gt_m, row(g_ffn[0]), sc_f, sh_f, gt_f,
             w_ffn_gu[0].astype(BF16), w_ffn_down[0].astype(BF16), tm=TM_FFN, tf=FF_CHUNK,
             tiles_per_batch=S // TM_FFN)

    sh_m, sc_m, gt_m, sh_f, sc_f, gt_f = mods(1)
    qkv = _l1_proj(h, row(g_kv_b), sc_kv, sh_kv, row(g_mix[1]), sc_m, sh_m,
                   w_kv_b.astype(BF16), w_q_b[0].astype(BF16), tm=TM_L1_PROJ, tiles_per_batch=S // TM_L1_PROJ)

    outs, lses = [], []
    for g, (window, dil) in enumerate(DIL_PATTERNS):
        bias = _dil_bias(rpb_table[:, g * DIL_HEADS:(g + 1) * DIL_HEADS], window // dil, dil)
        o_g, lse_g = _dil_attn(*qkv[3 * g:3 * g + 3], bias, group=g, dil=dil, batch=B, seq=S)
        outs.append(o_g)
        lses.append(lse_g)
    h = _dil_out(outs, lses, w_o_b[0].astype(BF16), h, gt_m, tm=TM_DIL_OUT, tiles_per_batch=S // TM_DIL_OUT)

    hn, eids, ranks, gates, counts = _route(h, row(g_ffn[1]), sc_f, sh_f, w_router[0].T, tm=TM_ROUTE,
                                            tiles_per_batch=S // TM_ROUTE)
    n_tiles = (T * 2) // MOE_TM + N_EXPERTS
    dest, tile_expert, n_active, pad_rows = _moe_plan(eids, ranks, counts[:, 0], tm=MOE_TM, n_tiles=n_tiles)
    xs = _dispatch(pad_rows, dest, hn, n_slots=n_tiles * MOE_TM, tm=TM_DISPATCH)
    y = _experts(tile_expert, n_active, xs, w_exp_gu[0].astype(BF16), w_exp_down[0].astype(BF16),
                 tm=MOE_TM, tf=FF_CHUNK)
    out = _combine(dest, y, gates.T, h, gt_f, row(g_final), tm=TM_COMBINE, tiles_per_batch=S // TM_COMBINE)
    return out.reshape(B, S, D)
```

```python
import functools
import math

import numpy as np
import jax
import jax.numpy as jnp
from jax import lax
from jax.experimental import pallas as pl
from jax.experimental.pallas import tpu as pltpu

D_MODEL = 1024
N_MOD = 6
EPS = 1e-6

MLA_HEADS = 16
MLA_Q_RANK = 384
MLA_KV_RANK = 256
MLA_NOPE = 64
MLA_ROPE = 32
MLA_V = 64
MLA_QK = MLA_NOPE + MLA_ROPE
ROPE_THETA = 10000.0
LANES = 128
ONES_ROWS = 16
STRIP = 256
HEAD_PAD = 128

DIL_PATTERNS = ((128, 1), (512, 4), (2048, 16))
DIL_GROUPS = len(DIL_PATTERNS)
DIL_HEADS = 8
DIL_HEAD_DIM = 64
DIL_BLOCK = 128
DIL_SUBBLOCKS = 8
DIL_W = DIL_HEADS * DIL_HEAD_DIM
RPB_BUCKETS = 32
RPB_MAX_DIST = 2048

D_FF = 2816
N_EXPERTS = 8
MOE_TM = 512

TM_MLA_PROJ = 1024
TM_FFN = 512
TM_L1_PROJ = 1024
TM_DIL_OUT = 1024
TM_ROUTE = 1024
TM_DISPATCH = 1024
TM_COMBINE = 512
COMBINE_PARTS = 4
MLA_TQ = 1024
FF_CHUNK = 256

MASK_VALUE = -1e30
LOG2E = math.log2(math.e)

F32 = jnp.float32
BF16 = jnp.bfloat16

VMEM_LIMIT = 56 * 1024 * 1024


def _cparams(sem):
    return pltpu.CompilerParams(dimension_semantics=sem, vmem_limit_bytes=VMEM_LIMIT)


def _dot(a, b):
    return jnp.dot(a, b, preferred_element_type=F32)


def _dot_nt(a, b):
    return lax.dot_general(a, b, (((1,), (1,)), ((), ())), preferred_element_type=F32)


def _rms_scale(x):
    return lax.rsqrt(jnp.mean(x * x, axis=-1, keepdims=True) + EPS)


def _normmod(x, g, sc, sh):
    return (x * _rms_scale(x)) * (g * (1.0 + sc)) + sh


def _silu(x):
    return x * (1.0 / (1.0 + jnp.exp(-x)))


def _ada_kernel(c_ref, w_ref, b_ref, o_ref):
    c = c_ref[...]
    o_ref[...] = jnp.dot(_silu(c), w_ref[...], preferred_element_type=F32,
                         precision=lax.Precision.HIGHEST) + b_ref[...]


def _ada(c8, w, b, tn=512):
    L, D, N = w.shape
    return pl.pallas_call(
        _ada_kernel,
        grid=(L, N // tn),
        in_specs=[pl.BlockSpec((8, D), lambda l, j: (0, 0)),
                  pl.BlockSpec((None, D, tn), lambda l, j: (l, 0, j)),
                  pl.BlockSpec((None, 1, tn), lambda l, j: (l, 0, j))],
        out_specs=pl.BlockSpec((None, 8, tn), lambda l, j: (l, 0, j)),
        out_shape=jax.ShapeDtypeStruct((L, 8, N), F32),
        compiler_params=_cparams(("parallel", "parallel")),
        name="ada_mod",
    )(c8, w, b)


def _mla_proj_kernel(h_ref, g_ref, sc_ref, sh_ref, cos_ref, sin_ref, w1_ref, gq_ref, gkv_ref,
                     wqa_ref, wqb_ref, wka_ref, wvt_ref, q_ref, k_ref, vt_ref, *, qscale):
    hn = _normmod(h_ref[...], g_ref[...], sc_ref[...], sh_ref[...]).astype(BF16)
    z = _dot(hn, w1_ref[...])
    ql = z[:, :MLA_Q_RANK]
    kvl = z[:, MLA_Q_RANK:MLA_Q_RANK + MLA_KV_RANK]
    kr = z[:, MLA_Q_RANK + MLA_KV_RANK:MLA_Q_RANK + MLA_KV_RANK + HEAD_PAD]
    krs = z[:, MLA_Q_RANK + MLA_KV_RANK + HEAD_PAD:]
    qn = (ql * _rms_scale(ql) * gq_ref[...]).astype(BF16)
    kvn = (kvl * _rms_scale(kvl) * gkv_ref[...]).astype(BF16)
    cos = cos_ref[...]
    sin = sin_ref[...]
    k_rope = kr * cos + krs * sin
    cos2 = jnp.concatenate([cos, cos], axis=1)
    sin2 = jnp.concatenate([sin, sin], axis=1)
    k_rope2 = jnp.concatenate([k_rope, k_rope], axis=1)
    for hp in range(MLA_HEADS // 2):
        sl = slice(2 * HEAD_PAD * hp, 2 * HEAD_PAD * (hp + 1))
        a = _dot(qn, wqa_ref[:, sl])
        b = _dot(qn, wqb_ref[:, sl])
        q_ref[:, sl] = ((a * cos2 + b * sin2) * qscale).astype(BF16)
        k_ref[:, sl] = (_dot(kvn, wka_ref[:, sl]) + k_rope2).astype(BF16)
    vt_ref[...] = _dot_nt(wvt_ref[...], kvn).astype(BF16)


def _mla_proj(h, g, sc, sh, cos_t, sin_t, w1, gq, gkv, wqa, wqb, wka, wvt, *, tm, batch, seq):
    T, D = h.shape
    HP = MLA_HEADS * HEAD_PAD
    tpb = seq // tm
    row = lambda n: pl.BlockSpec((tm, n), lambda i: (i, 0))
    bvec = pl.BlockSpec((None, 1, D), lambda i: (i // tpb, 0, 0))
    full = lambda a: pl.BlockSpec(a.shape, lambda i: (0,) * a.ndim)
    return pl.pallas_call(
        functools.partial(_mla_proj_kernel, qscale=(MLA_QK ** -0.5) * LOG2E),
        grid=(T // tm,),
        in_specs=[row(D), full(g), bvec, bvec, row(HEAD_PAD), row(HEAD_PAD), full(w1), full(gq), full(gkv),
                  full(wqa), full(wqb), full(wka), full(wvt)],
        out_specs=[row(HP), row(HP),
                   pl.BlockSpec((None, MLA_HEADS * MLA_V, tm), lambda i: (i // tpb, 0, i % tpb))],
        out_shape=[jax.ShapeDtypeStruct((T, HP), BF16), jax.ShapeDtypeStruct((T, HP), BF16),
                   jax.ShapeDtypeStruct((batch, MLA_HEADS * MLA_V, seq), BF16)],
        compiler_params=_cparams(("parallel",)),
        name="mla_proj",
    )(h, g, sc, sh, cos_t, sin_t, w1, gq, gkv, wqa, wqb, wka, wvt)


def _mla_attn_kernel(q_ref, k_ref, vt_ref, o_ref, m_sc, acc_sc, s_sc, mb_sc, *, tq):
    qi = pl.program_id(2)
    tk = tq
    m_sc[...] = jnp.full(m_sc.shape, -jnp.inf, F32)
    acc_sc[...] = jnp.zeros(acc_sc.shape, F32)

    def scores(j, diagonal, slot, hh, n):
        rows = (n + 1) * STRIP if diagonal else tk
        off = pl.multiple_of(j * tk, tk)
        cols = slice(n * STRIP, (n + 1) * STRIP)
        q = q_ref[cols, HEAD_PAD * hh:HEAD_PAD * (hh + 1)]
        k = k_ref[pl.ds(off, rows), HEAD_PAD * hh:HEAD_PAD * (hh + 1)]
        s = _dot_nt(k, q)
        if diagonal:
            key = lax.broadcasted_iota(jnp.int32, (rows, STRIP), 0)
            qry = lax.broadcasted_iota(jnp.int32, (rows, STRIP), 1) + n * STRIP
            s = jnp.where(key <= qry, s, -jnp.inf)
        s_sc[slot, hh, :rows, cols] = s
        m_blk = jnp.max(jnp.max(s.reshape(rows // 8, 8, STRIP), axis=0), axis=0, keepdims=True)
        mb_sc[slot, hh, :, cols] = jnp.broadcast_to(m_blk, (8, STRIP))

    def softmax_values(j, diagonal, slot, hh, n):
        rows = (n + 1) * STRIP if diagonal else tk
        off = pl.multiple_of(j * tk, tk)
        cols = slice(n * STRIP, (n + 1) * STRIP)
        s3 = s_sc[slot, hh, :rows, cols].reshape(rows // 8, 8, STRIP)
        m_prev = m_sc[hh, :, cols]
        m_new = jnp.maximum(m_prev, mb_sc[slot, hh, :, cols])
        alpha = jnp.exp2(m_prev - m_new)
        p = jnp.exp2(s3 - m_new[None]).reshape(rows, STRIP).astype(BF16)
        vt = vt_ref[MLA_V * hh:MLA_V * (hh + 1), pl.ds(off, rows)]
        vt1 = jnp.concatenate([vt, jnp.ones((ONES_ROWS, rows), BF16)], axis=0)
        acc_sc[hh, :, cols] = alpha[:1] * acc_sc[hh, :, cols] + _dot(vt1, p)
        m_sc[hh, :, cols] = m_new

    strips = [(hh, n) for hh in range(2) for n in range(tq // STRIP)]

    def step(j, even, diagonal_next=False, has_next=True):
        cur, nxt = (0, 1) if even else (1, 0)
        for hh, n in strips:
            if has_next:
                scores(j + 1, diagonal_next, nxt, hh, n)
            softmax_values(j, not has_next, cur, hh, n)

    def first_scores(diagonal):
        for hh, n in strips:
            scores(0, diagonal, 0, hh, n)

    @pl.when(qi == 0)
    def _():
        first_scores(True)

    @pl.when(qi > 0)
    def _():
        first_scores(False)

        def pair(t, carry):
            step(2 * t, True)
            step(2 * t + 1, False)
            return carry

        lax.fori_loop(0, (qi - 1) // 2, pair, 0)

    @pl.when(qi % 2 == 1)
    def _():
        step(qi - 1, True, diagonal_next=True)
        step(qi, False, has_next=False)

    @pl.when((qi % 2 == 0) & (qi > 0))
    def _():
        step(qi - 2, True)
        step(qi - 1, False, diagonal_next=True)

    @pl.when(qi % 2 == 0)
    def _():
        step(qi, True, has_next=False)

    o_t = jnp.concatenate([acc_sc[hh, :MLA_V] / acc_sc[hh, MLA_V:MLA_V + 1] for hh in range(2)], axis=0)
    o_ref[...] = o_t.T.astype(BF16)


def _mla_attn(q, k, vt, *, batch, seq, tq):
    return pl.pallas_call(
        functools.partial(_mla_attn_kernel, tq=tq),
        grid=(batch, MLA_HEADS // 2, seq // tq),
        in_specs=[pl.BlockSpec((None, tq, 2 * HEAD_PAD), lambda b, hp, i: (b, i, hp)),
                  pl.BlockSpec((None, seq, 2 * HEAD_PAD), lambda b, hp, i: (b, 0, hp)),
                  pl.BlockSpec((None, 2 * MLA_V, seq), lambda b, hp, i: (b, hp, 0))],
        out_specs=pl.BlockSpec((None, tq, 2 * MLA_V), lambda b, hp, i: (b, i, hp)),
        out_shape=jax.ShapeDtypeStruct((batch, seq, MLA_HEADS * MLA_V), BF16),
        scratch_shapes=[pltpu.VMEM((2, 8, tq), F32), pltpu.VMEM((2, MLA_V + ONES_ROWS, tq), F32),
                        pltpu.VMEM((2, 2, tq, tq), F32),
                        pltpu.VMEM((2, 2, 8, tq), F32)],
        compiler_params=_cparams(("parallel", "parallel", "arbitrary")),
        name="mla_attn",
    )(q, k, vt)


def _swiglu_tile(x, wgu_ref, wd_ref, acc_sc, tf):
    for f in range(D_FF // tf):
        g = _dot(x, wgu_ref[:, f * tf:(f + 1) * tf])
        u = _dot(x, wgu_ref[:, D_FF + f * tf:D_FF + (f + 1) * tf])
        part = _dot((_silu(g) * u).astype(BF16), wd_ref[f * tf:(f + 1) * tf, :])
        if f == 0:
            acc_sc[...] = part
        else:
            acc_sc[...] += part


def _ffn_kernel(a_ref, wo_ref, h_ref, gtm_ref, g_ref, sc_ref, sh_ref, gt_ref, wgu_ref, wd_ref, o_ref, acc_sc,
                *, tf):
    h = h_ref[...] + gtm_ref[...] * _dot(a_ref[...], wo_ref[...])
    x = _normmod(h, g_ref[...], sc_ref[...], sh_ref[...]).astype(BF16)
    _swiglu_tile(x, wgu_ref, wd_ref, acc_sc, tf)
    o_ref[...] = h + gt_ref[...] * acc_sc[...]


def _ffn(a, w_o, h, gt_m, g, sc, sh, gt, w_gu, w_d, *, tm, tf, tiles_per_batch):
    T, D = h.shape
    bvec = pl.BlockSpec((None, 1, D), lambda i: (i // tiles_per_batch, 0, 0))
    row = pl.BlockSpec((tm, D), lambda i: (i, 0))
    const = lambda w: pl.BlockSpec(w.shape, lambda i: (0,) * w.ndim, pipeline_mode=pl.Buffered(1))
    return pl.pallas_call(
        functools.partial(_ffn_kernel, tf=tf),
        grid=(T // tm,),
        in_specs=[row, const(w_o), row, bvec, const(g), bvec, bvec, bvec, const(w_gu), const(w_d)],
        out_specs=row,
        out_shape=jax.ShapeDtypeStruct((T, D), F32),
        scratch_shapes=[pltpu.VMEM((tm, D), F32)],
        compiler_params=_cparams(("parallel",)),
        name="ffn_dense",
    )(a, w_o, h, gt_m, g, sc, sh, gt, w_gu, w_d)


def _l1_proj_kernel(h_ref, gkv_ref, sckv_ref, shkv_ref, gq_ref, scq_ref, shq_ref, wkv_ref, wq_ref, *refs):
    out_refs, stage_sc = refs[:-1], refs[-1]
    tm = h_ref.shape[0]
    x = h_ref[...]
    xh = x * _rms_scale(x)
    hn_kv = (xh * (gkv_ref[...] * (1.0 + sckv_ref[...])) + shkv_ref[...]).astype(BF16)
    hn_q = (xh * (gq_ref[...] * (1.0 + scq_ref[...])) + shq_ref[...]).astype(BF16)
    for g, (_, dil) in enumerate(DIL_PATTERNS):
        ys = (_dot(hn_q, wq_ref[:, g * DIL_W:(g + 1) * DIL_W]) * (DIL_HEAD_DIM ** -0.5),
              _dot(hn_kv, wkv_ref[:, 2 * g * DIL_W:(2 * g + 1) * DIL_W]),
              _dot(hn_kv, wkv_ref[:, (2 * g + 1) * DIL_W:(2 * g + 2) * DIL_W]))
        for y, out_ref in zip(ys, out_refs[3 * g:3 * g + 3]):
            if dil == 1:
                out_ref[...] = y.astype(BF16)
            else:
                for c in range(DIL_W // LANES):
                    stage_sc[c] = y[:, c * LANES:(c + 1) * LANES]
                for r in range(dil):
                    for c in range(DIL_W // LANES):
                        col = r * DIL_W + c * LANES
                        out_ref[:, col:col + LANES] = stage_sc[c, pl.ds(r, tm // dil, stride=dil), :].astype(BF16)


def _l1_proj(h, gkv, sckv, shkv, gq, scq, shq, wkv, wq, *, tm, tiles_per_batch):
    T, D = h.shape
    bvec = pl.BlockSpec((None, 1, D), lambda i: (i // tiles_per_batch, 0, 0))
    full = lambda a: pl.BlockSpec(a.shape, lambda i: (0,) * a.ndim)
    out_specs, out_shape = [], []
    for _, dil in DIL_PATTERNS:
        for _ in range(3):
            out_specs.append(pl.BlockSpec((tm // dil, dil * DIL_W), lambda i: (i, 0)))
            out_shape.append(jax.ShapeDtypeStruct((T // dil, dil * DIL_W), BF16))
    return pl.pallas_call(
        _l1_proj_kernel,
        grid=(T // tm,),
        in_specs=[pl.BlockSpec((tm, D), lambda i: (i, 0)), full(gkv), bvec, bvec, full(gq), bvec, bvec,
                  full(wkv), full(wq)],
        out_specs=out_specs,
        out_shape=out_shape,
        scratch_shapes=[pltpu.VMEM((DIL_W // LANES, tm, LANES), F32)],
        compiler_params=_cparams(("parallel",)),
        name="l1_proj",
    )(h, gkv, sckv, shkv, gq, scq, shq, wkv, wq)


def _dil_attn_kernel(q_ref, kc_ref, kp_ref, vc_ref, vp_ref, bias_ref, o_ref, lse_ref):
    i = pl.program_id(2)
    nsub = q_ref.shape[0] // DIL_BLOCK
    lane = lax.broadcasted_iota(jnp.int32, (1, 2 * DIL_HEAD_DIM), 1)
    lo = lane < DIL_HEAD_DIM
    col = lax.broadcasted_iota(jnp.int32, (1, 2 * DIL_BLOCK), 1)
    edge = jnp.where((col < DIL_BLOCK) & (i == 0), MASK_VALUE, 0.0).astype(F32)
    sls = [slice(2 * DIL_HEAD_DIM * hp, 2 * DIL_HEAD_DIM * (hp + 1)) for hp in range(DIL_HEADS // 2)]
    for sub in range(nsub):
        rows = slice(DIL_BLOCK * sub, DIL_BLOCK * (sub + 1))
        before = slice(DIL_BLOCK * (sub - 1), DIL_BLOCK * sub)
        scores = []
        for hp, sl in enumerate(sls):
            q2 = q_ref[rows, sl]
            zero = jnp.zeros_like(q2)
            qs = jnp.concatenate([jnp.where(lo, q2, zero), jnp.where(lo, zero, q2)], axis=0)
            k_prev = kp_ref[:, sl] if sub == 0 else kc_ref[before, sl]
            k2 = jnp.concatenate([k_prev, kc_ref[rows, sl]], axis=0)
            bias = jnp.concatenate([bias_ref[2 * hp], bias_ref[2 * hp + 1]], axis=0)
            s = _dot_nt(qs, k2) + bias
            scores.append(s + edge if sub == 0 else s)
        probs, stats = [], []
        for s in scores:
            m = jnp.max(s, axis=1, keepdims=True)
            p = jnp.exp(s - m)
            probs.append(p.astype(BF16))
            stats.append((m, jnp.sum(p, axis=1, keepdims=True)))
        for sl, p, (m, l) in zip(sls, probs, stats):
            v_prev = vp_ref[:, sl] if sub == 0 else vc_ref[before, sl]
            v2 = jnp.concatenate([v_prev, vc_ref[rows, sl]], axis=0)
            o = _dot(p, v2) / l
            lse = jnp.broadcast_to(m + jnp.log(l), o.shape)
            o_ref[rows, sl] = jnp.where(lo, o[:DIL_BLOCK], o[DIL_BLOCK:])
            lse_ref[rows, sl] = jnp.where(lo, lse[:DIL_BLOCK], lse[DIL_BLOCK:])


def _dil_attn(q, k, v, bias, *, group, dil, batch, seq):
    n = seq // dil
    nsub = min(DIL_SUBBLOCKS, n // DIL_BLOCK)
    run = nsub * DIL_BLOCK
    assert n % run == 0, (n, run)
    q, k, v = (a.reshape(batch, n, dil * DIL_W) for a in (q, k, v))
    cur = pl.BlockSpec((None, run, DIL_W), lambda b, r, i: (b, i, r))
    prev = pl.BlockSpec((None, DIL_BLOCK, DIL_W), lambda b, r, i: (b, jnp.maximum(i * nsub - 1, 0), r))
    out_sd = jax.ShapeDtypeStruct((batch, n, dil * DIL_W), F32)
    o, lse = pl.pallas_call(
        _dil_attn_kernel,
        grid=(batch, dil, n // run),
        in_specs=[cur, cur, prev, cur, prev, pl.BlockSpec(bias.shape, lambda b, r, i: (0, 0, 0))],
        out_specs=[cur, cur],
        out_shape=[out_sd, out_sd],
        compiler_params=_cparams(("parallel", "parallel", "arbitrary")),
        name=f"dil_attn_g{group}",
    )(q, k, k, v, v, bias)
    return o.reshape(batch * n, dil * DIL_W), lse.reshape(batch * n, dil * DIL_W)


def _rpb_bucket(dist):
    exact = RPB_BUCKETS // 2
    d = jnp.maximum(dist, 0)
    d_f = jnp.maximum(d, 1).astype(F32)
    large = exact + (jnp.log(d_f / exact) / math.log(RPB_MAX_DIST / exact)
                     * (RPB_BUCKETS - exact)).astype(jnp.int32)
    return jnp.where(d < exact, d, jnp.minimum(large, RPB_BUCKETS - 1))


def _dil_bias(table, span, dil):
    period = 3 * DIL_BLOCK
    j = jnp.arange(period)
    k_minus_q = jnp.where(j < 2 * DIL_BLOCK, j, j - period)
    dist = DIL_BLOCK - k_minus_q
    band = (dist >= 0) & (dist <= span)
    prof = jnp.where(band[:, None], table[_rpb_bucket(dist * dil)].astype(F32), MASK_VALUE).T
    skew = jnp.tile(prof, (1, DIL_BLOCK))[:, :DIL_BLOCK * (period - 1)]
    return skew.reshape(DIL_HEADS, DIL_BLOCK, period - 1)[:, :, :2 * DIL_BLOCK]


def _dil_out_kernel(o0_ref, o1_ref, o2_ref, l0_ref, l1_ref, l2_ref, w_ref, h_ref, gt_ref, out_ref, *stages):
    tm = h_ref.shape[0]
    stages = list(stages)

    def token_major(ref, dil):
        if dil == 1:
            return ref[...]
        stage = stages.pop()
        for r in range(dil):
            for c in range(DIL_W // LANES):
                col = r * DIL_W + c * LANES
                stage[c, pl.ds(r, tm // dil, stride=dil), :] = ref[:, col:col + LANES]
        return jnp.concatenate([stage[c] for c in range(DIL_W // LANES)], axis=1)

    dils = [dil for _, dil in DIL_PATTERNS]
    l0, l1, l2 = (token_major(ref, d) for ref, d in zip((l0_ref, l1_ref, l2_ref), dils))
    o0, o1, o2 = (token_major(ref, d) for ref, d in zip((o0_ref, o1_ref, o2_ref), dils))
    m = jnp.maximum(jnp.maximum(l0, l1), l2)
    w0, w1, w2 = jnp.exp(l0 - m), jnp.exp(l1 - m), jnp.exp(l2 - m)
    o = (o0 * w0 + o1 * w1 + o2 * w2) / (w0 + w1 + w2)
    out_ref[...] = h_ref[...] + gt_ref[...] * _dot(o.astype(BF16), w_ref[...])


def _dil_out(os_, ls_, w, h, gt, *, tm, tiles_per_batch):
    T, D = h.shape
    row = lambda n: pl.BlockSpec((tm, n), lambda i: (i, 0))
    grp = [pl.BlockSpec((tm // dil, dil * DIL_W), lambda i: (i, 0)) for _, dil in DIL_PATTERNS]
    n_stage = 2 * sum(1 for _, dil in DIL_PATTERNS if dil > 1)
    return pl.pallas_call(
        _dil_out_kernel,
        grid=(T // tm,),
        in_specs=grp + grp + [pl.BlockSpec(w.shape, lambda i: (0, 0)), row(D),
                              pl.BlockSpec((None, 1, D), lambda i: (i // tiles_per_batch, 0, 0))],
        out_specs=row(D),
        out_shape=jax.ShapeDtypeStruct((T, D), F32),
        scratch_shapes=[pltpu.VMEM((DIL_W // LANES, tm, LANES), F32)] * n_stage,
        compiler_params=_cparams(("parallel",)),
        name="dil_out",
    )(*os_, *ls_, w, h, gt)


def _route_kernel(h_ref, g_ref, sc_ref, sh_ref, wrt_ref, hn_ref, eid_ref, rank_ref, gate_ref, cnt_ref, carry_sc):
    @pl.when(pl.program_id(0) == 0)
    def _():
        carry_sc[...] = jnp.zeros(carry_sc.shape, F32)

    hn = _normmod(h_ref[...], g_ref[...], sc_ref[...], sh_ref[...])
    hn_ref[...] = hn
    tm = hn.shape[0]
    logits = lax.dot_general(wrt_ref[...], hn, (((1,), (1,)), ((), ())), preferred_element_type=F32,
                             precision=lax.Precision.HIGHEST)
    idx = lax.broadcasted_iota(jnp.int32, logits.shape, 0)
    v1 = jnp.max(logits, axis=0, keepdims=True)
    i1 = jnp.min(jnp.where(logits == v1, idx, N_EXPERTS), axis=0, keepdims=True)
    rest = jnp.where(idx == i1, -jnp.inf, logits)
    v2 = jnp.max(rest, axis=0, keepdims=True)
    i2 = jnp.min(jnp.where(rest == v2, idx, N_EXPERTS), axis=0, keepdims=True)
    e = jnp.exp(v2 - v1)
    gate_ref[...] = jnp.concatenate([1.0 / (1.0 + e), e / (1.0 + e)], axis=0)
    eid_ref[...] = jnp.concatenate([i1, i2], axis=0)

    sel = ((idx == i1) | (idx == i2)).astype(BF16)
    before = (lax.broadcasted_iota(jnp.int32, (tm, tm), 0)
              < lax.broadcasted_iota(jnp.int32, (tm, tm), 1)).astype(BF16)
    rank_all = carry_sc[:, :1] + _dot(sel, before)
    r1 = jnp.sum(jnp.where(idx == i1, rank_all, 0.0), axis=0, keepdims=True)
    r2 = jnp.sum(jnp.where(idx == i2, rank_all, 0.0), axis=0, keepdims=True)
    rank_ref[...] = jnp.concatenate([r1, r2], axis=0).astype(jnp.int32)
    carry_sc[...] += jnp.sum(sel.astype(F32), axis=1, keepdims=True)
    cnt_ref[...] = carry_sc[...].astype(jnp.int32)


def _route(h, g, sc, sh, wrt, *, tm, tiles_per_batch):
    T, D = h.shape
    bvec = pl.BlockSpec((None, 1, D), lambda i: (i // tiles_per_batch, 0, 0))
    lane_blk = pl.BlockSpec((2, tm), lambda i: (0, i))
    return pl.pallas_call(
        _route_kernel,
        grid=(T // tm,),
        in_specs=[pl.BlockSpec((tm, D), lambda i: (i, 0)), pl.BlockSpec((1, D), lambda i: (0, 0)), bvec, bvec,
                  pl.BlockSpec(wrt.shape, lambda i: (0, 0))],
        out_specs=[pl.BlockSpec((tm, D), lambda i: (i, 0)), lane_blk, lane_blk, lane_blk,
                   pl.BlockSpec((N_EXPERTS, HEAD_PAD), lambda i: (0, 0))],
        out_shape=[jax.ShapeDtypeStruct((T, D), F32), jax.ShapeDtypeStruct((2, T), jnp.int32),
                   jax.ShapeDtypeStruct((2, T), jnp.int32), jax.ShapeDtypeStruct((2, T), F32),
                   jax.ShapeDtypeStruct((N_EXPERTS, HEAD_PAD), jnp.int32)],
        scratch_shapes=[pltpu.VMEM((N_EXPERTS, HEAD_PAD), F32)],
        compiler_params=_cparams(("arbitrary",)),
        name="moe_route",
    )(h, g, sc, sh, wrt)


def _dispatch_kernel(pad_ref, dest_ref, hn_ref, xs_ref, zero_sc, sem, zsem):
    tm = hn_ref.shape[0]

    def row_copy(r, k):
        return pltpu.make_async_copy(hn_ref.at[pl.ds(r, 1), :], xs_ref.at[pl.ds(dest_ref[k, r], 1), :], sem)

    def issue(r, carry):
        row_copy(r, 0).start(priority=0)
        row_copy(r, 1).start(priority=1)
        return carry

    for r in range(tm):
        issue(r, 0)

    @pl.when(pl.program_id(0) == pl.num_programs(0) - 1)
    def _():
        zero_sc[...] = jnp.zeros(zero_sc.shape, F32)

        def pad_copy(p):
            return pltpu.make_async_copy(zero_sc.at[pl.ds(0, 1), :], xs_ref.at[pl.ds(p, 1), :], zsem)

        for e in range(N_EXPERTS):
            lo, hi = pad_ref[0, e], pad_ref[1, e]

            def zissue(p, carry):
                pad_copy(p).start()
                return carry

            def zwait(p, carry):
                pad_copy(p).wait()
                return carry

            lax.fori_loop(lo, hi, zissue, 0)
            lax.fori_loop(lo, hi, zwait, 0)

        zrows = zero_sc.shape[0]

        def tail_copy(c):
            return pltpu.make_async_copy(zero_sc, xs_ref.at[pl.ds(pl.multiple_of(c * zrows, zrows), zrows), :], zsem)

        def tissue(c, carry):
            tail_copy(c).start()
            return carry

        def twait(c, carry):
            tail_copy(c).wait()
            return carry

        lo, hi = pad_ref[1, N_EXPERTS - 1] // zrows, xs_ref.shape[0] // zrows
        lax.fori_loop(lo, hi, tissue, 0)
        lax.fori_loop(lo, hi, twait, 0)

    for _ in range(2):
        pltpu.make_async_copy(hn_ref, xs_ref.at[pl.ds(0, tm), :], sem).wait()


def _dispatch(pad_rows, dest, hn, *, n_slots, tm):
    T, D = hn.shape
    return pl.pallas_call(
        _dispatch_kernel,
        grid_spec=pltpu.PrefetchScalarGridSpec(
            num_scalar_prefetch=1,
            grid=(T // tm,),
            in_specs=[pl.BlockSpec((2, tm), lambda i, pad: (0, i), memory_space=pltpu.SMEM),
                      pl.BlockSpec((tm, D), lambda i, pad: (i, 0))],
            out_specs=pl.BlockSpec(memory_space=pl.ANY),
            scratch_shapes=[pltpu.VMEM((64, D), F32), pltpu.SemaphoreType.DMA, pltpu.SemaphoreType.DMA]),
        out_shape=jax.ShapeDtypeStruct((n_slots, D), F32),
        compiler_params=_cparams(("arbitrary",)),
        name="moe_dispatch",
    )(pad_rows, dest, hn)


def _experts_kernel(te_ref, na_ref, xs_ref, wgu_ref, wd_ref, y_ref, acc_sc, *, tf):
    @pl.when(pl.program_id(0) < na_ref[0])
    def _():
        _swiglu_tile(xs_ref[...].astype(BF16), wgu_ref, wd_ref, acc_sc, tf)
        y_ref[...] = acc_sc[...]

    @pl.when(pl.program_id(0) >= na_ref[0])
    def _():
        y_ref[...] = jnp.zeros(y_ref.shape, F32)


def _experts(tile_expert, n_active, xs, w_gu, w_d, *, tm, tf):
    P, D = xs.shape

    def tile(j, te, na):
        return jnp.minimum(j, na[0] - 1)

    return pl.pallas_call(
        functools.partial(_experts_kernel, tf=tf),
        grid_spec=pltpu.PrefetchScalarGridSpec(
            num_scalar_prefetch=2,
            grid=(P // tm,),
            in_specs=[pl.BlockSpec((tm, D), lambda j, te, na: (tile(j, te, na), 0)),
                      pl.BlockSpec((None, D, 2 * D_FF), lambda j, te, na: (te[tile(j, te, na)], 0, 0)),
                      pl.BlockSpec((None, D_FF, D), lambda j, te, na: (te[tile(j, te, na)], 0, 0))],
            out_specs=pl.BlockSpec((tm, D), lambda j, te, na: (j, 0)),
            scratch_shapes=[pltpu.VMEM((tm, D), F32)]),
        out_shape=jax.ShapeDtypeStruct((P, D), F32),
        compiler_params=_cparams(("arbitrary",)),
        name="moe_experts",
    )(tile_expert, n_active, xs, w_gu, w_d)


def _combine_kernel(dest_ref, y_ref, gates_ref, h_ref, gt_ref, gfin_ref, o_ref, ybuf, sems):
    tm = h_ref.shape[0]
    half = tm // COMBINE_PARTS

    def row_copy(r, k):
        return pltpu.make_async_copy(y_ref.at[pl.ds(dest_ref[k, r], 1), :], ybuf.at[k, pl.ds(r, 1), :],
                                     sems.at[r // half])

    for r in range(tm):
        row_copy(r, 0).start(priority=0)
        row_copy(r, 1).start(priority=1)
    for part in range(COMBINE_PARTS):
        rows = slice(part * half, (part + 1) * half)
        for k in range(2):
            pltpu.make_async_copy(y_ref.at[pl.ds(0, half), :], ybuf.at[k, rows], sems.at[part]).wait()
        gates = gates_ref[rows, :]
        moe = gates[:, 0:1] * ybuf[0, rows] + gates[:, 1:2] * ybuf[1, rows]
        y = h_ref[rows, :] + gt_ref[...] * moe
        o_ref[rows, :] = y * _rms_scale(y) * gfin_ref[...]


def _combine(dest, y, gates, h, gt, gfin, *, tm, tiles_per_batch):
    T, D = h.shape
    return pl.pallas_call(
        _combine_kernel,
        grid=(T // tm,),
        in_specs=[pl.BlockSpec((2, tm), lambda i: (0, i), memory_space=pltpu.SMEM),
                  pl.BlockSpec(memory_space=pl.ANY),
                  pl.BlockSpec((tm, 2), lambda i: (i, 0)),
                  pl.BlockSpec((tm, D), lambda i: (i, 0)),
                  pl.BlockSpec((None, 1, D), lambda i: (i // tiles_per_batch, 0, 0)),
                  pl.BlockSpec((1, D), lambda i: (0, 0))],
        out_specs=pl.BlockSpec((tm, D), lambda i: (i, 0)),
        out_shape=jax.ShapeDtypeStruct((T, D), F32),
        scratch_shapes=[pltpu.VMEM((2, tm, D), F32), pltpu.SemaphoreType.DMA((COMBINE_PARTS,))],
        compiler_params=_cparams(("arbitrary",)),
        name="moe_combine",
    )(dest, y, gates, h, gt, gfin)


def _moe_plan(eids, ranks, counts, *, tm, n_tiles):
    padded = (counts + tm - 1) // tm * tm
    ends = jnp.cumsum(padded)
    starts = ends - padded
    dest = ranks
    for e in range(N_EXPERTS):
        dest = dest + jnp.where(eids == e, starts[e], 0)
    tile_start = jnp.arange(n_tiles, dtype=jnp.int32) * tm
    tile_expert = jnp.minimum(jnp.sum(tile_start[:, None] >= ends[None, :], axis=1), N_EXPERTS - 1)
    n_active = (ends[-1] // tm).reshape(1)
    pad_rows = jnp.stack([starts + counts, ends])
    return dest.astype(jnp.int32), tile_expert.astype(jnp.int32), n_active.astype(jnp.int32), pad_rows.astype(jnp.int32)


def _mla_weights(w_in, w_q_up, w_kv_up):
    D = w_in.shape[0]
    half = MLA_ROPE // 2
    w_ql = w_in[:, :MLA_Q_RANK]
    w_kvl = w_in[:, MLA_Q_RANK:MLA_Q_RANK + MLA_KV_RANK]
    w_kr = w_in[:, MLA_Q_RANK + MLA_KV_RANK:]
    z64 = jnp.zeros((D, MLA_NOPE), F32)
    z32 = jnp.zeros((D, HEAD_PAD - MLA_QK), F32)
    kr_pad = jnp.concatenate([z64, w_kr, z32], axis=1)
    kr_swap = jnp.concatenate([z64, w_kr[:, half:], w_kr[:, :half], z32], axis=1)
    w1 = jnp.concatenate([w_ql, w_kvl, kr_pad, kr_swap], axis=1).astype(BF16)

    wq = w_q_up.reshape(MLA_Q_RANK, MLA_HEADS, MLA_QK)
    nope, rope = wq[..., :MLA_NOPE], wq[..., MLA_NOPE:]
    zq = jnp.zeros((MLA_Q_RANK, MLA_HEADS, HEAD_PAD - MLA_QK), F32)
    wqa = jnp.concatenate([nope, rope, zq], axis=-1).reshape(MLA_Q_RANK, MLA_HEADS * HEAD_PAD).astype(BF16)
    wqb = jnp.concatenate([jnp.zeros_like(nope), rope[..., half:], rope[..., :half], zq], axis=-1)
    wqb = wqb.reshape(MLA_Q_RANK, MLA_HEADS * HEAD_PAD).astype(BF16)

    wkv = w_kv_up.reshape(MLA_KV_RANK, MLA_HEADS, MLA_NOPE + MLA_V)
    k_nope, v = wkv[..., :MLA_NOPE], wkv[..., MLA_NOPE:]
    zk = jnp.zeros((MLA_KV_RANK, MLA_HEADS, HEAD_PAD - MLA_NOPE), F32)
    wka = jnp.concatenate([k_nope, zk], axis=-1).reshape(MLA_KV_RANK, MLA_HEADS * HEAD_PAD).astype(BF16)
    wvt = v.reshape(MLA_KV_RANK, MLA_HEADS * MLA_V).T.astype(BF16)
    return w1, wqa, wqb, wka, wvt


def _rope_tables(positions):
    inv_freq = ROPE_THETA ** (-jnp.arange(0, MLA_ROPE, 2, dtype=F32) / MLA_ROPE)
    ang = positions.astype(F32).reshape(-1, 1) * inv_freq
    cos, sin = jnp.cos(ang), jnp.sin(ang)
    T = ang.shape[0]
    one = jnp.ones((T, MLA_NOPE), F32)
    z64 = jnp.zeros((T, MLA_NOPE), F32)
    z32 = jnp.zeros((T, HEAD_PAD - MLA_QK), F32)
    return (jnp.concatenate([one, cos, cos, z32], axis=1),
            jnp.concatenate([z64, -sin, sin, z32], axis=1))


def kernel(x, c, positions, g_mix, g_ffn, w_ada, b_ada, w_mla_in, g_mla_q, w_mla_q_up, g_mla_kv, w_mla_kv_up,
           w_mla_out, g_kv_b, w_ada_kv, b_ada_kv, w_kv_b, rpb_table, w_q_b, w_o_b, w_ffn_gu, w_ffn_down,
           w_router, w_exp_gu, w_exp_down, g_final):
    B, S, D = x.shape
    T = B * S
    h = x.reshape(T, D)

    c8 = jnp.zeros((8, D), F32).at[:B].set(c)
    mod = _ada(c8, w_ada, b_ada[:, None, :])[:, :B]
    mod = mod.reshape(2, B, N_MOD, 1, D)
    mod_kv = _ada(c8, w_ada_kv[None], b_ada_kv[None, None, :])[0, :B].reshape(B, 2, 1, D)
    sh_kv, sc_kv = mod_kv[:, 0], mod_kv[:, 1]

    def mods(layer):
        return [mod[layer, :, k] for k in range(N_MOD)]

    row = lambda v: v.reshape(1, -1)

    sh_m, sc_m, gt_m, sh_f, sc_f, gt_f = mods(0)
    w1, wqa, wqb, wka, wvt = _mla_weights(w_mla_in[0], w_mla_q_up[0], w_mla_kv_up[0])
    cos_t, sin_t = _rope_tables(positions)
    q, k, vt = _mla_proj(h, row(g_mix[0]), sc_m, sh_m, cos_t, sin_t, w1, row(g_mla_q[0]), row(g_mla_kv[0]),
                         wqa, wqb, wka, wvt, tm=TM_MLA_PROJ, batch=B, seq=S)
    o = _mla_attn(q.reshape(B, S, -1), k.reshape(B, S, -1), vt, batch=B, seq=S, tq=MLA_TQ)

    h = _ffn(o.reshape(T, -1), w_mla_out[0].astype(BF16), h, gt_m, row(g_ffn[0]), sc_f, sh_f, gt_f,
             w_ffn_gu[0].astype(BF16), w_ffn_down[0].astype(BF16), tm=TM_FFN, tf=FF_CHUNK,
             tiles_per_batch=S // TM_FFN)

    sh_m, sc_m, gt_m, sh_f, sc_f, gt_f = mods(1)
    qkv = _l1_proj(h, row(g_kv_b), sc_kv, sh_kv, row(g_mix[1]), sc_m, sh_m,
                   w_kv_b.astype(BF16), w_q_b[0].astype(BF16), tm=TM_L1_PROJ, tiles_per_batch=S // TM_L1_PROJ)

    outs, lses = [], []
    for g, (window, dil) in enumerate(DIL_PATTERNS):
        bias = _dil_bias(rpb_table[:, g * DIL_HEADS:(g + 1) * DIL_HEADS], window // dil, dil)
        o_g, lse_g = _dil_attn(*qkv[3 * g:3 * g + 3], bias, group=g, dil=dil, batch=B, seq=S)
        outs.append(o_g)
        lses.append(lse_g)
    h = _dil_out(outs, lses, w_o_b[0].astype(BF16), h, gt_m, tm=TM_DIL_OUT, tiles_per_batch=S // TM_DIL_OUT)

    hn, eids, ranks, gates, counts = _route(h, row(g_ffn[1]), sc_f, sh_f, w_router[0].T, tm=TM_ROUTE,
                                            tiles_per_batch=S // TM_ROUTE)
    n_tiles = (T * 2) // MOE_TM + N_EXPERTS
    dest, tile_expert, n_active, pad_rows = _moe_plan(eids, ranks, counts[:, 0], tm=MOE_TM, n_tiles=n_tiles)
    xs = _dispatch(pad_rows, dest, hn, n_slots=n_tiles * MOE_TM, tm=TM_DISPATCH)
    y = _experts(tile_expert, n_active, xs, w_exp_gu[0].astype(BF16), w_exp_down[0].astype(BF16),
                 tm=MOE_TM, tf=FF_CHUNK)
    out = _combine(dest, y, gates.T, h, gt_f, row(g_final), tm=TM_COMBINE, tiles_per_batch=S // TM_COMBINE)
    return out.reshape(B, S, D)
```

```python
import functools
import math

import jax
import jax.numpy as jnp
from jax import lax
from jax.experimental import pallas as pl
from jax.experimental.pallas import tpu as pltpu

D_MODEL = 1024
N_MOD = 6
EPS = 1e-6

MLA_HEADS = 16
MLA_Q_RANK = 384
MLA_KV_RANK = 256
MLA_NOPE = 64
MLA_ROPE = 32
MLA_V = 64
MLA_QK = MLA_NOPE + MLA_ROPE
ROPE_THETA = 10000.0
LANES = 128
ONES_ROWS = 16
STRIP = 256
HEAD_PAD = 128

DIL_PATTERNS = ((128, 1), (512, 4), (2048, 16))
DIL_GROUPS = len(DIL_PATTERNS)
DIL_HEADS = 8
DIL_HEAD_DIM = 64
DIL_BLOCK = 128
DIL_SUBBLOCKS = 8
DIL_W = DIL_HEADS * DIL_HEAD_DIM
RPB_BUCKETS = 32
RPB_MAX_DIST = 2048

D_FF = 2816
N_EXPERTS = 8
MOE_TM = 512

TM_MLA_PROJ = 1024
TM_FFN = 512
TM_L1_PROJ = 1024
TM_DIL_OUT = 1024
TM_ROUTE = 1024
TM_DISPATCH = 1024
TM_COMBINE = 512
COMBINE_PARTS = 4
MLA_TQ = 1024
FF_CHUNK = 256

MASK_VALUE = -1e30
LOG2E = math.log2(math.e)

F32 = jnp.float32
BF16 = jnp.bfloat16

VMEM_LIMIT = 56 * 1024 * 1024


def _cparams(sem):
    return pltpu.CompilerParams(dimension_semantics=sem, vmem_limit_bytes=VMEM_LIMIT)


def _dot(a, b):
    return jnp.dot(a, b, preferred_element_type=F32)


def _dot_nt(a, b):
    return lax.dot_general(a, b, (((1,), (1,)), ((), ())), preferred_element_type=F32)


def _rms_scale(x):
    return lax.rsqrt(jnp.mean(x * x, axis=-1, keepdims=True) + EPS)


def _normmod(x, g, sc, sh):
    return (x * _rms_scale(x)) * (g * (1.0 + sc)) + sh


def _silu(x):
    return x * (1.0 / (1.0 + jnp.exp(-x)))


def _ada_kernel(c_ref, w_ref, b_ref, o_ref):
    c = c_ref[...]
    o_ref[...] = jnp.dot(_silu(c), w_ref[...], preferred_element_type=F32,
                         precision=lax.Precision.HIGHEST) + b_ref[...]


def _ada(c8, w, b, tn=512):
    L, D, N = w.shape
    return pl.pallas_call(
        _ada_kernel,
        grid=(L, N // tn),
        in_specs=[pl.BlockSpec((8, D), lambda l, j: (0, 0)),
                  pl.BlockSpec((None, D, tn), lambda l, j: (l, 0, j)),
                  pl.BlockSpec((None, 1, tn), lambda l, j: (l, 0, j))],
        out_specs=pl.BlockSpec((None, 8, tn), lambda l, j: (l, 0, j)),
        out_shape=jax.ShapeDtypeStruct((L, 8, N), F32),
        compiler_params=_cparams(("parallel", "parallel")),
        name="ada_mod",
    )(c8, w, b)


def _swap_rope_halves(x):
    width = x.shape[1]
    lane = lax.broadcasted_iota(jnp.int32, (1, width), 1) % HEAD_PAD
    first_half = lane < MLA_NOPE + MLA_ROPE // 2
    return jnp.where(first_half, pltpu.roll(x, width - MLA_ROPE // 2, axis=1), pltpu.roll(x, MLA_ROPE // 2, axis=1))


def _mla_proj_kernel(h_ref, g_ref, sc_ref, sh_ref, cos_ref, sin_ref, w1_ref, gq_ref, gkv_ref,
                     wq_ref, wka_ref, wvt_ref, q_ref, k_ref, vt_ref, *, qscale):
    hn = _normmod(h_ref[...], g_ref[...], sc_ref[...], sh_ref[...]).astype(BF16)
    z = _dot(hn, w1_ref[...])
    ql = z[:, :MLA_Q_RANK]
    kvl = z[:, MLA_Q_RANK:MLA_Q_RANK + MLA_KV_RANK]
    kr = z[:, MLA_Q_RANK + MLA_KV_RANK:]
    qn = (ql * _rms_scale(ql) * gq_ref[...]).astype(BF16)
    kvn = (kvl * _rms_scale(kvl) * gkv_ref[...]).astype(BF16)
    cos = cos_ref[...]
    sin = sin_ref[...]
    k_rope = kr * cos + _swap_rope_halves(kr) * sin
    cos2 = jnp.concatenate([cos, cos], axis=1)
    sin2 = jnp.concatenate([sin, sin], axis=1)
    k_rope2 = jnp.concatenate([k_rope, k_rope], axis=1)
    for hp in range(MLA_HEADS // 2):
        sl = slice(2 * HEAD_PAD * hp, 2 * HEAD_PAD * (hp + 1))
        a = _dot(qn, wq_ref[:, sl])
        q_ref[:, sl] = ((a * cos2 + _swap_rope_halves(a) * sin2) * qscale).astype(BF16)
        k_ref[:, sl] = (_dot(kvn, wka_ref[:, sl]) + k_rope2).astype(BF16)
    vt_ref[...] = _dot_nt(wvt_ref[...], kvn).astype(BF16)


def _mla_proj(h, g, sc, sh, cos_t, sin_t, w1, gq, gkv, wq, wka, wvt, *, tm, batch, seq):
    T, D = h.shape
    HP = MLA_HEADS * HEAD_PAD
    tpb = seq // tm
    row = lambda n: pl.BlockSpec((tm, n), lambda i: (i, 0))
    bvec = pl.BlockSpec((None, 1, D), lambda i: (i // tpb, 0, 0))
    full = lambda a: pl.BlockSpec(a.shape, lambda i: (0,) * a.ndim)
    return pl.pallas_call(
        functools.partial(_mla_proj_kernel, qscale=(MLA_QK ** -0.5) * LOG2E),
        grid=(T // tm,),
        in_specs=[row(D), full(g), bvec, bvec, row(HEAD_PAD), row(HEAD_PAD), full(w1), full(gq), full(gkv),
                  full(wq), full(wka), full(wvt)],
        out_specs=[row(HP), row(HP),
                   pl.BlockSpec((None, MLA_HEADS * MLA_V, tm), lambda i: (i // tpb, 0, i % tpb))],
        out_shape=[jax.ShapeDtypeStruct((T, HP), BF16), jax.ShapeDtypeStruct((T, HP), BF16),
                   jax.ShapeDtypeStruct((batch, MLA_HEADS * MLA_V, seq), BF16)],
        compiler_params=_cparams(("parallel",)),
        name="mla_proj",
    )(h, g, sc, sh, cos_t, sin_t, w1, gq, gkv, wq, wka, wvt)


def _mla_attn_kernel(q_ref, k_ref, vt_ref, o_ref, m_sc, acc_sc, s_sc, mb_sc, *, tq):
    qi = pl.program_id(2)
    tk = tq
    m_sc[...] = jnp.full(m_sc.shape, -jnp.inf, F32)
    acc_sc[...] = jnp.zeros(acc_sc.shape, F32)

    def scores(j, diagonal, slot, hh, n):
        rows = (n + 1) * STRIP if diagonal else tk
        off = pl.multiple_of(j * tk, tk)
        cols = slice(n * STRIP, (n + 1) * STRIP)
        q = q_ref[cols, HEAD_PAD * hh:HEAD_PAD * (hh + 1)]
        k = k_ref[pl.ds(off, rows), HEAD_PAD * hh:HEAD_PAD * (hh + 1)]
        s = _dot_nt(k, q)
        if diagonal:
            key = lax.broadcasted_iota(jnp.int32, (rows, STRIP), 0)
            qry = lax.broadcasted_iota(jnp.int32, (rows, STRIP), 1) + n * STRIP
            s = jnp.where(key <= qry, s, -jnp.inf)
        s_sc[slot, hh, :rows, cols] = s
        m_blk = jnp.max(jnp.max(s.reshape(rows // 8, 8, STRIP), axis=0), axis=0, keepdims=True)
        mb_sc[slot, hh, :, cols] = jnp.broadcast_to(m_blk, (8, STRIP))

    def softmax_values(j, diagonal, slot, hh, n):
        rows = (n + 1) * STRIP if diagonal else tk
        off = pl.multiple_of(j * tk, tk)
        cols = slice(n * STRIP, (n + 1) * STRIP)
        s3 = s_sc[slot, hh, :rows, cols].reshape(rows // 8, 8, STRIP)
        m_prev = m_sc[hh, :, cols]
        m_new = jnp.maximum(m_prev, mb_sc[slot, hh, :, cols])
        alpha = jnp.exp2(m_prev - m_new)
        p = jnp.exp2(s3 - m_new[None]).reshape(rows, STRIP).astype(BF16)
        vt = vt_ref[MLA_V * hh:MLA_V * (hh + 1), pl.ds(off, rows)]
        vt1 = jnp.concatenate([vt, jnp.ones((ONES_ROWS, rows), BF16)], axis=0)
        acc_sc[hh, :, cols] = alpha[:1] * acc_sc[hh, :, cols] + _dot(vt1, p)
        m_sc[hh, :, cols] = m_new

    strips = [(hh, n) for hh in range(2) for n in range(tq // STRIP)]

    def step(j, even, diagonal_next=False, has_next=True):
        cur, nxt = (0, 1) if even else (1, 0)
        for hh, n in strips:
            if has_next:
                scores(j + 1, diagonal_next, nxt, hh, n)
            softmax_values(j, not has_next, cur, hh, n)

    def first_scores(diagonal):
        for hh, n in strips:
            scores(0, diagonal, 0, hh, n)

    @pl.when(qi == 0)
    def _():
        first_scores(True)

    @pl.when(qi > 0)
    def _():
        first_scores(False)

        def pair(t, carry):
            step(2 * t, True)
            step(2 * t + 1, False)
            return carry

        lax.fori_loop(0, (qi - 1) // 2, pair, 0)

    @pl.when(qi % 2 == 1)
    def _():
        step(qi - 1, True, diagonal_next=True)
        step(qi, False, has_next=False)

    @pl.when((qi % 2 == 0) & (qi > 0))
    def _():
        step(qi - 2, True)
        step(qi - 1, False, diagonal_next=True)

    @pl.when(qi % 2 == 0)
    def _():
        step(qi, True, has_next=False)

    o_t = jnp.concatenate([acc_sc[hh, :MLA_V] / acc_sc[hh, MLA_V:MLA_V + 1] for hh in range(2)], axis=0)
    o_ref[...] = o_t.T.astype(BF16)


def _mla_attn(q, k, vt, *, batch, seq, tq):
    return pl.pallas_call(
        functools.partial(_mla_attn_kernel, tq=tq),
        grid=(batch, MLA_HEADS // 2, seq // tq),
        in_specs=[pl.BlockSpec((None, tq, 2 * HEAD_PAD), lambda b, hp, i: (b, i, hp)),
                  pl.BlockSpec((None, seq, 2 * HEAD_PAD), lambda b, hp, i: (b, 0, hp)),
                  pl.BlockSpec((None, 2 * MLA_V, seq), lambda b, hp, i: (b, hp, 0))],
        out_specs=pl.BlockSpec((None, tq, 2 * MLA_V), lambda b, hp, i: (b, i, hp)),
        out_shape=jax.ShapeDtypeStruct((batch, seq, MLA_HEADS * MLA_V), BF16),
        scratch_shapes=[pltpu.VMEM((2, 8, tq), F32), pltpu.VMEM((2, MLA_V + ONES_ROWS, tq), F32),
                        pltpu.VMEM((2, 2, tq, tq), F32),
                        pltpu.VMEM((2, 2, 8, tq), F32)],
        compiler_params=_cparams(("parallel", "parallel", "arbitrary")),
        name="mla_attn",
    )(q, k, vt)


def _swiglu_tile(x, wgu_ref, wd_ref, acc_sc, tf):
    for f in range(D_FF // tf):
        g = _dot(x, wgu_ref[:, f * tf:(f + 1) * tf])
        u = _dot(x, wgu_ref[:, D_FF + f * tf:D_FF + (f + 1) * tf])
        part = _dot((_silu(g) * u).astype(BF16), wd_ref[f * tf:(f + 1) * tf, :])
        if f == 0:
            acc_sc[...] = part
        else:
            acc_sc[...] += part


def _ffn_kernel(a_ref, wo_ref, h_ref, gtm_ref, g_ref, sc_ref, sh_ref, gt_ref, wgu_ref, wd_ref, o_ref, acc_sc,
                *, tf):
    h = h_ref[...] + gtm_ref[...] * _dot(a_ref[...], wo_ref[...])
    x = _normmod(h, g_ref[...], sc_ref[...], sh_ref[...]).astype(BF16)
    _swiglu_tile(x, wgu_ref, wd_ref, acc_sc, tf)
    o_ref[...] = h + gt_ref[...] * acc_sc[...]


def _ffn(a, w_o, h, gt_m, g, sc, sh, gt, w_gu, w_d, *, tm, tf, tiles_per_batch):
    T, D = h.shape
    bvec = pl.BlockSpec((None, 1, D), lambda i: (i // tiles_per_batch, 0, 0))
    row = pl.BlockSpec((tm, D), lambda i: (i, 0))
    const = lambda w: pl.BlockSpec(w.shape, lambda i: (0,) * w.ndim, pipeline_mode=pl.Buffered(1))
    return pl.pallas_call(
        functools.partial(_ffn_kernel, tf=tf),
        grid=(T // tm,),
        in_specs=[row, const(w_o), row, bvec, const(g), bvec, bvec, bvec, const(w_gu), const(w_d)],
        out_specs=row,
        out_shape=jax.ShapeDtypeStruct((T, D), F32),
        scratch_shapes=[pltpu.VMEM((tm, D), F32)],
        compiler_params=_cparams(("parallel",)),
        name="ffn_dense",
    )(a, w_o, h, gt_m, g, sc, sh, gt, w_gu, w_d)


def _l1_proj_kernel(h_ref, gkv_ref, sckv_ref, shkv_ref, gq_ref, scq_ref, shq_ref, wkv_ref, wq_ref, *refs):
    out_refs, stage_sc = refs[:-1], refs[-1]
    tm = h_ref.shape[0]
    x = h_ref[...]
    xh = x * _rms_scale(x)
    hn_kv = (xh * (gkv_ref[...] * (1.0 + sckv_ref[...])) + shkv_ref[...]).astype(BF16)
    hn_q = (xh * (gq_ref[...] * (1.0 + scq_ref[...])) + shq_ref[...]).astype(BF16)
    for g, (_, dil) in enumerate(DIL_PATTERNS):
        ys = (_dot(hn_q, wq_ref[:, g * DIL_W:(g + 1) * DIL_W]) * (DIL_HEAD_DIM ** -0.5),
              _dot(hn_kv, wkv_ref[:, 2 * g * DIL_W:(2 * g + 1) * DIL_W]),
              _dot(hn_kv, wkv_ref[:, (2 * g + 1) * DIL_W:(2 * g + 2) * DIL_W]))
        for y, out_ref in zip(ys, out_refs[3 * g:3 * g + 3]):
            if dil == 1:
                out_ref[...] = y.astype(BF16)
            else:
                for c in range(DIL_W // LANES):
                    stage_sc[c] = y[:, c * LANES:(c + 1) * LANES]
                for r in range(dil):
                    for c in range(DIL_W // LANES):
                        col = r * DIL_W + c * LANES
                        out_ref[:, col:col + LANES] = stage_sc[c, pl.ds(r, tm // dil, stride=dil), :].astype(BF16)


def _l1_proj(h, gkv, sckv, shkv, gq, scq, shq, wkv, wq, *, tm, tiles_per_batch):
    T, D = h.shape
    bvec = pl.BlockSpec((None, 1, D), lambda i: (i // tiles_per_batch, 0, 0))
    full = lambda a: pl.BlockSpec(a.shape, lambda i: (0,) * a.ndim)
    out_specs, out_shape = [], []
    for _, dil in DIL_PATTERNS:
        for _ in range(3):
            out_specs.append(pl.BlockSpec((tm // dil, dil * DIL_W), lambda i: (i, 0)))
            out_shape.append(jax.ShapeDtypeStruct((T // dil, dil * DIL_W), BF16))
    return pl.pallas_call(
        _l1_proj_kernel,
        grid=(T // tm,),
        in_specs=[pl.BlockSpec((tm, D), lambda i: (i, 0)), full(gkv), bvec, bvec, full(gq), bvec, bvec,
                  full(wkv), full(wq)],
        out_specs=out_specs,
        out_shape=out_shape,
        scratch_shapes=[pltpu.VMEM((DIL_W // LANES, tm, LANES), F32)],
        compiler_params=_cparams(("parallel",)),
        name="l1_proj",
    )(h, gkv, sckv, shkv, gq, scq, shq, wkv, wq)


def _dil_attn_kernel(q_ref, kc_ref, kp_ref, vc_ref, vp_ref, bias_ref, o_ref, lse_ref):
    i = pl.program_id(2)
    nsub = q_ref.shape[0] // DIL_BLOCK
    lane = lax.broadcasted_iota(jnp.int32, (1, 2 * DIL_HEAD_DIM), 1)
    lo = lane < DIL_HEAD_DIM
    col = lax.broadcasted_iota(jnp.int32, (1, 2 * DIL_BLOCK), 1)
    edge = jnp.where((col < DIL_BLOCK) & (i == 0), MASK_VALUE, 0.0).astype(F32)
    sls = [slice(2 * DIL_HEAD_DIM * hp, 2 * DIL_HEAD_DIM * (hp + 1)) for hp in range(DIL_HEADS // 2)]
    for sub in range(nsub):
        rows = slice(DIL_BLOCK * sub, DIL_BLOCK * (sub + 1))
        before = slice(DIL_BLOCK * (sub - 1), DIL_BLOCK * sub)
        scores = []
        for hp, sl in enumerate(sls):
            q2 = q_ref[rows, sl]
            zero = jnp.zeros_like(q2)
            qs = jnp.concatenate([jnp.where(lo, q2, zero), jnp.where(lo, zero, q2)], axis=0)
            k_prev = kp_ref[:, sl] if sub == 0 else kc_ref[before, sl]
            k2 = jnp.concatenate([k_prev, kc_ref[rows, sl]], axis=0)
            bias = jnp.concatenate([bias_ref[2 * hp], bias_ref[2 * hp + 1]], axis=0)
            s = _dot_nt(qs, k2) + bias
            scores.append(s + edge if sub == 0 else s)
        probs, stats = [], []
        for s in scores:
            m = jnp.max(s, axis=1, keepdims=True)
            p = jnp.exp(s - m)
            probs.append(p.astype(BF16))
            stats.append((m, jnp.sum(p, axis=1, keepdims=True)))
        for sl, p, (m, l) in zip(sls, probs, stats):
            v_prev = vp_ref[:, sl] if sub == 0 else vc_ref[before, sl]
            v2 = jnp.concatenate([v_prev, vc_ref[rows, sl]], axis=0)
            o = _dot(p, v2) / l
            lse = jnp.broadcast_to(m + jnp.log(l), o.shape)
            o_ref[rows, sl] = jnp.where(lo, o[:DIL_BLOCK], o[DIL_BLOCK:])
            lse_ref[rows, sl] = jnp.where(lo, lse[:DIL_BLOCK], lse[DIL_BLOCK:])


def _dil_attn(q, k, v, bias, *, group, dil, batch, seq):
    n = seq // dil
    nsub = min(DIL_SUBBLOCKS, n // DIL_BLOCK)
    run = nsub * DIL_BLOCK
    assert n % run == 0, (n, run)
    q, k, v = (a.reshape(batch, n, dil * DIL_W) for a in (q, k, v))
    cur = pl.BlockSpec((None, run, DIL_W), lambda b, r, i: (b, i, r))
    prev = pl.BlockSpec((None, DIL_BLOCK, DIL_W), lambda b, r, i: (b, jnp.maximum(i * nsub - 1, 0), r))
    out_sd = jax.ShapeDtypeStruct((batch, n, dil * DIL_W), F32)
    o, lse = pl.pallas_call(
        _dil_attn_kernel,
        grid=(batch, dil, n // run),
        in_specs=[cur, cur, prev, cur, prev, pl.BlockSpec(bias.shape, lambda b, r, i: (0, 0, 0))],
        out_specs=[cur, cur],
        out_shape=[out_sd, out_sd],
        compiler_params=_cparams(("parallel", "parallel", "arbitrary")),
        name=f"dil_attn_g{group}",
    )(q, k, k, v, v, bias)
    return o.reshape(batch * n, dil * DIL_W), lse.reshape(batch * n, dil * DIL_W)


def _rpb_bucket(dist):
    exact = RPB_BUCKETS // 2
    d = jnp.maximum(dist, 0)
    d_f = jnp.maximum(d, 1).astype(F32)
    large = exact + (jnp.log(d_f / exact) / math.log(RPB_MAX_DIST / exact)
                     * (RPB_BUCKETS - exact)).astype(jnp.int32)
    return jnp.where(d < exact, d, jnp.minimum(large, RPB_BUCKETS - 1))


def _dil_bias(table, span, dil):
    period = 3 * DIL_BLOCK
    j = jnp.arange(period)
    k_minus_q = jnp.where(j < 2 * DIL_BLOCK, j, j - period)
    dist = DIL_BLOCK - k_minus_q
    band = (dist >= 0) & (dist <= span)
    prof = jnp.where(band[:, None], table[_rpb_bucket(dist * dil)].astype(F32), MASK_VALUE).T
    skew = jnp.tile(prof, (1, DIL_BLOCK))[:, :DIL_BLOCK * (period - 1)]
    return skew.reshape(DIL_HEADS, DIL_BLOCK, period - 1)[:, :, :2 * DIL_BLOCK]


def _dil_out_kernel(o0_ref, o1_ref, o2_ref, l0_ref, l1_ref, l2_ref, w_ref, h_ref, gt_ref, out_ref, *stages):
    tm = h_ref.shape[0]
    stages = list(stages)

    def token_major(ref, dil):
        if dil == 1:
            return ref[...]
        stage = stages.pop()
        for r in range(dil):
            for c in range(DIL_W // LANES):
                col = r * DIL_W + c * LANES
                stage[c, pl.ds(r, tm // dil, stride=dil), :] = ref[:, col:col + LANES]
        return jnp.concatenate([stage[c] for c in range(DIL_W // LANES)], axis=1)

    dils = [dil for _, dil in DIL_PATTERNS]
    l0, l1, l2 = (token_major(ref, d) for ref, d in zip((l0_ref, l1_ref, l2_ref), dils))
    o0, o1, o2 = (token_major(ref, d) for ref, d in zip((o0_ref, o1_ref, o2_ref), dils))
    m = jnp.maximum(jnp.maximum(l0, l1), l2)
    w0, w1, w2 = jnp.exp(l0 - m), jnp.exp(l1 - m), jnp.exp(l2 - m)
    o = (o0 * w0 + o1 * w1 + o2 * w2) / (w0 + w1 + w2)
    out_ref[...] = h_ref[...] + gt_ref[...] * _dot(o.astype(BF16), w_ref[...])


def _dil_out(os_, ls_, w, h, gt, *, tm, tiles_per_batch):
    T, D = h.shape
    row = lambda n: pl.BlockSpec((tm, n), lambda i: (i, 0))
    grp = [pl.BlockSpec((tm // dil, dil * DIL_W), lambda i: (i, 0)) for _, dil in DIL_PATTERNS]
    n_stage = 2 * sum(1 for _, dil in DIL_PATTERNS if dil > 1)
    return pl.pallas_call(
        _dil_out_kernel,
        grid=(T // tm,),
        in_specs=grp + grp + [pl.BlockSpec(w.shape, lambda i: (0, 0)), row(D),
                              pl.BlockSpec((None, 1, D), lambda i: (i // tiles_per_batch, 0, 0))],
        out_specs=row(D),
        out_shape=jax.ShapeDtypeStruct((T, D), F32),
        scratch_shapes=[pltpu.VMEM((DIL_W // LANES, tm, LANES), F32)] * n_stage,
        compiler_params=_cparams(("parallel",)),
        name="dil_out",
    )(*os_, *ls_, w, h, gt)


def _route_kernel(h_ref, g_ref, sc_ref, sh_ref, wrt_ref, hn_ref, eid_ref, rank_ref, gate_ref, cnt_ref, carry_sc):
    @pl.when(pl.program_id(0) == 0)
    def _():
        carry_sc[...] = jnp.zeros(carry_sc.shape, F32)

    hn = _normmod(h_ref[...], g_ref[...], sc_ref[...], sh_ref[...])
    hn_ref[...] = hn
    tm = hn.shape[0]
    logits = lax.dot_general(wrt_ref[...], hn, (((1,), (1,)), ((), ())), preferred_element_type=F32,
                             precision=lax.Precision.HIGHEST)
    idx = lax.broadcasted_iota(jnp.int32, logits.shape, 0)
    v1 = jnp.max(logits, axis=0, keepdims=True)
    i1 = jnp.min(jnp.where(logits == v1, idx, N_EXPERTS), axis=0, keepdims=True)
    rest = jnp.where(idx == i1, -jnp.inf, logits)
    v2 = jnp.max(rest, axis=0, keepdims=True)
    i2 = jnp.min(jnp.where(rest == v2, idx, N_EXPERTS), axis=0, keepdims=True)
    e = jnp.exp(v2 - v1)
    gate_ref[...] = jnp.concatenate([1.0 / (1.0 + e), e / (1.0 + e)], axis=0)
    eid_ref[...] = jnp.concatenate([i1, i2], axis=0)

    sel = ((idx == i1) | (idx == i2)).astype(BF16)
    before = (lax.broadcasted_iota(jnp.int32, (tm, tm), 0)
              < lax.broadcasted_iota(jnp.int32, (tm, tm), 1)).astype(BF16)
    rank_all = carry_sc[:, :1] + _dot(sel, before)
    r1 = jnp.sum(jnp.where(idx == i1, rank_all, 0.0), axis=0, keepdims=True)
    r2 = jnp.sum(jnp.where(idx == i2, rank_all, 0.0), axis=0, keepdims=True)
    rank_ref[...] = jnp.concatenate([r1, r2], axis=0).astype(jnp.int32)
    carry_sc[...] += jnp.sum(sel.astype(F32), axis=1, keepdims=True)
    cnt_ref[...] = carry_sc[...].astype(jnp.int32)


def _route(h, g, sc, sh, wrt, *, tm, tiles_per_batch):
    T, D = h.shape
    bvec = pl.BlockSpec((None, 1, D), lambda i: (i // tiles_per_batch, 0, 0))
    lane_blk = pl.BlockSpec((2, tm), lambda i: (0, i))
    return pl.pallas_call(
        _route_kernel,
        grid=(T // tm,),
        in_specs=[pl.BlockSpec((tm, D), lambda i: (i, 0)), pl.BlockSpec((1, D), lambda i: (0, 0)), bvec, bvec,
                  pl.BlockSpec(wrt.shape, lambda i: (0, 0))],
        out_specs=[pl.BlockSpec((tm, D), lambda i: (i, 0)), lane_blk, lane_blk, lane_blk,
                   pl.BlockSpec((N_EXPERTS, HEAD_PAD), lambda i: (0, 0))],
        out_shape=[jax.ShapeDtypeStruct((T, D), F32), jax.ShapeDtypeStruct((2, T), jnp.int32),
                   jax.ShapeDtypeStruct((2, T), jnp.int32), jax.ShapeDtypeStruct((2, T), F32),
                   jax.ShapeDtypeStruct((N_EXPERTS, HEAD_PAD), jnp.int32)],
        scratch_shapes=[pltpu.VMEM((N_EXPERTS, HEAD_PAD), F32)],
        compiler_params=_cparams(("arbitrary",)),
        name="moe_route",
    )(h, g, sc, sh, wrt)


def _dispatch_kernel(pad_ref, dest_ref, hn_ref, xs_ref, zero_sc, sem, zsem):
    tm = hn_ref.shape[0]

    def row_copy(r, k):
        return pltpu.make_async_copy(hn_ref.at[pl.ds(r, 1), :], xs_ref.at[pl.ds(dest_ref[k, r], 1), :], sem)

    def issue(r, carry):
        row_copy(r, 0).start(priority=0)
        row_copy(r, 1).start(priority=1)
        return carry

    for r in range(tm):
        issue(r, 0)

    @pl.when(pl.program_id(0) == pl.num_programs(0) - 1)
    def _():
        zero_sc[...] = jnp.zeros(zero_sc.shape, F32)

        def pad_copy(p):
            return pltpu.make_async_copy(zero_sc.at[pl.ds(0, 1), :], xs_ref.at[pl.ds(p, 1), :], zsem)

        for e in range(N_EXPERTS):
            lo, hi = pad_ref[0, e], pad_ref[1, e]

            def zissue(p, carry):
                pad_copy(p).start()
                return carry

            def zwait(p, carry):
                pad_copy(p).wait()
                return carry

            lax.fori_loop(lo, hi, zissue, 0)
            lax.fori_loop(lo, hi, zwait, 0)

        zrows = zero_sc.shape[0]

        def tail_copy(c):
            return pltpu.make_async_copy(zero_sc, xs_ref.at[pl.ds(pl.multiple_of(c * zrows, zrows), zrows), :], zsem)

        def tissue(c, carry):
            tail_copy(c).start()
            return carry

        def twait(c, carry):
            tail_copy(c).wait()
            return carry

        lo, hi = pad_ref[1, N_EXPERTS - 1] // zrows, xs_ref.shape[0] // zrows
        lax.fori_loop(lo, hi, tissue, 0)
        lax.fori_loop(lo, hi, twait, 0)

    for _ in range(2):
        pltpu.make_async_copy(hn_ref, xs_ref.at[pl.ds(0, tm), :], sem).wait()


def _dispatch(pad_rows, dest, hn, *, n_slots, tm):
    T, D = hn.shape
    return pl.pallas_call(
        _dispatch_kernel,
        grid_spec=pltpu.PrefetchScalarGridSpec(
            num_scalar_prefetch=1,
            grid=(T // tm,),
            in_specs=[pl.BlockSpec((2, tm), lambda i, pad: (0, i), memory_space=pltpu.SMEM),
                      pl.BlockSpec((tm, D), lambda i, pad: (i, 0))],
            out_specs=pl.BlockSpec(memory_space=pl.ANY),
            scratch_shapes=[pltpu.VMEM((64, D), F32), pltpu.SemaphoreType.DMA, pltpu.SemaphoreType.DMA]),
        out_shape=jax.ShapeDtypeStruct((n_slots, D), F32),
        compiler_params=_cparams(("arbitrary",)),
        name="moe_dispatch",
    )(pad_rows, dest, hn)


def _experts_kernel(te_ref, na_ref, xs_ref, wgu_ref, wd_ref, y_ref, acc_sc, *, tf):
    @pl.when(pl.program_id(0) < na_ref[0])
    def _():
        _swiglu_tile(xs_ref[...].astype(BF16), wgu_ref, wd_ref, acc_sc, tf)
        y_ref[...] = acc_sc[...]

    @pl.when(pl.program_id(0) >= na_ref[0])
    def _():
        y_ref[...] = jnp.zeros(y_ref.shape, F32)


def _experts(tile_expert, n_active, xs, w_gu, w_d, *, tm, tf):
    P, D = xs.shape

    def tile(j, te, na):
        return jnp.minimum(j, na[0] - 1)

    return pl.pallas_call(
        functools.partial(_experts_kernel, tf=tf),
        grid_spec=pltpu.PrefetchScalarGridSpec(
            num_scalar_prefetch=2,
            grid=(P // tm,),
            in_specs=[pl.BlockSpec((tm, D), lambda j, te, na: (tile(j, te, na), 0)),
                      pl.BlockSpec((None, D, 2 * D_FF), lambda j, te, na: (te[tile(j, te, na)], 0, 0)),
                      pl.BlockSpec((None, D_FF, D), lambda j, te, na: (te[tile(j, te, na)], 0, 0))],
            out_specs=pl.BlockSpec((tm, D), lambda j, te, na: (j, 0)),
            scratch_shapes=[pltpu.VMEM((tm, D), F32)]),
        out_shape=jax.ShapeDtypeStruct((P, D), F32),
        compiler_params=_cparams(("arbitrary",)),
        name="moe_experts",
    )(tile_expert, n_active, xs, w_gu, w_d)


def _combine_kernel(dest_ref, y_ref, gates_ref, h_ref, gt_ref, gfin_ref, o_ref, ybuf, sems):
    tm = h_ref.shape[0]
    half = tm // COMBINE_PARTS

    def row_copy(r, k):
        return pltpu.make_async_copy(y_ref.at[pl.ds(dest_ref[k, r], 1), :], ybuf.at[k, pl.ds(r, 1), :],
                                     sems.at[r // half])

    for r in range(tm):
        row_copy(r, 0).start(priority=0)
        row_copy(r, 1).start(priority=1)
    for part in range(COMBINE_PARTS):
        rows = slice(part * half, (part + 1) * half)
        for k in range(2):
            pltpu.make_async_copy(y_ref.at[pl.ds(0, half), :], ybuf.at[k, rows], sems.at[part]).wait()
        gates = gates_ref[rows, :]
        moe = gates[:, 0:1] * ybuf[0, rows] + gates[:, 1:2] * ybuf[1, rows]
        y = h_ref[rows, :] + gt_ref[...] * moe
        o_ref[rows, :] = y * _rms_scale(y) * gfin_ref[...]


def _combine(dest, y, gates, h, gt, gfin, *, tm, tiles_per_batch):
    T, D = h.shape
    return pl.pallas_call(
        _combine_kernel,
        grid=(T // tm,),
        in_specs=[pl.BlockSpec((2, tm), lambda i: (0, i), memory_space=pltpu.SMEM),
                  pl.BlockSpec(memory_space=pl.ANY),
                  pl.BlockSpec((tm, 2), lambda i: (i, 0)),
                  pl.BlockSpec((tm, D), lambda i: (i, 0)),
                  pl.BlockSpec((None, 1, D), lambda i: (i // tiles_per_batch, 0, 0)),
                  pl.BlockSpec((1, D), lambda i: (0, 0))],
        out_specs=pl.BlockSpec((tm, D), lambda i: (i, 0)),
        out_shape=jax.ShapeDtypeStruct((T, D), F32),
        scratch_shapes=[pltpu.VMEM((2, tm, D), F32), pltpu.SemaphoreType.DMA((COMBINE_PARTS,))],
        compiler_params=_cparams(("arbitrary",)),
        name="moe_combine",
    )(dest, y, gates, h, gt, gfin)


def _moe_plan(eids, ranks, counts, *, tm, n_tiles):
    padded = (counts + tm - 1) // tm * tm
    ends = jnp.cumsum(padded)
    starts = ends - padded
    dest = ranks
    for e in range(N_EXPERTS):
        dest = dest + jnp.where(eids == e, starts[e], 0)
    tile_start = jnp.arange(n_tiles, dtype=jnp.int32) * tm
    tile_expert = jnp.minimum(jnp.sum(tile_start[:, None] >= ends[None, :], axis=1), N_EXPERTS - 1)
    n_active = (ends[-1] // tm).reshape(1)
    pad_rows = jnp.stack([starts + counts, ends])
    return dest.astype(jnp.int32), tile_expert.astype(jnp.int32), n_active.astype(jnp.int32), pad_rows.astype(jnp.int32)


def _mla_weights(w_in, w_q_up, w_kv_up):
    D = w_in.shape[0]
    w_lat = w_in[:, :MLA_Q_RANK + MLA_KV_RANK]
    w_kr = w_in[:, MLA_Q_RANK + MLA_KV_RANK:]
    z64 = jnp.zeros((D, MLA_NOPE), F32)
    z32 = jnp.zeros((D, HEAD_PAD - MLA_QK), F32)
    w1 = jnp.concatenate([w_lat, z64, w_kr, z32], axis=1).astype(BF16)

    wq = w_q_up.reshape(MLA_Q_RANK, MLA_HEADS, MLA_QK)
    nope, rope = wq[..., :MLA_NOPE], wq[..., MLA_NOPE:]
    zq = jnp.zeros((MLA_Q_RANK, MLA_HEADS, HEAD_PAD - MLA_QK), F32)
    wq = jnp.concatenate([nope, rope, zq], axis=-1).reshape(MLA_Q_RANK, MLA_HEADS * HEAD_PAD).astype(BF16)

    wkv = w_kv_up.reshape(MLA_KV_RANK, MLA_HEADS, MLA_NOPE + MLA_V)
    k_nope, v = wkv[..., :MLA_NOPE], wkv[..., MLA_NOPE:]
    zk = jnp.zeros((MLA_KV_RANK, MLA_HEADS, HEAD_PAD - MLA_NOPE), F32)
    wka = jnp.concatenate([k_nope, zk], axis=-1).reshape(MLA_KV_RANK, MLA_HEADS * HEAD_PAD).astype(BF16)
    wvt = v.reshape(MLA_KV_RANK, MLA_HEADS * MLA_V).T.astype(BF16)
    return w1, wq, wka, wvt


def _rope_tables(positions):
    inv_freq = ROPE_THETA ** (-jnp.arange(0, MLA_ROPE, 2, dtype=F32) / MLA_ROPE)
    ang = positions.astype(F32).reshape(-1, 1) * inv_freq
    cos, sin = jnp.cos(ang), jnp.sin(ang)
    T = ang.shape[0]
    one = jnp.ones((T, MLA_NOPE), F32)
    z64 = jnp.zeros((T, MLA_NOPE), F32)
    z32 = jnp.zeros((T, HEAD_PAD - MLA_QK), F32)
    return (jnp.concatenate([one, cos, cos, z32], axis=1),
            jnp.concatenate([z64, -sin, sin, z32], axis=1))


def kernel(x, c, positions, g_mix, g_ffn, w_ada, b_ada, w_mla_in, g_mla_q, w_mla_q_up, g_mla_kv, w_mla_kv_up,
           w_mla_out, g_kv_b, w_ada_kv, b_ada_kv, w_kv_b, rpb_table, w_q_b, w_o_b, w_ffn_gu, w_ffn_down,
           w_router, w_exp_gu, w_exp_down, g_final):
    B, S, D = x.shape
    T = B * S
    h = x.reshape(T, D)

    c8 = jnp.zeros((8, D), F32).at[:B].set(c)
    mod = _ada(c8, w_ada, b_ada[:, None, :])[:, :B]
    mod = mod.reshape(2, B, N_MOD, 1, D)
    mod_kv = _ada(c8, w_ada_kv[None], b_ada_kv[None, None, :])[0, :B].reshape(B, 2, 1, D)
    sh_kv, sc_kv = mod_kv[:, 0], mod_kv[:, 1]

    def mods(layer):
        return [mod[layer, :, k] for k in range(N_MOD)]

    row = lambda v: v.reshape(1, -1)

    sh_m, sc_m, gt_m, sh_f, sc_f, gt_f = mods(0)
    w1, wq, wka, wvt = _mla_weights(w_mla_in[0], w_mla_q_up[0], w_mla_kv_up[0])
    cos_t, sin_t = _rope_tables(positions)
    q, k, vt = _mla_proj(h, row(g_mix[0]), sc_m, sh_m, cos_t, sin_t, w1, row(g_mla_q[0]), row(g_mla_kv[0]),
                         wq, wka, wvt, tm=TM_MLA_PROJ, batch=B, seq=S)
    o = _mla_attn(q.reshape(B, S, -1), k.reshape(B, S, -1), vt, batch=B, seq=S, tq=MLA_TQ)

    h = _ffn(o.reshape(T, -1), w_mla_out[0].astype(BF16), h, gt_m, row(g_ffn[0]), sc_f, sh_f, gt_f,
             w_ffn_gu[0].astype(BF16), w_ffn_down[0].astype(BF16), tm=TM_FFN, tf=FF_CHUNK,
             tiles_per_batch=S // TM_FFN)

    sh_m, sc_m, gt_m, sh_f, sc_f, gt_f = mods(1)
    qkv = _l1_proj(h, row(g_kv_b), sc_kv, sh_kv, row(g_mix[1]), sc_m, sh_m,
                   w_kv_b.astype(BF16), w_q_b[0].astype(BF16), tm=TM_L1_PROJ, tiles_per_batch=S // TM_L1_PROJ)

    outs, lses = [], []
    for g, (window, dil) in enumerate(DIL_PATTERNS):
        bias = _dil_bias(rpb_table[:, g * DIL_HEADS:(g + 1) * DIL_HEADS], window // dil, dil)
        o_g, lse_g = _dil_attn(*qkv[3 * g:3 * g + 3], bias, group=g, dil=dil, batch=B, seq=S)
        outs.append(o_g)
        lses.append(lse_g)
    h = _dil_out(outs, lses, w_o_b[0].astype(BF16), h, gt_m, tm=TM_DIL_OUT, tiles_per_batch=S // TM_DIL_OUT)

    hn, eids, ranks, gates, counts = _route(h, row(g_ffn[1]), sc_f, sh_f, w_router[0].T, tm=TM_ROUTE,
                                            tiles_per_batch=S // TM_ROUTE)
    n_tiles = (T * 2) // MOE_TM + N_EXPERTS
    dest, tile_expert, n_active, pad_rows = _moe_plan(eids, ranks, counts[:, 0], tm=MOE_TM, n_tiles=n_tiles)
    xs = _dispatch(pad_rows, dest, hn, n_slots=n_tiles * MOE_TM, tm=TM_DISPATCH)
    y = _experts(tile_expert, n_active, xs, w_exp_gu[0].astype(BF16), w_exp_down[0].astype(BF16),
                 tm=MOE_TM, tf=FF_CHUNK)
    out = _combine(dest, y, gates.T, h, gt_f, row(g_final), tm=TM_COMBINE, tiles_per_batch=S // TM_COMBINE)
    return out.reshape(B, S, D)
```

```python
import functools
import math

import jax
import jax.numpy as jnp
from jax import lax
from jax.experimental import pallas as pl
from jax.experimental.pallas import tpu as pltpu

D_MODEL = 1024
N_MOD = 6
EPS = 1e-6

MLA_HEADS = 16
MLA_Q_RANK = 384
MLA_KV_RANK = 256
MLA_NOPE = 64
MLA_ROPE = 32
MLA_V = 64
MLA_QK = MLA_NOPE + MLA_ROPE
ROPE_THETA = 10000.0
LANES = 128
ONES_ROWS = 16
STRIP = 256
HEAD_PAD = 128

DIL_PATTERNS = ((128, 1), (512, 4), (2048, 16))
DIL_GROUPS = len(DIL_PATTERNS)
DIL_HEADS = 8
DIL_HEAD_DIM = 64
DIL_BLOCK = 128
DIL_SUBBLOCKS = 8
DIL_W = DIL_HEADS * DIL_HEAD_DIM
RPB_BUCKETS = 32
RPB_MAX_DIST = 2048

D_FF = 2816
N_EXPERTS = 8
MOE_TM = 512

TM_MLA_PROJ = 1024
TM_FFN = 512
TM_L1_PROJ = 1024
TM_DIL_OUT = 1024
TM_ROUTE = 1024
TM_DISPATCH = 1024
TM_COMBINE = 512
COMBINE_PARTS = 4
MLA_TQ = 1024
FF_CHUNK = 256

MASK_VALUE = -1e30
LOG2E = math.log2(math.e)

F32 = jnp.float32
BF16 = jnp.bfloat16

VMEM_LIMIT = 56 * 1024 * 1024


def _cparams(sem):
    return pltpu.CompilerParams(dimension_semantics=sem, vmem_limit_bytes=VMEM_LIMIT)


def _dot(a, b):
    return jnp.dot(a, b, preferred_element_type=F32)


def _dot_nt(a, b):
    return lax.dot_general(a, b, (((1,), (1,)), ((), ())), preferred_element_type=F32)


def _rms_scale(x):
    return lax.rsqrt(jnp.mean(x * x, axis=-1, keepdims=True) + EPS)


def _normmod(x, g, sc, sh):
    return (x * _rms_scale(x)) * (g * (1.0 + sc)) + sh


def _silu(x):
    return x * (1.0 / (1.0 + jnp.exp(-x)))


def _ada_kernel(c_ref, w_ref, b_ref, o_ref):
    c = c_ref[...]
    o_ref[...] = jnp.dot(_silu(c), w_ref[...], preferred_element_type=F32,
                         precision=lax.Precision.HIGHEST) + b_ref[...]


def _ada(c8, w, b, tn=512):
    L, D, N = w.shape
    return pl.pallas_call(
        _ada_kernel,
        grid=(L, N // tn),
        in_specs=[pl.BlockSpec((8, D), lambda l, j: (0, 0)),
                  pl.BlockSpec((None, D, tn), lambda l, j: (l, 0, j)),
                  pl.BlockSpec((None, 1, tn), lambda l, j: (l, 0, j))],
        out_specs=pl.BlockSpec((None, 8, tn), lambda l, j: (l, 0, j)),
        out_shape=jax.ShapeDtypeStruct((L, 8, N), F32),
        compiler_params=_cparams(("parallel", "parallel")),
        name="ada_mod",
    )(c8, w, b)


def _swap_rope_halves(x, first_half):
    width = x.shape[1]
    return jnp.where(first_half, pltpu.roll(x, width - MLA_ROPE // 2, axis=1), pltpu.roll(x, MLA_ROPE // 2, axis=1))


def _mla_proj_kernel(h_ref, g_ref, sc_ref, sh_ref, cos_ref, sin_ref, w1_ref, gq_ref, gkv_ref,
                     wq_ref, wka_ref, wvt_ref, q_ref, k_ref, vt_ref, *, qscale):
    hn = _normmod(h_ref[...], g_ref[...], sc_ref[...], sh_ref[...]).astype(BF16)
    z = _dot(hn, w1_ref[...])
    ql = z[:, :MLA_Q_RANK]
    kvl = z[:, MLA_Q_RANK:MLA_Q_RANK + MLA_KV_RANK]
    kr = z[:, MLA_Q_RANK + MLA_KV_RANK:MLA_Q_RANK + MLA_KV_RANK + HEAD_PAD]
    krs = z[:, MLA_Q_RANK + MLA_KV_RANK + HEAD_PAD:]
    qn = (ql * _rms_scale(ql) * gq_ref[...]).astype(BF16)
    kvn = (kvl * _rms_scale(kvl) * gkv_ref[...]).astype(BF16)
    cos = cos_ref[...]
    sin = sin_ref[...]
    lane = lax.broadcasted_iota(jnp.int32, (1, 2 * HEAD_PAD), 1) % HEAD_PAD
    first_half = lane < MLA_NOPE + MLA_ROPE // 2
    k_rope = kr * cos + krs * sin
    cos2 = jnp.concatenate([cos, cos], axis=1)
    sin2 = jnp.concatenate([sin, sin], axis=1)
    k_rope2 = jnp.concatenate([k_rope, k_rope], axis=1)
    for hp in range(MLA_HEADS // 2):
        sl = slice(2 * HEAD_PAD * hp, 2 * HEAD_PAD * (hp + 1))
        a = _dot(qn, wq_ref[:, sl])
        q_ref[:, sl] = ((a * cos2 + _swap_rope_halves(a, first_half) * sin2) * qscale).astype(BF16)
        k_ref[:, sl] = (_dot(kvn, wka_ref[:, sl]) + k_rope2).astype(BF16)
    vt_ref[...] = _dot_nt(wvt_ref[...], kvn).astype(BF16)


def _mla_proj(h, g, sc, sh, cos_t, sin_t, w1, gq, gkv, wq, wka, wvt, *, tm, batch, seq):
    T, D = h.shape
    HP = MLA_HEADS * HEAD_PAD
    tpb = seq // tm
    row = lambda n: pl.BlockSpec((tm, n), lambda i: (i, 0))
    bvec = pl.BlockSpec((None, 1, D), lambda i: (i // tpb, 0, 0))
    full = lambda a: pl.BlockSpec(a.shape, lambda i: (0,) * a.ndim)
    return pl.pallas_call(
        functools.partial(_mla_proj_kernel, qscale=(MLA_QK ** -0.5) * LOG2E),
        grid=(T // tm,),
        in_specs=[row(D), full(g), bvec, bvec, row(HEAD_PAD), row(HEAD_PAD), full(w1), full(gq), full(gkv),
                  full(wq), full(wka), full(wvt)],
        out_specs=[row(HP), row(HP),
                   pl.BlockSpec((None, MLA_HEADS * MLA_V, tm), lambda i: (i // tpb, 0, i % tpb))],
        out_shape=[jax.ShapeDtypeStruct((T, HP), BF16), jax.ShapeDtypeStruct((T, HP), BF16),
                   jax.ShapeDtypeStruct((batch, MLA_HEADS * MLA_V, seq), BF16)],
        compiler_params=_cparams(("parallel",)),
        name="mla_proj",
    )(h, g, sc, sh, cos_t, sin_t, w1, gq, gkv, wq, wka, wvt)


def _mla_attn_kernel(q_ref, k_ref, vt_ref, o_ref, m_sc, acc_sc, s_sc, mb_sc, *, tq):
    qi = pl.program_id(2)
    tk = tq
    m_sc[...] = jnp.full(m_sc.shape, -jnp.inf, F32)
    acc_sc[...] = jnp.zeros(acc_sc.shape, F32)

    def scores(j, diagonal, slot, hh, n):
        rows = (n + 1) * STRIP if diagonal else tk
        off = pl.multiple_of(j * tk, tk)
        cols = slice(n * STRIP, (n + 1) * STRIP)
        q = q_ref[cols, HEAD_PAD * hh:HEAD_PAD * (hh + 1)]
        k = k_ref[pl.ds(off, rows), HEAD_PAD * hh:HEAD_PAD * (hh + 1)]
        s = _dot_nt(k, q)
        if diagonal:
            key = lax.broadcasted_iota(jnp.int32, (rows, STRIP), 0)
            qry = lax.broadcasted_iota(jnp.int32, (rows, STRIP), 1) + n * STRIP
            s = jnp.where(key <= qry, s, -jnp.inf)
        s_sc[slot, hh, :rows, cols] = s
        m_blk = jnp.max(jnp.max(s.reshape(rows // 8, 8, STRIP), axis=0), axis=0, keepdims=True)
        mb_sc[slot, hh, :, cols] = jnp.broadcast_to(m_blk, (8, STRIP))

    def softmax_values(j, diagonal, slot, hh, n):
        rows = (n + 1) * STRIP if diagonal else tk
        off = pl.multiple_of(j * tk, tk)
        cols = slice(n * STRIP, (n + 1) * STRIP)
        s3 = s_sc[slot, hh, :rows, cols].reshape(rows // 8, 8, STRIP)
        m_prev = m_sc[hh, :, cols]
        m_new = jnp.maximum(m_prev, mb_sc[slot, hh, :, cols])
        alpha = jnp.exp2(m_prev - m_new)
        p = jnp.exp2(s3 - m_new[None]).reshape(rows, STRIP).astype(BF16)
        vt = vt_ref[MLA_V * hh:MLA_V * (hh + 1), pl.ds(off, rows)]
        vt1 = jnp.concatenate([vt, jnp.ones((ONES_ROWS, rows), BF16)], axis=0)
        acc_sc[hh, :, cols] = alpha[:1] * acc_sc[hh, :, cols] + _dot(vt1, p)
        m_sc[hh, :, cols] = m_new

    strips = [(hh, n) for hh in range(2) for n in range(tq // STRIP)]

    def step(j, even, diagonal_next=False, has_next=True):
        cur, nxt = (0, 1) if even else (1, 0)
        for hh, n in strips:
            if has_next:
                scores(j + 1, diagonal_next, nxt, hh, n)
            softmax_values(j, not has_next, cur, hh, n)

    def first_scores(diagonal):
        for hh, n in strips:
            scores(0, diagonal, 0, hh, n)

    @pl.when(qi == 0)
    def _():
        first_scores(True)

    @pl.when(qi > 0)
    def _():
        first_scores(False)

        def pair(t, carry):
            step(2 * t, True)
            step(2 * t + 1, False)
            return carry

        lax.fori_loop(0, (qi - 1) // 2, pair, 0)

    @pl.when(qi % 2 == 1)
    def _():
        step(qi - 1, True, diagonal_next=True)
        step(qi, False, has_next=False)

    @pl.when((qi % 2 == 0) & (qi > 0))
    def _():
        step(qi - 2, True)
        step(qi - 1, False, diagonal_next=True)

    @pl.when(qi % 2 == 0)
    def _():
        step(qi, True, has_next=False)

    o_t = jnp.concatenate([acc_sc[hh, :MLA_V] / acc_sc[hh, MLA_V:MLA_V + 1] for hh in range(2)], axis=0)
    o_ref[...] = o_t.T.astype(BF16)


def _mla_attn(q, k, vt, *, batch, seq, tq):
    return pl.pallas_call(
        functools.partial(_mla_attn_kernel, tq=tq),
        grid=(batch, MLA_HEADS // 2, seq // tq),
        in_specs=[pl.BlockSpec((None, tq, 2 * HEAD_PAD), lambda b, hp, i: (b, i, hp)),
                  pl.BlockSpec((None, seq, 2 * HEAD_PAD), lambda b, hp, i: (b, 0, hp)),
                  pl.BlockSpec((None, 2 * MLA_V, seq), lambda b, hp, i: (b, hp, 0))],
        out_specs=pl.BlockSpec((None, tq, 2 * MLA_V), lambda b, hp, i: (b, i, hp)),
        out_shape=jax.ShapeDtypeStruct((batch, seq, MLA_HEADS * MLA_V), BF16),
        scratch_shapes=[pltpu.VMEM((2, 8, tq), F32), pltpu.VMEM((2, MLA_V + ONES_ROWS, tq), F32),
                        pltpu.VMEM((2, 2, tq, tq), F32),
                        pltpu.VMEM((2, 2, 8, tq), F32)],
        compiler_params=_cparams(("parallel", "parallel", "arbitrary")),
        name="mla_attn",
    )(q, k, vt)


def _swiglu_tile(x, wgu_ref, wd_ref, acc_sc, tf):
    for f in range(D_FF // tf):
        g = _dot(x, wgu_ref[:, f * tf:(f + 1) * tf])
        u = _dot(x, wgu_ref[:, D_FF + f * tf:D_FF + (f + 1) * tf])
        part = _dot((_silu(g) * u).astype(BF16), wd_ref[f * tf:(f + 1) * tf, :])
        if f == 0:
            acc_sc[...] = part
        else:
            acc_sc[...] += part


def _ffn_kernel(a_ref, wo_ref, h_ref, gtm_ref, g_ref, sc_ref, sh_ref, gt_ref, wgu_ref, wd_ref, o_ref, acc_sc,
                *, tf):
    h = h_ref[...] + gtm_ref[...] * _dot(a_ref[...], wo_ref[...])
    x = _normmod(h, g_ref[...], sc_ref[...], sh_ref[...]).astype(BF16)
    _swiglu_tile(x, wgu_ref, wd_ref, acc_sc, tf)
    o_ref[...] = h + gt_ref[...] * acc_sc[...]


def _ffn(a, w_o, h, gt_m, g, sc, sh, gt, w_gu, w_d, *, tm, tf, tiles_per_batch):
    T, D = h.shape
    bvec = pl.BlockSpec((None, 1, D), lambda i: (i // tiles_per_batch, 0, 0))
    row = pl.BlockSpec((tm, D), lambda i: (i, 0))
    const = lambda w: pl.BlockSpec(w.shape, lambda i: (0,) * w.ndim, pipeline_mode=pl.Buffered(1))
    return pl.pallas_call(
        functools.partial(_ffn_kernel, tf=tf),
        grid=(T // tm,),
        in_specs=[row, const(w_o), row, bvec, const(g), bvec, bvec, bvec, const(w_gu), const(w_d)],
        out_specs=row,
        out_shape=jax.ShapeDtypeStruct((T, D), F32),
        scratch_shapes=[pltpu.VMEM((tm, D), F32)],
        compiler_params=_cparams(("parallel",)),
        name="ffn_dense",
    )(a, w_o, h, gt_m, g, sc, sh, gt, w_gu, w_d)


def _l1_proj_kernel(h_ref, gkv_ref, sckv_ref, shkv_ref, gq_ref, scq_ref, shq_ref, wkv_ref, wq_ref, *refs):
    out_refs, stage_sc = refs[:-1], refs[-1]
    tm = h_ref.shape[0]
    x = h_ref[...]
    xh = x * _rms_scale(x)
    hn_kv = (xh * (gkv_ref[...] * (1.0 + sckv_ref[...])) + shkv_ref[...]).astype(BF16)
    hn_q = (xh * (gq_ref[...] * (1.0 + scq_ref[...])) + shq_ref[...]).astype(BF16)
    for g, (_, dil) in enumerate(DIL_PATTERNS):
        ys = (_dot(hn_q, wq_ref[:, g * DIL_W:(g + 1) * DIL_W]) * (DIL_HEAD_DIM ** -0.5),
              _dot(hn_kv, wkv_ref[:, 2 * g * DIL_W:(2 * g + 1) * DIL_W]),
              _dot(hn_kv, wkv_ref[:, (2 * g + 1) * DIL_W:(2 * g + 2) * DIL_W]))
        for y, out_ref in zip(ys, out_refs[3 * g:3 * g + 3]):
            if dil == 1:
                out_ref[...] = y.astype(BF16)
            else:
                for c in range(DIL_W // LANES):
                    stage_sc[c] = y[:, c * LANES:(c + 1) * LANES]
                for r in range(dil):
                    for c in range(DIL_W // LANES):
                        col = r * DIL_W + c * LANES
                        out_ref[:, col:col + LANES] = stage_sc[c, pl.ds(r, tm // dil, stride=dil), :].astype(BF16)


def _l1_proj(h, gkv, sckv, shkv, gq, scq, shq, wkv, wq, *, tm, tiles_per_batch):
    T, D = h.shape
    bvec = pl.BlockSpec((None, 1, D), lambda i: (i // tiles_per_batch, 0, 0))
    full = lambda a: pl.BlockSpec(a.shape, lambda i: (0,) * a.ndim)
    out_specs, out_shape = [], []
    for _, dil in DIL_PATTERNS:
        for _ in range(3):
            out_specs.append(pl.BlockSpec((tm // dil, dil * DIL_W), lambda i: (i, 0)))
            out_shape.append(jax.ShapeDtypeStruct((T // dil, dil * DIL_W), BF16))
    return pl.pallas_call(
        _l1_proj_kernel,
        grid=(T // tm,),
        in_specs=[pl.BlockSpec((tm, D), lambda i: (i, 0)), full(gkv), bvec, bvec, full(gq), bvec, bvec,
                  full(wkv), full(wq)],
        out_specs=out_specs,
        out_shape=out_shape,
        scratch_shapes=[pltpu.VMEM((DIL_W // LANES, tm, LANES), F32)],
        compiler_params=_cparams(("parallel",)),
        name="l1_proj",
    )(h, gkv, sckv, shkv, gq, scq, shq, wkv, wq)


def _dil_attn_kernel(q_ref, kc_ref, kp_ref, vc_ref, vp_ref, bias_ref, o_ref, lse_ref):
    i = pl.program_id(2)
    nsub = q_ref.shape[0] // DIL_BLOCK
    lane = lax.broadcasted_iota(jnp.int32, (1, 2 * DIL_HEAD_DIM), 1)
    lo = lane < DIL_HEAD_DIM
    col = lax.broadcasted_iota(jnp.int32, (1, 2 * DIL_BLOCK), 1)
    edge = jnp.where((col < DIL_BLOCK) & (i == 0), MASK_VALUE, 0.0).astype(F32)
    sls = [slice(2 * DIL_HEAD_DIM * hp, 2 * DIL_HEAD_DIM * (hp + 1)) for hp in range(DIL_HEADS // 2)]
    for sub in range(nsub):
        rows = slice(DIL_BLOCK * sub, DIL_BLOCK * (sub + 1))
        before = slice(DIL_BLOCK * (sub - 1), DIL_BLOCK * sub)
        scores = []
        for hp, sl in enumerate(sls):
            q2 = q_ref[rows, sl]
            zero = jnp.zeros_like(q2)
            qs = jnp.concatenate([jnp.where(lo, q2, zero), jnp.where(lo, zero, q2)], axis=0)
            k_prev = kp_ref[:, sl] if sub == 0 else kc_ref[before, sl]
            k2 = jnp.concatenate([k_prev, kc_ref[rows, sl]], axis=0)
            bias = jnp.concatenate([bias_ref[2 * hp], bias_ref[2 * hp + 1]], axis=0)
            s = _dot_nt(qs, k2) + bias
            scores.append(s + edge if sub == 0 else s)
        probs, stats = [], []
        for s in scores:
            m = jnp.max(s, axis=1, keepdims=True)
            p = jnp.exp(s - m)
            probs.append(p.astype(BF16))
            stats.append((m, jnp.sum(p, axis=1, keepdims=True)))
        for sl, p, (m, l) in zip(sls, probs, stats):
            v_prev = vp_ref[:, sl] if sub == 0 else vc_ref[before, sl]
            v2 = jnp.concatenate([v_prev, vc_ref[rows, sl]], axis=0)
            o = _dot(p, v2) / l
            lse = jnp.broadcast_to(m + jnp.log(l), o.shape)
            o_ref[rows, sl] = jnp.where(lo, o[:DIL_BLOCK], o[DIL_BLOCK:])
            lse_ref[rows, sl] = jnp.where(lo, lse[:DIL_BLOCK], lse[DIL_BLOCK:])


def _dil_attn(q, k, v, bias, *, group, dil, batch, seq):
    n = seq // dil
    nsub = min(DIL_SUBBLOCKS, n // DIL_BLOCK)
    run = nsub * DIL_BLOCK
    assert n % run == 0, (n, run)
    q, k, v = (a.reshape(batch, n, dil * DIL_W) for a in (q, k, v))
    cur = pl.BlockSpec((None, run, DIL_W), lambda b, r, i: (b, i, r))
    prev = pl.BlockSpec((None, DIL_BLOCK, DIL_W), lambda b, r, i: (b, jnp.maximum(i * nsub - 1, 0), r))
    out_sd = jax.ShapeDtypeStruct((batch, n, dil * DIL_W), F32)
    o, lse = pl.pallas_call(
        _dil_attn_kernel,
        grid=(batch, dil, n // run),
        in_specs=[cur, cur, prev, cur, prev, pl.BlockSpec(bias.shape, lambda b, r, i: (0, 0, 0))],
        out_specs=[cur, cur],
        out_shape=[out_sd, out_sd],
        compiler_params=_cparams(("parallel", "parallel", "arbitrary")),
        name=f"dil_attn_g{group}",
    )(q, k, k, v, v, bias)
    return o.reshape(batch * n, dil * DIL_W), lse.reshape(batch * n, dil * DIL_W)


def _rpb_bucket(dist):
    exact = RPB_BUCKETS // 2
    d = jnp.maximum(dist, 0)
    d_f = jnp.maximum(d, 1).astype(F32)
    large = exact + (jnp.log(d_f / exact) / math.log(RPB_MAX_DIST / exact)
                     * (RPB_BUCKETS - exact)).astype(jnp.int32)
    return jnp.where(d < exact, d, jnp.minimum(large, RPB_BUCKETS - 1))


def _dil_bias(table, span, dil):
    period = 3 * DIL_BLOCK
    j = jnp.arange(period)
    k_minus_q = jnp.where(j < 2 * DIL_BLOCK, j, j - period)
    dist = DIL_BLOCK - k_minus_q
    band = (dist >= 0) & (dist <= span)
    prof = jnp.where(band[:, None], table[_rpb_bucket(dist * dil)].astype(F32), MASK_VALUE).T
    skew = jnp.tile(prof, (1, DIL_BLOCK))[:, :DIL_BLOCK * (period - 1)]
    return skew.reshape(DIL_HEADS, DIL_BLOCK, period - 1)[:, :, :2 * DIL_BLOCK]


def _dil_out_kernel(o0_ref, o1_ref, o2_ref, l0_ref, l1_ref, l2_ref, w_ref, h_ref, gt_ref, out_ref, *stages):
    tm = h_ref.shape[0]
    stages = list(stages)

    def token_major(ref, dil):
        if dil == 1:
            return ref[...]
        stage = stages.pop()
        for r in range(dil):
            for c in range(DIL_W // LANES):
                col = r * DIL_W + c * LANES
                stage[c, pl.ds(r, tm // dil, stride=dil), :] = ref[:, col:col + LANES]
        return jnp.concatenate([stage[c] for c in range(DIL_W // LANES)], axis=1)

    dils = [dil for _, dil in DIL_PATTERNS]
    l0, l1, l2 = (token_major(ref, d) for ref, d in zip((l0_ref, l1_ref, l2_ref), dils))
    o0, o1, o2 = (token_major(ref, d) for ref, d in zip((o0_ref, o1_ref, o2_ref), dils))
    m = jnp.maximum(jnp.maximum(l0, l1), l2)
    w0, w1, w2 = jnp.exp(l0 - m), jnp.exp(l1 - m), jnp.exp(l2 - m)
    o = (o0 * w0 + o1 * w1 + o2 * w2) / (w0 + w1 + w2)
    out_ref[...] = h_ref[...] + gt_ref[...] * _dot(o.astype(BF16), w_ref[...])


def _dil_out(os_, ls_, w, h, gt, *, tm, tiles_per_batch):
    T, D = h.shape
    row = lambda n: pl.BlockSpec((tm, n), lambda i: (i, 0))
    grp = [pl.BlockSpec((tm // dil, dil * DIL_W), lambda i: (i, 0)) for _, dil in DIL_PATTERNS]
    n_stage = 2 * sum(1 for _, dil in DIL_PATTERNS if dil > 1)
    return pl.pallas_call(
        _dil_out_kernel,
        grid=(T // tm,),
        in_specs=grp + grp + [pl.BlockSpec(w.shape, lambda i: (0, 0)), row(D),
                              pl.BlockSpec((None, 1, D), lambda i: (i // tiles_per_batch, 0, 0))],
        out_specs=row(D),
        out_shape=jax.ShapeDtypeStruct((T, D), F32),
        scratch_shapes=[pltpu.VMEM((DIL_W // LANES, tm, LANES), F32)] * n_stage,
        compiler_params=_cparams(("parallel",)),
        name="dil_out",
    )(*os_, *ls_, w, h, gt)


def _route_kernel(h_ref, g_ref, sc_ref, sh_ref, wrt_ref, hn_ref, eid_ref, rank_ref, gate_ref, cnt_ref, carry_sc):
    @pl.when(pl.program_id(0) == 0)
    def _():
        carry_sc[...] = jnp.zeros(carry_sc.shape, F32)

    hn = _normmod(h_ref[...], g_ref[...], sc_ref[...], sh_ref[...])
    hn_ref[...] = hn
    tm = hn.shape[0]
    logits = lax.dot_general(wrt_ref[...], hn, (((1,), (1,)), ((), ())), preferred_element_type=F32,
                             precision=lax.Precision.HIGHEST)
    idx = lax.broadcasted_iota(jnp.int32, logits.shape, 0)
    v1 = jnp.max(logits, axis=0, keepdims=True)
    i1 = jnp.min(jnp.where(logits == v1, idx, N_EXPERTS), axis=0, keepdims=True)
    rest = jnp.where(idx == i1, -jnp.inf, logits)
    v2 = jnp.max(rest, axis=0, keepdims=True)
    i2 = jnp.min(jnp.where(rest == v2, idx, N_EXPERTS), axis=0, keepdims=True)
    e = jnp.exp(v2 - v1)
    gate_ref[...] = jnp.concatenate([1.0 / (1.0 + e), e / (1.0 + e)], axis=0)
    eid_ref[...] = jnp.concatenate([i1, i2], axis=0)

    sel = ((idx == i1) | (idx == i2)).astype(BF16)
    before = (lax.broadcasted_iota(jnp.int32, (tm, tm), 0)
              < lax.broadcasted_iota(jnp.int32, (tm, tm), 1)).astype(BF16)
    rank_all = carry_sc[:, :1] + _dot(sel, before)
    r1 = jnp.sum(jnp.where(idx == i1, rank_all, 0.0), axis=0, keepdims=True)
    r2 = jnp.sum(jnp.where(idx == i2, rank_all, 0.0), axis=0, keepdims=True)
    rank_ref[...] = jnp.concatenate([r1, r2], axis=0).astype(jnp.int32)
    carry_sc[...] += jnp.sum(sel.astype(F32), axis=1, keepdims=True)
    cnt_ref[...] = carry_sc[...].astype(jnp.int32)


def _route(h, g, sc, sh, wrt, *, tm, tiles_per_batch):
    T, D = h.shape
    bvec = pl.BlockSpec((None, 1, D), lambda i: (i // tiles_per_batch, 0, 0))
    lane_blk = pl.BlockSpec((2, tm), lambda i: (0, i))
    return pl.pallas_call(
        _route_kernel,
        grid=(T // tm,),
        in_specs=[pl.BlockSpec((tm, D), lambda i: (i, 0)), pl.BlockSpec((1, D), lambda i: (0, 0)), bvec, bvec,
                  pl.BlockSpec(wrt.shape, lambda i: (0, 0))],
        out_specs=[pl.BlockSpec((tm, D), lambda i: (i, 0)), lane_blk, lane_blk, lane_blk,
                   pl.BlockSpec((N_EXPERTS, HEAD_PAD), lambda i: (0, 0))],
        out_shape=[jax.ShapeDtypeStruct((T, D), F32), jax.ShapeDtypeStruct((2, T), jnp.int32),
                   jax.ShapeDtypeStruct((2, T), jnp.int32), jax.ShapeDtypeStruct((2, T), F32),
                   jax.ShapeDtypeStruct((N_EXPERTS, HEAD_PAD), jnp.int32)],
        scratch_shapes=[pltpu.VMEM((N_EXPERTS, HEAD_PAD), F32)],
        compiler_params=_cparams(("arbitrary",)),
        name="moe_route",
    )(h, g, sc, sh, wrt)


def _dispatch_kernel(pad_ref, dest_ref, hn_ref, xs_ref, zero_sc, sem, zsem):
    tm = hn_ref.shape[0]

    def row_copy(r, k):
        return pltpu.make_async_copy(hn_ref.at[pl.ds(r, 1), :], xs_ref.at[pl.ds(dest_ref[k, r], 1), :], sem)

    def issue(r, carry):
        row_copy(r, 0).start(priority=0)
        row_copy(r, 1).start(priority=1)
        return carry

    for r in range(tm):
        issue(r, 0)

    @pl.when(pl.program_id(0) == pl.num_programs(0) - 1)
    def _():
        zero_sc[...] = jnp.zeros(zero_sc.shape, F32)

        def pad_copy(p):
            return pltpu.make_async_copy(zero_sc.at[pl.ds(0, 1), :], xs_ref.at[pl.ds(p, 1), :], zsem)

        for e in range(N_EXPERTS):
            lo, hi = pad_ref[0, e], pad_ref[1, e]

            def zissue(p, carry):
                pad_copy(p).start()
                return carry

            def zwait(p, carry):
                pad_copy(p).wait()
                return carry

            lax.fori_loop(lo, hi, zissue, 0)
            lax.fori_loop(lo, hi, zwait, 0)

        zrows = zero_sc.shape[0]

        def tail_copy(c):
            return pltpu.make_async_copy(zero_sc, xs_ref.at[pl.ds(pl.multiple_of(c * zrows, zrows), zrows), :], zsem)

        def tissue(c, carry):
            tail_copy(c).start()
            return carry

        def twait(c, carry):
            tail_copy(c).wait()
            return carry

        lo, hi = pad_ref[1, N_EXPERTS - 1] // zrows, xs_ref.shape[0] // zrows
        lax.fori_loop(lo, hi, tissue, 0)
        lax.fori_loop(lo, hi, twait, 0)

    for _ in range(2):
        pltpu.make_async_copy(hn_ref, xs_ref.at[pl.ds(0, tm), :], sem).wait()


def _dispatch(pad_rows, dest, hn, *, n_slots, tm):
    T, D = hn.shape
    return pl.pallas_call(
        _dispatch_kernel,
        grid_spec=pltpu.PrefetchScalarGridSpec(
            num_scalar_prefetch=1,
            grid=(T // tm,),
            in_specs=[pl.BlockSpec((2, tm), lambda i, pad: (0, i), memory_space=pltpu.SMEM),
                      pl.BlockSpec((tm, D), lambda i, pad: (i, 0))],
            out_specs=pl.BlockSpec(memory_space=pl.ANY),
            scratch_shapes=[pltpu.VMEM((64, D), F32), pltpu.SemaphoreType.DMA, pltpu.SemaphoreType.DMA]),
        out_shape=jax.ShapeDtypeStruct((n_slots, D), F32),
        compiler_params=_cparams(("arbitrary",)),
        name="moe_dispatch",
    )(pad_rows, dest, hn)


def _experts_kernel(te_ref, na_ref, xs_ref, wgu_ref, wd_ref, y_ref, acc_sc, *, tf):
    @pl.when(pl.program_id(0) < na_ref[0])
    def _():
        _swiglu_tile(xs_ref[...].astype(BF16), wgu_ref, wd_ref, acc_sc, tf)
        y_ref[...] = acc_sc[...]

    @pl.when(pl.program_id(0) >= na_ref[0])
    def _():
        y_ref[...] = jnp.zeros(y_ref.shape, F32)


def _experts(tile_expert, n_active, xs, w_gu, w_d, *, tm, tf):
    P, D = xs.shape

    def tile(j, te, na):
        return jnp.minimum(j, na[0] - 1)

    return pl.pallas_call(
        functools.partial(_experts_kernel, tf=tf),
        grid_spec=pltpu.PrefetchScalarGridSpec(
            num_scalar_prefetch=2,
            grid=(P // tm,),
            in_specs=[pl.BlockSpec((tm, D), lambda j, te, na: (tile(j, te, na), 0)),
                      pl.BlockSpec((None, D, 2 * D_FF), lambda j, te, na: (te[tile(j, te, na)], 0, 0)),
                      pl.BlockSpec((None, D_FF, D), lambda j, te, na: (te[tile(j, te, na)], 0, 0))],
            out_specs=pl.BlockSpec((tm, D), lambda j, te, na: (j, 0)),
            scratch_shapes=[pltpu.VMEM((tm, D), F32)]),
        out_shape=jax.ShapeDtypeStruct((P, D), F32),
        compiler_params=_cparams(("arbitrary",)),
        name="moe_experts",
    )(tile_expert, n_active, xs, w_gu, w_d)


def _combine_kernel(dest_ref, y_ref, gates_ref, h_ref, gt_ref, gfin_ref, o_ref, ybuf, sems):
    tm = h_ref.shape[0]
    half = tm // COMBINE_PARTS

    def row_copy(r, k):
        return pltpu.make_async_copy(y_ref.at[pl.ds(dest_ref[k, r], 1), :], ybuf.at[k, pl.ds(r, 1), :],
                                     sems.at[r // half])

    for r in range(tm):
        row_copy(r, 0).start(priority=0)
        row_copy(r, 1).start(priority=1)
    for part in range(COMBINE_PARTS):
        rows = slice(part * half, (part + 1) * half)
        for k in range(2):
            pltpu.make_async_copy(y_ref.at[pl.ds(0, half), :], ybuf.at[k, rows], sems.at[part]).wait()
        gates = gates_ref[rows, :]
        moe = gates[:, 0:1] * ybuf[0, rows] + gates[:, 1:2] * ybuf[1, rows]
        y = h_ref[rows, :] + gt_ref[...] * moe
        o_ref[rows, :] = y * _rms_scale(y) * gfin_ref[...]


def _combine(dest, y, gates, h, gt, gfin, *, tm, tiles_per_batch):
    T, D = h.shape
    return pl.pallas_call(
        _combine_kernel,
        grid=(T // tm,),
        in_specs=[pl.BlockSpec((2, tm), lambda i: (0, i), memory_space=pltpu.SMEM),
                  pl.BlockSpec(memory_space=pl.ANY),
                  pl.BlockSpec((tm, 2), lambda i: (i, 0)),
                  pl.BlockSpec((tm, D), lambda i: (i, 0)),
                  pl.BlockSpec((None, 1, D), lambda i: (i // tiles_per_batch, 0, 0)),
                  pl.BlockSpec((1, D), lambda i: (0, 0))],
        out_specs=pl.BlockSpec((tm, D), lambda i: (i, 0)),
        out_shape=jax.ShapeDtypeStruct((T, D), F32),
        scratch_shapes=[pltpu.VMEM((2, tm, D), F32), pltpu.SemaphoreType.DMA((COMBINE_PARTS,))],
        compiler_params=_cparams(("arbitrary",)),
        name="moe_combine",
    )(dest, y, gates, h, gt, gfin)


def _moe_plan(eids, ranks, counts, *, tm, n_tiles):
    padded = (counts + tm - 1) // tm * tm
    ends = jnp.cumsum(padded)
    starts = ends - padded
    dest = ranks
    for e in range(N_EXPERTS):
        dest = dest + jnp.where(eids == e, starts[e], 0)
    tile_start = jnp.arange(n_tiles, dtype=jnp.int32) * tm
    tile_expert = jnp.minimum(jnp.sum(tile_start[:, None] >= ends[None, :], axis=1), N_EXPERTS - 1)
    n_active = (ends[-1] // tm).reshape(1)
    pad_rows = jnp.stack([starts + counts, ends])
    return dest.astype(jnp.int32), tile_expert.astype(jnp.int32), n_active.astype(jnp.int32), pad_rows.astype(jnp.int32)


def _mla_weights(w_in, w_q_up, w_kv_up):
    D = w_in.shape[0]
    half = MLA_ROPE // 2
    w_lat = w_in[:, :MLA_Q_RANK + MLA_KV_RANK]
    w_kr = w_in[:, MLA_Q_RANK + MLA_KV_RANK:]
    z64 = jnp.zeros((D, MLA_NOPE), F32)
    z32 = jnp.zeros((D, HEAD_PAD - MLA_QK), F32)
    w1 = jnp.concatenate([w_lat, z64, w_kr, z32, z64, w_kr[:, half:], w_kr[:, :half], z32], axis=1).astype(BF16)

    wq = w_q_up.reshape(MLA_Q_RANK, MLA_HEADS, MLA_QK)
    nope, rope = wq[..., :MLA_NOPE], wq[..., MLA_NOPE:]
    zq = jnp.zeros((MLA_Q_RANK, MLA_HEADS, HEAD_PAD - MLA_QK), F32)
    wq = jnp.concatenate([nope, rope, zq], axis=-1).reshape(MLA_Q_RANK, MLA_HEADS * HEAD_PAD).astype(BF16)

    wkv = w_kv_up.reshape(MLA_KV_RANK, MLA_HEADS, MLA_NOPE + MLA_V)
    k_nope, v = wkv[..., :MLA_NOPE], wkv[..., MLA_NOPE:]
    zk = jnp.zeros((MLA_KV_RANK, MLA_HEADS, HEAD_PAD - MLA_NOPE), F32)
    wka = jnp.concatenate([k_nope, zk], axis=-1).reshape(MLA_KV_RANK, MLA_HEADS * HEAD_PAD).astype(BF16)
    wvt = v.reshape(MLA_KV_RANK, MLA_HEADS * MLA_V).T.astype(BF16)
    return w1, wq, wka, wvt


def _rope_tables(positions):
    inv_freq = ROPE_THETA ** (-jnp.arange(0, MLA_ROPE, 2, dtype=F32) / MLA_ROPE)
    ang = positions.astype(F32).reshape(-1, 1) * inv_freq
    cos, sin = jnp.cos(ang), jnp.sin(ang)
    T = ang.shape[0]
    one = jnp.ones((T, MLA_NOPE), F32)
    z64 = jnp.zeros((T, MLA_NOPE), F32)
    z32 = jnp.zeros((T, HEAD_PAD - MLA_QK), F32)
    return (jnp.concatenate([one, cos, cos, z32], axis=1),
            jnp.concatenate([z64, -sin, sin, z32], axis=1))


def kernel(x, c, positions, g_mix, g_ffn, w_ada, b_ada, w_mla_in, g_mla_q, w_mla_q_up, g_mla_kv, w_mla_kv_up,
           w_mla_out, g_kv_b, w_ada_kv, b_ada_kv, w_kv_b, rpb_table, w_q_b, w_o_b, w_ffn_gu, w_ffn_down,
           w_router, w_exp_gu, w_exp_down, g_final):
    B, S, D = x.shape
    T = B * S
    h = x.reshape(T, D)

    c8 = jnp.zeros((8, D), F32).at[:B].set(c)
    mod = _ada(c8, w_ada, b_ada[:, None, :])[:, :B]
    mod = mod.reshape(2, B, N_MOD, 1, D)
    mod_kv = _ada(c8, w_ada_kv[None], b_ada_kv[None, None, :])[0, :B].reshape(B, 2, 1, D)
    sh_kv, sc_kv = mod_kv[:, 0], mod_kv[:, 1]

    def mods(layer):
        return [mod[layer, :, k] for k in range(N_MOD)]

    row = lambda v: v.reshape(1, -1)

    sh_m, sc_m, gt_m, sh_f, sc_f, gt_f = mods(0)
    w1, wq, wka, wvt = _mla_weights(w_mla_in[0], w_mla_q_up[0], w_mla_kv_up[0])
    cos_t, sin_t = _rope_tables(positions)
    q, k, vt = _mla_proj(h, row(g_mix[0]), sc_m, sh_m, cos_t, sin_t, w1, row(g_mla_q[0]), row(g_mla_kv[0]),
                         wq, wka, wvt, tm=TM_MLA_PROJ, batch=B, seq=S)
    o = _mla_attn(q.reshape(B, S, -1), k.reshape(B, S, -1), vt, batch=B, seq=S, tq=MLA_TQ)

    h = _ffn(o.reshape(T, -1), w_mla_out[0].astype(BF16), h, gt_m, row(g_ffn[0]), sc_f, sh_f, gt_f,
             w_ffn_gu[0].astype(BF16), w_ffn_down[0].astype(BF16), tm=TM_FFN, tf=FF_CHUNK,
             tiles_per_batch=S // TM_FFN)

    sh_m, sc_m, gt_m, sh_f, sc_f, gt_f = mods(1)
    qkv = _l1_proj(h, row(g_kv_b), sc_kv, sh_kv, row(g_mix[1]), sc_m, sh_m,
                   w_kv_b.astype(BF16), w_q_b[0].astype(BF16), tm=TM_L1_PROJ, tiles_per_batch=S // TM_L1_PROJ)

    outs, lses = [], []
    for g, (window, dil) in enumerate(DIL_PATTERNS):
        bias = _dil_bias(rpb_table[:, g * DIL_HEADS:(g + 1) * DIL_HEADS], window // dil, dil)
        o_g, lse_g = _dil_attn(*qkv[3 * g:3 * g + 3], bias, group=g, dil=dil, batch=B, seq=S)
        outs.append(o_g)
        lses.append(lse_g)
    h = _dil_out(outs, lses, w_o_b[0].astype(BF16), h, gt_m, tm=TM_DIL_OUT, tiles_per_batch=S // TM_DIL_OUT)

    hn, eids, ranks, gates, counts = _route(h, row(g_ffn[1]), sc_f, sh_f, w_router[0].T, tm=TM_ROUTE,
                                            tiles_per_batch=S // TM_ROUTE)
    n_tiles = (T * 2) // MOE_TM + N_EXPERTS
    dest, tile_expert, n_active, pad_rows = _moe_plan(eids, ranks, counts[:, 0], tm=MOE_TM, n_tiles=n_tiles)
    xs = _dispatch(pad_rows, dest, hn, n_slots=n_tiles * MOE_TM, tm=TM_DISPATCH)
    y = _experts(tile_expert, n_active, xs, w_exp_gu[0].astype(BF16), w_exp_down[0].astype(BF16),
                 tm=MOE_TM, tf=FF_CHUNK)
    out = _combine(dest, y, gates.T, h, gt_f, row(g_final), tm=TM_COMBINE, tiles_per_batch=S // TM_COMBINE)
    return out.reshape(B, S, D)
```

```python
import functools
import math

import jax
import jax.numpy as jnp
from jax import lax
from jax.experimental import pallas as pl
from jax.experimental.pallas import tpu as pltpu

D_MODEL = 1024
N_MOD = 6
EPS = 1e-6

MLA_HEADS = 16
MLA_Q_RANK = 384
MLA_KV_RANK = 256
MLA_NOPE = 64
MLA_ROPE = 32
MLA_V = 64
MLA_QK = MLA_NOPE + MLA_ROPE
ROPE_THETA = 10000.0
LANES = 128
ONES_ROWS = 16
STRIP = 256
HEAD_PAD = 128

DIL_PATTERNS = ((128, 1), (512, 4), (2048, 16))
DIL_GROUPS = len(DIL_PATTERNS)
DIL_HEADS = 8
DIL_HEAD_DIM = 64
DIL_BLOCK = 128
DIL_SUBBLOCKS = 8
DIL_W = DIL_HEADS * DIL_HEAD_DIM
RPB_BUCKETS = 32
RPB_MAX_DIST = 2048

D_FF = 2816
N_EXPERTS = 8
MOE_TM = 512

TM_MLA_PROJ = 1024
TM_FFN = 512
TM_L1_PROJ = 1024
TM_DIL_OUT = 1024
TM_ROUTE = 1024
TM_DISPATCH = 1024
TM_COMBINE = 512
COMBINE_PARTS = 4
MLA_TQ = 1024
FF_CHUNK = 256

MASK_VALUE = -1e30
LOG2E = math.log2(math.e)

F32 = jnp.float32
BF16 = jnp.bfloat16

VMEM_LIMIT = 56 * 1024 * 1024


def _cparams(sem):
    return pltpu.CompilerParams(dimension_semantics=sem, vmem_limit_bytes=VMEM_LIMIT)


def _dot(a, b):
    return jnp.dot(a, b, preferred_element_type=F32)


def _dot_nt(a, b):
    return lax.dot_general(a, b, (((1,), (1,)), ((), ())), preferred_element_type=F32)


def _rms_scale(x):
    return lax.rsqrt(jnp.mean(x * x, axis=-1, keepdims=True) + EPS)


def _normmod(x, g, sc, sh):
    return (x * _rms_scale(x)) * (g * (1.0 + sc)) + sh


def _silu(x):
    return x * (1.0 / (1.0 + jnp.exp(-x)))


def _ada_kernel(c_ref, w_ref, b_ref, o_ref):
    c = c_ref[...]
    o_ref[...] = jnp.dot(_silu(c), w_ref[...], preferred_element_type=F32,
                         precision=lax.Precision.HIGHEST) + b_ref[...]


def _ada(c8, w, b, tn=512):
    L, D, N = w.shape
    return pl.pallas_call(
        _ada_kernel,
        grid=(L, N // tn),
        in_specs=[pl.BlockSpec((8, D), lambda l, j: (0, 0)),
                  pl.BlockSpec((None, D, tn), lambda l, j: (l, 0, j)),
                  pl.BlockSpec((None, 1, tn), lambda l, j: (l, 0, j))],
        out_specs=pl.BlockSpec((None, 8, tn), lambda l, j: (l, 0, j)),
        out_shape=jax.ShapeDtypeStruct((L, 8, N), F32),
        compiler_params=_cparams(("parallel", "parallel")),
        name="ada_mod",
    )(c8, w, b)


def _swap_rope_halves(x, first_half):
    width = x.shape[1]
    return jnp.where(first_half, pltpu.roll(x, width - MLA_ROPE // 2, axis=1), pltpu.roll(x, MLA_ROPE // 2, axis=1))


def _mla_proj_kernel(h_ref, g_ref, sc_ref, sh_ref, pos_ref, rope_ref, w1_ref, gq_ref, gkv_ref,
                     wq_ref, wka_ref, wvt_ref, q_ref, k_ref, vt_ref, *, qscale):
    hn = _normmod(h_ref[...], g_ref[...], sc_ref[...], sh_ref[...]).astype(BF16)
    z = _dot(hn, w1_ref[...])
    ql = z[:, :MLA_Q_RANK]
    kvl = z[:, MLA_Q_RANK:MLA_Q_RANK + MLA_KV_RANK]
    kr = z[:, MLA_Q_RANK + MLA_KV_RANK:MLA_Q_RANK + MLA_KV_RANK + HEAD_PAD]
    krs = z[:, MLA_Q_RANK + MLA_KV_RANK + HEAD_PAD:]
    qn = (ql * _rms_scale(ql) * gq_ref[...]).astype(BF16)
    kvn = (kvl * _rms_scale(kvl) * gkv_ref[...]).astype(BF16)
    ang = pos_ref[...] * rope_ref[0:1, :]
    cos = rope_ref[1:2, :] + rope_ref[2:3, :] * jnp.cos(ang)
    sin = rope_ref[3:4, :] * jnp.sin(ang)
    lane = lax.broadcasted_iota(jnp.int32, (1, 2 * HEAD_PAD), 1) % HEAD_PAD
    first_half = lane < MLA_NOPE + MLA_ROPE // 2
    k_rope = kr * cos + krs * sin
    cos2 = jnp.concatenate([cos, cos], axis=1)
    sin2 = jnp.concatenate([sin, sin], axis=1)
    k_rope2 = jnp.concatenate([k_rope, k_rope], axis=1)
    for hp in range(MLA_HEADS // 2):
        sl = slice(2 * HEAD_PAD * hp, 2 * HEAD_PAD * (hp + 1))
        a = _dot(qn, wq_ref[:, sl])
        q_ref[:, sl] = ((a * cos2 + _swap_rope_halves(a, first_half) * sin2) * qscale).astype(BF16)
        k_ref[:, sl] = (_dot(kvn, wka_ref[:, sl]) + k_rope2).astype(BF16)
    vt_ref[...] = _dot_nt(wvt_ref[...], kvn).astype(BF16)


def _mla_proj(h, g, sc, sh, pos, rope, w1, gq, gkv, wq, wka, wvt, *, tm, batch, seq):
    T, D = h.shape
    HP = MLA_HEADS * HEAD_PAD
    tpb = seq // tm
    row = lambda n: pl.BlockSpec((tm, n), lambda i: (i, 0))
    bvec = pl.BlockSpec((None, 1, D), lambda i: (i // tpb, 0, 0))
    full = lambda a: pl.BlockSpec(a.shape, lambda i: (0,) * a.ndim)
    return pl.pallas_call(
        functools.partial(_mla_proj_kernel, qscale=(MLA_QK ** -0.5) * LOG2E),
        grid=(T // tm,),
        in_specs=[row(D), full(g), bvec, bvec, row(1), full(rope), full(w1), full(gq), full(gkv),
                  full(wq), full(wka), full(wvt)],
        out_specs=[row(HP), row(HP),
                   pl.BlockSpec((None, MLA_HEADS * MLA_V, tm), lambda i: (i // tpb, 0, i % tpb))],
        out_shape=[jax.ShapeDtypeStruct((T, HP), BF16), jax.ShapeDtypeStruct((T, HP), BF16),
                   jax.ShapeDtypeStruct((batch, MLA_HEADS * MLA_V, seq), BF16)],
        compiler_params=_cparams(("parallel",)),
        name="mla_proj",
    )(h, g, sc, sh, pos, rope, w1, gq, gkv, wq, wka, wvt)


def _mla_attn_kernel(q_ref, k_ref, vt_ref, o_ref, m_sc, acc_sc, s_sc, mb_sc, *, tq):
    qi = pl.program_id(2)
    tk = tq
    m_sc[...] = jnp.full(m_sc.shape, -jnp.inf, F32)
    acc_sc[...] = jnp.zeros(acc_sc.shape, F32)

    def scores(j, diagonal, slot, hh, n):
        rows = (n + 1) * STRIP if diagonal else tk
        off = pl.multiple_of(j * tk, tk)
        cols = slice(n * STRIP, (n + 1) * STRIP)
        q = q_ref[cols, HEAD_PAD * hh:HEAD_PAD * (hh + 1)]
        k = k_ref[pl.ds(off, rows), HEAD_PAD * hh:HEAD_PAD * (hh + 1)]
        s = _dot_nt(k, q)
        if diagonal:
            key = lax.broadcasted_iota(jnp.int32, (rows, STRIP), 0)
            qry = lax.broadcasted_iota(jnp.int32, (rows, STRIP), 1) + n * STRIP
            s = jnp.where(key <= qry, s, -jnp.inf)
        s_sc[slot, hh, :rows, cols] = s
        m_blk = jnp.max(jnp.max(s.reshape(rows // 8, 8, STRIP), axis=0), axis=0, keepdims=True)
        mb_sc[slot, hh, :, cols] = jnp.broadcast_to(m_blk, (8, STRIP))

    def softmax_values(j, diagonal, slot, hh, n):
        rows = (n + 1) * STRIP if diagonal else tk
        off = pl.multiple_of(j * tk, tk)
        cols = slice(n * STRIP, (n + 1) * STRIP)
        s3 = s_sc[slot, hh, :rows, cols].reshape(rows // 8, 8, STRIP)
        m_prev = m_sc[hh, :, cols]
        m_new = jnp.maximum(m_prev, mb_sc[slot, hh, :, cols])
        alpha = jnp.exp2(m_prev - m_new)
        p = jnp.exp2(s3 - m_new[None]).reshape(rows, STRIP).astype(BF16)
        vt = vt_ref[MLA_V * hh:MLA_V * (hh + 1), pl.ds(off, rows)]
        vt1 = jnp.concatenate([vt, jnp.ones((ONES_ROWS, rows), BF16)], axis=0)
        acc_sc[hh, :, cols] = alpha[:1] * acc_sc[hh, :, cols] + _dot(vt1, p)
        m_sc[hh, :, cols] = m_new

    strips = [(hh, n) for hh in range(2) for n in range(tq // STRIP)]

    def step(j, even, diagonal_next=False, has_next=True):
        cur, nxt = (0, 1) if even else (1, 0)
        for hh, n in strips:
            if has_next:
                scores(j + 1, diagonal_next, nxt, hh, n)
            softmax_values(j, not has_next, cur, hh, n)

    def first_scores(diagonal):
        for hh, n in strips:
            scores(0, diagonal, 0, hh, n)

    @pl.when(qi == 0)
    def _():
        first_scores(True)

    @pl.when(qi > 0)
    def _():
        first_scores(False)

        def pair(t, carry):
            step(2 * t, True)
            step(2 * t + 1, False)
            return carry

        lax.fori_loop(0, (qi - 1) // 2, pair, 0)

    @pl.when(qi % 2 == 1)
    def _():
        step(qi - 1, True, diagonal_next=True)
        step(qi, False, has_next=False)

    @pl.when((qi % 2 == 0) & (qi > 0))
    def _():
        step(qi - 2, True)
        step(qi - 1, False, diagonal_next=True)

    @pl.when(qi % 2 == 0)
    def _():
        step(qi, True, has_next=False)

    o_t = jnp.concatenate([acc_sc[hh, :MLA_V] / acc_sc[hh, MLA_V:MLA_V + 1] for hh in range(2)], axis=0)
    o_ref[...] = o_t.T.astype(BF16)


def _mla_attn(q, k, vt, *, batch, seq, tq):
    return pl.pallas_call(
        functools.partial(_mla_attn_kernel, tq=tq),
        grid=(batch, MLA_HEADS // 2, seq // tq),
        in_specs=[pl.BlockSpec((None, tq, 2 * HEAD_PAD), lambda b, hp, i: (b, i, hp)),
                  pl.BlockSpec((None, seq, 2 * HEAD_PAD), lambda b, hp, i: (b, 0, hp)),
                  pl.BlockSpec((None, 2 * MLA_V, seq), lambda b, hp, i: (b, hp, 0))],
        out_specs=pl.BlockSpec((None, tq, 2 * MLA_V), lambda b, hp, i: (b, i, hp)),
        out_shape=jax.ShapeDtypeStruct((batch, seq, MLA_HEADS * MLA_V), BF16),
        scratch_shapes=[pltpu.VMEM((2, 8, tq), F32), pltpu.VMEM((2, MLA_V + ONES_ROWS, tq), F32),
                        pltpu.VMEM((2, 2, tq, tq), F32),
                        pltpu.VMEM((2, 2, 8, tq), F32)],
        compiler_params=_cparams(("parallel", "parallel", "arbitrary")),
        name="mla_attn",
    )(q, k, vt)


def _swiglu_tile(x, wgu_ref, wd_ref, acc_sc, tf):
    for f in range(D_FF // tf):
        g = _dot(x, wgu_ref[:, f * tf:(f + 1) * tf])
        u = _dot(x, wgu_ref[:, D_FF + f * tf:D_FF + (f + 1) * tf])
        part = _dot((_silu(g) * u).astype(BF16), wd_ref[f * tf:(f + 1) * tf, :])
        if f == 0:
            acc_sc[...] = part
        else:
            acc_sc[...] += part


def _ffn_kernel(a_ref, wo_ref, h_ref, gtm_ref, g_ref, sc_ref, sh_ref, gt_ref, wgu_ref, wd_ref, o_ref, acc_sc,
                *, tf):
    h = h_ref[...] + gtm_ref[...] * _dot(a_ref[...], wo_ref[...])
    x = _normmod(h, g_ref[...], sc_ref[...], sh_ref[...]).astype(BF16)
    _swiglu_tile(x, wgu_ref, wd_ref, acc_sc, tf)
    o_ref[...] = h + gt_ref[...] * acc_sc[...]


def _ffn(a, w_o, h, gt_m, g, sc, sh, gt, w_gu, w_d, *, tm, tf, tiles_per_batch):
    T, D = h.shape
    bvec = pl.BlockSpec((None, 1, D), lambda i: (i // tiles_per_batch, 0, 0))
    row = pl.BlockSpec((tm, D), lambda i: (i, 0))
    const = lambda w: pl.BlockSpec(w.shape, lambda i: (0,) * w.ndim, pipeline_mode=pl.Buffered(1))
    return pl.pallas_call(
        functools.partial(_ffn_kernel, tf=tf),
        grid=(T // tm,),
        in_specs=[row, const(w_o), row, bvec, const(g), bvec, bvec, bvec, const(w_gu), const(w_d)],
        out_specs=row,
        out_shape=jax.ShapeDtypeStruct((T, D), F32),
        scratch_shapes=[pltpu.VMEM((tm, D), F32)],
        compiler_params=_cparams(("parallel",)),
        name="ffn_dense",
    )(a, w_o, h, gt_m, g, sc, sh, gt, w_gu, w_d)


def _l1_proj_kernel(h_ref, gkv_ref, sckv_ref, shkv_ref, gq_ref, scq_ref, shq_ref, wkv_ref, wq_ref, *refs):
    out_refs, stage_sc = refs[:-1], refs[-1]
    tm = h_ref.shape[0]
    x = h_ref[...]
    xh = x * _rms_scale(x)
    hn_kv = (xh * (gkv_ref[...] * (1.0 + sckv_ref[...])) + shkv_ref[...]).astype(BF16)
    hn_q = (xh * (gq_ref[...] * (1.0 + scq_ref[...])) + shq_ref[...]).astype(BF16)
    for g, (_, dil) in enumerate(DIL_PATTERNS):
        ys = (_dot(hn_q, wq_ref[:, g * DIL_W:(g + 1) * DIL_W]) * (DIL_HEAD_DIM ** -0.5),
              _dot(hn_kv, wkv_ref[:, 2 * g * DIL_W:(2 * g + 1) * DIL_W]),
              _dot(hn_kv, wkv_ref[:, (2 * g + 1) * DIL_W:(2 * g + 2) * DIL_W]))
        for y, out_ref in zip(ys, out_refs[3 * g:3 * g + 3]):
            if dil == 1:
                out_ref[...] = y.astype(BF16)
            else:
                for c in range(DIL_W // LANES):
                    stage_sc[c] = y[:, c * LANES:(c + 1) * LANES]
                for r in range(dil):
                    for c in range(DIL_W // LANES):
                        col = r * DIL_W + c * LANES
                        out_ref[:, col:col + LANES] = stage_sc[c, pl.ds(r, tm // dil, stride=dil), :].astype(BF16)


def _l1_proj(h, gkv, sckv, shkv, gq, scq, shq, wkv, wq, *, tm, tiles_per_batch):
    T, D = h.shape
    bvec = pl.BlockSpec((None, 1, D), lambda i: (i // tiles_per_batch, 0, 0))
    full = lambda a: pl.BlockSpec(a.shape, lambda i: (0,) * a.ndim)
    out_specs, out_shape = [], []
    for _, dil in DIL_PATTERNS:
        for _ in range(3):
            out_specs.append(pl.BlockSpec((tm // dil, dil * DIL_W), lambda i: (i, 0)))
            out_shape.append(jax.ShapeDtypeStruct((T // dil, dil * DIL_W), BF16))
    return pl.pallas_call(
        _l1_proj_kernel,
        grid=(T // tm,),
        in_specs=[pl.BlockSpec((tm, D), lambda i: (i, 0)), full(gkv), bvec, bvec, full(gq), bvec, bvec,
                  full(wkv), full(wq)],
        out_specs=out_specs,
        out_shape=out_shape,
        scratch_shapes=[pltpu.VMEM((DIL_W // LANES, tm, LANES), F32)],
        compiler_params=_cparams(("parallel",)),
        name="l1_proj",
    )(h, gkv, sckv, shkv, gq, scq, shq, wkv, wq)


def _dil_attn_kernel(q_ref, kc_ref, kp_ref, vc_ref, vp_ref, bias_ref, o_ref, lse_ref):
    i = pl.program_id(2)
    nsub = q_ref.shape[0] // DIL_BLOCK
    lane = lax.broadcasted_iota(jnp.int32, (1, 2 * DIL_HEAD_DIM), 1)
    lo = lane < DIL_HEAD_DIM
    col = lax.broadcasted_iota(jnp.int32, (1, 2 * DIL_BLOCK), 1)
    edge = jnp.where((col < DIL_BLOCK) & (i == 0), MASK_VALUE, 0.0).astype(F32)
    sls = [slice(2 * DIL_HEAD_DIM * hp, 2 * DIL_HEAD_DIM * (hp + 1)) for hp in range(DIL_HEADS // 2)]
    for sub in range(nsub):
        rows = slice(DIL_BLOCK * sub, DIL_BLOCK * (sub + 1))
        before = slice(DIL_BLOCK * (sub - 1), DIL_BLOCK * sub)
        scores = []
        for hp, sl in enumerate(sls):
            q2 = q_ref[rows, sl]
            zero = jnp.zeros_like(q2)
            qs = jnp.concatenate([jnp.where(lo, q2, zero), jnp.where(lo, zero, q2)], axis=0)
            k_prev = kp_ref[:, sl] if sub == 0 else kc_ref[before, sl]
            k2 = jnp.concatenate([k_prev, kc_ref[rows, sl]], axis=0)
            bias = jnp.concatenate([bias_ref[2 * hp], bias_ref[2 * hp + 1]], axis=0)
            s = _dot_nt(qs, k2) + bias
            scores.append(s + edge if sub == 0 else s)
        probs, stats = [], []
        for s in scores:
            m = jnp.max(s, axis=1, keepdims=True)
            p = jnp.exp(s - m)
            probs.append(p.astype(BF16))
            stats.append((m, jnp.sum(p, axis=1, keepdims=True)))
        for sl, p, (m, l) in zip(sls, probs, stats):
            v_prev = vp_ref[:, sl] if sub == 0 else vc_ref[before, sl]
            v2 = jnp.concatenate([v_prev, vc_ref[rows, sl]], axis=0)
            o = _dot(p, v2) / l
            lse = jnp.broadcast_to(m + jnp.log(l), o.shape)
            o_ref[rows, sl] = jnp.where(lo, o[:DIL_BLOCK], o[DIL_BLOCK:])
            lse_ref[rows, sl] = jnp.where(lo, lse[:DIL_BLOCK], lse[DIL_BLOCK:])


def _dil_attn(q, k, v, bias, *, group, dil, batch, seq):
    n = seq // dil
    nsub = min(DIL_SUBBLOCKS, n // DIL_BLOCK)
    run = nsub * DIL_BLOCK
    assert n % run == 0, (n, run)
    q, k, v = (a.reshape(batch, n, dil * DIL_W) for a in (q, k, v))
    cur = pl.BlockSpec((None, run, DIL_W), lambda b, r, i: (b, i, r))
    prev = pl.BlockSpec((None, DIL_BLOCK, DIL_W), lambda b, r, i: (b, jnp.maximum(i * nsub - 1, 0), r))
    out_sd = jax.ShapeDtypeStruct((batch, n, dil * DIL_W), F32)
    o, lse = pl.pallas_call(
        _dil_attn_kernel,
        grid=(batch, dil, n // run),
        in_specs=[cur, cur, prev, cur, prev, pl.BlockSpec(bias.shape, lambda b, r, i: (0, 0, 0))],
        out_specs=[cur, cur],
        out_shape=[out_sd, out_sd],
        compiler_params=_cparams(("parallel", "parallel", "arbitrary")),
        name=f"dil_attn_g{group}",
    )(q, k, k, v, v, bias)
    return o.reshape(batch * n, dil * DIL_W), lse.reshape(batch * n, dil * DIL_W)


def _rpb_bucket(dist):
    exact = RPB_BUCKETS // 2
    d = jnp.maximum(dist, 0)
    d_f = jnp.maximum(d, 1).astype(F32)
    large = exact + (jnp.log(d_f / exact) / math.log(RPB_MAX_DIST / exact)
                     * (RPB_BUCKETS - exact)).astype(jnp.int32)
    return jnp.where(d < exact, d, jnp.minimum(large, RPB_BUCKETS - 1))


def _dil_bias(table, span, dil):
    period = 3 * DIL_BLOCK
    j = jnp.arange(period)
    k_minus_q = jnp.where(j < 2 * DIL_BLOCK, j, j - period)
    dist = DIL_BLOCK - k_minus_q
    band = (dist >= 0) & (dist <= span)
    prof = jnp.where(band[:, None], table[_rpb_bucket(dist * dil)].astype(F32), MASK_VALUE).T
    skew = jnp.tile(prof, (1, DIL_BLOCK))[:, :DIL_BLOCK * (period - 1)]
    return skew.reshape(DIL_HEADS, DIL_BLOCK, period - 1)[:, :, :2 * DIL_BLOCK]


def _dil_out_kernel(o0_ref, o1_ref, o2_ref, l0_ref, l1_ref, l2_ref, w_ref, h_ref, gt_ref, out_ref, *stages):
    tm = h_ref.shape[0]
    stages = list(stages)

    def token_major(ref, dil):
        if dil == 1:
            return ref[...]
        stage = stages.pop()
        for r in range(dil):
            for c in range(DIL_W // LANES):
                col = r * DIL_W + c * LANES
                stage[c, pl.ds(r, tm // dil, stride=dil), :] = ref[:, col:col + LANES]
        return jnp.concatenate([stage[c] for c in range(DIL_W // LANES)], axis=1)

    dils = [dil for _, dil in DIL_PATTERNS]
    l0, l1, l2 = (token_major(ref, d) for ref, d in zip((l0_ref, l1_ref, l2_ref), dils))
    o0, o1, o2 = (token_major(ref, d) for ref, d in zip((o0_ref, o1_ref, o2_ref), dils))
    m = jnp.maximum(jnp.maximum(l0, l1), l2)
    w0, w1, w2 = jnp.exp(l0 - m), jnp.exp(l1 - m), jnp.exp(l2 - m)
    o = (o0 * w0 + o1 * w1 + o2 * w2) / (w0 + w1 + w2)
    out_ref[...] = h_ref[...] + gt_ref[...] * _dot(o.astype(BF16), w_ref[...])


def _dil_out(os_, ls_, w, h, gt, *, tm, tiles_per_batch):
    T, D = h.shape
    row = lambda n: pl.BlockSpec((tm, n), lambda i: (i, 0))
    grp = [pl.BlockSpec((tm // dil, dil * DIL_W), lambda i: (i, 0)) for _, dil in DIL_PATTERNS]
    n_stage = 2 * sum(1 for _, dil in DIL_PATTERNS if dil > 1)
    return pl.pallas_call(
        _dil_out_kernel,
        grid=(T // tm,),
        in_specs=grp + grp + [pl.BlockSpec(w.shape, lambda i: (0, 0)), row(D),
                              pl.BlockSpec((None, 1, D), lambda i: (i // tiles_per_batch, 0, 0))],
        out_specs=row(D),
        out_shape=jax.ShapeDtypeStruct((T, D), F32),
        scratch_shapes=[pltpu.VMEM((DIL_W // LANES, tm, LANES), F32)] * n_stage,
        compiler_params=_cparams(("parallel",)),
        name="dil_out",
    )(*os_, *ls_, w, h, gt)


def _route_kernel(h_ref, g_ref, sc_ref, sh_ref, wrt_ref, hn_ref, eid_ref, rank_ref, gate_ref, cnt_ref, carry_sc):
    @pl.when(pl.program_id(0) == 0)
    def _():
        carry_sc[...] = jnp.zeros(carry_sc.shape, F32)

    hn = _normmod(h_ref[...], g_ref[...], sc_ref[...], sh_ref[...])
    hn_ref[...] = hn
    tm = hn.shape[0]
    logits = lax.dot_general(wrt_ref[...], hn, (((1,), (1,)), ((), ())), preferred_element_type=F32,
                             precision=lax.Precision.HIGHEST)
    idx = lax.broadcasted_iota(jnp.int32, logits.shape, 0)
    v1 = jnp.max(logits, axis=0, keepdims=True)
    i1 = jnp.min(jnp.where(logits == v1, idx, N_EXPERTS), axis=0, keepdims=True)
    rest = jnp.where(idx == i1, -jnp.inf, logits)
    v2 = jnp.max(rest, axis=0, keepdims=True)
    i2 = jnp.min(jnp.where(rest == v2, idx, N_EXPERTS), axis=0, keepdims=True)
    e = jnp.exp(v2 - v1)
    gate_ref[...] = jnp.concatenate([1.0 / (1.0 + e), e / (1.0 + e)], axis=0)
    eid_ref[...] = jnp.concatenate([i1, i2], axis=0)

    sel = ((idx == i1) | (idx == i2)).astype(BF16)
    before = (lax.broadcasted_iota(jnp.int32, (tm, tm), 0)
              < lax.broadcasted_iota(jnp.int32, (tm, tm), 1)).astype(BF16)
    rank_all = carry_sc[:, :1] + _dot(sel, before)
    r1 = jnp.sum(jnp.where(idx == i1, rank_all, 0.0), axis=0, keepdims=True)
    r2 = jnp.sum(jnp.where(idx == i2, rank_all, 0.0), axis=0, keepdims=True)
    rank_ref[...] = jnp.concatenate([r1, r2], axis=0).astype(jnp.int32)
    carry_sc[...] += jnp.sum(sel.astype(F32), axis=1, keepdims=True)
    cnt_ref[...] = carry_sc[...].astype(jnp.int32)


def _route(h, g, sc, sh, wrt, *, tm, tiles_per_batch):
    T, D = h.shape
    bvec = pl.BlockSpec((None, 1, D), lambda i: (i // tiles_per_batch, 0, 0))
    lane_blk = pl.BlockSpec((2, tm), lambda i: (0, i))
    return pl.pallas_call(
        _route_kernel,
        grid=(T // tm,),
        in_specs=[pl.BlockSpec((tm, D), lambda i: (i, 0)), pl.BlockSpec((1, D), lambda i: (0, 0)), bvec, bvec,
                  pl.BlockSpec(wrt.shape, lambda i: (0, 0))],
        out_specs=[pl.BlockSpec((tm, D), lambda i: (i, 0)), lane_blk, lane_blk, lane_blk,
                   pl.BlockSpec((N_EXPERTS, HEAD_PAD), lambda i: (0, 0))],
        out_shape=[jax.ShapeDtypeStruct((T, D), F32), jax.ShapeDtypeStruct((2, T), jnp.int32),
                   jax.ShapeDtypeStruct((2, T), jnp.int32), jax.ShapeDtypeStruct((2, T), F32),
                   jax.ShapeDtypeStruct((N_EXPERTS, HEAD_PAD), jnp.int32)],
        scratch_shapes=[pltpu.VMEM((N_EXPERTS, HEAD_PAD), F32)],
        compiler_params=_cparams(("arbitrary",)),
        name="moe_route",
    )(h, g, sc, sh, wrt)


def _dispatch_kernel(pad_ref, dest_ref, hn_ref, xs_ref, zero_sc, sem, zsem):
    tm = hn_ref.shape[0]

    def row_copy(r, k):
        return pltpu.make_async_copy(hn_ref.at[pl.ds(r, 1), :], xs_ref.at[pl.ds(dest_ref[k, r], 1), :], sem)

    def issue(r, carry):
        row_copy(r, 0).start(priority=0)
        row_copy(r, 1).start(priority=1)
        return carry

    for r in range(tm):
        issue(r, 0)

    @pl.when(pl.program_id(0) == pl.num_programs(0) - 1)
    def _():
        zero_sc[...] = jnp.zeros(zero_sc.shape, F32)

        def pad_copy(p):
            return pltpu.make_async_copy(zero_sc.at[pl.ds(0, 1), :], xs_ref.at[pl.ds(p, 1), :], zsem)

        for e in range(N_EXPERTS):
            lo, hi = pad_ref[0, e], pad_ref[1, e]

            def zissue(p, carry):
                pad_copy(p).start()
                return carry

            def zwait(p, carry):
                pad_copy(p).wait()
                return carry

            lax.fori_loop(lo, hi, zissue, 0)
            lax.fori_loop(lo, hi, zwait, 0)

        zrows = zero_sc.shape[0]

        def tail_copy(c):
            return pltpu.make_async_copy(zero_sc, xs_ref.at[pl.ds(pl.multiple_of(c * zrows, zrows), zrows), :], zsem)

        def tissue(c, carry):
            tail_copy(c).start()
            return carry

        def twait(c, carry):
            tail_copy(c).wait()
            return carry

        lo, hi = pad_ref[1, N_EXPERTS - 1] // zrows, xs_ref.shape[0] // zrows
        lax.fori_loop(lo, hi, tissue, 0)
        lax.fori_loop(lo, hi, twait, 0)

    for _ in range(2):
        pltpu.make_async_copy(hn_ref, xs_ref.at[pl.ds(0, tm), :], sem).wait()


def _dispatch(pad_rows, dest, hn, *, n_slots, tm):
    T, D = hn.shape
    return pl.pallas_call(
        _dispatch_kernel,
        grid_spec=pltpu.PrefetchScalarGridSpec(
            num_scalar_prefetch=1,
            grid=(T // tm,),
            in_specs=[pl.BlockSpec((2, tm), lambda i, pad: (0, i), memory_space=pltpu.SMEM),
                      pl.BlockSpec((tm, D), lambda i, pad: (i, 0))],
            out_specs=pl.BlockSpec(memory_space=pl.ANY),
            scratch_shapes=[pltpu.VMEM((64, D), F32), pltpu.SemaphoreType.DMA, pltpu.SemaphoreType.DMA]),
        out_shape=jax.ShapeDtypeStruct((n_slots, D), F32),
        compiler_params=_cparams(("arbitrary",)),
        name="moe_dispatch",
    )(pad_rows, dest, hn)


def _experts_kernel(te_ref, na_ref, xs_ref, wgu_ref, wd_ref, y_ref, acc_sc, *, tf):
    @pl.when(pl.program_id(0) < na_ref[0])
    def _():
        _swiglu_tile(xs_ref[...].astype(BF16), wgu_ref, wd_ref, acc_sc, tf)
        y_ref[...] = acc_sc[...]

    @pl.when(pl.program_id(0) >= na_ref[0])
    def _():
        y_ref[...] = jnp.zeros(y_ref.shape, F32)


def _experts(tile_expert, n_active, xs, w_gu, w_d, *, tm, tf):
    P, D = xs.shape

    def tile(j, te, na):
        return jnp.minimum(j, na[0] - 1)

    return pl.pallas_call(
        functools.partial(_experts_kernel, tf=tf),
        grid_spec=pltpu.PrefetchScalarGridSpec(
            num_scalar_prefetch=2,
            grid=(P // tm,),
            in_specs=[pl.BlockSpec((tm, D), lambda j, te, na: (tile(j, te, na), 0)),
                      pl.BlockSpec((None, D, 2 * D_FF), lambda j, te, na: (te[tile(j, te, na)], 0, 0)),
                      pl.BlockSpec((None, D_FF, D), lambda j, te, na: (te[tile(j, te, na)], 0, 0))],
            out_specs=pl.BlockSpec((tm, D), lambda j, te, na: (j, 0)),
            scratch_shapes=[pltpu.VMEM((tm, D), F32)]),
        out_shape=jax.ShapeDtypeStruct((P, D), F32),
        compiler_params=_cparams(("arbitrary",)),
        name="moe_experts",
    )(tile_expert, n_active, xs, w_gu, w_d)


def _combine_kernel(dest_ref, y_ref, gates_ref, h_ref, gt_ref, gfin_ref, o_ref, ybuf, sems):
    tm = h_ref.shape[0]
    half = tm // COMBINE_PARTS

    def row_copy(r, k):
        return pltpu.make_async_copy(y_ref.at[pl.ds(dest_ref[k, r], 1), :], ybuf.at[k, pl.ds(r, 1), :],
                                     sems.at[r // half])

    for r in range(tm):
        row_copy(r, 0).start(priority=0)
        row_copy(r, 1).start(priority=1)
    for part in range(COMBINE_PARTS):
        rows = slice(part * half, (part + 1) * half)
        for k in range(2):
            pltpu.make_async_copy(y_ref.at[pl.ds(0, half), :], ybuf.at[k, rows], sems.at[part]).wait()
        gates = gates_ref[rows, :]
        moe = gates[:, 0:1] * ybuf[0, rows] + gates[:, 1:2] * ybuf[1, rows]
        y = h_ref[rows, :] + gt_ref[...] * moe
        o_ref[rows, :] = y * _rms_scale(y) * gfin_ref[...]


def _combine(dest, y, gates, h, gt, gfin, *, tm, tiles_per_batch):
    T, D = h.shape
    return pl.pallas_call(
        _combine_kernel,
        grid=(T // tm,),
        in_specs=[pl.BlockSpec((2, tm), lambda i: (0, i), memory_space=pltpu.SMEM),
                  pl.BlockSpec(memory_space=pl.ANY),
                  pl.BlockSpec((tm, 2), lambda i: (i, 0)),
                  pl.BlockSpec((tm, D), lambda i: (i, 0)),
                  pl.BlockSpec((None, 1, D), lambda i: (i // tiles_per_batch, 0, 0)),
                  pl.BlockSpec((1, D), lambda i: (0, 0))],
        out_specs=pl.BlockSpec((tm, D), lambda i: (i, 0)),
        out_shape=jax.ShapeDtypeStruct((T, D), F32),
        scratch_shapes=[pltpu.VMEM((2, tm, D), F32), pltpu.SemaphoreType.DMA((COMBINE_PARTS,))],
        compiler_params=_cparams(("arbitrary",)),
        name="moe_combine",
    )(dest, y, gates, h, gt, gfin)


def _moe_plan(eids, ranks, counts, *, tm, n_tiles):
    padded = (counts + tm - 1) // tm * tm
    ends = jnp.cumsum(padded)
    starts = ends - padded
    dest = ranks
    for e in range(N_EXPERTS):
        dest = dest + jnp.where(eids == e, starts[e], 0)
    tile_start = jnp.arange(n_tiles, dtype=jnp.int32) * tm
    tile_expert = jnp.minimum(jnp.sum(tile_start[:, None] >= ends[None, :], axis=1), N_EXPERTS - 1)
    n_active = (ends[-1] // tm).reshape(1)
    pad_rows = jnp.stack([starts + counts, ends])
    return dest.astype(jnp.int32), tile_expert.astype(jnp.int32), n_active.astype(jnp.int32), pad_rows.astype(jnp.int32)


def _mla_weights(w_in, w_q_up, w_kv_up):
    D = w_in.shape[0]
    half = MLA_ROPE // 2
    w_lat = w_in[:, :MLA_Q_RANK + MLA_KV_RANK]
    w_kr = w_in[:, MLA_Q_RANK + MLA_KV_RANK:]
    z64 = jnp.zeros((D, MLA_NOPE), F32)
    z32 = jnp.zeros((D, HEAD_PAD - MLA_QK), F32)
    w1 = jnp.concatenate([w_lat, z64, w_kr, z32, z64, w_kr[:, half:], w_kr[:, :half], z32], axis=1).astype(BF16)

    wq = w_q_up.reshape(MLA_Q_RANK, MLA_HEADS, MLA_QK)
    nope, rope = wq[..., :MLA_NOPE], wq[..., MLA_NOPE:]
    zq = jnp.zeros((MLA_Q_RANK, MLA_HEADS, HEAD_PAD - MLA_QK), F32)
    wq = jnp.concatenate([nope, rope, zq], axis=-1).reshape(MLA_Q_RANK, MLA_HEADS * HEAD_PAD).astype(BF16)

    wkv = w_kv_up.reshape(MLA_KV_RANK, MLA_HEADS, MLA_NOPE + MLA_V)
    k_nope, v = wkv[..., :MLA_NOPE], wkv[..., MLA_NOPE:]
    zk = jnp.zeros((MLA_KV_RANK, MLA_HEADS, HEAD_PAD - MLA_NOPE), F32)
    wka = jnp.concatenate([k_nope, zk], axis=-1).reshape(MLA_KV_RANK, MLA_HEADS * HEAD_PAD).astype(BF16)
    wvt = v.reshape(MLA_KV_RANK, MLA_HEADS * MLA_V).T.astype(BF16)
    return w1, wq, wka, wvt


def _rope_lanes():
    half = MLA_ROPE // 2
    inv_freq = ROPE_THETA ** (-jnp.arange(0, MLA_ROPE, 2, dtype=F32) / MLA_ROPE)
    z = lambda n: jnp.zeros((n,), F32)
    o = lambda n: jnp.ones((n,), F32)
    pad = HEAD_PAD - MLA_QK
    rows = [jnp.concatenate([z(MLA_NOPE), inv_freq, inv_freq, z(pad)]),
            jnp.concatenate([o(MLA_NOPE), z(MLA_ROPE), z(pad)]),
            jnp.concatenate([z(MLA_NOPE), o(MLA_ROPE), z(pad)]),
            jnp.concatenate([z(MLA_NOPE), -o(half), o(half), z(pad)])]
    return jnp.stack(rows + [z(HEAD_PAD)] * 4)


def kernel(x, c, positions, g_mix, g_ffn, w_ada, b_ada, w_mla_in, g_mla_q, w_mla_q_up, g_mla_kv, w_mla_kv_up,
           w_mla_out, g_kv_b, w_ada_kv, b_ada_kv, w_kv_b, rpb_table, w_q_b, w_o_b, w_ffn_gu, w_ffn_down,
           w_router, w_exp_gu, w_exp_down, g_final):
    B, S, D = x.shape
    T = B * S
    h = x.reshape(T, D)

    c8 = jnp.zeros((8, D), F32).at[:B].set(c)
    mod = _ada(c8, w_ada, b_ada[:, None, :])[:, :B]
    mod = mod.reshape(2, B, N_MOD, 1, D)
    mod_kv = _ada(c8, w_ada_kv[None], b_ada_kv[None, None, :])[0, :B].reshape(B, 2, 1, D)
    sh_kv, sc_kv = mod_kv[:, 0], mod_kv[:, 1]

    def mods(layer):
        return [mod[layer, :, k] for k in range(N_MOD)]

    row = lambda v: v.reshape(1, -1)

    sh_m, sc_m, gt_m, sh_f, sc_f, gt_f = mods(0)
    w1, wq, wka, wvt = _mla_weights(w_mla_in[0], w_mla_q_up[0], w_mla_kv_up[0])
    pos = positions.astype(F32).reshape(T, 1)
    q, k, vt = _mla_proj(h, row(g_mix[0]), sc_m, sh_m, pos, _rope_lanes(), w1, row(g_mla_q[0]), row(g_mla_kv[0]),
                         wq, wka, wvt, tm=TM_MLA_PROJ, batch=B, seq=S)
    o = _mla_attn(q.reshape(B, S, -1), k.reshape(B, S, -1), vt, batch=B, seq=S, tq=MLA_TQ)

    h = _ffn(o.reshape(T, -1), w_mla_out[0].astype(BF16), h, gt_m, row(g_ffn[0]), sc_f, sh_f, gt_f,
             w_ffn_gu[0].astype(BF16), w_ffn_down[0].astype(BF16), tm=TM_FFN, tf=FF_CHUNK,
             tiles_per_batch=S // TM_FFN)

    sh_m, sc_m, gt_m, sh_f, sc_f, gt_f = mods(1)
    qkv = _l1_proj(h, row(g_kv_b), sc_kv, sh_kv, row(g_mix[1]), sc_m, sh_m,
                   w_kv_b.astype(BF16), w_q_b[0].astype(BF16), tm=TM_L1_PROJ, tiles_per_batch=S // TM_L1_PROJ)

    outs, lses = [], []
    for g, (window, dil) in enumerate(DIL_PATTERNS):
        bias = _dil_bias(rpb_table[:, g * DIL_HEADS:(g + 1) * DIL_HEADS], window // dil, dil)
        o_g, lse_g = _dil_attn(*qkv[3 * g:3 * g + 3], bias, group=g, dil=dil, batch=B, seq=S)
        outs.append(o_g)
        lses.append(lse_g)
    h = _dil_out(outs, lses, w_o_b[0].astype(BF16), h, gt_m, tm=TM_DIL_OUT, tiles_per_batch=S // TM_DIL_OUT)

    hn, eids, ranks, gates, counts = _route(h, row(g_ffn[1]), sc_f, sh_f, w_router[0].T, tm=TM_ROUTE,
                                            tiles_per_batch=S // TM_ROUTE)
    n_tiles = (T * 2) // MOE_TM + N_EXPERTS
    dest, tile_expert, n_active, pad_rows = _moe_plan(eids, ranks, counts[:, 0], tm=MOE_TM, n_tiles=n_tiles)
    xs = _dispatch(pad_rows, dest, hn, n_slots=n_tiles * MOE_TM, tm=TM_DISPATCH)
    y = _experts(tile_expert, n_active, xs, w_exp_gu[0].astype(BF16), w_exp_down[0].astype(BF16),
                 tm=MOE_TM, tf=FF_CHUNK)
    out = _combine(dest, y, gates.T, h, gt_f, row(g_final), tm=TM_COMBINE, tiles_per_batch=S // TM_COMBINE)
    return out.reshape(B, S, D)
```

```python
import functools
import math

import jax
import jax.numpy as jnp
from jax import lax
from jax.experimental import pallas as pl
from jax.experimental.pallas import tpu as pltpu

D_MODEL = 1024
N_MOD = 6
EPS = 1e-6

MLA_HEADS = 16
MLA_Q_RANK = 384
MLA_KV_RANK = 256
MLA_NOPE = 64
MLA_ROPE = 32
MLA_V = 64
MLA_QK = MLA_NOPE + MLA_ROPE
ROPE_THETA = 10000.0
LANES = 128
ONES_ROWS = 16
STRIP = 256
HEAD_PAD = 128

DIL_PATTERNS = ((128, 1), (512, 4), (2048, 16))
DIL_GROUPS = len(DIL_PATTERNS)
DIL_HEADS = 8
DIL_HEAD_DIM = 64
DIL_BLOCK = 128
DIL_SUBBLOCKS = 8
DIL_W = DIL_HEADS * DIL_HEAD_DIM
RPB_BUCKETS = 32
RPB_MAX_DIST = 2048

D_FF = 2816
N_EXPERTS = 8
MOE_TM = 512

TM_MLA_PROJ = 1024
TM_FFN = 512
TM_L1_PROJ = 1024
TM_DIL_OUT = 1024
TM_ROUTE = 1024
TM_DISPATCH = 1024
TM_COMBINE = 512
COMBINE_PARTS = 4
MLA_TQ = 1024
FF_CHUNK = 256

MASK_VALUE = -1e30
LOG2E = math.log2(math.e)

F32 = jnp.float32
BF16 = jnp.bfloat16

VMEM_LIMIT = 56 * 1024 * 1024


def _cparams(sem):
    return pltpu.CompilerParams(dimension_semantics=sem, vmem_limit_bytes=VMEM_LIMIT)


def _dot(a, b):
    return jnp.dot(a, b, preferred_element_type=F32)


def _dot_nt(a, b):
    return lax.dot_general(a, b, (((1,), (1,)), ((), ())), preferred_element_type=F32)


def _rms_scale(x):
    return lax.rsqrt(jnp.mean(x * x, axis=-1, keepdims=True) + EPS)


def _normmod(x, g, sc, sh):
    return (x * _rms_scale(x)) * (g * (1.0 + sc)) + sh


def _silu(x):
    return x * (1.0 / (1.0 + jnp.exp(-x)))


def _ada_kernel(c_ref, w_ref, b_ref, o_ref):
    c = c_ref[...]
    o_ref[...] = jnp.dot(_silu(c), w_ref[...], preferred_element_type=F32,
                         precision=lax.Precision.HIGHEST) + b_ref[...]


def _ada(c8, w, b, tn=512):
    L, D, N = w.shape
    return pl.pallas_call(
        _ada_kernel,
        grid=(L, N // tn),
        in_specs=[pl.BlockSpec((8, D), lambda l, j: (0, 0)),
                  pl.BlockSpec((None, D, tn), lambda l, j: (l, 0, j)),
                  pl.BlockSpec((None, 1, tn), lambda l, j: (l, 0, j))],
        out_specs=pl.BlockSpec((None, 8, tn), lambda l, j: (l, 0, j)),
        out_shape=jax.ShapeDtypeStruct((L, 8, N), F32),
        compiler_params=_cparams(("parallel", "parallel")),
        name="ada_mod",
    )(c8, w, b)


def _swap_rope_halves(x, first_half):
    width = x.shape[1]
    return jnp.where(first_half, pltpu.roll(x, width - MLA_ROPE // 2, axis=1), pltpu.roll(x, MLA_ROPE // 2, axis=1))


def _mla_proj_kernel(h_ref, g_ref, sc_ref, sh_ref, pos_ref, rope_ref, w1_ref, gq_ref, gkv_ref,
                     wq_ref, wka_ref, wvt_ref, q_ref, k_ref, vt_ref, *, qscale):
    hn = _normmod(h_ref[...], g_ref[...], sc_ref[...], sh_ref[...]).astype(BF16)
    z = _dot(hn, w1_ref[...])
    ql = z[:, :MLA_Q_RANK]
    kvl = z[:, MLA_Q_RANK:MLA_Q_RANK + MLA_KV_RANK]
    kr = z[:, MLA_Q_RANK + MLA_KV_RANK:MLA_Q_RANK + MLA_KV_RANK + HEAD_PAD]
    krs = z[:, MLA_Q_RANK + MLA_KV_RANK + HEAD_PAD:]
    qn = (ql * _rms_scale(ql) * gq_ref[...]).astype(BF16)
    kvn = (kvl * _rms_scale(kvl) * gkv_ref[...]).astype(BF16)
    ang = pos_ref[...] * rope_ref[0:1, :]
    cos = rope_ref[1:2, :] + rope_ref[2:3, :] * jnp.cos(ang)
    sin = rope_ref[3:4, :] * jnp.sin(ang)
    lane = lax.broadcasted_iota(jnp.int32, (1, 2 * HEAD_PAD), 1) % HEAD_PAD
    first_half = lane < MLA_NOPE + MLA_ROPE // 2
    k_rope = kr * cos + krs * sin
    cos2 = jnp.concatenate([cos, cos], axis=1)
    sin2 = jnp.concatenate([sin, sin], axis=1)
    k_rope2 = jnp.concatenate([k_rope, k_rope], axis=1)
    for hp in range(MLA_HEADS // 2):
        sl = slice(2 * HEAD_PAD * hp, 2 * HEAD_PAD * (hp + 1))
        a = _dot(qn, wq_ref[:, sl])
        q_ref[:, sl] = ((a * cos2 + _swap_rope_halves(a, first_half) * sin2) * qscale).astype(BF16)
        k_ref[:, sl] = (_dot(kvn, wka_ref[:, sl]) + k_rope2).astype(BF16)
    vt_ref[...] = _dot_nt(wvt_ref[...], kvn).astype(BF16)


def _mla_proj(h, g, sc, sh, pos, rope, w1, gq, gkv, wq, wka, wvt, *, tm, batch, seq):
    T, D = h.shape
    HP = MLA_HEADS * HEAD_PAD
    tpb = seq // tm
    row = lambda n: pl.BlockSpec((tm, n), lambda i: (i, 0))
    bvec = pl.BlockSpec((None, 1, D), lambda i: (i // tpb, 0, 0))
    full = lambda a: pl.BlockSpec(a.shape, lambda i: (0,) * a.ndim)
    return pl.pallas_call(
        functools.partial(_mla_proj_kernel, qscale=(MLA_QK ** -0.5) * LOG2E),
        grid=(T // tm,),
        in_specs=[row(D), full(g), bvec, bvec, row(1), full(rope), full(w1), full(gq), full(gkv),
                  full(wq), full(wka), full(wvt)],
        out_specs=[row(HP), row(HP),
                   pl.BlockSpec((None, MLA_HEADS * MLA_V, tm), lambda i: (i // tpb, 0, i % tpb))],
        out_shape=[jax.ShapeDtypeStruct((T, HP), BF16), jax.ShapeDtypeStruct((T, HP), BF16),
                   jax.ShapeDtypeStruct((batch, MLA_HEADS * MLA_V, seq), BF16)],
        compiler_params=_cparams(("parallel",)),
        name="mla_proj",
    )(h, g, sc, sh, pos, rope, w1, gq, gkv, wq, wka, wvt)


def _mla_attn_kernel(q_ref, k_ref, vt_ref, o_ref, m_sc, acc_sc, s_sc, mb_sc, *, tq):
    qi = pl.program_id(2)
    tk = tq
    m_sc[...] = jnp.full(m_sc.shape, -jnp.inf, F32)
    acc_sc[...] = jnp.zeros(acc_sc.shape, F32)

    def scores(j, diagonal, slot, hh, n):
        rows = (n + 1) * STRIP if diagonal else tk
        off = pl.multiple_of(j * tk, tk)
        cols = slice(n * STRIP, (n + 1) * STRIP)
        q = q_ref[cols, HEAD_PAD * hh:HEAD_PAD * (hh + 1)]
        k = k_ref[pl.ds(off, rows), HEAD_PAD * hh:HEAD_PAD * (hh + 1)]
        s = _dot_nt(k, q)
        if diagonal:
            key = lax.broadcasted_iota(jnp.int32, (rows, STRIP), 0)
            qry = lax.broadcasted_iota(jnp.int32, (rows, STRIP), 1) + n * STRIP
            s = jnp.where(key <= qry, s, -jnp.inf)
        s_sc[slot, hh, :rows, cols] = s
        m_blk = jnp.max(jnp.max(s.reshape(rows // 8, 8, STRIP), axis=0), axis=0, keepdims=True)
        mb_sc[slot, hh, :, cols] = jnp.broadcast_to(m_blk, (8, STRIP))

    def softmax_values(j, diagonal, slot, hh, n):
        rows = (n + 1) * STRIP if diagonal else tk
        off = pl.multiple_of(j * tk, tk)
        cols = slice(n * STRIP, (n + 1) * STRIP)
        s3 = s_sc[slot, hh, :rows, cols].reshape(rows // 8, 8, STRIP)
        m_prev = m_sc[hh, :, cols]
        m_new = jnp.maximum(m_prev, mb_sc[slot, hh, :, cols])
        alpha = jnp.exp2(m_prev - m_new)
        p = jnp.exp2(s3 - m_new[None]).reshape(rows, STRIP).astype(BF16)
        vt = vt_ref[MLA_V * hh:MLA_V * (hh + 1), pl.ds(off, rows)]
        vt1 = jnp.concatenate([vt, jnp.ones((ONES_ROWS, rows), BF16)], axis=0)
        acc_sc[hh, :, cols] = alpha[:1] * acc_sc[hh, :, cols] + _dot(vt1, p)
        m_sc[hh, :, cols] = m_new

    strips = [(hh, n) for hh in range(2) for n in range(tq // STRIP)]

    def step(j, even, diagonal_next=False, has_next=True):
        cur, nxt = (0, 1) if even else (1, 0)
        for hh, n in strips:
            if has_next:
                scores(j + 1, diagonal_next, nxt, hh, n)
            softmax_values(j, not has_next, cur, hh, n)

    def first_scores(diagonal):
        for hh, n in strips:
            scores(0, diagonal, 0, hh, n)

    @pl.when(qi == 0)
    def _():
        first_scores(True)

    @pl.when(qi > 0)
    def _():
        first_scores(False)

        def pair(t, carry):
            step(2 * t, True)
            step(2 * t + 1, False)
            return carry

        lax.fori_loop(0, (qi - 1) // 2, pair, 0)

    @pl.when(qi % 2 == 1)
    def _():
        step(qi - 1, True, diagonal_next=True)
        step(qi, False, has_next=False)

    @pl.when((qi % 2 == 0) & (qi > 0))
    def _():
        step(qi - 2, True)
        step(qi - 1, False, diagonal_next=True)

    @pl.when(qi % 2 == 0)
    def _():
        step(qi, True, has_next=False)

    o_t = jnp.concatenate([acc_sc[hh, :MLA_V] / acc_sc[hh, MLA_V:MLA_V + 1] for hh in range(2)], axis=0)
    o_ref[...] = o_t.T.astype(BF16)


def _mla_attn(q, k, vt, *, batch, seq, tq):
    return pl.pallas_call(
        functools.partial(_mla_attn_kernel, tq=tq),
        grid=(batch, MLA_HEADS // 2, seq // tq),
        in_specs=[pl.BlockSpec((None, tq, 2 * HEAD_PAD), lambda b, hp, i: (b, i, hp)),
                  pl.BlockSpec((None, seq, 2 * HEAD_PAD), lambda b, hp, i: (b, 0, hp)),
                  pl.BlockSpec((None, 2 * MLA_V, seq), lambda b, hp, i: (b, hp, 0))],
        out_specs=pl.BlockSpec((None, tq, 2 * MLA_V), lambda b, hp, i: (b, i, hp)),
        out_shape=jax.ShapeDtypeStruct((batch, seq, MLA_HEADS * MLA_V), BF16),
        scratch_shapes=[pltpu.VMEM((2, 8, tq), F32), pltpu.VMEM((2, MLA_V + ONES_ROWS, tq), F32),
                        pltpu.VMEM((2, 2, tq, tq), F32),
                        pltpu.VMEM((2, 2, 8, tq), F32)],
        compiler_params=_cparams(("parallel", "parallel", "arbitrary")),
        name="mla_attn",
    )(q, k, vt)


def _swiglu_tile(x, wgu_ref, wd_ref, act_sc, tf):
    for f in range(D_FF // tf):
        g = _dot(x, wgu_ref[:, f * tf:(f + 1) * tf])
        u = _dot(x, wgu_ref[:, D_FF + f * tf:D_FF + (f + 1) * tf])
        act_sc[:, f * tf:(f + 1) * tf] = (_silu(g) * u).astype(BF16)
    return _dot(act_sc[...], wd_ref[...])


def _ffn_kernel(a_ref, wo_ref, h_ref, gtm_ref, g_ref, sc_ref, sh_ref, gt_ref, wgu_ref, wd_ref, o_ref, act_sc,
                *, tf):
    h = h_ref[...] + gtm_ref[...] * _dot(a_ref[...], wo_ref[...])
    x = _normmod(h, g_ref[...], sc_ref[...], sh_ref[...]).astype(BF16)
    o_ref[...] = h + gt_ref[...] * _swiglu_tile(x, wgu_ref, wd_ref, act_sc, tf)


def _ffn(a, w_o, h, gt_m, g, sc, sh, gt, w_gu, w_d, *, tm, tf, tiles_per_batch):
    T, D = h.shape
    bvec = pl.BlockSpec((None, 1, D), lambda i: (i // tiles_per_batch, 0, 0))
    row = pl.BlockSpec((tm, D), lambda i: (i, 0))
    const = lambda w: pl.BlockSpec(w.shape, lambda i: (0,) * w.ndim, pipeline_mode=pl.Buffered(1))
    return pl.pallas_call(
        functools.partial(_ffn_kernel, tf=tf),
        grid=(T // tm,),
        in_specs=[row, const(w_o), row, bvec, const(g), bvec, bvec, bvec, const(w_gu), const(w_d)],
        out_specs=row,
        out_shape=jax.ShapeDtypeStruct((T, D), F32),
        scratch_shapes=[pltpu.VMEM((tm, D_FF), BF16)],
        compiler_params=_cparams(("parallel",)),
        name="ffn_dense",
    )(a, w_o, h, gt_m, g, sc, sh, gt, w_gu, w_d)


def _l1_proj_kernel(h_ref, gkv_ref, sckv_ref, shkv_ref, gq_ref, scq_ref, shq_ref, wkv_ref, wq_ref, *refs):
    out_refs, stage_sc = refs[:-1], refs[-1]
    tm = h_ref.shape[0]
    x = h_ref[...]
    xh = x * _rms_scale(x)
    hn_kv = (xh * (gkv_ref[...] * (1.0 + sckv_ref[...])) + shkv_ref[...]).astype(BF16)
    hn_q = (xh * (gq_ref[...] * (1.0 + scq_ref[...])) + shq_ref[...]).astype(BF16)
    for g, (_, dil) in enumerate(DIL_PATTERNS):
        ys = (_dot(hn_q, wq_ref[:, g * DIL_W:(g + 1) * DIL_W]) * (DIL_HEAD_DIM ** -0.5),
              _dot(hn_kv, wkv_ref[:, 2 * g * DIL_W:(2 * g + 1) * DIL_W]),
              _dot(hn_kv, wkv_ref[:, (2 * g + 1) * DIL_W:(2 * g + 2) * DIL_W]))
        for y, out_ref in zip(ys, out_refs[3 * g:3 * g + 3]):
            if dil == 1:
                out_ref[...] = y.astype(BF16)
            else:
                for c in range(DIL_W // LANES):
                    stage_sc[c] = y[:, c * LANES:(c + 1) * LANES]
                for r in range(dil):
                    for c in range(DIL_W // LANES):
                        col = r * DIL_W + c * LANES
                        out_ref[:, col:col + LANES] = stage_sc[c, pl.ds(r, tm // dil, stride=dil), :].astype(BF16)


def _l1_proj(h, gkv, sckv, shkv, gq, scq, shq, wkv, wq, *, tm, tiles_per_batch):
    T, D = h.shape
    bvec = pl.BlockSpec((None, 1, D), lambda i: (i // tiles_per_batch, 0, 0))
    full = lambda a: pl.BlockSpec(a.shape, lambda i: (0,) * a.ndim)
    out_specs, out_shape = [], []
    for _, dil in DIL_PATTERNS:
        for _ in range(3):
            out_specs.append(pl.BlockSpec((tm // dil, dil * DIL_W), lambda i: (i, 0)))
            out_shape.append(jax.ShapeDtypeStruct((T // dil, dil * DIL_W), BF16))
    return pl.pallas_call(
        _l1_proj_kernel,
        grid=(T // tm,),
        in_specs=[pl.BlockSpec((tm, D), lambda i: (i, 0)), full(gkv), bvec, bvec, full(gq), bvec, bvec,
                  full(wkv), full(wq)],
        out_specs=out_specs,
        out_shape=out_shape,
        scratch_shapes=[pltpu.VMEM((DIL_W // LANES, tm, LANES), F32)],
        compiler_params=_cparams(("parallel",)),
        name="l1_proj",
    )(h, gkv, sckv, shkv, gq, scq, shq, wkv, wq)


def _dil_attn_kernel(q_ref, kc_ref, kp_ref, vc_ref, vp_ref, bias_ref, o_ref, lse_ref):
    i = pl.program_id(2)
    nsub = q_ref.shape[0] // DIL_BLOCK
    lane = lax.broadcasted_iota(jnp.int32, (1, 2 * DIL_HEAD_DIM), 1)
    lo = lane < DIL_HEAD_DIM
    col = lax.broadcasted_iota(jnp.int32, (1, 2 * DIL_BLOCK), 1)
    edge = jnp.where((col < DIL_BLOCK) & (i == 0), MASK_VALUE, 0.0).astype(F32)
    sls = [slice(2 * DIL_HEAD_DIM * hp, 2 * DIL_HEAD_DIM * (hp + 1)) for hp in range(DIL_HEADS // 2)]
    for sub in range(nsub):
        rows = slice(DIL_BLOCK * sub, DIL_BLOCK * (sub + 1))
        before = slice(DIL_BLOCK * (sub - 1), DIL_BLOCK * sub)
        scores = []
        for hp, sl in enumerate(sls):
            q2 = q_ref[rows, sl]
            zero = jnp.zeros_like(q2)
            qs = jnp.concatenate([jnp.where(lo, q2, zero), jnp.where(lo, zero, q2)], axis=0)
            k_prev = kp_ref[:, sl] if sub == 0 else kc_ref[before, sl]
            k2 = jnp.concatenate([k_prev, kc_ref[rows, sl]], axis=0)
            bias = jnp.concatenate([bias_ref[2 * hp], bias_ref[2 * hp + 1]], axis=0)
            s = _dot_nt(qs, k2) + bias
            scores.append(s + edge if sub == 0 else s)
        probs, stats = [], []
        for s in scores:
            m = jnp.max(s, axis=1, keepdims=True)
            p = jnp.exp(s - m)
            probs.append(p.astype(BF16))
            stats.append((m, jnp.sum(p, axis=1, keepdims=True)))
        for sl, p, (m, l) in zip(sls, probs, stats):
            v_prev = vp_ref[:, sl] if sub == 0 else vc_ref[before, sl]
            v2 = jnp.concatenate([v_prev, vc_ref[rows, sl]], axis=0)
            o = _dot(p, v2) / l
            lse = jnp.broadcast_to(m + jnp.log(l), o.shape)
            o_ref[rows, sl] = jnp.where(lo, o[:DIL_BLOCK], o[DIL_BLOCK:])
            lse_ref[rows, sl] = jnp.where(lo, lse[:DIL_BLOCK], lse[DIL_BLOCK:])


def _dil_attn(q, k, v, bias, *, group, dil, batch, seq):
    n = seq // dil
    nsub = min(DIL_SUBBLOCKS, n // DIL_BLOCK)
    run = nsub * DIL_BLOCK
    assert n % run == 0, (n, run)
    q, k, v = (a.reshape(batch, n, dil * DIL_W) for a in (q, k, v))
    cur = pl.BlockSpec((None, run, DIL_W), lambda b, r, i: (b, i, r))
    prev = pl.BlockSpec((None, DIL_BLOCK, DIL_W), lambda b, r, i: (b, jnp.maximum(i * nsub - 1, 0), r))
    out_sd = jax.ShapeDtypeStruct((batch, n, dil * DIL_W), F32)
    o, lse = pl.pallas_call(
        _dil_attn_kernel,
        grid=(batch, dil, n // run),
        in_specs=[cur, cur, prev, cur, prev, pl.BlockSpec(bias.shape, lambda b, r, i: (0, 0, 0))],
        out_specs=[cur, cur],
        out_shape=[out_sd, out_sd],
        compiler_params=_cparams(("parallel", "parallel", "arbitrary")),
        name=f"dil_attn_g{group}",
    )(q, k, k, v, v, bias)
    return o.reshape(batch * n, dil * DIL_W), lse.reshape(batch * n, dil * DIL_W)


def _rpb_bucket(dist):
    exact = RPB_BUCKETS // 2
    d = jnp.maximum(dist, 0)
    d_f = jnp.maximum(d, 1).astype(F32)
    large = exact + (jnp.log(d_f / exact) / math.log(RPB_MAX_DIST / exact)
                     * (RPB_BUCKETS - exact)).astype(jnp.int32)
    return jnp.where(d < exact, d, jnp.minimum(large, RPB_BUCKETS - 1))


def _dil_bias(table, span, dil):
    period = 3 * DIL_BLOCK
    j = jnp.arange(period)
    k_minus_q = jnp.where(j < 2 * DIL_BLOCK, j, j - period)
    dist = DIL_BLOCK - k_minus_q
    band = (dist >= 0) & (dist <= span)
    prof = jnp.where(band[:, None], table[_rpb_bucket(dist * dil)].astype(F32), MASK_VALUE).T
    skew = jnp.tile(prof, (1, DIL_BLOCK))[:, :DIL_BLOCK * (period - 1)]
    return skew.reshape(DIL_HEADS, DIL_BLOCK, period - 1)[:, :, :2 * DIL_BLOCK]


def _dil_out_kernel(o0_ref, o1_ref, o2_ref, l0_ref, l1_ref, l2_ref, w_ref, h_ref, gt_ref, out_ref, *stages):
    tm = h_ref.shape[0]
    stages = list(stages)

    def token_major(ref, dil):
        if dil == 1:
            return ref[...]
        stage = stages.pop()
        for r in range(dil):
            for c in range(DIL_W // LANES):
                col = r * DIL_W + c * LANES
                stage[c, pl.ds(r, tm // dil, stride=dil), :] = ref[:, col:col + LANES]
        return jnp.concatenate([stage[c] for c in range(DIL_W // LANES)], axis=1)

    dils = [dil for _, dil in DIL_PATTERNS]
    l0, l1, l2 = (token_major(ref, d) for ref, d in zip((l0_ref, l1_ref, l2_ref), dils))
    o0, o1, o2 = (token_major(ref, d) for ref, d in zip((o0_ref, o1_ref, o2_ref), dils))
    m = jnp.maximum(jnp.maximum(l0, l1), l2)
    w0, w1, w2 = jnp.exp(l0 - m), jnp.exp(l1 - m), jnp.exp(l2 - m)
    o = (o0 * w0 + o1 * w1 + o2 * w2) / (w0 + w1 + w2)
    out_ref[...] = h_ref[...] + gt_ref[...] * _dot(o.astype(BF16), w_ref[...])


def _dil_out(os_, ls_, w, h, gt, *, tm, tiles_per_batch):
    T, D = h.shape
    row = lambda n: pl.BlockSpec((tm, n), lambda i: (i, 0))
    grp = [pl.BlockSpec((tm // dil, dil * DIL_W), lambda i: (i, 0)) for _, dil in DIL_PATTERNS]
    n_stage = 2 * sum(1 for _, dil in DIL_PATTERNS if dil > 1)
    return pl.pallas_call(
        _dil_out_kernel,
        grid=(T // tm,),
        in_specs=grp + grp + [pl.BlockSpec(w.shape, lambda i: (0, 0)), row(D),
                              pl.BlockSpec((None, 1, D), lambda i: (i // tiles_per_batch, 0, 0))],
        out_specs=row(D),
        out_shape=jax.ShapeDtypeStruct((T, D), F32),
        scratch_shapes=[pltpu.VMEM((DIL_W // LANES, tm, LANES), F32)] * n_stage,
        compiler_params=_cparams(("parallel",)),
        name="dil_out",
    )(*os_, *ls_, w, h, gt)


def _route_kernel(h_ref, g_ref, sc_ref, sh_ref, wrt_ref, hn_ref, eid_ref, rank_ref, gate_ref, cnt_ref, carry_sc):
    @pl.when(pl.program_id(0) == 0)
    def _():
        carry_sc[...] = jnp.zeros(carry_sc.shape, F32)

    hn = _normmod(h_ref[...], g_ref[...], sc_ref[...], sh_ref[...])
    hn_ref[...] = hn
    tm = hn.shape[0]
    logits = lax.dot_general(wrt_ref[...], hn, (((1,), (1,)), ((), ())), preferred_element_type=F32,
                             precision=lax.Precision.HIGHEST)
    idx = lax.broadcasted_iota(jnp.int32, logits.shape, 0)
    v1 = jnp.max(logits, axis=0, keepdims=True)
    i1 = jnp.min(jnp.where(logits == v1, idx, N_EXPERTS), axis=0, keepdims=True)
    rest = jnp.where(idx == i1, -jnp.inf, logits)
    v2 = jnp.max(rest, axis=0, keepdims=True)
    i2 = jnp.min(jnp.where(rest == v2, idx, N_EXPERTS), axis=0, keepdims=True)
    e = jnp.exp(v2 - v1)
    gate_ref[...] = jnp.concatenate([1.0 / (1.0 + e), e / (1.0 + e)], axis=0)
    eid_ref[...] = jnp.concatenate([i1, i2], axis=0)

    sel = ((idx == i1) | (idx == i2)).astype(BF16)
    before = (lax.broadcasted_iota(jnp.int32, (tm, tm), 0)
              < lax.broadcasted_iota(jnp.int32, (tm, tm), 1)).astype(BF16)
    rank_all = carry_sc[:, :1] + _dot(sel, before)
    r1 = jnp.sum(jnp.where(idx == i1, rank_all, 0.0), axis=0, keepdims=True)
    r2 = jnp.sum(jnp.where(idx == i2, rank_all, 0.0), axis=0, keepdims=True)
    rank_ref[...] = jnp.concatenate([r1, r2], axis=0).astype(jnp.int32)
    carry_sc[...] += jnp.sum(sel.astype(F32), axis=1, keepdims=True)
    cnt_ref[...] = carry_sc[...].astype(jnp.int32)


def _route(h, g, sc, sh, wrt, *, tm, tiles_per_batch):
    T, D = h.shape
    bvec = pl.BlockSpec((None, 1, D), lambda i: (i // tiles_per_batch, 0, 0))
    lane_blk = pl.BlockSpec((2, tm), lambda i: (0, i))
    return pl.pallas_call(
        _route_kernel,
        grid=(T // tm,),
        in_specs=[pl.BlockSpec((tm, D), lambda i: (i, 0)), pl.BlockSpec((1, D), lambda i: (0, 0)), bvec, bvec,
                  pl.BlockSpec(wrt.shape, lambda i: (0, 0))],
        out_specs=[pl.BlockSpec((tm, D), lambda i: (i, 0)), lane_blk, lane_blk, lane_blk,
                   pl.BlockSpec((N_EXPERTS, HEAD_PAD), lambda i: (0, 0))],
        out_shape=[jax.ShapeDtypeStruct((T, D), F32), jax.ShapeDtypeStruct((2, T), jnp.int32),
                   jax.ShapeDtypeStruct((2, T), jnp.int32), jax.ShapeDtypeStruct((2, T), F32),
                   jax.ShapeDtypeStruct((N_EXPERTS, HEAD_PAD), jnp.int32)],
        scratch_shapes=[pltpu.VMEM((N_EXPERTS, HEAD_PAD), F32)],
        compiler_params=_cparams(("arbitrary",)),
        name="moe_route",
    )(h, g, sc, sh, wrt)


def _dispatch_kernel(pad_ref, dest_ref, hn_ref, xs_ref, zero_sc, sem, zsem):
    tm = hn_ref.shape[0]

    def row_copy(r, k):
        return pltpu.make_async_copy(hn_ref.at[pl.ds(r, 1), :], xs_ref.at[pl.ds(dest_ref[k, r], 1), :], sem)

    def issue(r, carry):
        row_copy(r, 0).start(priority=0)
        row_copy(r, 1).start(priority=1)
        return carry

    for r in range(tm):
        issue(r, 0)

    @pl.when(pl.program_id(0) == pl.num_programs(0) - 1)
    def _():
        zero_sc[...] = jnp.zeros(zero_sc.shape, F32)

        def pad_copy(p):
            return pltpu.make_async_copy(zero_sc.at[pl.ds(0, 1), :], xs_ref.at[pl.ds(p, 1), :], zsem)

        for e in range(N_EXPERTS):
            lo, hi = pad_ref[0, e], pad_ref[1, e]

            def zissue(p, carry):
                pad_copy(p).start()
                return carry

            def zwait(p, carry):
                pad_copy(p).wait()
                return carry

            lax.fori_loop(lo, hi, zissue, 0)
            lax.fori_loop(lo, hi, zwait, 0)

        zrows = zero_sc.shape[0]

        def tail_copy(c):
            return pltpu.make_async_copy(zero_sc, xs_ref.at[pl.ds(pl.multiple_of(c * zrows, zrows), zrows), :], zsem)

        def tissue(c, carry):
            tail_copy(c).start()
            return carry

        def twait(c, carry):
            tail_copy(c).wait()
            return carry

        lo, hi = pad_ref[1, N_EXPERTS - 1] // zrows, xs_ref.shape[0] // zrows
        lax.fori_loop(lo, hi, tissue, 0)
        lax.fori_loop(lo, hi, twait, 0)

    for _ in range(2):
        pltpu.make_async_copy(hn_ref, xs_ref.at[pl.ds(0, tm), :], sem).wait()


def _dispatch(pad_rows, dest, hn, *, n_slots, tm):
    T, D = hn.shape
    return pl.pallas_call(
        _dispatch_kernel,
        grid_spec=pltpu.PrefetchScalarGridSpec(
            num_scalar_prefetch=1,
            grid=(T // tm,),
            in_specs=[pl.BlockSpec((2, tm), lambda i, pad: (0, i), memory_space=pltpu.SMEM),
                      pl.BlockSpec((tm, D), lambda i, pad: (i, 0))],
            out_specs=pl.BlockSpec(memory_space=pl.ANY),
            scratch_shapes=[pltpu.VMEM((64, D), F32), pltpu.SemaphoreType.DMA, pltpu.SemaphoreType.DMA]),
        out_shape=jax.ShapeDtypeStruct((n_slots, D), F32),
        compiler_params=_cparams(("arbitrary",)),
        name="moe_dispatch",
    )(pad_rows, dest, hn)


def _experts_kernel(te_ref, na_ref, xs_ref, wgu_ref, wd_ref, y_ref, act_sc, *, tf):
    @pl.when(pl.program_id(0) < na_ref[0])
    def _():
        y_ref[...] = _swiglu_tile(xs_ref[...].astype(BF16), wgu_ref, wd_ref, act_sc, tf)

    @pl.when(pl.program_id(0) >= na_ref[0])
    def _():
        y_ref[...] = jnp.zeros(y_ref.shape, F32)


def _experts(tile_expert, n_active, xs, w_gu, w_d, *, tm, tf):
    P, D = xs.shape

    def tile(j, te, na):
        return jnp.minimum(j, na[0] - 1)

    return pl.pallas_call(
        functools.partial(_experts_kernel, tf=tf),
        grid_spec=pltpu.PrefetchScalarGridSpec(
            num_scalar_prefetch=2,
            grid=(P // tm,),
            in_specs=[pl.BlockSpec((tm, D), lambda j, te, na: (tile(j, te, na), 0)),
                      pl.BlockSpec((None, D, 2 * D_FF), lambda j, te, na: (te[tile(j, te, na)], 0, 0)),
                      pl.BlockSpec((None, D_FF, D), lambda j, te, na: (te[tile(j, te, na)], 0, 0))],
            out_specs=pl.BlockSpec((tm, D), lambda j, te, na: (j, 0)),
            scratch_shapes=[pltpu.VMEM((tm, D_FF), BF16)]),
        out_shape=jax.ShapeDtypeStruct((P, D), F32),
        compiler_params=_cparams(("arbitrary",)),
        name="moe_experts",
    )(tile_expert, n_active, xs, w_gu, w_d)


def _combine_kernel(dest_ref, y_ref, gates_ref, h_ref, gt_ref, gfin_ref, o_ref, ybuf, sems):
    tm = h_ref.shape[0]
    half = tm // COMBINE_PARTS

    def row_copy(r, k):
        return pltpu.make_async_copy(y_ref.at[pl.ds(dest_ref[k, r], 1), :], ybuf.at[k, pl.ds(r, 1), :],
                                     sems.at[r // half])

    for r in range(tm):
        row_copy(r, 0).start(priority=0)
        row_copy(r, 1).start(priority=1)
    for part in range(COMBINE_PARTS):
        rows = slice(part * half, (part + 1) * half)
        for k in range(2):
            pltpu.make_async_copy(y_ref.at[pl.ds(0, half), :], ybuf.at[k, rows], sems.at[part]).wait()
        gates = gates_ref[rows, :]
        moe = gates[:, 0:1] * ybuf[0, rows] + gates[:, 1:2] * ybuf[1, rows]
        y = h_ref[rows, :] + gt_ref[...] * moe
        o_ref[rows, :] = y * _rms_scale(y) * gfin_ref[...]


def _combine(dest, y, gates, h, gt, gfin, *, tm, tiles_per_batch):
    T, D = h.shape
    return pl.pallas_call(
        _combine_kernel,
        grid=(T // tm,),
        in_specs=[pl.BlockSpec((2, tm), lambda i: (0, i), memory_space=pltpu.SMEM),
                  pl.BlockSpec(memory_space=pl.ANY),
                  pl.BlockSpec((tm, 2), lambda i: (i, 0)),
                  pl.BlockSpec((tm, D), lambda i: (i, 0)),
                  pl.BlockSpec((None, 1, D), lambda i: (i // tiles_per_batch, 0, 0)),
                  pl.BlockSpec((1, D), lambda i: (0, 0))],
        out_specs=pl.BlockSpec((tm, D), lambda i: (i, 0)),
        out_shape=jax.ShapeDtypeStruct((T, D), F32),
        scratch_shapes=[pltpu.VMEM((2, tm, D), F32), pltpu.SemaphoreType.DMA((COMBINE_PARTS,))],
        compiler_params=_cparams(("arbitrary",)),
        name="moe_combine",
    )(dest, y, gates, h, gt, gfin)


def _moe_plan(eids, ranks, counts, *, tm, n_tiles):
    padded = (counts + tm - 1) // tm * tm
    ends = jnp.cumsum(padded)
    starts = ends - padded
    dest = ranks
    for e in range(N_EXPERTS):
        dest = dest + jnp.where(eids == e, starts[e], 0)
    tile_start = jnp.arange(n_tiles, dtype=jnp.int32) * tm
    tile_expert = jnp.minimum(jnp.sum(tile_start[:, None] >= ends[None, :], axis=1), N_EXPERTS - 1)
    n_active = (ends[-1] // tm).reshape(1)
    pad_rows = jnp.stack([starts + counts, ends])
    return dest.astype(jnp.int32), tile_expert.astype(jnp.int32), n_active.astype(jnp.int32), pad_rows.astype(jnp.int32)


def _mla_weights(w_in, w_q_up, w_kv_up):
    D = w_in.shape[0]
    half = MLA_ROPE // 2
    w_lat = w_in[:, :MLA_Q_RANK + MLA_KV_RANK]
    w_kr = w_in[:, MLA_Q_RANK + MLA_KV_RANK:]
    z64 = jnp.zeros((D, MLA_NOPE), F32)
    z32 = jnp.zeros((D, HEAD_PAD - MLA_QK), F32)
    w1 = jnp.concatenate([w_lat, z64, w_kr, z32, z64, w_kr[:, half:], w_kr[:, :half], z32], axis=1).astype(BF16)

    wq = w_q_up.reshape(MLA_Q_RANK, MLA_HEADS, MLA_QK)
    nope, rope = wq[..., :MLA_NOPE], wq[..., MLA_NOPE:]
    zq = jnp.zeros((MLA_Q_RANK, MLA_HEADS, HEAD_PAD - MLA_QK), F32)
    wq = jnp.concatenate([nope, rope, zq], axis=-1).reshape(MLA_Q_RANK, MLA_HEADS * HEAD_PAD).astype(BF16)

    wkv = w_kv_up.reshape(MLA_KV_RANK, MLA_HEADS, MLA_NOPE + MLA_V)
    k_nope, v = wkv[..., :MLA_NOPE], wkv[..., MLA_NOPE:]
    zk = jnp.zeros((MLA_KV_RANK, MLA_HEADS, HEAD_PAD - MLA_NOPE), F32)
    wka = jnp.concatenate([k_nope, zk], axis=-1).reshape(MLA_KV_RANK, MLA_HEADS * HEAD_PAD).astype(BF16)
    wvt = v.reshape(MLA_KV_RANK, MLA_HEADS * MLA_V).T.astype(BF16)
    return w1, wq, wka, wvt


def _rope_lanes():
    half = MLA_ROPE // 2
    inv_freq = ROPE_THETA ** (-jnp.arange(0, MLA_ROPE, 2, dtype=F32) / MLA_ROPE)
    z = lambda n: jnp.zeros((n,), F32)
    o = lambda n: jnp.ones((n,), F32)
    pad = HEAD_PAD - MLA_QK
    rows = [jnp.concatenate([z(MLA_NOPE), inv_freq, inv_freq, z(pad)]),
            jnp.concatenate([o(MLA_NOPE), z(MLA_ROPE), z(pad)]),
            jnp.concatenate([z(MLA_NOPE), o(MLA_ROPE), z(pad)]),
            jnp.concatenate([z(MLA_NOPE), -o(half), o(half), z(pad)])]
    return jnp.stack(rows + [z(HEAD_PAD)] * 4)


def kernel(x, c, positions, g_mix, g_ffn, w_ada, b_ada, w_mla_in, g_mla_q, w_mla_q_up, g_mla_kv, w_mla_kv_up,
           w_mla_out, g_kv_b, w_ada_kv, b_ada_kv, w_kv_b, rpb_table, w_q_b, w_o_b, w_ffn_gu, w_ffn_down,
           w_router, w_exp_gu, w_exp_down, g_final):
    B, S, D = x.shape
    T = B * S
    h = x.reshape(T, D)

    c8 = jnp.zeros((8, D), F32).at[:B].set(c)
    mod = _ada(c8, w_ada, b_ada[:, None, :])[:, :B]
    mod = mod.reshape(2, B, N_MOD, 1, D)
    mod_kv = _ada(c8, w_ada_kv[None], b_ada_kv[None, None, :])[0, :B].reshape(B, 2, 1, D)
    sh_kv, sc_kv = mod_kv[:, 0], mod_kv[:, 1]

    def mods(layer):
        return [mod[layer, :, k] for k in range(N_MOD)]

    row = lambda v: v.reshape(1, -1)

    sh_m, sc_m, gt_m, sh_f, sc_f, gt_f = mods(0)
    w1, wq, wka, wvt = _mla_weights(w_mla_in[0], w_mla_q_up[0], w_mla_kv_up[0])
    pos = positions.astype(F32).reshape(T, 1)
    q, k, vt = _mla_proj(h, row(g_mix[0]), sc_m, sh_m, pos, _rope_lanes(), w1, row(g_mla_q[0]), row(g_mla_kv[0]),
                         wq, wka, wvt, tm=TM_MLA_PROJ, batch=B, seq=S)
    o = _mla_attn(q.reshape(B, S, -1), k.reshape(B, S, -1), vt, batch=B, seq=S, tq=MLA_TQ)

    h = _ffn(o.reshape(T, -1), w_mla_out[0].astype(BF16), h, gt_m, row(g_ffn[0]), sc_f, sh_f, gt_f,
             w_ffn_gu[0].astype(BF16), w_ffn_down[0].astype(BF16), tm=TM_FFN, tf=FF_CHUNK,
             tiles_per_batch=S // TM_FFN)

    sh_m, sc_m, gt_m, sh_f, sc_f, gt_f = mods(1)
    qkv = _l1_proj(h, row(g_kv_b), sc_kv, sh_kv, row(g_mix[1]), sc_m, sh_m,
                   w_kv_b.astype(BF16), w_q_b[0].astype(BF16), tm=TM_L1_PROJ, tiles_per_batch=S // TM_L1_PROJ)

    outs, lses = [], []
    for g, (window, dil) in enumerate(DIL_PATTERNS):
        bias = _dil_bias(rpb_table[:, g * DIL_HEADS:(g + 1) * DIL_HEADS], window // dil, dil)
        o_g, lse_g = _dil_attn(*qkv[3 * g:3 * g + 3], bias, group=g, dil=dil, batch=B, seq=S)
        outs.append(o_g)
        lses.append(lse_g)
    h = _dil_out(outs, lses, w_o_b[0].astype(BF16), h, gt_m, tm=TM_DIL_OUT, tiles_per_batch=S // TM_DIL_OUT)

    hn, eids, ranks, gates, counts = _route(h, row(g_ffn[1]), sc_f, sh_f, w_router[0].T, tm=TM_ROUTE,
                                            tiles_per_batch=S // TM_ROUTE)
    n_tiles = (T * 2) // MOE_TM + N_EXPERTS
    dest, tile_expert, n_active, pad_rows = _moe_plan(eids, ranks, counts[:, 0], tm=MOE_TM, n_tiles=n_tiles)
    xs = _dispatch(pad_rows, dest, hn, n_slots=n_tiles * MOE_TM, tm=TM_DISPATCH)
    y = _experts(tile_expert, n_active, xs, w_exp_gu[0].astype(BF16), w_exp_down[0].astype(BF16),
                 tm=MOE_TM, tf=FF_CHUNK)
    out = _combine(dest, y, gates.T, h, gt_f, row(g_final), tm=TM_COMBINE, tiles_per_batch=S // TM_COMBINE)
    return out.reshape(B, S, D)
```

```python
import functools
import math

import jax
import jax.numpy as jnp
from jax import lax
from jax.experimental import pallas as pl
from jax.experimental.pallas import tpu as pltpu

D_MODEL = 1024
N_MOD = 6
EPS = 1e-6

MLA_HEADS = 16
MLA_Q_RANK = 384
MLA_KV_RANK = 256
MLA_NOPE = 64
MLA_ROPE = 32
MLA_V = 64
MLA_QK = MLA_NOPE + MLA_ROPE
ROPE_THETA = 10000.0
LANES = 128
ONES_ROWS = 16
STRIP = 256
HEAD_PAD = 128

DIL_PATTERNS = ((128, 1), (512, 4), (2048, 16))
DIL_GROUPS = len(DIL_PATTERNS)
DIL_HEADS = 8
DIL_HEAD_DIM = 64
DIL_BLOCK = 128
DIL_SUBBLOCKS = 8
DIL_W = DIL_HEADS * DIL_HEAD_DIM
RPB_BUCKETS = 32
RPB_MAX_DIST = 2048

D_FF = 2816
N_EXPERTS = 8
MOE_TM = 512

TM_MLA_PROJ = 1024
TM_FFN = 512
TM_L1_PROJ = 1024
TM_DIL_OUT_ROUTE = 512
TM_DISPATCH = 1024
TM_COMBINE = 512
COMBINE_PARTS = 4
MLA_TQ = 1024
FF_CHUNK = 256

MASK_VALUE = -1e30
LOG2E = math.log2(math.e)

F32 = jnp.float32
BF16 = jnp.bfloat16

VMEM_LIMIT = 56 * 1024 * 1024


def _cparams(sem):
    return pltpu.CompilerParams(dimension_semantics=sem, vmem_limit_bytes=VMEM_LIMIT)


def _dot(a, b):
    return jnp.dot(a, b, preferred_element_type=F32)


def _dot_nt(a, b):
    return lax.dot_general(a, b, (((1,), (1,)), ((), ())), preferred_element_type=F32)


def _rms_scale(x):
    return lax.rsqrt(jnp.mean(x * x, axis=-1, keepdims=True) + EPS)


def _normmod(x, g, sc, sh):
    return (x * _rms_scale(x)) * (g * (1.0 + sc)) + sh


def _silu(x):
    return x * (1.0 / (1.0 + jnp.exp(-x)))


def _ada_kernel(c_ref, w_ref, b_ref, o_ref):
    c = c_ref[...]
    o_ref[...] = jnp.dot(_silu(c), w_ref[...], preferred_element_type=F32,
                         precision=lax.Precision.HIGHEST) + b_ref[...]


def _ada(c8, w, b, tn=512):
    L, D, N = w.shape
    return pl.pallas_call(
        _ada_kernel,
        grid=(L, N // tn),
        in_specs=[pl.BlockSpec((8, D), lambda l, j: (0, 0)),
                  pl.BlockSpec((None, D, tn), lambda l, j: (l, 0, j)),
                  pl.BlockSpec((None, 1, tn), lambda l, j: (l, 0, j))],
        out_specs=pl.BlockSpec((None, 8, tn), lambda l, j: (l, 0, j)),
        out_shape=jax.ShapeDtypeStruct((L, 8, N), F32),
        compiler_params=_cparams(("parallel", "parallel")),
        name="ada_mod",
    )(c8, w, b)


def _swap_rope_halves(x, first_half):
    width = x.shape[1]
    return jnp.where(first_half, pltpu.roll(x, width - MLA_ROPE // 2, axis=1), pltpu.roll(x, MLA_ROPE // 2, axis=1))


def _mla_proj_kernel(h_ref, g_ref, sc_ref, sh_ref, pos_ref, rope_ref, w1_ref, gq_ref, gkv_ref,
                     wq_ref, wka_ref, wvt_ref, q_ref, k_ref, vt_ref, *, qscale):
    hn = _normmod(h_ref[...], g_ref[...], sc_ref[...], sh_ref[...]).astype(BF16)
    z = _dot(hn, w1_ref[...])
    ql = z[:, :MLA_Q_RANK]
    kvl = z[:, MLA_Q_RANK:MLA_Q_RANK + MLA_KV_RANK]
    kr = z[:, MLA_Q_RANK + MLA_KV_RANK:MLA_Q_RANK + MLA_KV_RANK + HEAD_PAD]
    krs = z[:, MLA_Q_RANK + MLA_KV_RANK + HEAD_PAD:]
    qn = (ql * _rms_scale(ql) * gq_ref[...]).astype(BF16)
    kvn = (kvl * _rms_scale(kvl) * gkv_ref[...]).astype(BF16)
    ang = pos_ref[...] * rope_ref[0:1, :]
    cos = rope_ref[1:2, :] + rope_ref[2:3, :] * jnp.cos(ang)
    sin = rope_ref[3:4, :] * jnp.sin(ang)
    lane = lax.broadcasted_iota(jnp.int32, (1, 2 * HEAD_PAD), 1) % HEAD_PAD
    first_half = lane < MLA_NOPE + MLA_ROPE // 2
    k_rope = kr * cos + krs * sin
    cos2 = jnp.concatenate([cos, cos], axis=1)
    sin2 = jnp.concatenate([sin, sin], axis=1)
    k_rope2 = jnp.concatenate([k_rope, k_rope], axis=1)
    for hp in range(MLA_HEADS // 2):
        sl = slice(2 * HEAD_PAD * hp, 2 * HEAD_PAD * (hp + 1))
        a = _dot(qn, wq_ref[:, sl])
        q_ref[:, sl] = ((a * cos2 + _swap_rope_halves(a, first_half) * sin2) * qscale).astype(BF16)
        k_ref[:, sl] = (_dot(kvn, wka_ref[:, sl]) + k_rope2).astype(BF16)
    vt_ref[...] = _dot_nt(wvt_ref[...], kvn).astype(BF16)


def _mla_proj(h, g, sc, sh, pos, rope, w1, gq, gkv, wq, wka, wvt, *, tm, batch, seq):
    T, D = h.shape
    HP = MLA_HEADS * HEAD_PAD
    tpb = seq // tm
    row = lambda n: pl.BlockSpec((tm, n), lambda i: (i, 0))
    bvec = pl.BlockSpec((None, 1, D), lambda i: (i // tpb, 0, 0))
    full = lambda a: pl.BlockSpec(a.shape, lambda i: (0,) * a.ndim)
    return pl.pallas_call(
        functools.partial(_mla_proj_kernel, qscale=(MLA_QK ** -0.5) * LOG2E),
        grid=(T // tm,),
        in_specs=[row(D), full(g), bvec, bvec, row(1), full(rope), full(w1), full(gq), full(gkv),
                  full(wq), full(wka), full(wvt)],
        out_specs=[row(HP), row(HP),
                   pl.BlockSpec((None, MLA_HEADS * MLA_V, tm), lambda i: (i // tpb, 0, i % tpb))],
        out_shape=[jax.ShapeDtypeStruct((T, HP), BF16), jax.ShapeDtypeStruct((T, HP), BF16),
                   jax.ShapeDtypeStruct((batch, MLA_HEADS * MLA_V, seq), BF16)],
        compiler_params=_cparams(("parallel",)),
        name="mla_proj",
    )(h, g, sc, sh, pos, rope, w1, gq, gkv, wq, wka, wvt)


def _mla_attn_kernel(q_ref, k_ref, vt_ref, o_ref, m_sc, acc_sc, s_sc, mb_sc, *, tq):
    qi = pl.program_id(2)
    tk = tq
    m_sc[...] = jnp.full(m_sc.shape, -jnp.inf, F32)
    acc_sc[...] = jnp.zeros(acc_sc.shape, F32)

    def scores(j, diagonal, slot, hh, n):
        rows = (n + 1) * STRIP if diagonal else tk
        off = pl.multiple_of(j * tk, tk)
        cols = slice(n * STRIP, (n + 1) * STRIP)
        q = q_ref[cols, HEAD_PAD * hh:HEAD_PAD * (hh + 1)]
        k = k_ref[pl.ds(off, rows), HEAD_PAD * hh:HEAD_PAD * (hh + 1)]
        s = _dot_nt(k, q)
        if diagonal:
            key = lax.broadcasted_iota(jnp.int32, (rows, STRIP), 0)
            qry = lax.broadcasted_iota(jnp.int32, (rows, STRIP), 1) + n * STRIP
            s = jnp.where(key <= qry, s, -jnp.inf)
        s_sc[slot, hh, :rows, cols] = s
        m_blk = jnp.max(jnp.max(s.reshape(rows // 8, 8, STRIP), axis=0), axis=0, keepdims=True)
        mb_sc[slot, hh, :, cols] = jnp.broadcast_to(m_blk, (8, STRIP))

    def softmax_values(j, diagonal, slot, hh, n):
        rows = (n + 1) * STRIP if diagonal else tk
        off = pl.multiple_of(j * tk, tk)
        cols = slice(n * STRIP, (n + 1) * STRIP)
        s3 = s_sc[slot, hh, :rows, cols].reshape(rows // 8, 8, STRIP)
        m_prev = m_sc[hh, :, cols]
        m_new = jnp.maximum(m_prev, mb_sc[slot, hh, :, cols])
        alpha = jnp.exp2(m_prev - m_new)
        p = jnp.exp2(s3 - m_new[None]).reshape(rows, STRIP).astype(BF16)
        vt = vt_ref[MLA_V * hh:MLA_V * (hh + 1), pl.ds(off, rows)]
        vt1 = jnp.concatenate([vt, jnp.ones((ONES_ROWS, rows), BF16)], axis=0)
        acc_sc[hh, :, cols] = alpha[:1] * acc_sc[hh, :, cols] + _dot(vt1, p)
        m_sc[hh, :, cols] = m_new

    strips = [(hh, n) for hh in range(2) for n in range(tq // STRIP)]

    def step(j, even, diagonal_next=False, has_next=True):
        cur, nxt = (0, 1) if even else (1, 0)
        for hh, n in strips:
            if has_next:
                scores(j + 1, diagonal_next, nxt, hh, n)
            softmax_values(j, not has_next, cur, hh, n)

    def first_scores(diagonal):
        for hh, n in strips:
            scores(0, diagonal, 0, hh, n)

    @pl.when(qi == 0)
    def _():
        first_scores(True)

    @pl.when(qi > 0)
    def _():
        first_scores(False)

        def pair(t, carry):
            step(2 * t, True)
            step(2 * t + 1, False)
            return carry

        lax.fori_loop(0, (qi - 1) // 2, pair, 0)

    @pl.when(qi % 2 == 1)
    def _():
        step(qi - 1, True, diagonal_next=True)
        step(qi, False, has_next=False)

    @pl.when((qi % 2 == 0) & (qi > 0))
    def _():
        step(qi - 2, True)
        step(qi - 1, False, diagonal_next=True)

    @pl.when(qi % 2 == 0)
    def _():
        step(qi, True, has_next=False)

    o_t = jnp.concatenate([acc_sc[hh, :MLA_V] / acc_sc[hh, MLA_V:MLA_V + 1] for hh in range(2)], axis=0)
    o_ref[...] = o_t.T.astype(BF16)


def _mla_attn(q, k, vt, *, batch, seq, tq):
    return pl.pallas_call(
        functools.partial(_mla_attn_kernel, tq=tq),
        grid=(batch, MLA_HEADS // 2, seq // tq),
        in_specs=[pl.BlockSpec((None, tq, 2 * HEAD_PAD), lambda b, hp, i: (b, i, hp)),
                  pl.BlockSpec((None, seq, 2 * HEAD_PAD), lambda b, hp, i: (b, 0, hp)),
                  pl.BlockSpec((None, 2 * MLA_V, seq), lambda b, hp, i: (b, hp, 0))],
        out_specs=pl.BlockSpec((None, tq, 2 * MLA_V), lambda b, hp, i: (b, i, hp)),
        out_shape=jax.ShapeDtypeStruct((batch, seq, MLA_HEADS * MLA_V), BF16),
        scratch_shapes=[pltpu.VMEM((2, 8, tq), F32), pltpu.VMEM((2, MLA_V + ONES_ROWS, tq), F32),
                        pltpu.VMEM((2, 2, tq, tq), F32),
                        pltpu.VMEM((2, 2, 8, tq), F32)],
        compiler_params=_cparams(("parallel", "parallel", "arbitrary")),
        name="mla_attn",
    )(q, k, vt)


def _swiglu_tile(x, wgu_ref, wd_ref, act_sc, tf):
    for f in range(D_FF // tf):
        g = _dot(x, wgu_ref[:, f * tf:(f + 1) * tf])
        u = _dot(x, wgu_ref[:, D_FF + f * tf:D_FF + (f + 1) * tf])
        act_sc[:, f * tf:(f + 1) * tf] = (_silu(g) * u).astype(BF16)
    return _dot(act_sc[...], wd_ref[...])


def _ffn_kernel(a_ref, wo_ref, h_ref, gtm_ref, g_ref, sc_ref, sh_ref, gt_ref, wgu_ref, wd_ref, o_ref, act_sc,
                *, tf):
    h = h_ref[...] + gtm_ref[...] * _dot(a_ref[...], wo_ref[...])
    x = _normmod(h, g_ref[...], sc_ref[...], sh_ref[...]).astype(BF16)
    o_ref[...] = h + gt_ref[...] * _swiglu_tile(x, wgu_ref, wd_ref, act_sc, tf)


def _ffn(a, w_o, h, gt_m, g, sc, sh, gt, w_gu, w_d, *, tm, tf, tiles_per_batch):
    T, D = h.shape
    bvec = pl.BlockSpec((None, 1, D), lambda i: (i // tiles_per_batch, 0, 0))
    row = pl.BlockSpec((tm, D), lambda i: (i, 0))
    const = lambda w: pl.BlockSpec(w.shape, lambda i: (0,) * w.ndim, pipeline_mode=pl.Buffered(1))
    return pl.pallas_call(
        functools.partial(_ffn_kernel, tf=tf),
        grid=(T // tm,),
        in_specs=[row, const(w_o), row, bvec, const(g), bvec, bvec, bvec, const(w_gu), const(w_d)],
        out_specs=row,
        out_shape=jax.ShapeDtypeStruct((T, D), F32),
        scratch_shapes=[pltpu.VMEM((tm, D_FF), BF16)],
        compiler_params=_cparams(("parallel",)),
        name="ffn_dense",
    )(a, w_o, h, gt_m, g, sc, sh, gt, w_gu, w_d)


def _l1_proj_kernel(h_ref, gkv_ref, sckv_ref, shkv_ref, gq_ref, scq_ref, shq_ref, wkv_ref, wq_ref, *refs):
    out_refs, stage_sc = refs[:-1], refs[-1]
    tm = h_ref.shape[0]
    x = h_ref[...]
    xh = x * _rms_scale(x)
    hn_kv = (xh * (gkv_ref[...] * (1.0 + sckv_ref[...])) + shkv_ref[...]).astype(BF16)
    hn_q = (xh * (gq_ref[...] * (1.0 + scq_ref[...])) + shq_ref[...]).astype(BF16)
    for g, (_, dil) in enumerate(DIL_PATTERNS):
        ys = (_dot(hn_q, wq_ref[:, g * DIL_W:(g + 1) * DIL_W]) * (DIL_HEAD_DIM ** -0.5),
              _dot(hn_kv, wkv_ref[:, 2 * g * DIL_W:(2 * g + 1) * DIL_W]),
              _dot(hn_kv, wkv_ref[:, (2 * g + 1) * DIL_W:(2 * g + 2) * DIL_W]))
        for y, out_ref in zip(ys, out_refs[3 * g:3 * g + 3]):
            if dil == 1:
                out_ref[...] = y.astype(BF16)
            else:
                for c in range(DIL_W // LANES):
                    stage_sc[c] = y[:, c * LANES:(c + 1) * LANES]
                for r in range(dil):
                    for c in range(DIL_W // LANES):
                        col = r * DIL_W + c * LANES
                        out_ref[:, col:col + LANES] = stage_sc[c, pl.ds(r, tm // dil, stride=dil), :].astype(BF16)


def _l1_proj(h, gkv, sckv, shkv, gq, scq, shq, wkv, wq, *, tm, tiles_per_batch):
    T, D = h.shape
    bvec = pl.BlockSpec((None, 1, D), lambda i: (i // tiles_per_batch, 0, 0))
    full = lambda a: pl.BlockSpec(a.shape, lambda i: (0,) * a.ndim)
    out_specs, out_shape = [], []
    for _, dil in DIL_PATTERNS:
        for _ in range(3):
            out_specs.append(pl.BlockSpec((tm // dil, dil * DIL_W), lambda i: (i, 0)))
            out_shape.append(jax.ShapeDtypeStruct((T // dil, dil * DIL_W), BF16))
    return pl.pallas_call(
        _l1_proj_kernel,
        grid=(T // tm,),
        in_specs=[pl.BlockSpec((tm, D), lambda i: (i, 0)), full(gkv), bvec, bvec, full(gq), bvec, bvec,
                  full(wkv), full(wq)],
        out_specs=out_specs,
        out_shape=out_shape,
        scratch_shapes=[pltpu.VMEM((DIL_W // LANES, tm, LANES), F32)],
        compiler_params=_cparams(("parallel",)),
        name="l1_proj",
    )(h, gkv, sckv, shkv, gq, scq, shq, wkv, wq)


def _dil_attn_kernel(q_ref, kc_ref, kp_ref, vc_ref, vp_ref, bias_ref, o_ref, lse_ref):
    i = pl.program_id(2)
    nsub = q_ref.shape[0] // DIL_BLOCK
    lane = lax.broadcasted_iota(jnp.int32, (1, 2 * DIL_HEAD_DIM), 1)
    lo = lane < DIL_HEAD_DIM
    col = lax.broadcasted_iota(jnp.int32, (1, 2 * DIL_BLOCK), 1)
    edge = jnp.where((col < DIL_BLOCK) & (i == 0), MASK_VALUE, 0.0).astype(F32)
    sls = [slice(2 * DIL_HEAD_DIM * hp, 2 * DIL_HEAD_DIM * (hp + 1)) for hp in range(DIL_HEADS // 2)]
    for sub in range(nsub):
        rows = slice(DIL_BLOCK * sub, DIL_BLOCK * (sub + 1))
        before = slice(DIL_BLOCK * (sub - 1), DIL_BLOCK * sub)
        scores = []
        for hp, sl in enumerate(sls):
            q2 = q_ref[rows, sl]
            zero = jnp.zeros_like(q2)
            qs = jnp.concatenate([jnp.where(lo, q2, zero), jnp.where(lo, zero, q2)], axis=0)
            k_prev = kp_ref[:, sl] if sub == 0 else kc_ref[before, sl]
            k2 = jnp.concatenate([k_prev, kc_ref[rows, sl]], axis=0)
            bias = jnp.concatenate([bias_ref[2 * hp], bias_ref[2 * hp + 1]], axis=0)
            s = _dot_nt(qs, k2) + bias
            scores.append(s + edge if sub == 0 else s)
        probs, stats = [], []
        for s in scores:
            m = jnp.max(s, axis=1, keepdims=True)
            p = jnp.exp(s - m)
            probs.append(p.astype(BF16))
            stats.append((m, jnp.sum(p, axis=1, keepdims=True)))
        for sl, p, (m, l) in zip(sls, probs, stats):
            v_prev = vp_ref[:, sl] if sub == 0 else vc_ref[before, sl]
            v2 = jnp.concatenate([v_prev, vc_ref[rows, sl]], axis=0)
            o = _dot(p, v2) / l
            lse = jnp.broadcast_to(m + jnp.log(l), o.shape)
            o_ref[rows, sl] = jnp.where(lo, o[:DIL_BLOCK], o[DIL_BLOCK:])
            lse_ref[rows, sl] = jnp.where(lo, lse[:DIL_BLOCK], lse[DIL_BLOCK:])


def _dil_attn(q, k, v, bias, *, group, dil, batch, seq):
    n = seq // dil
    nsub = min(DIL_SUBBLOCKS, n // DIL_BLOCK)
    run = nsub * DIL_BLOCK
    assert n % run == 0, (n, run)
    q, k, v = (a.reshape(batch, n, dil * DIL_W) for a in (q, k, v))
    cur = pl.BlockSpec((None, run, DIL_W), lambda b, r, i: (b, i, r))
    prev = pl.BlockSpec((None, DIL_BLOCK, DIL_W), lambda b, r, i: (b, jnp.maximum(i * nsub - 1, 0), r))
    out_sd = jax.ShapeDtypeStruct((batch, n, dil * DIL_W), F32)
    o, lse = pl.pallas_call(
        _dil_attn_kernel,
        grid=(batch, dil, n // run),
        in_specs=[cur, cur, prev, cur, prev, pl.BlockSpec(bias.shape, lambda b, r, i: (0, 0, 0))],
        out_specs=[cur, cur],
        out_shape=[out_sd, out_sd],
        compiler_params=_cparams(("parallel", "parallel", "arbitrary")),
        name=f"dil_attn_g{group}",
    )(q, k, k, v, v, bias)
    return o.reshape(batch * n, dil * DIL_W), lse.reshape(batch * n, dil * DIL_W)


def _rpb_bucket(dist):
    exact = RPB_BUCKETS // 2
    d = jnp.maximum(dist, 0)
    d_f = jnp.maximum(d, 1).astype(F32)
    large = exact + (jnp.log(d_f / exact) / math.log(RPB_MAX_DIST / exact)
                     * (RPB_BUCKETS - exact)).astype(jnp.int32)
    return jnp.where(d < exact, d, jnp.minimum(large, RPB_BUCKETS - 1))


def _dil_bias(table, span, dil):
    period = 3 * DIL_BLOCK
    j = jnp.arange(period)
    k_minus_q = jnp.where(j < 2 * DIL_BLOCK, j, j - period)
    dist = DIL_BLOCK - k_minus_q
    band = (dist >= 0) & (dist <= span)
    prof = jnp.where(band[:, None], table[_rpb_bucket(dist * dil)].astype(F32), MASK_VALUE).T
    skew = jnp.tile(prof, (1, DIL_BLOCK))[:, :DIL_BLOCK * (period - 1)]
    return skew.reshape(DIL_HEADS, DIL_BLOCK, period - 1)[:, :, :2 * DIL_BLOCK]


def _dil_out_kernel(o0_ref, o1_ref, o2_ref, l0_ref, l1_ref, l2_ref, w_ref, h_ref, gt_ref, out_ref, *stages):
    tm = h_ref.shape[0]
    stages = list(stages)

    def token_major(ref, dil):
        if dil == 1:
            return ref[...]
        stage = stages.pop()
        for r in range(dil):
            for c in range(DIL_W // LANES):
                col = r * DIL_W + c * LANES
                stage[c, pl.ds(r, tm // dil, stride=dil), :] = ref[:, col:col + LANES]
        return jnp.concatenate([stage[c] for c in range(DIL_W // LANES)], axis=1)

    dils = [dil for _, dil in DIL_PATTERNS]
    l0, l1, l2 = (token_major(ref, d) for ref, d in zip((l0_ref, l1_ref, l2_ref), dils))
    o0, o1, o2 = (token_major(ref, d) for ref, d in zip((o0_ref, o1_ref, o2_ref), dils))
    m = jnp.maximum(jnp.maximum(l0, l1), l2)
    w0, w1, w2 = jnp.exp(l0 - m), jnp.exp(l1 - m), jnp.exp(l2 - m)
    o = (o0 * w0 + o1 * w1 + o2 * w2) / (w0 + w1 + w2)
    out_ref[...] = h_ref[...] + gt_ref[...] * _dot(o.astype(BF16), w_ref[...])


def _route_kernel(h_ref, g_ref, sc_ref, sh_ref, wrt_ref, hn_ref, eid_ref, rank_ref, gate_ref, cnt_ref, carry_sc):
    @pl.when(pl.program_id(0) == 0)
    def _():
        carry_sc[...] = jnp.zeros(carry_sc.shape, F32)

    hn = _normmod(h_ref[...], g_ref[...], sc_ref[...], sh_ref[...])
    hn_ref[...] = hn
    tm = hn.shape[0]
    logits = lax.dot_general(wrt_ref[...], hn, (((1,), (1,)), ((), ())), preferred_element_type=F32,
                             precision=lax.Precision.HIGHEST)
    idx = lax.broadcasted_iota(jnp.int32, logits.shape, 0)
    v1 = jnp.max(logits, axis=0, keepdims=True)
    i1 = jnp.min(jnp.where(logits == v1, idx, N_EXPERTS), axis=0, keepdims=True)
    rest = jnp.where(idx == i1, -jnp.inf, logits)
    v2 = jnp.max(rest, axis=0, keepdims=True)
    i2 = jnp.min(jnp.where(rest == v2, idx, N_EXPERTS), axis=0, keepdims=True)
    e = jnp.exp(v2 - v1)
    gate_ref[...] = jnp.concatenate([1.0 / (1.0 + e), e / (1.0 + e)], axis=0)
    eid_ref[...] = jnp.concatenate([i1, i2], axis=0)

    sel = ((idx == i1) | (idx == i2)).astype(BF16)
    before = (lax.broadcasted_iota(jnp.int32, (tm, tm), 0)
              < lax.broadcasted_iota(jnp.int32, (tm, tm), 1)).astype(BF16)
    rank_all = carry_sc[:, :1] + _dot(sel, before)
    r1 = jnp.sum(jnp.where(idx == i1, rank_all, 0.0), axis=0, keepdims=True)
    r2 = jnp.sum(jnp.where(idx == i2, rank_all, 0.0), axis=0, keepdims=True)
    rank_ref[...] = jnp.concatenate([r1, r2], axis=0).astype(jnp.int32)
    carry_sc[...] += jnp.sum(sel.astype(F32), axis=1, keepdims=True)
    cnt_ref[...] = carry_sc[...].astype(jnp.int32)


def _dil_out_route_kernel(*refs):
    (o0, o1, o2, l0, l1, l2, w_ref, h_ref, gt_ref, g_ref, sc_ref, sh_ref, wrt_ref,
     hout_ref, hn_ref, eid_ref, rank_ref, gate_ref, cnt_ref, *scratch) = refs
    stages, carry_sc = scratch[:-1], scratch[-1]
    _dil_out_kernel(o0, o1, o2, l0, l1, l2, w_ref, h_ref, gt_ref, hout_ref, *stages)
    _route_kernel(hout_ref, g_ref, sc_ref, sh_ref, wrt_ref, hn_ref, eid_ref, rank_ref, gate_ref, cnt_ref, carry_sc)


def _dil_out_route(os_, ls_, w, h, gt, g, sc, sh, wrt, *, tm, tiles_per_batch):
    T, D = h.shape
    row = pl.BlockSpec((tm, D), lambda i: (i, 0))
    bvec = pl.BlockSpec((None, 1, D), lambda i: (i // tiles_per_batch, 0, 0))
    grp = [pl.BlockSpec((tm // dil, dil * DIL_W), lambda i: (i, 0)) for _, dil in DIL_PATTERNS]
    lane_blk = pl.BlockSpec((2, tm), lambda i: (0, i))
    n_stage = 2 * sum(1 for _, dil in DIL_PATTERNS if dil > 1)
    return pl.pallas_call(
        _dil_out_route_kernel,
        grid=(T // tm,),
        in_specs=grp + grp + [pl.BlockSpec(w.shape, lambda i: (0, 0)), row, bvec,
                              pl.BlockSpec((1, D), lambda i: (0, 0)), bvec, bvec,
                              pl.BlockSpec(wrt.shape, lambda i: (0, 0))],
        out_specs=[row, row, lane_blk, lane_blk, lane_blk, pl.BlockSpec((N_EXPERTS, HEAD_PAD), lambda i: (0, 0))],
        out_shape=[jax.ShapeDtypeStruct((T, D), F32), jax.ShapeDtypeStruct((T, D), F32),
                   jax.ShapeDtypeStruct((2, T), jnp.int32), jax.ShapeDtypeStruct((2, T), jnp.int32),
                   jax.ShapeDtypeStruct((2, T), F32), jax.ShapeDtypeStruct((N_EXPERTS, HEAD_PAD), jnp.int32)],
        scratch_shapes=[pltpu.VMEM((DIL_W // LANES, tm, LANES), F32)] * n_stage
        + [pltpu.VMEM((N_EXPERTS, HEAD_PAD), F32)],
        compiler_params=_cparams(("arbitrary",)),
        name="dil_out_route",
    )(*os_, *ls_, w, h, gt, g, sc, sh, wrt)


def _dispatch_kernel(pad_ref, dest_ref, hn_ref, xs_ref, zero_sc, sem, zsem):
    tm = hn_ref.shape[0]

    def row_copy(r, k):
        return pltpu.make_async_copy(hn_ref.at[pl.ds(r, 1), :], xs_ref.at[pl.ds(dest_ref[k, r], 1), :], sem)

    def issue(r, carry):
        row_copy(r, 0).start(priority=0)
        row_copy(r, 1).start(priority=1)
        return carry

    for r in range(tm):
        issue(r, 0)

    @pl.when(pl.program_id(0) == pl.num_programs(0) - 1)
    def _():
        zero_sc[...] = jnp.zeros(zero_sc.shape, F32)

        def pad_copy(p):
            return pltpu.make_async_copy(zero_sc.at[pl.ds(0, 1), :], xs_ref.at[pl.ds(p, 1), :], zsem)

        for e in range(N_EXPERTS):
            lo, hi = pad_ref[0, e], pad_ref[1, e]

            def zissue(p, carry):
                pad_copy(p).start()
                return carry

            def zwait(p, carry):
                pad_copy(p).wait()
                return carry

            lax.fori_loop(lo, hi, zissue, 0)
            lax.fori_loop(lo, hi, zwait, 0)

        zrows = zero_sc.shape[0]

        def tail_copy(c):
            return pltpu.make_async_copy(zero_sc, xs_ref.at[pl.ds(pl.multiple_of(c * zrows, zrows), zrows), :], zsem)

        def tissue(c, carry):
            tail_copy(c).start()
            return carry

        def twait(c, carry):
            tail_copy(c).wait()
            return carry

        lo, hi = pad_ref[1, N_EXPERTS - 1] // zrows, xs_ref.shape[0] // zrows
        lax.fori_loop(lo, hi, tissue, 0)
        lax.fori_loop(lo, hi, twait, 0)

    for _ in range(2):
        pltpu.make_async_copy(hn_ref, xs_ref.at[pl.ds(0, tm), :], sem).wait()


def _dispatch(pad_rows, dest, hn, *, n_slots, tm):
    T, D = hn.shape
    return pl.pallas_call(
        _dispatch_kernel,
        grid_spec=pltpu.PrefetchScalarGridSpec(
            num_scalar_prefetch=1,
            grid=(T // tm,),
            in_specs=[pl.BlockSpec((2, tm), lambda i, pad: (0, i), memory_space=pltpu.SMEM),
                      pl.BlockSpec((tm, D), lambda i, pad: (i, 0))],
            out_specs=pl.BlockSpec(memory_space=pl.ANY),
            scratch_shapes=[pltpu.VMEM((64, D), F32), pltpu.SemaphoreType.DMA, pltpu.SemaphoreType.DMA]),
        out_shape=jax.ShapeDtypeStruct((n_slots, D), F32),
        compiler_params=_cparams(("arbitrary",)),
        name="moe_dispatch",
    )(pad_rows, dest, hn)


def _experts_kernel(te_ref, na_ref, xs_ref, wgu_ref, wd_ref, y_ref, act_sc, *, tf):
    @pl.when(pl.program_id(0) < na_ref[0])
    def _():
        y_ref[...] = _swiglu_tile(xs_ref[...].astype(BF16), wgu_ref, wd_ref, act_sc, tf)

    @pl.when(pl.program_id(0) >= na_ref[0])
    def _():
        y_ref[...] = jnp.zeros(y_ref.shape, F32)


def _experts(tile_expert, n_active, xs, w_gu, w_d, *, tm, tf):
    P, D = xs.shape

    def tile(j, te, na):
        return jnp.minimum(j, na[0] - 1)

    return pl.pallas_call(
        functools.partial(_experts_kernel, tf=tf),
        grid_spec=pltpu.PrefetchScalarGridSpec(
            num_scalar_prefetch=2,
            grid=(P // tm,),
            in_specs=[pl.BlockSpec((tm, D), lambda j, te, na: (tile(j, te, na), 0)),
                      pl.BlockSpec((None, D, 2 * D_FF), lambda j, te, na: (te[tile(j, te, na)], 0, 0)),
                      pl.BlockSpec((None, D_FF, D), lambda j, te, na: (te[tile(j, te, na)], 0, 0))],
            out_specs=pl.BlockSpec((tm, D), lambda j, te, na: (j, 0)),
            scratch_shapes=[pltpu.VMEM((tm, D_FF), BF16)]),
        out_shape=jax.ShapeDtypeStruct((P, D), F32),
        compiler_params=_cparams(("arbitrary",)),
        name="moe_experts",
    )(tile_expert, n_active, xs, w_gu, w_d)


def _combine_kernel(dest_ref, y_ref, gates_ref, h_ref, gt_ref, gfin_ref, o_ref, ybuf, sems):
    tm = h_ref.shape[0]
    half = tm // COMBINE_PARTS

    def row_copy(r, k):
        return pltpu.make_async_copy(y_ref.at[pl.ds(dest_ref[k, r], 1), :], ybuf.at[k, pl.ds(r, 1), :],
                                     sems.at[r // half])

    for r in range(tm):
        row_copy(r, 0).start(priority=0)
        row_copy(r, 1).start(priority=1)
    for part in range(COMBINE_PARTS):
        rows = slice(part * half, (part + 1) * half)
        for k in range(2):
            pltpu.make_async_copy(y_ref.at[pl.ds(0, half), :], ybuf.at[k, rows], sems.at[part]).wait()
        gates = gates_ref[rows, :]
        moe = gates[:, 0:1] * ybuf[0, rows] + gates[:, 1:2] * ybuf[1, rows]
        y = h_ref[rows, :] + gt_ref[...] * moe
        o_ref[rows, :] = y * _rms_scale(y) * gfin_ref[...]


def _combine(dest, y, gates, h, gt, gfin, *, tm, tiles_per_batch):
    T, D = h.shape
    return pl.pallas_call(
        _combine_kernel,
        grid=(T // tm,),
        in_specs=[pl.BlockSpec((2, tm), lambda i: (0, i), memory_space=pltpu.SMEM),
                  pl.BlockSpec(memory_space=pl.ANY),
                  pl.BlockSpec((tm, 2), lambda i: (i, 0)),
                  pl.BlockSpec((tm, D), lambda i: (i, 0)),
                  pl.BlockSpec((None, 1, D), lambda i: (i // tiles_per_batch, 0, 0)),
                  pl.BlockSpec((1, D), lambda i: (0, 0))],
        out_specs=pl.BlockSpec((tm, D), lambda i: (i, 0)),
        out_shape=jax.ShapeDtypeStruct((T, D), F32),
        scratch_shapes=[pltpu.VMEM((2, tm, D), F32), pltpu.SemaphoreType.DMA((COMBINE_PARTS,))],
        compiler_params=_cparams(("arbitrary",)),
        name="moe_combine",
    )(dest, y, gates, h, gt, gfin)


def _moe_plan(eids, ranks, counts, *, tm, n_tiles):
    padded = (counts + tm - 1) // tm * tm
    ends = jnp.cumsum(padded)
    starts = ends - padded
    dest = ranks
    for e in range(N_EXPERTS):
        dest = dest + jnp.where(eids == e, starts[e], 0)
    tile_start = jnp.arange(n_tiles, dtype=jnp.int32) * tm
    tile_expert = jnp.minimum(jnp.sum(tile_start[:, None] >= ends[None, :], axis=1), N_EXPERTS - 1)
    n_active = (ends[-1] // tm).reshape(1)
    pad_rows = jnp.stack([starts + counts, ends])
    return dest.astype(jnp.int32), tile_expert.astype(jnp.int32), n_active.astype(jnp.int32), pad_rows.astype(jnp.int32)


def _mla_weights(w_in, w_q_up, w_kv_up):
    D = w_in.shape[0]
    half = MLA_ROPE // 2
    w_lat = w_in[:, :MLA_Q_RANK + MLA_KV_RANK]
    w_kr = w_in[:, MLA_Q_RANK + MLA_KV_RANK:]
    z64 = jnp.zeros((D, MLA_NOPE), F32)
    z32 = jnp.zeros((D, HEAD_PAD - MLA_QK), F32)
    w1 = jnp.concatenate([w_lat, z64, w_kr, z32, z64, w_kr[:, half:], w_kr[:, :half], z32], axis=1).astype(BF16)

    wq = w_q_up.reshape(MLA_Q_RANK, MLA_HEADS, MLA_QK)
    nope, rope = wq[..., :MLA_NOPE], wq[..., MLA_NOPE:]
    zq = jnp.zeros((MLA_Q_RANK, MLA_HEADS, HEAD_PAD - MLA_QK), F32)
    wq = jnp.concatenate([nope, rope, zq], axis=-1).reshape(MLA_Q_RANK, MLA_HEADS * HEAD_PAD).astype(BF16)

    wkv = w_kv_up.reshape(MLA_KV_RANK, MLA_HEADS, MLA_NOPE + MLA_V)
    k_nope, v = wkv[..., :MLA_NOPE], wkv[..., MLA_NOPE:]
    zk = jnp.zeros((MLA_KV_RANK, MLA_HEADS, HEAD_PAD - MLA_NOPE), F32)
    wka = jnp.concatenate([k_nope, zk], axis=-1).reshape(MLA_KV_RANK, MLA_HEADS * HEAD_PAD).astype(BF16)
    wvt = v.reshape(MLA_KV_RANK, MLA_HEADS * MLA_V).T.astype(BF16)
    return w1, wq, wka, wvt


def _rope_lanes():
    half = MLA_ROPE // 2
    inv_freq = ROPE_THETA ** (-jnp.arange(0, MLA_ROPE, 2, dtype=F32) / MLA_ROPE)
    z = lambda n: jnp.zeros((n,), F32)
    o = lambda n: jnp.ones((n,), F32)
    pad = HEAD_PAD - MLA_QK
    rows = [jnp.concatenate([z(MLA_NOPE), inv_freq, inv_freq, z(pad)]),
            jnp.concatenate([o(MLA_NOPE), z(MLA_ROPE), z(pad)]),
            jnp.concatenate([z(MLA_NOPE), o(MLA_ROPE), z(pad)]),
            jnp.concatenate([z(MLA_NOPE), -o(half), o(half), z(pad)])]
    return jnp.stack(rows + [z(HEAD_PAD)] * 4)


def kernel(x, c, positions, g_mix, g_ffn, w_ada, b_ada, w_mla_in, g_mla_q, w_mla_q_up, g_mla_kv, w_mla_kv_up,
           w_mla_out, g_kv_b, w_ada_kv, b_ada_kv, w_kv_b, rpb_table, w_q_b, w_o_b, w_ffn_gu, w_ffn_down,
           w_router, w_exp_gu, w_exp_down, g_final):
    B, S, D = x.shape
    T = B * S
    h = x.reshape(T, D)

    c8 = jnp.zeros((8, D), F32).at[:B].set(c)
    mod = _ada(c8, w_ada, b_ada[:, None, :])[:, :B]
    mod = mod.reshape(2, B, N_MOD, 1, D)
    mod_kv = _ada(c8, w_ada_kv[None], b_ada_kv[None, None, :])[0, :B].reshape(B, 2, 1, D)
    sh_kv, sc_kv = mod_kv[:, 0], mod_kv[:, 1]

    def mods(layer):
        return [mod[layer, :, k] for k in range(N_MOD)]

    row = lambda v: v.reshape(1, -1)

    sh_m, sc_m, gt_m, sh_f, sc_f, gt_f = mods(0)
    w1, wq, wka, wvt = _mla_weights(w_mla_in[0], w_mla_q_up[0], w_mla_kv_up[0])
    pos = positions.astype(F32).reshape(T, 1)
    q, k, vt = _mla_proj(h, row(g_mix[0]), sc_m, sh_m, pos, _rope_lanes(), w1, row(g_mla_q[0]), row(g_mla_kv[0]),
                         wq, wka, wvt, tm=TM_MLA_PROJ, batch=B, seq=S)
    o = _mla_attn(q.reshape(B, S, -1), k.reshape(B, S, -1), vt, batch=B, seq=S, tq=MLA_TQ)

    h = _ffn(o.reshape(T, -1), w_mla_out[0].astype(BF16), h, gt_m, row(g_ffn[0]), sc_f, sh_f, gt_f,
             w_ffn_gu[0].astype(BF16), w_ffn_down[0].astype(BF16), tm=TM_FFN, tf=FF_CHUNK,
             tiles_per_batch=S // TM_FFN)

    sh_m, sc_m, gt_m, sh_f, sc_f, gt_f = mods(1)
    qkv = _l1_proj(h, row(g_kv_b), sc_kv, sh_kv, row(g_mix[1]), sc_m, sh_m,
                   w_kv_b.astype(BF16), w_q_b[0].astype(BF16), tm=TM_L1_PROJ, tiles_per_batch=S // TM_L1_PROJ)

    outs, lses = [], []
    for g, (window, dil) in enumerate(DIL_PATTERNS):
        bias = _dil_bias(rpb_table[:, g * DIL_HEADS:(g + 1) * DIL_HEADS], window // dil, dil)
        o_g, lse_g = _dil_attn(*qkv[3 * g:3 * g + 3], bias, group=g, dil=dil, batch=B, seq=S)
        outs.append(o_g)
        lses.append(lse_g)
    h, hn, eids, ranks, gates, counts = _dil_out_route(
        outs, lses, w_o_b[0].astype(BF16), h, gt_m, row(g_ffn[1]), sc_f, sh_f, w_router[0].T,
        tm=TM_DIL_OUT_ROUTE, tiles_per_batch=S // TM_DIL_OUT_ROUTE)
    n_tiles = (T * 2) // MOE_TM + N_EXPERTS
    dest, tile_expert, n_active, pad_rows = _moe_plan(eids, ranks, counts[:, 0], tm=MOE_TM, n_tiles=n_tiles)
    xs = _dispatch(pad_rows, dest, hn, n_slots=n_tiles * MOE_TM, tm=TM_DISPATCH)
    y = _experts(tile_expert, n_active, xs, w_exp_gu[0].astype(BF16), w_exp_down[0].astype(BF16),
                 tm=MOE_TM, tf=FF_CHUNK)
    out = _combine(dest, y, gates.T, h, gt_f, row(g_final), tm=TM_COMBINE, tiles_per_batch=S // TM_COMBINE)
    return out.reshape(B, S, D)
```
